```python
import jax, jax.numpy as jnp
from jax import lax
import numpy as np

D_MODEL = 1024
BATCH = 8
SEQ = 2048
DEPTH = 1
DEC_BATCH = 32
DEC_SEQ = 1
PAST_LEN = 16384
PAGE_SIZE = 128

N_META = 16
FFN_DIM = 2816
MLA_HEADS = 8
MLA_NOPE = 64
MLA_ROPE = 32
MLA_V = 64
Q_RANK = 384
KV_RANK = 256
MLA_SCALE = (MLA_NOPE + MLA_ROPE) ** -0.5
Q_BLOCK = 128
RET_HEADS = 4
RET_DK = 128
RET_DV = 256
RET_CHUNK = 128
ROPE_BASE = 10000.0
NORM_EPS = 1e-6
SPLITS = (Q_RANK, KV_RANK, MLA_ROPE, RET_HEADS * RET_DK, RET_HEADS * RET_DK,
          RET_HEADS * RET_DV, RET_HEADS * RET_DV, D_MODEL, D_MODEL)
IN_DIM = Q_RANK + KV_RANK + MLA_ROPE + 2 * RET_HEADS * RET_DK + 2 * RET_HEADS * RET_DV + 2 * D_MODEL

kernel_name = "mla_retention_gated_macaron_step"


def rms_norm(x, g):
    xf = x.astype(jnp.float32)
    y = xf * lax.rsqrt(jnp.mean(xf * xf, axis=-1, keepdims=True) + NORM_EPS)
    return (y * g.astype(jnp.float32)).astype(x.dtype)


def half_ffn(h, norm, w_gate, w_up, w_down):
    u = rms_norm(h, norm)
    return h + 0.5 * ((jax.nn.silu(u @ w_gate) * (u @ w_up)) @ w_down)


def rope(x, pos):
    half = x.shape[-1] // 2
    inv_freq = ROPE_BASE ** (-jnp.arange(half, dtype=jnp.float32) / half)
    ang = pos.astype(jnp.float32)[:, None] * inv_freq[None, :]
    ang = ang.reshape((ang.shape[0],) + (1,) * (x.ndim - 3) + (half,))
    cos, sin = jnp.cos(ang), jnp.sin(ang)
    xf = x.astype(jnp.float32)
    x1, x2 = xf[..., :half], xf[..., half:]
    return jnp.concatenate([x1 * cos - x2 * sin, x1 * sin + x2 * cos], axis=-1).astype(x.dtype)


def ret_log_gamma():
    return jnp.log1p(-jnp.exp2(-5.0 - jnp.arange(RET_HEADS, dtype=jnp.float32)))


def mixer_inputs(u, pos, W):
    B, T, _ = u.shape
    z = u @ W["w_in"]
    c_q, c_kv, k_r, rq, rk, rv, rg, ga, gb = jnp.split(z, np.cumsum(SPLITS)[:-1].tolist(), axis=-1)
    c_q = rms_norm(c_q, W["q_norm"])
    q = (c_q @ W["w_uq"]).reshape(B, T, MLA_HEADS, MLA_NOPE + MLA_ROPE)
    q_lat = jnp.einsum("bthn,rhn->bthr", q[..., :MLA_NOPE], W["w_uk"])
    q_rope = rope(q[..., MLA_NOPE:], pos)
    c_kv = rms_norm(c_kv, W["kv_norm"])
    k_r = rope(k_r, pos)
    rq = rope(rq.reshape(B, T, RET_HEADS, RET_DK), pos)
    rk = rope(rk.reshape(B, T, RET_HEADS, RET_DK), pos) * (RET_DK ** -0.5)
    rv = rv.reshape(B, T, RET_HEADS, RET_DV)
    return q_lat, q_rope, c_kv, k_r, rq, rk, rv, rg, ga, gb


def latent_attend(q_lat, q_rope, ckv, krope, q_pos, k_pos):
    s = (jnp.einsum("bqhr,bkr->bhqk", q_lat, ckv, preferred_element_type=jnp.float32)
         + jnp.einsum("bqhp,bkp->bhqk", q_rope, krope, preferred_element_type=jnp.float32))
    s = jnp.where(k_pos[None, :] <= q_pos[:, None], s * MLA_SCALE, -jnp.inf)
    p = jax.nn.softmax(s, axis=-1)
    return jnp.einsum("bhqk,bkr->bqhr", p.astype(ckv.dtype), ckv)


def prompt_latent_attention(q_lat, q_rope, c_kv, k_r):
    B, T = q_lat.shape[:2]
    n_blk = -(-T // Q_BLOCK)
    pad = n_blk * Q_BLOCK - T

    def blocks(a):
        a = jnp.pad(a, ((0, 0), (0, pad)) + ((0, 0),) * (a.ndim - 2))
        return jnp.moveaxis(a.reshape((B, n_blk, Q_BLOCK) + a.shape[2:]), 1, 0)

    q_pos = jnp.arange(n_blk * Q_BLOCK).reshape(n_blk, Q_BLOCK)
    k_pos = jnp.arange(T)
    o = lax.map(lambda blk: latent_attend(blk[0], blk[1], c_kv, k_r, blk[2], k_pos),
                (blocks(q_lat), blocks(q_rope), q_pos))
    o = jnp.moveaxis(o, 0, 1).reshape((B, n_blk * Q_BLOCK) + o.shape[3:])
    return o[:, :T]


def ret_chunk(q, k, v, S, log_gamma):
    C = q.shape[1]
    q, k, v, S = (a.astype(jnp.float32) for a in (q, k, v, S))
    idx = jnp.arange(C, dtype=jnp.float32)
    diff = idx[:, None] - idx[None, :]
    decay = jnp.where(diff >= 0, jnp.exp(log_gamma[:, None, None] * jnp.maximum(diff, 0.0)), 0.0)
    scores = jnp.einsum("bihd,bjhd->bhij", q, k) * decay
    o = jnp.einsum("bhij,bjhe->bihe", scores, v)
    q_decay = jnp.exp(log_gamma[None, :] * (idx[:, None] + 1.0))
    o = o + jnp.einsum("bihd,bhde->bihe", q, S) * q_decay[None, :, :, None]
    k_decay = jnp.exp(log_gamma[None, :] * (C - 1.0 - idx[:, None]))
    S_new = (jnp.exp(log_gamma * C)[None, :, None, None] * S
             + jnp.einsum("bjhd,bjhe->bhde", k * k_decay[None, :, :, None], v))
    return o, S_new


def prompt_retention(rq, rk, rv):
    B, T, H = rq.shape[:3]
    lg = ret_log_gamma()
    S0 = jnp.zeros((B, H, RET_DK, RET_DV), jnp.float32)
    o_meta, S = ret_chunk(rq[:, :N_META], rk[:, :N_META], rv[:, :N_META], S0, lg)
    n_c = (T - N_META) // RET_CHUNK

    def chunks(a):
        return jnp.moveaxis(a[:, N_META:].reshape((B, n_c, RET_CHUNK) + a.shape[2:]), 1, 0)

    def step(S, qkv):
        o, S = ret_chunk(qkv[0], qkv[1], qkv[2], S, lg)
        return S, o

    S, o = lax.scan(step, S, (chunks(rq), chunks(rk), chunks(rv)))
    o = jnp.moveaxis(o, 0, 1).reshape(B, T - N_META, H, RET_DV)
    return jnp.concatenate([o_meta, o], axis=1), S


def mixer_output(h, o_lat, o_ret, rg, ga, gb, W):
    B, T = h.shape[:2]
    a = jnp.einsum("bthr,rhv->bthv", o_lat, W["w_uv"]).reshape(B, T, MLA_HEADS * MLA_V) @ W["w_mla_o"]
    of = o_ret.astype(jnp.float32)
    mu = jnp.mean(of, axis=-1, keepdims=True)
    var = jnp.mean(jnp.square(of - mu), axis=-1, keepdims=True)
    on = ((of - mu) * lax.rsqrt(var + NORM_EPS)).reshape(B, T, RET_HEADS * RET_DV) * W["ret_gn"].astype(jnp.float32)
    r = (jax.nn.silu(rg) * on.astype(h.dtype)) @ W["w_ret_o"]
    m = jax.nn.sigmoid(ga) * a + jax.nn.sigmoid(gb) * r
    return h + m @ W["w_out"]


def prompt_layer(x, W):
    T = x.shape[1]
    pos = jnp.arange(T)
    h = half_ffn(x, W["ffn1_norm"], W["ffn1_gate"], W["ffn1_up"], W["ffn1_down"])
    u = rms_norm(h, W["mix_norm"])
    q_lat, q_rope, c_kv, k_r, rq, rk, rv, rg, ga, gb = mixer_inputs(u, pos, W)
    o_lat = prompt_latent_attention(q_lat, q_rope, c_kv, k_r)
    o_ret, s_ret = prompt_retention(rq, rk, rv)
    h = mixer_output(h, o_lat, o_ret, rg, ga, gb, W)
    h = half_ffn(h, W["ffn2_norm"], W["ffn2_gate"], W["ffn2_up"], W["ffn2_down"])
    return h, c_kv, k_r, s_ret


def sample_layer(x, ckv_pool, kr_pool, s_ret, page_table, W):
    B, S, _ = x.shape
    past_len = page_table.shape[1] * PAGE_SIZE
    pos = past_len + jnp.arange(S)
    h = half_ffn(x, W["ffn1_norm"], W["ffn1_gate"], W["ffn1_up"], W["ffn1_down"])
    u = rms_norm(h, W["mix_norm"])
    q_lat, q_rope, c_kv, k_r, rq, rk, rv, rg, ga, gb = mixer_inputs(u, pos, W)
    past_ckv = ckv_pool[page_table].reshape(B, past_len, KV_RANK)
    past_kr = kr_pool[page_table].reshape(B, past_len, MLA_ROPE)
    keys_ckv = jnp.concatenate([past_ckv, c_kv.astype(past_ckv.dtype)], axis=1)
    keys_kr = jnp.concatenate([past_kr, k_r.astype(past_kr.dtype)], axis=1)
    o_lat = latent_attend(q_lat, q_rope, keys_ckv, keys_kr, pos, jnp.arange(past_len + S))
    o_ret, s_new = ret_chunk(rq, rk, rv, s_ret, ret_log_gamma())
    h = mixer_output(h, o_lat, o_ret, rg, ga, gb, W)
    h = half_ffn(h, W["ffn2_norm"], W["ffn2_gate"], W["ffn2_up"], W["ffn2_down"])
    return h, c_kv, k_r, s_new


def setup_inputs(seed: int = 0) -> dict:
    key = jax.random.key(seed)
    ks = iter(jax.random.split(key, 32))
    f32 = jnp.float32
    L = DEPTH

    def nrm(shape, scale=1.0):
        return jax.random.normal(next(ks), shape, f32) * scale

    def gain(shape):
        return 1.0 + nrm(shape, 0.02)

    n_pages = PAST_LEN // PAGE_SIZE
    n_used = DEC_BATCH * n_pages
    n_pool = n_used + n_used // 4
    x_prompt = nrm((BATCH, SEQ, D_MODEL))
    x_sample = nrm((DEC_BATCH, DEC_SEQ, D_MODEL))
    cache_ckv = nrm((L, n_pool, PAGE_SIZE, KV_RANK))
    cache_krope = nrm((L, n_pool, PAGE_SIZE, MLA_ROPE))
    state_ret = nrm((L, DEC_BATCH, RET_HEADS, RET_DK, RET_DV), 0.5)
    page_table = jax.random.permutation(next(ks), n_pool)[:n_used].reshape(DEC_BATCH, n_pages).astype(jnp.int32)
    return {
        "x_prompt": x_prompt,
        "x_sample": x_sample,
        "cache_ckv": cache_ckv,
        "cache_krope": cache_krope,
        "state_ret": state_ret,
        "page_table": page_table,
        "meta_tokens": nrm((N_META, D_MODEL)),
        "ffn1_norm": gain((L, D_MODEL)),
        "ffn1_gate": nrm((L, D_MODEL, FFN_DIM), D_MODEL ** -0.5),
        "ffn1_up": nrm((L, D_MODEL, FFN_DIM), D_MODEL ** -0.5),
        "ffn1_down": nrm((L, FFN_DIM, D_MODEL), FFN_DIM ** -0.5),
        "mix_norm": gain((L, D_MODEL)),
        "w_in": nrm((L, D_MODEL, IN_DIM), D_MODEL ** -0.5),
        "q_norm": gain((L, Q_RANK)),
        "kv_norm": gain((L, KV_RANK)),
        "w_uq": nrm((L, Q_RANK, MLA_HEADS * (MLA_NOPE + MLA_ROPE)), Q_RANK ** -0.5),
        "w_uk": nrm((L, KV_RANK, MLA_HEADS, MLA_NOPE), KV_RANK ** -0.5),
        "w_uv": nrm((L, KV_RANK, MLA_HEADS, MLA_V), KV_RANK ** -0.5),
        "w_mla_o": nrm((L, MLA_HEADS * MLA_V, D_MODEL), (MLA_HEADS * MLA_V) ** -0.5),
        "ret_gn": gain((L, RET_HEADS * RET_DV)),
        "w_ret_o": nrm((L, RET_HEADS * RET_DV, D_MODEL), (RET_HEADS * RET_DV) ** -0.5),
        "w_out": nrm((L, D_MODEL, D_MODEL), D_MODEL ** -0.5),
        "ffn2_norm": gain((L, D_MODEL)),
        "ffn2_gate": nrm((L, D_MODEL, FFN_DIM), D_MODEL ** -0.5),
        "ffn2_up": nrm((L, D_MODEL, FFN_DIM), D_MODEL ** -0.5),
        "ffn2_down": nrm((L, FFN_DIM, D_MODEL), FFN_DIM ** -0.5),
        "final_norm": gain((D_MODEL,)),
    }


def reference(x_prompt, x_sample, cache_ckv, cache_krope, state_ret, page_table, meta_tokens,
              ffn1_norm, ffn1_gate, ffn1_up, ffn1_down, mix_norm, w_in, q_norm, kv_norm,
              w_uq, w_uk, w_uv, w_mla_o, ret_gn, w_ret_o, w_out,
              ffn2_norm, ffn2_gate, ffn2_up, ffn2_down, final_norm):
    B = x_prompt.shape[0]
    meta = jnp.broadcast_to(meta_tokens[None].astype(x_prompt.dtype), (B, N_META, D_MODEL))
    hp = jnp.concatenate([meta, x_prompt], axis=1)
    hs = x_sample
    ckv_p, kr_p, sr_p, ckv_s, kr_s, sr_s = [], [], [], [], [], []
    for l in range(DEPTH):
        W = dict(ffn1_norm=ffn1_norm[l], ffn1_gate=ffn1_gate[l], ffn1_up=ffn1_up[l], ffn1_down=ffn1_down[l],
                 mix_norm=mix_norm[l], w_in=w_in[l], q_norm=q_norm[l], kv_norm=kv_norm[l],
                 w_uq=w_uq[l], w_uk=w_uk[l], w_uv=w_uv[l], w_mla_o=w_mla_o[l],
                 ret_gn=ret_gn[l], w_ret_o=w_ret_o[l], w_out=w_out[l],
                 ffn2_norm=ffn2_norm[l], ffn2_gate=ffn2_gate[l], ffn2_up=ffn2_up[l], ffn2_down=ffn2_down[l])
        hp, c1, k1, s1 = prompt_layer(hp, W)
        hs, c2, k2, s2 = sample_layer(hs, cache_ckv[l], cache_krope[l], state_ret[l], page_table, W)
        ckv_p.append(c1); kr_p.append(k1); sr_p.append(s1)
        ckv_s.append(c2); kr_s.append(k2); sr_s.append(s2)
    y_prompt = rms_norm(hp[:, N_META:], final_norm)
    y_sample = rms_norm(hs, final_norm)
    return (y_prompt, y_sample, jnp.stack(ckv_p), jnp.stack(kr_p), jnp.stack(sr_p),
            jnp.stack(ckv_s), jnp.stack(kr_s), jnp.stack(sr_s))
```

```python
import functools
import math

import jax
import jax.numpy as jnp
from jax import lax
from jax.experimental import pallas as pl
from jax.experimental.pallas import tpu as pltpu

N_META = 16
PAGE_SIZE = 128
MLA_HEADS = 8
MLA_NOPE = 64
MLA_ROPE = 32
MLA_V = 64
Q_RANK = 384
KV_RANK = 256
MLA_SCALE = (MLA_NOPE + MLA_ROPE) ** -0.5
RET_HEADS = 4
RET_DK = 128
RET_DV = 256
ROPE_BASE = 10000.0
NORM_EPS = 1e-6
RET_LOG_GAMMA = tuple(math.log1p(-(2.0 ** (-5.0 - h))) for h in range(RET_HEADS))

VMEM_LIMIT_BYTES = 56 * 1024 * 1024
FFN_CHUNK = 256
TOKEN_TILE = 512
ATTN_BLOCK = 256
RET_CHUNK = 256
DEC_PAGES_PER_STEP = 16
MASK_VALUE = -1e30

F32 = jnp.float32
BF16 = jnp.bfloat16


def _params(*sem):
    return pltpu.CompilerParams(dimension_semantics=sem, vmem_limit_bytes=VMEM_LIMIT_BYTES)


def _const_spec(shape):
    n = len(shape)
    return pl.BlockSpec(shape, lambda *_: (0,) * n, pipeline_mode=pl.Buffered(1))


def _rms(x, g):
    return x * lax.rsqrt(jnp.mean(x * x, axis=-1, keepdims=True) + NORM_EPS) * g


def _sigmoid(x):
    return 1.0 / (1.0 + jnp.exp(-x))


def _dot(a, b):
    return jnp.dot(a, b, preferred_element_type=F32)


def _dot_nt(a, b):
    return lax.dot_general(a, b, (((1,), (1,)), ((), ())), preferred_element_type=F32)


def _ffn_body(x_ref, g_ref, wg_ref, wu_ref, wd_ref, *rest, final):
    o_ref = rest[-1]
    x = x_ref[...]
    u = _rms(x, g_ref[...]).astype(BF16)

    def chunk(c, acc):
        gate = _dot(u, wg_ref[c])
        up = _dot(u, wu_ref[c])
        act = (gate * _sigmoid(gate) * up).astype(BF16)
        return acc + _dot(act, wd_ref[c])

    acc = lax.fori_loop(0, wg_ref.shape[0], chunk, jnp.zeros(x.shape, F32))
    y = x + 0.5 * acc
    if final:
        y = _rms(y, rest[0][...])
    o_ref[...] = y


def _ffn(x, norm, wg, wu, wd, final_norm=None):
    m, d = x.shape
    tm = min(m, TOKEN_TILE)
    assert m % tm == 0
    nc, _, fc = wg.shape
    final = final_norm is not None
    in_specs = [
        pl.BlockSpec((tm, d), lambda i: (i, 0)),
        _const_spec((1, d)),
        _const_spec((nc, d, fc)),
        _const_spec((nc, d, fc)),
        _const_spec((nc, fc, d)),
    ]
    args = [x, norm, wg, wu, wd]
    if final:
        in_specs.append(_const_spec((1, d)))
        args.append(final_norm)
    return pl.pallas_call(
        functools.partial(_ffn_body, final=final),
        grid=(m // tm,),
        in_specs=in_specs,
        out_specs=pl.BlockSpec((tm, d), lambda i: (i, 0)),
        out_shape=jax.ShapeDtypeStruct((m, d), F32),
        compiler_params=_params("parallel"),
        name="ffn_final" if final else "ffn",
    )(*args)


def _rope32(x, cos, sin):
    lane = lax.broadcasted_iota(jnp.int32, x.shape, 1)
    width = x.shape[1]
    partner = jnp.where((lane & 31) < 16, pltpu.roll(x, width - 16, axis=1), pltpu.roll(x, 16, axis=1))
    return x * cos + partner * sin


def _mix_in_body(h_ref, g_ref, wq_ref, wkv_ref, wkr_ref, wrq_ref, wrk_ref, wrv_ref, wrg_ref, wga_ref, wgb_ref,
                 qn_ref, kvn_ref, wuqn_ref, wuqr_ref, wuk_ref, rope_r_ref, rope_q_ref,
                 qlat_ref, qrope_ref, ckv_ref, kr_ref, kcat_ref, rq_ref, rk_ref, rv_ref, srg_ref, sga_ref, sgb_ref):
    u = _rms(h_ref[...], g_ref[...]).astype(BF16)
    cos_q, sin_q = rope_q_ref[0], rope_q_ref[1]
    cos_r, sin_r = rope_r_ref[0], rope_r_ref[1]

    cq = _rms(_dot(u, wq_ref[...]), qn_ref[...]).astype(BF16)
    q_nope = (_dot(cq, wuqn_ref[...]) * MLA_SCALE).astype(BF16)
    for p in range(MLA_HEADS // 2):
        pair = _dot(q_nope[:, p * 128:(p + 1) * 128], wuk_ref[p])
        qlat_ref[2 * p] = pair[:, :KV_RANK].astype(BF16)
        qlat_ref[2 * p + 1] = pair[:, KV_RANK:].astype(BF16)
    q_rope = _rope32(_dot(cq, wuqr_ref[...]), cos_q, sin_q) * MLA_SCALE
    qrope_ref[...] = q_rope.astype(BF16)

    ckv = _rms(_dot(u, wkv_ref[...]), kvn_ref[...])
    ckv_ref[...] = ckv
    kcat_ref[:, :KV_RANK] = ckv.astype(BF16)
    kr = _rope32(_dot(u, wkr_ref[...]), cos_q, sin_q)
    kr_ref[...] = kr[:, :MLA_ROPE]
    kcat_ref[:, KV_RANK:] = kr.astype(BF16)

    rq = _dot(u, wrq_ref[...])
    rk = _dot(u, wrk_ref[...])
    for hd in range(RET_HEADS):
        sl = slice(hd * RET_DK, (hd + 1) * RET_DK)
        xq, xk = rq[:, sl], rk[:, sl]
        rq_ref[:, sl] = (xq * cos_r + pltpu.roll(xq, RET_DK // 2, axis=1) * sin_r).astype(BF16)
        rk_ref[:, sl] = ((xk * cos_r + pltpu.roll(xk, RET_DK // 2, axis=1) * sin_r) * (RET_DK ** -0.5)).astype(BF16)
    rv_ref[...] = _dot(u, wrv_ref[...]).astype(BF16)

    rg = _dot(u, wrg_ref[...])
    srg_ref[...] = (rg * _sigmoid(rg)).astype(BF16)
    sga_ref[...] = _sigmoid(_dot(u, wga_ref[...])).astype(BF16)
    sgb_ref[...] = _sigmoid(_dot(u, wgb_ref[...])).astype(BF16)


def _mix_in(h, W, rope_r, rope_q):
    b, t, d = h.shape
    tm = min(t, TOKEN_TILE)
    assert t % tm == 0
    hr = RET_HEADS * RET_DK
    hv = RET_HEADS * RET_DV
    qw = MLA_HEADS * MLA_ROPE
    weights = [W["mix_norm"], W["w_q"], W["w_kv"], W["w_kr"], W["w_rq"], W["w_rk"], W["w_rv"], W["w_rg"],
               W["w_ga"], W["w_gb"], W["q_norm"], W["kv_norm"], W["w_uq_nope"], W["w_uq_rope"], W["w_uk_pair"]]
    tok = lambda width: pl.BlockSpec((None, tm, width), lambda bi, ti: (bi, ti, 0))
    in_specs = ([tok(d)] + [_const_spec(w.shape) for w in weights]
                + [pl.BlockSpec((2, tm, RET_DK), lambda bi, ti: (0, ti, 0)),
                   pl.BlockSpec((2, tm, qw), lambda bi, ti: (0, ti, 0))])
    out_specs = [
        pl.BlockSpec((None, MLA_HEADS, tm, KV_RANK), lambda bi, ti: (bi, 0, ti, 0)),
        tok(qw), tok(KV_RANK), tok(MLA_ROPE), tok(2 * KV_RANK), tok(hr), tok(hr), tok(hv), tok(hv), tok(d), tok(d),
    ]
    sds = jax.ShapeDtypeStruct
    out_shape = [
        sds((b, MLA_HEADS, t, KV_RANK), BF16), sds((b, t, qw), BF16), sds((b, t, KV_RANK), F32),
        sds((b, t, MLA_ROPE), F32), sds((b, t, 2 * KV_RANK), BF16), sds((b, t, hr), BF16), sds((b, t, hr), BF16),
        sds((b, t, hv), BF16), sds((b, t, hv), BF16), sds((b, t, d), BF16), sds((b, t, d), BF16),
    ]
    names = ("qlat", "qrope", "ckv", "kr", "kcat", "rq", "rk", "rv", "srg", "sga", "sgb")
    outs = pl.pallas_call(
        _mix_in_body,
        grid=(b, t // tm),
        in_specs=in_specs,
        out_specs=out_specs,
        out_shape=out_shape,
        compiler_params=_params("parallel", "parallel"),
        name="mixer_inputs",
    )(h, *weights, rope_r, rope_q)
    return dict(zip(names, outs))


def _softmax_step(s, k_lat, m_ref, l_ref, acc_ref):
    m_prev = m_ref[...]
    m_new = jnp.maximum(m_prev, jnp.max(s, axis=-1, keepdims=True))
    alpha = jnp.exp(m_prev - m_new)
    p = jnp.exp(s - m_new)
    l_ref[...] = alpha * l_ref[...] + jnp.sum(p, axis=-1, keepdims=True)
    acc_ref[...] = alpha * acc_ref[...] + _dot(p.astype(BF16), k_lat)
    m_ref[...] = m_new


def _attn_body(qlat_ref, qrope_ref, kcat_ref, kmeta_ref, wuv_ref, o_ref, qcat_ref, m_ref, l_ref, acc_ref):
    i = pl.program_id(1)
    blk = qrope_ref.shape[0]
    rows = MLA_HEADS * blk

    lane_head = lax.broadcasted_iota(jnp.int32, (blk, MLA_HEADS * MLA_ROPE), 1) // MLA_ROPE
    qr = qrope_ref[...]
    for hd in range(MLA_HEADS):
        qcat_ref[hd * blk:(hd + 1) * blk, :KV_RANK] = qlat_ref[hd]
        qcat_ref[hd * blk:(hd + 1) * blk, KV_RANK:] = jnp.where(lane_head == hd, qr, jnp.zeros_like(qr))
    q = qcat_ref[...]

    km = kmeta_ref[...]
    s = _dot_nt(q, km)
    m0 = jnp.max(s, axis=-1, keepdims=True)
    p = jnp.exp(s - m0)
    m_ref[...] = m0
    l_ref[...] = jnp.sum(p, axis=-1, keepdims=True)
    acc_ref[...] = _dot(p.astype(BF16), km[:, :KV_RANK])

    def full_block(j, carry):
        kb = kcat_ref[pl.ds(pl.multiple_of(j * blk, blk), blk), :]
        _softmax_step(_dot_nt(q, kb), kb[:, :KV_RANK], m_ref, l_ref, acc_ref)
        return carry

    lax.fori_loop(0, i, full_block, 0)

    kb = kcat_ref[pl.ds(pl.multiple_of(i * blk, blk), blk), :]
    tok = lax.broadcasted_iota(jnp.int32, (rows, blk), 0) & (blk - 1)
    key = lax.broadcasted_iota(jnp.int32, (rows, blk), 1)
    s = jnp.where(key <= tok, _dot_nt(q, kb), MASK_VALUE)
    _softmax_step(s, kb[:, :KV_RANK], m_ref, l_ref, acc_ref)

    o = (acc_ref[...] / l_ref[...]).astype(BF16)
    for pr in range(MLA_HEADS // 2):
        pair = jnp.concatenate([o[(2 * pr) * blk:(2 * pr + 1) * blk], o[(2 * pr + 1) * blk:(2 * pr + 2) * blk]], axis=1)
        o_ref[:, pr * 2 * MLA_V:(pr + 1) * 2 * MLA_V] = _dot(pair, wuv_ref[pr]).astype(BF16)


def _prompt_attention(qlat, qrope, kcat, kmeta, wuv_pair):
    b, nh, t, r = qlat.shape
    blk = min(t, ATTN_BLOCK)
    assert t % blk == 0 and blk & (blk - 1) == 0
    rows = nh * blk
    return pl.pallas_call(
        _attn_body,
        grid=(b, t // blk),
        in_specs=[
            pl.BlockSpec((None, nh, blk, r), lambda bi, i: (bi, 0, i, 0)),
            pl.BlockSpec((None, blk, qrope.shape[2]), lambda bi, i: (bi, i, 0)),
            pl.BlockSpec((None, t, kcat.shape[2]), lambda bi, i: (bi, 0, 0)),
            _const_spec(kmeta.shape),
            _const_spec(wuv_pair.shape),
        ],
        out_specs=pl.BlockSpec((None, blk, nh * MLA_V), lambda bi, i: (bi, i, 0)),
        out_shape=jax.ShapeDtypeStruct((b, t, nh * MLA_V), BF16),
        scratch_shapes=[
            pltpu.VMEM((rows, 2 * r), BF16),
            pltpu.VMEM((rows, 1), F32),
            pltpu.VMEM((rows, 1), F32),
            pltpu.VMEM((rows, r), F32),
        ],
        compiler_params=_params("parallel", "arbitrary"),
        name="prompt_attention",
    )(qlat, qrope, kcat, kmeta, wuv_pair)


def _group_norm(o):
    mu = jnp.mean(o, axis=-1, keepdims=True)
    c = o - mu
    return c * lax.rsqrt(jnp.mean(c * c, axis=-1, keepdims=True) + NORM_EPS)


def _ret_body(rq_ref, rk_ref, rv_ref, kmeta_ref, vmeta_ref, on_ref, sout_ref, s_ref):
    c = pl.program_id(1)
    ch = rq_ref.shape[0]
    n_meta = kmeta_ref.shape[0]

    @pl.when(c == 0)
    def _():
        j = lax.broadcasted_iota(jnp.int32, (n_meta, 1), 0).astype(F32)
        for hd in range(RET_HEADS):
            kd = kmeta_ref[:, hd * RET_DK:(hd + 1) * RET_DK].astype(F32) * jnp.exp(RET_LOG_GAMMA[hd] * (n_meta - 1.0 - j))
            s_ref[hd] = _dot(kd.T.astype(BF16), vmeta_ref[:, hd * RET_DV:(hd + 1) * RET_DV])

    row = lax.broadcasted_iota(jnp.int32, (ch, 1), 0).astype(F32)
    diff = (lax.broadcasted_iota(jnp.int32, (ch, ch), 0) - lax.broadcasted_iota(jnp.int32, (ch, ch), 1)).astype(F32)
    for hd in range(RET_HEADS):
        lg = RET_LOG_GAMMA[hd]
        q = rq_ref[:, hd * RET_DK:(hd + 1) * RET_DK]
        k = rk_ref[:, hd * RET_DK:(hd + 1) * RET_DK]
        v = rv_ref[:, hd * RET_DV:(hd + 1) * RET_DV]
        decay = jnp.where(diff >= 0, jnp.exp(lg * jnp.maximum(diff, 0.0)), 0.0)
        scores = _dot_nt(q, k) * decay
        s_old = s_ref[hd]
        o = _dot(scores.astype(BF16), v) + _dot(q, s_old.astype(BF16)) * jnp.exp(lg * (row + 1.0))
        kd = k.astype(F32) * jnp.exp(lg * (ch - 1.0 - row))
        s_ref[hd] = math.exp(lg * ch) * s_old + _dot(kd.T.astype(BF16), v)
        on_ref[:, hd * RET_DV:(hd + 1) * RET_DV] = _group_norm(o).astype(BF16)

    @pl.when(c == pl.num_programs(1) - 1)
    def _():
        sout_ref[...] = s_ref[...]


def _prompt_retention(rq, rk, rv, kmeta, vmeta):
    b, t, hr = rq.shape
    hv = rv.shape[2]
    ch = min(t, RET_CHUNK)
    assert t % ch == 0
    tok = lambda width: pl.BlockSpec((None, ch, width), lambda bi, ci: (bi, ci, 0))
    return pl.pallas_call(
        _ret_body,
        grid=(b, t // ch),
        in_specs=[tok(hr), tok(hr), tok(hv), _const_spec(kmeta.shape), _const_spec(vmeta.shape)],
        out_specs=[tok(hv), pl.BlockSpec((None, RET_HEADS, RET_DK, RET_DV), lambda bi, ci: (bi, 0, 0, 0))],
        out_shape=[jax.ShapeDtypeStruct((b, t, hv), BF16),
                   jax.ShapeDtypeStruct((b, RET_HEADS, RET_DK, RET_DV), F32)],
        scratch_shapes=[pltpu.VMEM((RET_HEADS, RET_DK, RET_DV), F32)],
        compiler_params=_params("parallel", "arbitrary"),
        name="prompt_retention",
    )(rq, rk, rv, kmeta, vmeta)


def _dec_attn_body(pt_ref, qlat_ref, qrope_ref, kself_ref, wuv_ref, *rest, pages):
    ckv_refs = rest[:pages]
    kr_refs = rest[pages:2 * pages]
    o_ref, m_ref, l_ref, acc_ref = rest[2 * pages:]
    c = pl.program_id(1)
    ql = qlat_ref[...]
    qr = qrope_ref[...]

    @pl.when(c == 0)
    def _():
        ks = kself_ref[...].astype(F32)
        qf = jnp.concatenate([ql, qr], axis=1).astype(F32)
        m_ref[...] = jnp.sum(qf * ks, axis=-1, keepdims=True)
        l_ref[...] = jnp.ones(l_ref.shape, F32)
        acc_ref[...] = jnp.broadcast_to(ks[:, :KV_RANK], acc_ref.shape)

    k_lat = jnp.concatenate([r[...].astype(BF16) for r in ckv_refs], axis=0)
    k_rope = jnp.concatenate([r[...].astype(BF16) for r in kr_refs], axis=0)
    _softmax_step(_dot_nt(ql, k_lat) + _dot_nt(qr, k_rope), k_lat, m_ref, l_ref, acc_ref)

    @pl.when(c == pl.num_programs(1) - 1)
    def _():
        o = (acc_ref[...] / l_ref[...]).astype(BF16)
        z = _dot(o, wuv_ref[...])
        head = lax.broadcasted_iota(jnp.int32, z.shape, 0)
        col_head = lax.broadcasted_iota(jnp.int32, z.shape, 1) // MLA_V
        o_ref[...] = jnp.sum(jnp.where(head == col_head, z, 0.0), axis=0, keepdims=True).astype(BF16)


def _decode_attention(page_table, qlat, qrope, kself, wuv_all, cache_ckv, cache_kr):
    bd, nh, r = qlat.shape
    n_pages = page_table.shape[1]
    pages = min(n_pages, DEC_PAGES_PER_STEP)
    assert n_pages % pages == 0

    def page_spec(width, k):
        return pl.BlockSpec((None, PAGE_SIZE, width), lambda b, c, pt: (pt[b, c * pages + k], 0, 0))

    per_req = lambda shape: pl.BlockSpec((None,) + shape, lambda b, c, pt: (b, 0, 0))
    grid_spec = pltpu.PrefetchScalarGridSpec(
        num_scalar_prefetch=1,
        grid=(bd, n_pages // pages),
        in_specs=([per_req((nh, r)), per_req((nh, MLA_ROPE)), per_req((1, r + MLA_ROPE)),
                   pl.BlockSpec(wuv_all.shape, lambda b, c, pt: (0, 0))]
                  + [page_spec(r, k) for k in range(pages)]
                  + [page_spec(MLA_ROPE, k) for k in range(pages)]),
        out_specs=per_req((1, nh * MLA_V)),
        scratch_shapes=[pltpu.VMEM((nh, 1), F32), pltpu.VMEM((nh, 1), F32), pltpu.VMEM((nh, r), F32)],
    )
    return pl.pallas_call(
        functools.partial(_dec_attn_body, pages=pages),
        grid_spec=grid_spec,
        out_shape=jax.ShapeDtypeStruct((bd, 1, nh * MLA_V), BF16),
        compiler_params=_params("parallel", "arbitrary"),
        name="decode_attention",
    )(page_table, qlat, qrope, kself, wuv_all, *([cache_ckv] * pages), *([cache_kr] * pages))


def _dec_ret_body(rq_ref, rk_ref, rv_ref, s_ref, on_ref, snew_ref):
    eye = (lax.broadcasted_iota(jnp.int32, (RET_DK, RET_DK), 0)
           == lax.broadcasted_iota(jnp.int32, (RET_DK, RET_DK), 1))

    def column(x):
        return jnp.sum(jnp.where(eye, jnp.broadcast_to(x, (RET_DK, RET_DK)), 0.0), axis=1, keepdims=True)

    for hd in range(RET_HEADS):
        gamma = math.exp(RET_LOG_GAMMA[hd])
        q = rq_ref[:, hd * RET_DK:(hd + 1) * RET_DK].astype(F32)
        k = rk_ref[:, hd * RET_DK:(hd + 1) * RET_DK].astype(F32)
        v = rv_ref[:, hd * RET_DV:(hd + 1) * RET_DV].astype(F32)
        s_old = s_ref[hd]
        qs = jnp.sum(column(q) * s_old, axis=0, keepdims=True)
        o = jnp.sum(q * k, axis=-1, keepdims=True) * v + qs * gamma
        snew_ref[hd] = gamma * s_old + column(k) * v
        on_ref[:, hd * RET_DV:(hd + 1) * RET_DV] = _group_norm(o).astype(BF16)


def _decode_retention(rq, rk, rv, state):
    bd = rq.shape[0]
    row = lambda width: pl.BlockSpec((None, 1, width), lambda b: (b, 0, 0))
    st = pl.BlockSpec((None, RET_HEADS, RET_DK, RET_DV), lambda b: (b, 0, 0, 0))
    return pl.pallas_call(
        _dec_ret_body,
        grid=(bd,),
        in_specs=[row(rq.shape[2]), row(rk.shape[2]), row(rv.shape[2]), st],
        out_specs=[row(rv.shape[2]), st],
        out_shape=[jax.ShapeDtypeStruct((bd, 1, rv.shape[2]), BF16), jax.ShapeDtypeStruct(state.shape, F32)],
        compiler_params=_params("parallel"),
        name="decode_retention",
    )(rq, rk, rv, state)


def _mix_out_body(h_ref, a_ref, on_ref, srg_ref, sga_ref, sgb_ref, gn_ref, wa_ref, wr_ref, wo_ref, o_ref):
    a = _dot(a_ref[...], wa_ref[...])
    gated = srg_ref[...].astype(F32) * (on_ref[...].astype(F32) * gn_ref[...])
    r = _dot(gated.astype(BF16), wr_ref[...])
    m = sga_ref[...].astype(F32) * a + sgb_ref[...].astype(F32) * r
    o_ref[...] = h_ref[...] + _dot(m.astype(BF16), wo_ref[...])


def _mix_out(h, apre, on, srg, sga, sgb, W):
    m, d = h.shape
    tm = min(m, TOKEN_TILE)
    assert m % tm == 0
    tok = lambda width: pl.BlockSpec((tm, width), lambda i: (i, 0))
    weights = [W["ret_gn"], W["w_mla_o"], W["w_ret_o"], W["w_out"]]
    return pl.pallas_call(
        _mix_out_body,
        grid=(m // tm,),
        in_specs=[tok(d), tok(apre.shape[1]), tok(on.shape[1]), tok(d), tok(d), tok(d)]
                 + [_const_spec(w.shape) for w in weights],
        out_specs=tok(d),
        out_shape=jax.ShapeDtypeStruct((m, d), F32),
        compiler_params=_params("parallel"),
        name="mixer_output",
    )(h, apre, on, srg, sga, sgb, *weights)


def _rope_tables(pos):
    pos = pos.astype(F32)[:, None]

    def table(width, reps):
        half = width // 2
        inv_freq = ROPE_BASE ** (-jnp.arange(half, dtype=F32) / half)
        ang = pos * inv_freq[None, :]
        cos, sin = jnp.cos(ang), jnp.sin(ang)
        return jnp.stack([jnp.tile(jnp.concatenate([cos, cos], axis=1), (1, reps)),
                          jnp.tile(jnp.concatenate([-sin, sin], axis=1), (1, reps))])

    return table(RET_DK, 1), table(MLA_ROPE, MLA_HEADS)


def _prep_weights(ffn1_norm, ffn1_gate, ffn1_up, ffn1_down, mix_norm, w_in, q_norm, kv_norm, w_uq, w_uk, w_uv,
                  w_mla_o, ret_gn, w_ret_o, w_out, ffn2_norm, ffn2_gate, ffn2_up, ffn2_down, final_norm):
    d = w_in.shape[0]
    hr, hv = RET_HEADS * RET_DK, RET_HEADS * RET_DV
    nh, hd = MLA_HEADS, MLA_NOPE + MLA_ROPE

    def ffn_w(gate, up, down):
        f = gate.shape[1]
        nc = f // FFN_CHUNK
        cols = lambda w: w.astype(BF16).reshape(d, nc, FFN_CHUNK).transpose(1, 0, 2)
        return cols(gate), cols(up), down.astype(BF16).reshape(nc, FFN_CHUNK, d)

    bounds = [0]
    for width in (Q_RANK, KV_RANK, MLA_ROPE, hr, hr, hv, hv, d, d):
        bounds.append(bounds[-1] + width)
    assert bounds[-1] == w_in.shape[1]
    w_q, w_kv, w_kr, w_rq, w_rk, w_rv, w_rg, w_ga, w_gb = (
        w_in[:, bounds[i]:bounds[i + 1]].astype(BF16) for i in range(9))

    w_uq3 = w_uq.reshape(Q_RANK, nh, hd)
    zeros = jnp.zeros((MLA_NOPE, KV_RANK), F32)
    w_uk_t = w_uk.transpose(1, 2, 0)
    w_uk_pair = jnp.stack([jnp.block([[w_uk_t[2 * p], zeros], [zeros, w_uk_t[2 * p + 1]]])
                           for p in range(nh // 2)]).astype(BF16)
    w_uv_t = w_uv.transpose(1, 0, 2)
    zv = jnp.zeros((KV_RANK, MLA_V), F32)
    w_uv_pair = jnp.stack([jnp.block([[w_uv_t[2 * p], zv], [zv, w_uv_t[2 * p + 1]]])
                           for p in range(nh // 2)]).astype(BF16)
    row = lambda v: v.reshape(1, -1)
    return dict(
        ffn1_norm=row(ffn1_norm), ffn1=ffn_w(ffn1_gate, ffn1_up, ffn1_down),
        ffn2_norm=row(ffn2_norm), ffn2=ffn_w(ffn2_gate, ffn2_up, ffn2_down), final_norm=row(final_norm),
        mix_norm=row(mix_norm), w_q=w_q, w_kv=w_kv, w_kr=jnp.tile(w_kr, (1, nh)), w_rq=w_rq, w_rk=w_rk, w_rv=w_rv,
        w_rg=w_rg, w_ga=w_ga, w_gb=w_gb, q_norm=row(q_norm), kv_norm=row(kv_norm),
        w_uq_nope=w_uq3[:, :, :MLA_NOPE].reshape(Q_RANK, nh * MLA_NOPE).astype(BF16),
        w_uq_rope=w_uq3[:, :, MLA_NOPE:].reshape(Q_RANK, nh * MLA_ROPE).astype(BF16),
        w_uk_pair=w_uk_pair, w_uv_pair=w_uv_pair, w_uv_all=w_uv.reshape(KV_RANK, nh * MLA_V).astype(BF16),
        ret_gn=row(ret_gn), w_mla_o=w_mla_o.astype(BF16), w_ret_o=w_ret_o.astype(BF16), w_out=w_out.astype(BF16),
    )


def kernel(x_prompt, x_sample, cache_ckv, cache_krope, state_ret, page_table, meta_tokens, ffn1_norm, ffn1_gate, ffn1_up, ffn1_down, mix_norm, w_in, q_norm, kv_norm, w_uq, w_uk, w_uv, w_mla_o, ret_gn, w_ret_o, w_out, ffn2_norm, ffn2_gate, ffn2_up, ffn2_down, final_norm):
    assert ffn1_gate.shape[0] == 1, "single-layer trunk"
    b, t, d = x_prompt.shape
    bd, dec_seq, _ = x_sample.shape
    assert dec_seq == 1
    n_meta = meta_tokens.shape[0]
    past_len = page_table.shape[1] * PAGE_SIZE
    W = _prep_weights(ffn1_norm[0], ffn1_gate[0], ffn1_up[0], ffn1_down[0], mix_norm[0], w_in[0], q_norm[0],
                      kv_norm[0], w_uq[0], w_uk[0], w_uv[0], w_mla_o[0], ret_gn[0], w_ret_o[0], w_out[0],
                      ffn2_norm[0], ffn2_gate[0], ffn2_up[0], ffn2_down[0], final_norm)

    rope_r, rope_q = _rope_tables(n_meta + jnp.arange(t))
    pos_small = jnp.concatenate([jnp.arange(n_meta), jnp.full((bd,), past_len)])
    rope_r_s, rope_q_s = _rope_tables(pos_small)
    x_small = jnp.concatenate([meta_tokens.astype(F32), x_sample.reshape(bd, d)], axis=0)

    h1 = _ffn(x_prompt.reshape(b * t, d), W["ffn1_norm"], *W["ffn1"])
    h1s = _ffn(x_small, W["ffn1_norm"], *W["ffn1"])
    P = _mix_in(h1.reshape(b, t, d), W, rope_r, rope_q)
    S = _mix_in(h1s.reshape(1, n_meta + bd, d), W, rope_r_s, rope_q_s)

    apre = _prompt_attention(P["qlat"], P["qrope"], P["kcat"], S["kcat"][0, :n_meta], W["w_uv_pair"])
    on, s_prompt = _prompt_retention(P["rq"], P["rk"], P["rv"], S["rk"][0, :n_meta], S["rv"][0, :n_meta])

    nh = MLA_HEADS
    qlat_s = S["qlat"][0, :, n_meta:].transpose(1, 0, 2)
    qrope_s = S["qrope"][0, n_meta:].reshape(bd, nh, MLA_ROPE)
    kself = S["kcat"][0, n_meta:, :KV_RANK + MLA_ROPE].reshape(bd, 1, KV_RANK + MLA_ROPE)
    apre_s = _decode_attention(page_table, qlat_s, qrope_s, kself, W["w_uv_all"], cache_ckv[0], cache_krope[0])
    dec = lambda name: S[name][0, n_meta:].reshape(bd, 1, -1)
    on_s, s_sample = _decode_retention(dec("rq"), dec("rk"), dec("rv"), state_ret[0])

    flat = lambda a: a.reshape(b * t, -1)
    h2 = _mix_out(h1, flat(apre), flat(on), flat(P["srg"]), flat(P["sga"]), flat(P["sgb"]), W)
    y_prompt = _ffn(h2, W["ffn2_norm"], *W["ffn2"], final_norm=W["final_norm"])
    tail = lambda name: S[name][0, n_meta:]
    h2s = _mix_out(h1s[n_meta:], apre_s.reshape(bd, -1), on_s.reshape(bd, -1), tail("srg"), tail("sga"), tail("sgb"), W)
    y_sample = _ffn(h2s, W["ffn2_norm"], *W["ffn2"], final_norm=W["final_norm"])

    with_meta = lambda small, main: jnp.concatenate(
        [jnp.broadcast_to(small[:, :n_meta], (b, n_meta, small.shape[2])), main], axis=1)[None]
    return (
        y_prompt.reshape(b, t, d),
        y_sample.reshape(bd, 1, d),
        with_meta(S["ckv"], P["ckv"]),
        with_meta(S["kr"], P["kr"]),
        s_prompt[None],
        S["ckv"][0, n_meta:].reshape(1, bd, 1, KV_RANK),
        S["kr"][0, n_meta:].reshape(1, bd, 1, MLA_ROPE),
        s_sample[None],
    )
```

```python
import functools
import math

import jax
import jax.numpy as jnp
from jax import lax
from jax.experimental import pallas as pl
from jax.experimental.pallas import tpu as pltpu

N_META = 16
PAGE_SIZE = 128
MLA_HEADS = 8
MLA_NOPE = 64
MLA_ROPE = 32
MLA_V = 64
Q_RANK = 384
KV_RANK = 256
MLA_SCALE = (MLA_NOPE + MLA_ROPE) ** -0.5
RET_HEADS = 4
RET_DK = 128
RET_DV = 256
ROPE_BASE = 10000.0
NORM_EPS = 1e-6
RET_LOG_GAMMA = tuple(math.log1p(-(2.0 ** (-5.0 - h))) for h in range(RET_HEADS))

VMEM_LIMIT_BYTES = 56 * 1024 * 1024
FFN_CHUNK = 256
TOKEN_TILE = 512
ATTN_BLOCK = 256
ATTN_GROUP_HEADS = 2
ATTN_ROW_CHUNK = 32
RET_CHUNK = 256
DEC_PAGES_PER_STEP = 16
MASK_VALUE = -1e30

F32 = jnp.float32
BF16 = jnp.bfloat16


def _params(*sem):
    return pltpu.CompilerParams(dimension_semantics=sem, vmem_limit_bytes=VMEM_LIMIT_BYTES)


def _const_spec(shape):
    n = len(shape)
    return pl.BlockSpec(shape, lambda *_: (0,) * n, pipeline_mode=pl.Buffered(1))


def _rms(x, g):
    return x * lax.rsqrt(jnp.mean(x * x, axis=-1, keepdims=True) + NORM_EPS) * g


def _sigmoid(x):
    return 1.0 / (1.0 + jnp.exp(-x))


def _dot(a, b):
    return jnp.dot(a, b, preferred_element_type=F32)


def _dot_nt(a, b):
    return lax.dot_general(a, b, (((1,), (1,)), ((), ())), preferred_element_type=F32)


def _ffn_body(x_ref, g_ref, wg_ref, wu_ref, wd_ref, *rest, final):
    o_ref = rest[-1]
    x = x_ref[...]
    u = _rms(x, g_ref[...]).astype(BF16)

    def chunk(c, acc):
        gate = _dot(u, wg_ref[c])
        up = _dot(u, wu_ref[c])
        act = (gate * _sigmoid(gate) * up).astype(BF16)
        return acc + _dot(act, wd_ref[c])

    acc = lax.fori_loop(0, wg_ref.shape[0], chunk, jnp.zeros(x.shape, F32))
    y = x + 0.5 * acc
    if final:
        y = _rms(y, rest[0][...])
    o_ref[...] = y


def _ffn(x, norm, wg, wu, wd, final_norm=None):
    m, d = x.shape
    tm = min(m, TOKEN_TILE)
    assert m % tm == 0
    nc, _, fc = wg.shape
    final = final_norm is not None
    in_specs = [
        pl.BlockSpec((tm, d), lambda i: (i, 0)),
        _const_spec((1, d)),
        _const_spec((nc, d, fc)),
        _const_spec((nc, d, fc)),
        _const_spec((nc, fc, d)),
    ]
    args = [x, norm, wg, wu, wd]
    if final:
        in_specs.append(_const_spec((1, d)))
        args.append(final_norm)
    return pl.pallas_call(
        functools.partial(_ffn_body, final=final),
        grid=(m // tm,),
        in_specs=in_specs,
        out_specs=pl.BlockSpec((tm, d), lambda i: (i, 0)),
        out_shape=jax.ShapeDtypeStruct((m, d), F32),
        compiler_params=_params("parallel"),
        name="ffn_final" if final else "ffn",
    )(*args)


def _rope32(x, cos, sin):
    lane = lax.broadcasted_iota(jnp.int32, x.shape, 1)
    width = x.shape[1]
    partner = jnp.where((lane & 31) < 16, pltpu.roll(x, width - 16, axis=1), pltpu.roll(x, 16, axis=1))
    return x * cos + partner * sin


def _mix_in_body(h_ref, g_ref, wq_ref, wkv_ref, wkr_ref, wrq_ref, wrk_ref, wrv_ref, wrg_ref, wga_ref, wgb_ref,
                 qn_ref, kvn_ref, wuqn_ref, wuqr_ref, wuk_ref, rope_r_ref, rope_q_ref,
                 qlat_ref, qrope_ref, ckv_ref, kr_ref, kcat_ref, kt_ref, rq_ref, rk_ref, rv_ref, srg_ref, sga_ref,
                 sgb_ref):
    u = _rms(h_ref[...], g_ref[...]).astype(BF16)
    cos_q, sin_q = rope_q_ref[0], rope_q_ref[1]
    cos_r, sin_r = rope_r_ref[0], rope_r_ref[1]

    cq = _rms(_dot(u, wq_ref[...]), qn_ref[...]).astype(BF16)
    q_nope = (_dot(cq, wuqn_ref[...]) * MLA_SCALE).astype(BF16)
    for p in range(MLA_HEADS // 2):
        pair = _dot(q_nope[:, p * 128:(p + 1) * 128], wuk_ref[p])
        qlat_ref[2 * p] = pair[:, :KV_RANK].astype(BF16)
        qlat_ref[2 * p + 1] = pair[:, KV_RANK:].astype(BF16)
    q_rope = _rope32(_dot(cq, wuqr_ref[...]), cos_q, sin_q) * MLA_SCALE
    qrope_ref[...] = q_rope.astype(BF16)

    ckv = _rms(_dot(u, wkv_ref[...]), kvn_ref[...])
    ckv_ref[...] = ckv
    kcat_ref[:, :KV_RANK] = ckv.astype(BF16)
    kr = _rope32(_dot(u, wkr_ref[...]), cos_q, sin_q)
    kr_ref[...] = kr[:, :MLA_ROPE]
    kcat_ref[:, KV_RANK:] = kr.astype(BF16)
    k_t = jnp.concatenate([ckv, kr], axis=1).T.astype(BF16)
    kb = kt_ref.shape[2]
    for blk in range(kt_ref.shape[0]):
        kt_ref[blk] = k_t[:, blk * kb:(blk + 1) * kb]

    rq = _dot(u, wrq_ref[...])
    rk = _dot(u, wrk_ref[...])
    for hd in range(RET_HEADS):
        sl = slice(hd * RET_DK, (hd + 1) * RET_DK)
        xq, xk = rq[:, sl], rk[:, sl]
        rq_ref[:, sl] = (xq * cos_r + pltpu.roll(xq, RET_DK // 2, axis=1) * sin_r).astype(BF16)
        rk_ref[:, sl] = ((xk * cos_r + pltpu.roll(xk, RET_DK // 2, axis=1) * sin_r) * (RET_DK ** -0.5)).astype(BF16)
    rv_ref[...] = _dot(u, wrv_ref[...]).astype(BF16)

    rg = _dot(u, wrg_ref[...])
    srg_ref[...] = (rg * _sigmoid(rg)).astype(BF16)
    sga_ref[...] = _sigmoid(_dot(u, wga_ref[...])).astype(BF16)
    sgb_ref[...] = _sigmoid(_dot(u, wgb_ref[...])).astype(BF16)


def _mix_in(h, W, rope_r, rope_q):
    b, t, d = h.shape
    tm = min(t, TOKEN_TILE)
    assert t % tm == 0
    hr = RET_HEADS * RET_DK
    hv = RET_HEADS * RET_DV
    qw = MLA_HEADS * MLA_ROPE
    weights = [W["mix_norm"], W["w_q"], W["w_kv"], W["w_kr"], W["w_rq"], W["w_rk"], W["w_rv"], W["w_rg"],
               W["w_ga"], W["w_gb"], W["q_norm"], W["kv_norm"], W["w_uq_nope"], W["w_uq_rope"], W["w_uk_pair"]]
    tok = lambda width: pl.BlockSpec((None, tm, width), lambda bi, ti: (bi, ti, 0))
    in_specs = ([tok(d)] + [_const_spec(w.shape) for w in weights]
                + [pl.BlockSpec((2, tm, RET_DK), lambda bi, ti: (0, ti, 0)),
                   pl.BlockSpec((2, tm, qw), lambda bi, ti: (0, ti, 0))])
    kb = min(t, ATTN_BLOCK)
    assert tm % kb == 0
    out_specs = [
        pl.BlockSpec((None, MLA_HEADS, tm, KV_RANK), lambda bi, ti: (bi, 0, ti, 0)),
        tok(qw), tok(KV_RANK), tok(MLA_ROPE), tok(2 * KV_RANK),
        pl.BlockSpec((None, tm // kb, 2 * KV_RANK, kb), lambda bi, ti: (bi, ti, 0, 0)),
        tok(hr), tok(hr), tok(hv), tok(hv), tok(d), tok(d),
    ]
    sds = jax.ShapeDtypeStruct
    out_shape = [
        sds((b, MLA_HEADS, t, KV_RANK), BF16), sds((b, t, qw), BF16), sds((b, t, KV_RANK), F32),
        sds((b, t, MLA_ROPE), F32), sds((b, t, 2 * KV_RANK), BF16), sds((b, t // kb, 2 * KV_RANK, kb), BF16),
        sds((b, t, hr), BF16), sds((b, t, hr), BF16),
        sds((b, t, hv), BF16), sds((b, t, hv), BF16), sds((b, t, d), BF16), sds((b, t, d), BF16),
    ]
    names = ("qlat", "qrope", "ckv", "kr", "kcat", "kt", "rq", "rk", "rv", "srg", "sga", "sgb")
    outs = pl.pallas_call(
        _mix_in_body,
        grid=(b, t // tm),
        in_specs=in_specs,
        out_specs=out_specs,
        out_shape=out_shape,
        compiler_params=_params("parallel", "parallel"),
        name="mixer_inputs",
    )(h, *weights, rope_r, rope_q)
    return dict(zip(names, outs))


def _softmax_step(s, k_lat, m_ref, l_ref, acc_ref):
    m_prev = m_ref[...]
    m_new = jnp.maximum(m_prev, jnp.max(s, axis=-1, keepdims=True))
    alpha = jnp.exp(m_prev - m_new)
    p = jnp.exp(s - m_new)
    l_ref[...] = alpha * l_ref[...] + jnp.sum(p, axis=-1, keepdims=True)
    acc_ref[...] = alpha * acc_ref[...] + _dot(p.astype(BF16), k_lat)
    m_ref[...] = m_new


def _lanes(x, width):
    return x if width == x.shape[1] else jnp.concatenate([x] * (width // x.shape[1]), axis=1)


def _attn_body(qlat_ref, qrope_ref, kt_ref, v_ref, kmt_ref, vm_ref, wuv_ref, o_ref,
               qcat_ref, s_ref, p_ref, m_ref, l_ref, alpha_ref, acc_ref):
    i = pl.program_id(1)
    blk = qrope_ref.shape[0]
    grp = ATTN_GROUP_HEADS * blk
    n_grp = MLA_HEADS // ATTN_GROUP_HEADS

    lane_head = lax.broadcasted_iota(jnp.int32, (blk, MLA_HEADS * MLA_ROPE), 1) // MLA_ROPE
    qr = qrope_ref[...]
    for hd in range(MLA_HEADS):
        qcat_ref[hd * blk:(hd + 1) * blk, :KV_RANK] = qlat_ref[hd]
        qcat_ref[hd * blk:(hd + 1) * blk, KV_RANK:] = jnp.where(lane_head == hd, qr, jnp.zeros_like(qr))

    for g in range(n_grp):
        rows = slice(g * grp, (g + 1) * grp)
        s = _dot(qcat_ref[rows, :], kmt_ref[...])
        m0 = jnp.max(s, axis=-1, keepdims=True)
        p = jnp.exp(s - m0)
        m_ref[rows, :] = jnp.broadcast_to(m0, (grp, 128))
        l_ref[rows, :] = jnp.broadcast_to(jnp.sum(p, axis=-1, keepdims=True), (grp, 128))
        acc_ref[rows, :] = _dot(p.astype(BF16), vm_ref[...])

    def block(j, masked):
        k_t = kt_ref[j]
        v = v_ref[pl.ds(pl.multiple_of(j * blk, blk), blk), :]
        if masked:
            tok = lax.broadcasted_iota(jnp.int32, (ATTN_ROW_CHUNK, blk), 0)
            key = lax.broadcasted_iota(jnp.int32, (ATTN_ROW_CHUNK, blk), 1)
        for g in range(n_grp):
            rows = slice(g * grp, (g + 1) * grp)
            s_ref[rows, :] = _dot(qcat_ref[rows, :], k_t)
        for g in range(n_grp):
            for c in range(g * grp, (g + 1) * grp, ATTN_ROW_CHUNK):
                r = slice(c, c + ATTN_ROW_CHUNK)
                s = s_ref[r, :]
                if masked:
                    s = jnp.where(key <= tok + (c % blk), s, MASK_VALUE)
                m_prev = m_ref[r, :]
                m_new = jnp.maximum(m_prev, jnp.max(s, axis=-1, keepdims=True))
                alpha = jnp.exp(m_prev - m_new)
                p = jnp.exp(s - _lanes(m_new, blk))
                l_ref[r, :] = alpha * l_ref[r, :] + jnp.sum(p, axis=-1, keepdims=True)
                m_ref[r, :] = m_new
                alpha_ref[r, :] = alpha
                p_ref[r, :] = p.astype(BF16)
            rows = slice(g * grp, (g + 1) * grp)
            acc_ref[rows, :] = _lanes(alpha_ref[rows, :], KV_RANK) * acc_ref[rows, :] + _dot(p_ref[rows, :], v)

    def full_block(j, carry):
        block(j, False)
        return carry

    lax.fori_loop(0, i, full_block, 0)
    block(i, True)

    for pr in range(MLA_HEADS // 2):
        halves = []
        for hd in (2 * pr, 2 * pr + 1):
            rows = slice(hd * blk, (hd + 1) * blk)
            halves.append((acc_ref[rows, :] / _lanes(l_ref[rows, :], KV_RANK)).astype(BF16))
        o_ref[:, pr * 2 * MLA_V:(pr + 1) * 2 * MLA_V] = _dot(jnp.concatenate(halves, axis=1), wuv_ref[pr]).astype(BF16)


def _prompt_attention(qlat, qrope, kt, kcat, kmeta_t, vmeta, wuv_pair):
    b, nh, t, r = qlat.shape
    blk = kt.shape[3]
    assert t % blk == 0 and blk & (blk - 1) == 0 and blk % 128 == 0
    rows = nh * blk
    return pl.pallas_call(
        _attn_body,
        grid=(b, t // blk),
        in_specs=[
            pl.BlockSpec((None, nh, blk, r), lambda bi, i: (bi, 0, i, 0)),
            pl.BlockSpec((None, blk, qrope.shape[2]), lambda bi, i: (bi, i, 0)),
            pl.BlockSpec((None,) + kt.shape[1:], lambda bi, i: (bi, 0, 0, 0)),
            pl.BlockSpec((None, t, r), lambda bi, i: (bi, 0, 0)),
            _const_spec(kmeta_t.shape),
            _const_spec(vmeta.shape),
            _const_spec(wuv_pair.shape),
        ],
        out_specs=pl.BlockSpec((None, blk, nh * MLA_V), lambda bi, i: (bi, i, 0)),
        out_shape=jax.ShapeDtypeStruct((b, t, nh * MLA_V), BF16),
        scratch_shapes=[
            pltpu.VMEM((rows, 2 * r), BF16),
            pltpu.VMEM((rows, blk), F32),
            pltpu.VMEM((rows, blk), BF16),
            pltpu.VMEM((rows, 128), F32),
            pltpu.VMEM((rows, 128), F32),
            pltpu.VMEM((rows, 128), F32),
            pltpu.VMEM((rows, r), F32),
        ],
        compiler_params=_params("parallel", "arbitrary"),
        name="prompt_attention",
    )(qlat, qrope, kt, kcat, kmeta_t, vmeta, wuv_pair)


def _group_norm(o):
    mu = jnp.mean(o, axis=-1, keepdims=True)
    c = o - mu
    return c * lax.rsqrt(jnp.mean(c * c, axis=-1, keepdims=True) + NORM_EPS)


def _ret_body(rq_ref, rk_ref, rv_ref, kmeta_ref, vmeta_ref, on_ref, sout_ref, s_ref):
    c = pl.program_id(1)
    ch = rq_ref.shape[0]
    n_meta = kmeta_ref.shape[0]

    @pl.when(c == 0)
    def _():
        j = lax.broadcasted_iota(jnp.int32, (n_meta, 1), 0).astype(F32)
        for hd in range(RET_HEADS):
            kd = kmeta_ref[:, hd * RET_DK:(hd + 1) * RET_DK].astype(F32) * jnp.exp(RET_LOG_GAMMA[hd] * (n_meta - 1.0 - j))
            s_ref[hd] = _dot(kd.T.astype(BF16), vmeta_ref[:, hd * RET_DV:(hd + 1) * RET_DV])

    row = lax.broadcasted_iota(jnp.int32, (ch, 1), 0).astype(F32)
    diff = (lax.broadcasted_iota(jnp.int32, (ch, ch), 0) - lax.broadcasted_iota(jnp.int32, (ch, ch), 1)).astype(F32)
    for hd in range(RET_HEADS):
        lg = RET_LOG_GAMMA[hd]
        q = rq_ref[:, hd * RET_DK:(hd + 1) * RET_DK]
        k = rk_ref[:, hd * RET_DK:(hd + 1) * RET_DK]
        v = rv_ref[:, hd * RET_DV:(hd + 1) * RET_DV]
        decay = jnp.where(diff >= 0, jnp.exp(lg * jnp.maximum(diff, 0.0)), 0.0)
        scores = _dot_nt(q, k) * decay
        s_old = s_ref[hd]
        o = _dot(scores.astype(BF16), v) + _dot(q, s_old.astype(BF16)) * jnp.exp(lg * (row + 1.0))
        kd = k.astype(F32) * jnp.exp(lg * (ch - 1.0 - row))
        s_ref[hd] = math.exp(lg * ch) * s_old + _dot(kd.T.astype(BF16), v)
        on_ref[:, hd * RET_DV:(hd + 1) * RET_DV] = _group_norm(o).astype(BF16)

    @pl.when(c == pl.num_programs(1) - 1)
    def _():
        sout_ref[...] = s_ref[...]


def _prompt_retention(rq, rk, rv, kmeta, vmeta):
    b, t, hr = rq.shape
    hv = rv.shape[2]
    ch = min(t, RET_CHUNK)
    assert t % ch == 0
    tok = lambda width: pl.BlockSpec((None, ch, width), lambda bi, ci: (bi, ci, 0))
    return pl.pallas_call(
        _ret_body,
        grid=(b, t // ch),
        in_specs=[tok(hr), tok(hr), tok(hv), _const_spec(kmeta.shape), _const_spec(vmeta.shape)],
        out_specs=[tok(hv), pl.BlockSpec((None, RET_HEADS, RET_DK, RET_DV), lambda bi, ci: (bi, 0, 0, 0))],
        out_shape=[jax.ShapeDtypeStruct((b, t, hv), BF16),
                   jax.ShapeDtypeStruct((b, RET_HEADS, RET_DK, RET_DV), F32)],
        scratch_shapes=[pltpu.VMEM((RET_HEADS, RET_DK, RET_DV), F32)],
        compiler_params=_params("parallel", "arbitrary"),
        name="prompt_retention",
    )(rq, rk, rv, kmeta, vmeta)


def _dec_attn_body(pt_ref, qlat_ref, qrope_ref, kself_ref, wuv_ref, *rest, pages):
    ckv_refs = rest[:pages]
    kr_refs = rest[pages:2 * pages]
    o_ref, m_ref, l_ref, acc_ref = rest[2 * pages:]
    c = pl.program_id(1)
    ql = qlat_ref[...]
    qr = qrope_ref[...]

    @pl.when(c == 0)
    def _():
        ks = kself_ref[...].astype(F32)
        qf = jnp.concatenate([ql, qr], axis=1).astype(F32)
        m_ref[...] = jnp.sum(qf * ks, axis=-1, keepdims=True)
        l_ref[...] = jnp.ones(l_ref.shape, F32)
        acc_ref[...] = jnp.broadcast_to(ks[:, :KV_RANK], acc_ref.shape)

    k_lat = jnp.concatenate([r[...].astype(BF16) for r in ckv_refs], axis=0)
    k_rope_t = jnp.concatenate([r[...].astype(BF16) for r in kr_refs], axis=1)
    _softmax_step(_dot_nt(ql, k_lat) + _dot(qr, k_rope_t), k_lat, m_ref, l_ref, acc_ref)

    @pl.when(c == pl.num_programs(1) - 1)
    def _():
        o = (acc_ref[...] / l_ref[...]).astype(BF16)
        z = _dot(o, wuv_ref[...])
        head = lax.broadcasted_iota(jnp.int32, z.shape, 0)
        col_head = lax.broadcasted_iota(jnp.int32, z.shape, 1) // MLA_V
        o_ref[...] = jnp.sum(jnp.where(head == col_head, z, 0.0), axis=0, keepdims=True).astype(BF16)


def _decode_attention(page_table, qlat, qrope, kself, wuv_all, cache_ckv, cache_kr_t):
    bd, nh, r = qlat.shape
    n_pages = page_table.shape[1]
    pages = min(n_pages, DEC_PAGES_PER_STEP)
    assert n_pages % pages == 0

    def page_spec(shape, k):
        return pl.BlockSpec((None,) + shape, lambda b, c, pt: (pt[b, c * pages + k], 0, 0))

    per_req = lambda shape: pl.BlockSpec((None,) + shape, lambda b, c, pt: (b, 0, 0))
    grid_spec = pltpu.PrefetchScalarGridSpec(
        num_scalar_prefetch=1,
        grid=(bd, n_pages // pages),
        in_specs=([per_req((nh, r)), per_req((nh, MLA_ROPE)), per_req((1, r + MLA_ROPE)),
                   pl.BlockSpec(wuv_all.shape, lambda b, c, pt: (0, 0))]
                  + [page_spec((PAGE_SIZE, r), k) for k in range(pages)]
                  + [page_spec((MLA_ROPE, PAGE_SIZE), k) for k in range(pages)]),
        out_specs=per_req((1, nh * MLA_V)),
        scratch_shapes=[pltpu.VMEM((nh, 1), F32), pltpu.VMEM((nh, 1), F32), pltpu.VMEM((nh, r), F32)],
    )
    return pl.pallas_call(
        functools.partial(_dec_attn_body, pages=pages),
        grid_spec=grid_spec,
        out_shape=jax.ShapeDtypeStruct((bd, 1, nh * MLA_V), BF16),
        compiler_params=_params("parallel", "arbitrary"),
        name="decode_attention",
    )(page_table, qlat, qrope, kself, wuv_all, *([cache_ckv] * pages), *([cache_kr_t] * pages))


def _dec_ret_body(rq_ref, rk_ref, rv_ref, s_ref, on_ref, snew_ref):
    eye = (lax.broadcasted_iota(jnp.int32, (RET_DK, RET_DK), 0)
           == lax.broadcasted_iota(jnp.int32, (RET_DK, RET_DK), 1))

    def column(x):
        return jnp.sum(jnp.where(eye, jnp.broadcast_to(x, (RET_DK, RET_DK)), 0.0), axis=1, keepdims=True)

    for hd in range(RET_HEADS):
        gamma = math.exp(RET_LOG_GAMMA[hd])
        q = rq_ref[:, hd * RET_DK:(hd + 1) * RET_DK].astype(F32)
        k = rk_ref[:, hd * RET_DK:(hd + 1) * RET_DK].astype(F32)
        v = rv_ref[:, hd * RET_DV:(hd + 1) * RET_DV].astype(F32)
        s_old = s_ref[hd]
        qs = jnp.sum(column(q) * s_old, axis=0, keepdims=True)
        o = jnp.sum(q * k, axis=-1, keepdims=True) * v + qs * gamma
        snew_ref[hd] = gamma * s_old + column(k) * v
        on_ref[:, hd * RET_DV:(hd + 1) * RET_DV] = _group_norm(o).astype(BF16)


def _decode_retention(rq, rk, rv, state):
    bd = rq.shape[0]
    row = lambda width: pl.BlockSpec((None, 1, width), lambda b: (b, 0, 0))
    st = pl.BlockSpec((None, RET_HEADS, RET_DK, RET_DV), lambda b: (b, 0, 0, 0))
    return pl.pallas_call(
        _dec_ret_body,
        grid=(bd,),
        in_specs=[row(rq.shape[2]), row(rk.shape[2]), row(rv.shape[2]), st],
        out_specs=[row(rv.shape[2]), st],
        out_shape=[jax.ShapeDtypeStruct((bd, 1, rv.shape[2]), BF16), jax.ShapeDtypeStruct(state.shape, F32)],
        compiler_params=_params("parallel"),
        name="decode_retention",
    )(rq, rk, rv, state)


def _mix_out_body(h_ref, a_ref, on_ref, srg_ref, sga_ref, sgb_ref, gn_ref, wa_ref, wr_ref, wo_ref, o_ref):
    a = _dot(a_ref[...], wa_ref[...])
    gated = srg_ref[...].astype(F32) * (on_ref[...].astype(F32) * gn_ref[...])
    r = _dot(gated.astype(BF16), wr_ref[...])
    m = sga_ref[...].astype(F32) * a + sgb_ref[...].astype(F32) * r
    o_ref[...] = h_ref[...] + _dot(m.astype(BF16), wo_ref[...])


def _mix_out(h, apre, on, srg, sga, sgb, W):
    m, d = h.shape
    tm = min(m, TOKEN_TILE)
    assert m % tm == 0
    tok = lambda width: pl.BlockSpec((tm, width), lambda i: (i, 0))
    weights = [W["ret_gn"], W["w_mla_o"], W["w_ret_o"], W["w_out"]]
    return pl.pallas_call(
        _mix_out_body,
        grid=(m // tm,),
        in_specs=[tok(d), tok(apre.shape[1]), tok(on.shape[1]), tok(d), tok(d), tok(d)]
                 + [_const_spec(w.shape) for w in weights],
        out_specs=tok(d),
        out_shape=jax.ShapeDtypeStruct((m, d), F32),
        compiler_params=_params("parallel"),
        name="mixer_output",
    )(h, apre, on, srg, sga, sgb, *weights)


def _rope_tables(pos):
    pos = pos.astype(F32)[:, None]

    def table(width, reps):
        half = width // 2
        inv_freq = ROPE_BASE ** (-jnp.arange(half, dtype=F32) / half)
        ang = pos * inv_freq[None, :]
        cos, sin = jnp.cos(ang), jnp.sin(ang)
        return jnp.stack([jnp.tile(jnp.concatenate([cos, cos], axis=1), (1, reps)),
                          jnp.tile(jnp.concatenate([-sin, sin], axis=1), (1, reps))])

    return table(RET_DK, 1), table(MLA_ROPE, MLA_HEADS)


def _prep_weights(ffn1_norm, ffn1_gate, ffn1_up, ffn1_down, mix_norm, w_in, q_norm, kv_norm, w_uq, w_uk, w_uv,
                  w_mla_o, ret_gn, w_ret_o, w_out, ffn2_norm, ffn2_gate, ffn2_up, ffn2_down, final_norm):
    d = w_in.shape[0]
    hr, hv = RET_HEADS * RET_DK, RET_HEADS * RET_DV
    nh, hd = MLA_HEADS, MLA_NOPE + MLA_ROPE

    def ffn_w(gate, up, down):
        f = gate.shape[1]
        nc = f // FFN_CHUNK
        cols = lambda w: w.astype(BF16).reshape(d, nc, FFN_CHUNK).transpose(1, 0, 2)
        return cols(gate), cols(up), down.astype(BF16).reshape(nc, FFN_CHUNK, d)

    bounds = [0]
    for width in (Q_RANK, KV_RANK, MLA_ROPE, hr, hr, hv, hv, d, d):
        bounds.append(bounds[-1] + width)
    assert bounds[-1] == w_in.shape[1]
    w_q, w_kv, w_kr, w_rq, w_rk, w_rv, w_rg, w_ga, w_gb = (
        w_in[:, bounds[i]:bounds[i + 1]].astype(BF16) for i in range(9))

    w_uq3 = w_uq.reshape(Q_RANK, nh, hd)
    zeros = jnp.zeros((MLA_NOPE, KV_RANK), F32)
    w_uk_t = w_uk.transpose(1, 2, 0)
    w_uk_pair = jnp.stack([jnp.block([[w_uk_t[2 * p], zeros], [zeros, w_uk_t[2 * p + 1]]])
                           for p in range(nh // 2)]).astype(BF16)
    w_uv_t = w_uv.transpose(1, 0, 2)
    zv = jnp.zeros((KV_RANK, MLA_V), F32)
    w_uv_pair = jnp.stack([jnp.block([[w_uv_t[2 * p], zv], [zv, w_uv_t[2 * p + 1]]])
                           for p in range(nh // 2)]).astype(BF16)
    row = lambda v: v.reshape(1, -1)
    return dict(
        ffn1_norm=row(ffn1_norm), ffn1=ffn_w(ffn1_gate, ffn1_up, ffn1_down),
        ffn2_norm=row(ffn2_norm), ffn2=ffn_w(ffn2_gate, ffn2_up, ffn2_down), final_norm=row(final_norm),
        mix_norm=row(mix_norm), w_q=w_q, w_kv=w_kv, w_kr=jnp.tile(w_kr, (1, nh)), w_rq=w_rq, w_rk=w_rk, w_rv=w_rv,
        w_rg=w_rg, w_ga=w_ga, w_gb=w_gb, q_norm=row(q_norm), kv_norm=row(kv_norm),
        w_uq_nope=w_uq3[:, :, :MLA_NOPE].reshape(Q_RANK, nh * MLA_NOPE).astype(BF16),
        w_uq_rope=w_uq3[:, :, MLA_NOPE:].reshape(Q_RANK, nh * MLA_ROPE).astype(BF16),
        w_uk_pair=w_uk_pair, w_uv_pair=w_uv_pair, w_uv_all=w_uv.reshape(KV_RANK, nh * MLA_V).astype(BF16),
        ret_gn=row(ret_gn), w_mla_o=w_mla_o.astype(BF16), w_ret_o=w_ret_o.astype(BF16), w_out=w_out.astype(BF16),
    )


def kernel(x_prompt, x_sample, cache_ckv, cache_krope, state_ret, page_table, meta_tokens, ffn1_norm, ffn1_gate, ffn1_up, ffn1_down, mix_norm, w_in, q_norm, kv_norm, w_uq, w_uk, w_uv, w_mla_o, ret_gn, w_ret_o, w_out, ffn2_norm, ffn2_gate, ffn2_up, ffn2_down, final_norm):
    assert ffn1_gate.shape[0] == 1, "single-layer trunk"
    b, t, d = x_prompt.shape
    bd, dec_seq, _ = x_sample.shape
    assert dec_seq == 1
    n_meta = meta_tokens.shape[0]
    past_len = page_table.shape[1] * PAGE_SIZE
    W = _prep_weights(ffn1_norm[0], ffn1_gate[0], ffn1_up[0], ffn1_down[0], mix_norm[0], w_in[0], q_norm[0],
                      kv_norm[0], w_uq[0], w_uk[0], w_uv[0], w_mla_o[0], ret_gn[0], w_ret_o[0], w_out[0],
                      ffn2_norm[0], ffn2_gate[0], ffn2_up[0], ffn2_down[0], final_norm)

    rope_r, rope_q = _rope_tables(n_meta + jnp.arange(t))
    pos_small = jnp.concatenate([jnp.arange(n_meta), jnp.full((bd,), past_len)])
    rope_r_s, rope_q_s = _rope_tables(pos_small)
    x_small = jnp.concatenate([meta_tokens.astype(F32), x_sample.reshape(bd, d)], axis=0)

    h1 = _ffn(x_prompt.reshape(b * t, d), W["ffn1_norm"], *W["ffn1"])
    h1s = _ffn(x_small, W["ffn1_norm"], *W["ffn1"])
    P = _mix_in(h1.reshape(b, t, d), W, rope_r, rope_q)
    S = _mix_in(h1s.reshape(1, n_meta + bd, d), W, rope_r_s, rope_q_s)

    apre = _prompt_attention(P["qlat"], P["qrope"], P["kt"], P["kcat"], S["kt"][0, 0, :, :n_meta],
                             S["kcat"][0, :n_meta, :KV_RANK], W["w_uv_pair"])
    on, s_prompt = _prompt_retention(P["rq"], P["rk"], P["rv"], S["rk"][0, :n_meta], S["rv"][0, :n_meta])

    nh = MLA_HEADS
    qlat_s = S["qlat"][0, :, n_meta:].transpose(1, 0, 2)
    qrope_s = S["qrope"][0, n_meta:].reshape(bd, nh, MLA_ROPE)
    kself = S["kcat"][0, n_meta:, :KV_RANK + MLA_ROPE].reshape(bd, 1, KV_RANK + MLA_ROPE)
    apre_s = _decode_attention(page_table, qlat_s, qrope_s, kself, W["w_uv_all"], cache_ckv[0],
                               cache_krope[0].transpose(0, 2, 1))
    dec = lambda name: S[name][0, n_meta:].reshape(bd, 1, -1)
    on_s, s_sample = _decode_retention(dec("rq"), dec("rk"), dec("rv"), state_ret[0])

    flat = lambda a: a.reshape(b * t, -1)
    h2 = _mix_out(h1, flat(apre), flat(on), flat(P["srg"]), flat(P["sga"]), flat(P["sgb"]), W)
    y_prompt = _ffn(h2, W["ffn2_norm"], *W["ffn2"], final_norm=W["final_norm"])
    tail = lambda name: S[name][0, n_meta:]
    h2s = _mix_out(h1s[n_meta:], apre_s.reshape(bd, -1), on_s.reshape(bd, -1), tail("srg"), tail("sga"), tail("sgb"), W)
    y_sample = _ffn(h2s, W["ffn2_norm"], *W["ffn2"], final_norm=W["final_norm"])

    with_meta = lambda small, main: jnp.concatenate(
        [jnp.broadcast_to(small[:, :n_meta], (b, n_meta, small.shape[2])), main], axis=1)[None]
    return (
        y_prompt.reshape(b, t, d),
        y_sample.reshape(bd, 1, d),
        with_meta(S["ckv"], P["ckv"]),
        with_meta(S["kr"], P["kr"]),
        s_prompt[None],
        S["ckv"][0, n_meta:].reshape(1, bd, 1, KV_RANK),
        S["kr"][0, n_meta:].reshape(1, bd, 1, MLA_ROPE),
        s_sample[None],
    )
```

```python
import functools
import math

import jax
import jax.numpy as jnp
from jax import lax
from jax.experimental import pallas as pl
from jax.experimental.pallas import tpu as pltpu

N_META = 16
PAGE_SIZE = 128
MLA_HEADS = 8
MLA_NOPE = 64
MLA_ROPE = 32
MLA_V = 64
Q_RANK = 384
KV_RANK = 256
MLA_SCALE = (MLA_NOPE + MLA_ROPE) ** -0.5
RET_HEADS = 4
RET_DK = 128
RET_DV = 256
ROPE_BASE = 10000.0
NORM_EPS = 1e-6
RET_LOG_GAMMA = tuple(math.log1p(-(2.0 ** (-5.0 - h))) for h in range(RET_HEADS))

VMEM_LIMIT_BYTES = 56 * 1024 * 1024
FFN_CHUNK = 256
TOKEN_TILE = 512
ATTN_BLOCK = 256
ATTN_GROUP_HEADS = 2
ATTN_ROW_CHUNK = 32
RET_CHUNK = 256
DEC_PAGES_PER_STEP = 16
MASK_VALUE = -1e30

F32 = jnp.float32
BF16 = jnp.bfloat16


def _params(*sem):
    return pltpu.CompilerParams(dimension_semantics=sem, vmem_limit_bytes=VMEM_LIMIT_BYTES)


def _const_spec(shape):
    n = len(shape)
    return pl.BlockSpec(shape, lambda *_: (0,) * n, pipeline_mode=pl.Buffered(1))


def _rms(x, g):
    return x * lax.rsqrt(jnp.mean(x * x, axis=-1, keepdims=True) + NORM_EPS) * g


def _sigmoid(x):
    return 1.0 / (1.0 + jnp.exp(-x))


def _dot(a, b):
    return jnp.dot(a, b, preferred_element_type=F32)


def _dot_nt(a, b):
    return lax.dot_general(a, b, (((1,), (1,)), ((), ())), preferred_element_type=F32)


def _ffn_body(x_ref, g_ref, wg_ref, wu_ref, wd_ref, *rest, final):
    o_ref = rest[-1]
    x = x_ref[...]
    u = _rms(x, g_ref[...]).astype(BF16)

    def chunk(c, acc):
        gate = _dot(u, wg_ref[c])
        up = _dot(u, wu_ref[c])
        act = (gate * _sigmoid(gate) * up).astype(BF16)
        return acc + _dot(act, wd_ref[c])

    acc = lax.fori_loop(0, wg_ref.shape[0], chunk, jnp.zeros(x.shape, F32), unroll=True)
    y = x + 0.5 * acc
    if final:
        y = _rms(y, rest[0][...])
    o_ref[...] = y


def _ffn(x, norm, wg, wu, wd, final_norm=None):
    m, d = x.shape
    tm = min(m, TOKEN_TILE)
    assert m % tm == 0
    nc, _, fc = wg.shape
    final = final_norm is not None
    in_specs = [
        pl.BlockSpec((tm, d), lambda i: (i, 0)),
        _const_spec((1, d)),
        _const_spec((nc, d, fc)),
        _const_spec((nc, d, fc)),
        _const_spec((nc, fc, d)),
    ]
    args = [x, norm, wg, wu, wd]
    if final:
        in_specs.append(_const_spec((1, d)))
        args.append(final_norm)
    return pl.pallas_call(
        functools.partial(_ffn_body, final=final),
        grid=(m // tm,),
        in_specs=in_specs,
        out_specs=pl.BlockSpec((tm, d), lambda i: (i, 0)),
        out_shape=jax.ShapeDtypeStruct((m, d), F32),
        compiler_params=_params("parallel"),
        name="ffn_final" if final else "ffn",
    )(*args)


def _rope32(x, cos, sin):
    lane = lax.broadcasted_iota(jnp.int32, x.shape, 1)
    width = x.shape[1]
    partner = jnp.where((lane & 31) < 16, pltpu.roll(x, width - 16, axis=1), pltpu.roll(x, 16, axis=1))
    return x * cos + partner * sin


def _mix_in_body(h_ref, g_ref, wq_ref, wkv_ref, wkr_ref, wrq_ref, wrk_ref, wrv_ref, wrg_ref, wga_ref, wgb_ref,
                 qn_ref, kvn_ref, wuqn_ref, wuqr_ref, wuk_ref, rope_r_ref, rope_q_ref,
                 qlat_ref, qrope_ref, ckv_ref, kr_ref, kcat_ref, kt_ref, rq_ref, rk_ref, rv_ref, srg_ref, sga_ref,
                 sgb_ref):
    u = _rms(h_ref[...], g_ref[...]).astype(BF16)
    cos_q, sin_q = rope_q_ref[0], rope_q_ref[1]
    cos_r, sin_r = rope_r_ref[0], rope_r_ref[1]

    cq = _rms(_dot(u, wq_ref[...]), qn_ref[...]).astype(BF16)
    q_nope = (_dot(cq, wuqn_ref[...]) * MLA_SCALE).astype(BF16)
    for p in range(MLA_HEADS // 2):
        pair = _dot(q_nope[:, p * 128:(p + 1) * 128], wuk_ref[p])
        qlat_ref[2 * p] = pair[:, :KV_RANK].astype(BF16)
        qlat_ref[2 * p + 1] = pair[:, KV_RANK:].astype(BF16)
    q_rope = _rope32(_dot(cq, wuqr_ref[...]), cos_q, sin_q) * MLA_SCALE
    qrope_ref[...] = q_rope.astype(BF16)

    ckv = _rms(_dot(u, wkv_ref[...]), kvn_ref[...])
    ckv_ref[...] = ckv
    kcat_ref[:, :KV_RANK] = ckv.astype(BF16)
    kr = _rope32(_dot(u, wkr_ref[...]), cos_q, sin_q)
    kr_ref[...] = kr[:, :MLA_ROPE]
    kcat_ref[:, KV_RANK:] = kr.astype(BF16)
    k_t = jnp.concatenate([ckv, kr], axis=1).T.astype(BF16)
    kb = kt_ref.shape[2]
    for blk in range(kt_ref.shape[0]):
        kt_ref[blk] = k_t[:, blk * kb:(blk + 1) * kb]

    rq = _dot(u, wrq_ref[...])
    rk = _dot(u, wrk_ref[...])
    for hd in range(RET_HEADS):
        sl = slice(hd * RET_DK, (hd + 1) * RET_DK)
        xq, xk = rq[:, sl], rk[:, sl]
        rq_ref[:, sl] = (xq * cos_r + pltpu.roll(xq, RET_DK // 2, axis=1) * sin_r).astype(BF16)
        rk_ref[:, sl] = ((xk * cos_r + pltpu.roll(xk, RET_DK // 2, axis=1) * sin_r) * (RET_DK ** -0.5)).astype(BF16)
    rv_ref[...] = _dot(u, wrv_ref[...]).astype(BF16)

    rg = _dot(u, wrg_ref[...])
    srg_ref[...] = (rg * _sigmoid(rg)).astype(BF16)
    sga_ref[...] = _sigmoid(_dot(u, wga_ref[...])).astype(BF16)
    sgb_ref[...] = _sigmoid(_dot(u, wgb_ref[...])).astype(BF16)


def _mix_in(h, W, rope_r, rope_q):
    b, t, d = h.shape
    tm = min(t, TOKEN_TILE)
    assert t % tm == 0
    hr = RET_HEADS * RET_DK
    hv = RET_HEADS * RET_DV
    qw = MLA_HEADS * MLA_ROPE
    weights = [W["mix_norm"], W["w_q"], W["w_kv"], W["w_kr"], W["w_rq"], W["w_rk"], W["w_rv"], W["w_rg"],
               W["w_ga"], W["w_gb"], W["q_norm"], W["kv_norm"], W["w_uq_nope"], W["w_uq_rope"], W["w_uk_pair"]]
    tok = lambda width: pl.BlockSpec((None, tm, width), lambda bi, ti: (bi, ti, 0))
    in_specs = ([tok(d)] + [_const_spec(w.shape) for w in weights]
                + [pl.BlockSpec((2, tm, RET_DK), lambda bi, ti: (0, ti, 0)),
                   pl.BlockSpec((2, tm, qw), lambda bi, ti: (0, ti, 0))])
    kb = min(t, ATTN_BLOCK)
    assert tm % kb == 0
    out_specs = [
        pl.BlockSpec((None, MLA_HEADS, tm, KV_RANK), lambda bi, ti: (bi, 0, ti, 0)),
        tok(qw), tok(KV_RANK), tok(MLA_ROPE), tok(2 * KV_RANK),
        pl.BlockSpec((None, tm // kb, 2 * KV_RANK, kb), lambda bi, ti: (bi, ti, 0, 0)),
        tok(hr), tok(hr), tok(hv), tok(hv), tok(d), tok(d),
    ]
    sds = jax.ShapeDtypeStruct
    out_shape = [
        sds((b, MLA_HEADS, t, KV_RANK), BF16), sds((b, t, qw), BF16), sds((b, t, KV_RANK), F32),
        sds((b, t, MLA_ROPE), F32), sds((b, t, 2 * KV_RANK), BF16), sds((b, t // kb, 2 * KV_RANK, kb), BF16),
        sds((b, t, hr), BF16), sds((b, t, hr), BF16),
        sds((b, t, hv), BF16), sds((b, t, hv), BF16), sds((b, t, d), BF16), sds((b, t, d), BF16),
    ]
    names = ("qlat", "qrope", "ckv", "kr", "kcat", "kt", "rq", "rk", "rv", "srg", "sga", "sgb")
    outs = pl.pallas_call(
        _mix_in_body,
        grid=(b, t // tm),
        in_specs=in_specs,
        out_specs=out_specs,
        out_shape=out_shape,
        compiler_params=_params("parallel", "parallel"),
        name="mixer_inputs",
    )(h, *weights, rope_r, rope_q)
    return dict(zip(names, outs))


def _softmax_step(s, k_lat, m_ref, l_ref, acc_ref):
    m_prev = m_ref[...]
    m_new = jnp.maximum(m_prev, jnp.max(s, axis=-1, keepdims=True))
    alpha = jnp.exp(m_prev - m_new)
    p = jnp.exp(s - m_new)
    l_ref[...] = alpha * l_ref[...] + jnp.sum(p, axis=-1, keepdims=True)
    acc_ref[...] = alpha * acc_ref[...] + _dot(p.astype(BF16), k_lat)
    m_ref[...] = m_new


def _lanes(x, width):
    return x if width == x.shape[1] else jnp.concatenate([x] * (width // x.shape[1]), axis=1)


def _attn_body(qlat_ref, qrope_ref, kt_ref, v_ref, kmt_ref, vm_ref, wuv_ref, o_ref,
               qcat_ref, s_ref, p_ref, m_ref, l_ref, alpha_ref, acc_ref):
    i = pl.program_id(1)
    blk = qrope_ref.shape[0]
    grp = ATTN_GROUP_HEADS * blk
    n_grp = MLA_HEADS // ATTN_GROUP_HEADS

    lane_head = lax.broadcasted_iota(jnp.int32, (blk, MLA_HEADS * MLA_ROPE), 1) // MLA_ROPE
    qr = qrope_ref[...]
    for hd in range(MLA_HEADS):
        qcat_ref[hd * blk:(hd + 1) * blk, :KV_RANK] = qlat_ref[hd]
        qcat_ref[hd * blk:(hd + 1) * blk, KV_RANK:] = jnp.where(lane_head == hd, qr, jnp.zeros_like(qr))

    for g in range(n_grp):
        rows = slice(g * grp, (g + 1) * grp)
        s = _dot(qcat_ref[rows, :], kmt_ref[...])
        m0 = jnp.max(s, axis=-1, keepdims=True)
        p = jnp.exp(s - m0)
        m_ref[rows, :] = jnp.broadcast_to(m0, (grp, 128))
        l_ref[rows, :] = jnp.broadcast_to(jnp.sum(p, axis=-1, keepdims=True), (grp, 128))
        acc_ref[rows, :] = _dot(p.astype(BF16), vm_ref[...])

    def block(j, masked):
        k_t = kt_ref[j]
        v = v_ref[pl.ds(pl.multiple_of(j * blk, blk), blk), :]
        if masked:
            tok = lax.broadcasted_iota(jnp.int32, (ATTN_ROW_CHUNK, blk), 0)
            key = lax.broadcasted_iota(jnp.int32, (ATTN_ROW_CHUNK, blk), 1)
        for g in range(n_grp):
            rows = slice(g * grp, (g + 1) * grp)
            s_ref[rows, :] = _dot(qcat_ref[rows, :], k_t)
        for g in range(n_grp):
            for c in range(g * grp, (g + 1) * grp, ATTN_ROW_CHUNK):
                r = slice(c, c + ATTN_ROW_CHUNK)
                s = s_ref[r, :]
                if masked:
                    s = jnp.where(key <= tok + (c % blk), s, MASK_VALUE)
                m_prev = m_ref[r, :]
                m_new = jnp.maximum(m_prev, jnp.max(s, axis=-1, keepdims=True))
                alpha = jnp.exp(m_prev - m_new)
                p = jnp.exp(s - _lanes(m_new, blk))
                l_ref[r, :] = alpha * l_ref[r, :] + jnp.sum(p, axis=-1, keepdims=True)
                m_ref[r, :] = m_new
                alpha_ref[r, :] = alpha
                p_ref[r, :] = p.astype(BF16)
            rows = slice(g * grp, (g + 1) * grp)
            acc_ref[rows, :] = _lanes(alpha_ref[rows, :], KV_RANK) * acc_ref[rows, :] + _dot(p_ref[rows, :], v)

    def full_block(j, carry):
        block(j, False)
        return carry

    lax.fori_loop(0, i, full_block, 0)
    block(i, True)

    for pr in range(MLA_HEADS // 2):
        halves = []
        for hd in (2 * pr, 2 * pr + 1):
            rows = slice(hd * blk, (hd + 1) * blk)
            halves.append((acc_ref[rows, :] / _lanes(l_ref[rows, :], KV_RANK)).astype(BF16))
        o_ref[:, pr * 2 * MLA_V:(pr + 1) * 2 * MLA_V] = _dot(jnp.concatenate(halves, axis=1), wuv_ref[pr]).astype(BF16)


def _prompt_attention(qlat, qrope, kt, kcat, kmeta_t, vmeta, wuv_pair):
    b, nh, t, r = qlat.shape
    blk = kt.shape[3]
    assert t % blk == 0 and blk & (blk - 1) == 0 and blk % 128 == 0
    rows = nh * blk
    return pl.pallas_call(
        _attn_body,
        grid=(b, t // blk),
        in_specs=[
            pl.BlockSpec((None, nh, blk, r), lambda bi, i: (bi, 0, i, 0)),
            pl.BlockSpec((None, blk, qrope.shape[2]), lambda bi, i: (bi, i, 0)),
            pl.BlockSpec((None,) + kt.shape[1:], lambda bi, i: (bi, 0, 0, 0)),
            pl.BlockSpec((None, t, r), lambda bi, i: (bi, 0, 0)),
            _const_spec(kmeta_t.shape),
            _const_spec(vmeta.shape),
            _const_spec(wuv_pair.shape),
        ],
        out_specs=pl.BlockSpec((None, blk, nh * MLA_V), lambda bi, i: (bi, i, 0)),
        out_shape=jax.ShapeDtypeStruct((b, t, nh * MLA_V), BF16),
        scratch_shapes=[
            pltpu.VMEM((rows, 2 * r), BF16),
            pltpu.VMEM((rows, blk), F32),
            pltpu.VMEM((rows, blk), BF16),
            pltpu.VMEM((rows, 128), F32),
            pltpu.VMEM((rows, 128), F32),
            pltpu.VMEM((rows, 128), F32),
            pltpu.VMEM((rows, r), F32),
        ],
        compiler_params=_params("parallel", "arbitrary"),
        name="prompt_attention",
    )(qlat, qrope, kt, kcat, kmeta_t, vmeta, wuv_pair)


def _group_norm(o):
    mu = jnp.mean(o, axis=-1, keepdims=True)
    c = o - mu
    return c * lax.rsqrt(jnp.mean(c * c, axis=-1, keepdims=True) + NORM_EPS)


def _ret_body(rq_ref, rk_ref, rv_ref, kmeta_ref, vmeta_ref, on_ref, sout_ref, s_ref):
    c = pl.program_id(1)
    ch = rq_ref.shape[0]
    n_meta = kmeta_ref.shape[0]

    @pl.when(c == 0)
    def _():
        j = lax.broadcasted_iota(jnp.int32, (n_meta, 1), 0).astype(F32)
        for hd in range(RET_HEADS):
            kd = kmeta_ref[:, hd * RET_DK:(hd + 1) * RET_DK].astype(F32) * jnp.exp(RET_LOG_GAMMA[hd] * (n_meta - 1.0 - j))
            s_ref[hd] = _dot(kd.T.astype(BF16), vmeta_ref[:, hd * RET_DV:(hd + 1) * RET_DV])

    row = lax.broadcasted_iota(jnp.int32, (ch, 1), 0).astype(F32)
    diff = (lax.broadcasted_iota(jnp.int32, (ch, ch), 0) - lax.broadcasted_iota(jnp.int32, (ch, ch), 1)).astype(F32)
    for hd in range(RET_HEADS):
        lg = RET_LOG_GAMMA[hd]
        q = rq_ref[:, hd * RET_DK:(hd + 1) * RET_DK]
        k = rk_ref[:, hd * RET_DK:(hd + 1) * RET_DK]
        v = rv_ref[:, hd * RET_DV:(hd + 1) * RET_DV]
        decay = jnp.where(diff >= 0, jnp.exp(lg * jnp.maximum(diff, 0.0)), 0.0)
        scores = _dot_nt(q, k) * decay
        s_old = s_ref[hd]
        o = _dot(scores.astype(BF16), v) + _dot(q, s_old.astype(BF16)) * jnp.exp(lg * (row + 1.0))
        kd = k.astype(F32) * jnp.exp(lg * (ch - 1.0 - row))
        s_ref[hd] = math.exp(lg * ch) * s_old + _dot(kd.T.astype(BF16), v)
        on_ref[:, hd * RET_DV:(hd + 1) * RET_DV] = _group_norm(o).astype(BF16)

    @pl.when(c == pl.num_programs(1) - 1)
    def _():
        sout_ref[...] = s_ref[...]


def _prompt_retention(rq, rk, rv, kmeta, vmeta):
    b, t, hr = rq.shape
    hv = rv.shape[2]
    ch = min(t, RET_CHUNK)
    assert t % ch == 0
    tok = lambda width: pl.BlockSpec((None, ch, width), lambda bi, ci: (bi, ci, 0))
    return pl.pallas_call(
        _ret_body,
        grid=(b, t // ch),
        in_specs=[tok(hr), tok(hr), tok(hv), _const_spec(kmeta.shape), _const_spec(vmeta.shape)],
        out_specs=[tok(hv), pl.BlockSpec((None, RET_HEADS, RET_DK, RET_DV), lambda bi, ci: (bi, 0, 0, 0))],
        out_shape=[jax.ShapeDtypeStruct((b, t, hv), BF16),
                   jax.ShapeDtypeStruct((b, RET_HEADS, RET_DK, RET_DV), F32)],
        scratch_shapes=[pltpu.VMEM((RET_HEADS, RET_DK, RET_DV), F32)],
        compiler_params=_params("parallel", "arbitrary"),
        name="prompt_retention",
    )(rq, rk, rv, kmeta, vmeta)


def _dec_attn_body(pt_ref, qlat_ref, qrope_ref, kself_ref, wuv_ref, ckv_hbm, kr_hbm, o_ref,
                   kbuf, krbuf, sem_k, sem_r, m_ref, l_ref, acc_ref, *, pages, n_chunks):
    b = pl.program_id(0)
    ql = qlat_ref[...]
    qr = qrope_ref[...]

    def page_copies(page, slot, p):
        lat = pltpu.make_async_copy(ckv_hbm.at[page], kbuf.at[slot, pl.ds(p * PAGE_SIZE, PAGE_SIZE), :], sem_k.at[slot])
        rope = pltpu.make_async_copy(kr_hbm.at[page], krbuf.at[slot, :, pl.ds(p * PAGE_SIZE, PAGE_SIZE)], sem_r.at[slot])
        return lat, rope

    def start_chunk(req, chunk, slot):
        for p in range(pages):
            for cp in page_copies(pt_ref[req, chunk * pages + p], slot, p):
                cp.start()

    def wait_chunk(slot):
        for p in range(pages):
            for cp in page_copies(0, slot, p):
                cp.wait()

    @pl.when(b == 0)
    def _():
        start_chunk(0, 0, 0)

    ks = kself_ref[...].astype(F32)
    qf = jnp.concatenate([ql, qr], axis=1).astype(F32)
    m_ref[...] = jnp.sum(qf * ks, axis=-1, keepdims=True)
    l_ref[...] = jnp.ones(l_ref.shape, F32)
    acc_ref[...] = jnp.broadcast_to(ks[:, :KV_RANK], acc_ref.shape)

    def two_chunks(c2, carry):
        for slot in (0, 1):
            chunk = 2 * c2 + slot
            if slot == 0:
                start_chunk(b, chunk + 1, 1)
            else:
                @pl.when(chunk + 1 < n_chunks)
                def _():
                    start_chunk(b, chunk + 1, 0)

                @pl.when((chunk + 1 == n_chunks) & (b + 1 < pl.num_programs(0)))
                def _():
                    start_chunk(b + 1, 0, 0)
            wait_chunk(slot)
            k_lat = kbuf[slot].astype(BF16)
            k_rope_t = krbuf[slot].astype(BF16)
            _softmax_step(_dot_nt(ql, k_lat) + _dot(qr, k_rope_t), k_lat, m_ref, l_ref, acc_ref)
        return carry

    lax.fori_loop(0, n_chunks // 2, two_chunks, 0)

    o = (acc_ref[...] / l_ref[...]).astype(BF16)
    z = _dot(o, wuv_ref[...])
    head = lax.broadcasted_iota(jnp.int32, z.shape, 0)
    col_head = lax.broadcasted_iota(jnp.int32, z.shape, 1) // MLA_V
    o_ref[...] = jnp.sum(jnp.where(head == col_head, z, 0.0), axis=0, keepdims=True).astype(BF16)


def _decode_attention(page_table, qlat, qrope, kself, wuv_all, cache_ckv, cache_kr_t):
    bd, nh, r = qlat.shape
    n_pages = page_table.shape[1]
    pages = min(n_pages // 2, DEC_PAGES_PER_STEP)
    n_chunks = n_pages // pages
    assert n_chunks * pages == n_pages and n_chunks % 2 == 0

    per_req = lambda shape: pl.BlockSpec((None,) + shape, lambda b, pt: (b, 0, 0))
    grid_spec = pltpu.PrefetchScalarGridSpec(
        num_scalar_prefetch=1,
        grid=(bd,),
        in_specs=[per_req((nh, r)), per_req((nh, MLA_ROPE)), per_req((1, r + MLA_ROPE)),
                  pl.BlockSpec(wuv_all.shape, lambda b, pt: (0, 0)),
                  pl.BlockSpec(memory_space=pl.ANY), pl.BlockSpec(memory_space=pl.ANY)],
        out_specs=per_req((1, nh * MLA_V)),
        scratch_shapes=[
            pltpu.VMEM((2, pages * PAGE_SIZE, r), F32),
            pltpu.VMEM((2, MLA_ROPE, pages * PAGE_SIZE), F32),
            pltpu.SemaphoreType.DMA((2,)),
            pltpu.SemaphoreType.DMA((2,)),
            pltpu.VMEM((nh, 1), F32), pltpu.VMEM((nh, 1), F32), pltpu.VMEM((nh, r), F32),
        ],
    )
    return pl.pallas_call(
        functools.partial(_dec_attn_body, pages=pages, n_chunks=n_chunks),
        grid_spec=grid_spec,
        out_shape=jax.ShapeDtypeStruct((bd, 1, nh * MLA_V), BF16),
        compiler_params=_params("arbitrary"),
        name="decode_attention",
    )(page_table, qlat, qrope, kself, wuv_all, cache_ckv, cache_kr_t)


def _dec_ret_body(rq_ref, rk_ref, rv_ref, s_ref, on_ref, snew_ref):
    eye = (lax.broadcasted_iota(jnp.int32, (RET_DK, RET_DK), 0)
           == lax.broadcasted_iota(jnp.int32, (RET_DK, RET_DK), 1))

    def column(x):
        return jnp.sum(jnp.where(eye, jnp.broadcast_to(x, (RET_DK, RET_DK)), 0.0), axis=1, keepdims=True)

    for hd in range(RET_HEADS):
        gamma = math.exp(RET_LOG_GAMMA[hd])
        q = rq_ref[:, hd * RET_DK:(hd + 1) * RET_DK].astype(F32)
        k = rk_ref[:, hd * RET_DK:(hd + 1) * RET_DK].astype(F32)
        v = rv_ref[:, hd * RET_DV:(hd + 1) * RET_DV].astype(F32)
        s_old = s_ref[hd]
        qs = jnp.sum(column(q) * s_old, axis=0, keepdims=True)
        o = jnp.sum(q * k, axis=-1, keepdims=True) * v + qs * gamma
        snew_ref[hd] = gamma * s_old + column(k) * v
        on_ref[:, hd * RET_DV:(hd + 1) * RET_DV] = _group_norm(o).astype(BF16)


def _decode_retention(rq, rk, rv, state):
    bd = rq.shape[0]
    row = lambda width: pl.BlockSpec((None, 1, width), lambda b: (b, 0, 0))
    st = pl.BlockSpec((None, RET_HEADS, RET_DK, RET_DV), lambda b: (b, 0, 0, 0))
    return pl.pallas_call(
        _dec_ret_body,
        grid=(bd,),
        in_specs=[row(rq.shape[2]), row(rk.shape[2]), row(rv.shape[2]), st],
        out_specs=[row(rv.shape[2]), st],
        out_shape=[jax.ShapeDtypeStruct((bd, 1, rv.shape[2]), BF16), jax.ShapeDtypeStruct(state.shape, F32)],
        compiler_params=_params("parallel"),
        name="decode_retention",
    )(rq, rk, rv, state)


def _mix_out_body(h_ref, a_ref, on_ref, srg_ref, sga_ref, sgb_ref, gn_ref, wa_ref, wr_ref, wo_ref, o_ref):
    a = _dot(a_ref[...], wa_ref[...])
    gated = srg_ref[...].astype(F32) * (on_ref[...].astype(F32) * gn_ref[...])
    r = _dot(gated.astype(BF16), wr_ref[...])
    m = sga_ref[...].astype(F32) * a + sgb_ref[...].astype(F32) * r
    o_ref[...] = h_ref[...] + _dot(m.astype(BF16), wo_ref[...])


def _mix_out(h, apre, on, srg, sga, sgb, W):
    m, d = h.shape
    tm = min(m, TOKEN_TILE)
    assert m % tm == 0
    tok = lambda width: pl.BlockSpec((tm, width), lambda i: (i, 0))
    weights = [W["ret_gn"], W["w_mla_o"], W["w_ret_o"], W["w_out"]]
    return pl.pallas_call(
        _mix_out_body,
        grid=(m // tm,),
        in_specs=[tok(d), tok(apre.shape[1]), tok(on.shape[1]), tok(d), tok(d), tok(d)]
                 + [_const_spec(w.shape) for w in weights],
        out_specs=tok(d),
        out_shape=jax.ShapeDtypeStruct((m, d), F32),
        compiler_params=_params("parallel"),
        name="mixer_output",
    )(h, apre, on, srg, sga, sgb, *weights)


def _rope_tables(pos):
    pos = pos.astype(F32)[:, None]

    def table(width, reps):
        half = width // 2
        inv_freq = ROPE_BASE ** (-jnp.arange(half, dtype=F32) / half)
        ang = pos * inv_freq[None, :]
        cos, sin = jnp.cos(ang), jnp.sin(ang)
        return jnp.stack([jnp.tile(jnp.concatenate([cos, cos], axis=1), (1, reps)),
                          jnp.tile(jnp.concatenate([-sin, sin], axis=1), (1, reps))])

    return table(RET_DK, 1), table(MLA_ROPE, MLA_HEADS)


def _prep_weights(ffn1_norm, ffn1_gate, ffn1_up, ffn1_down, mix_norm, w_in, q_norm, kv_norm, w_uq, w_uk, w_uv,
                  w_mla_o, ret_gn, w_ret_o, w_out, ffn2_norm, ffn2_gate, ffn2_up, ffn2_down, final_norm):
    d = w_in.shape[0]
    hr, hv = RET_HEADS * RET_DK, RET_HEADS * RET_DV
    nh, hd = MLA_HEADS, MLA_NOPE + MLA_ROPE

    def ffn_w(gate, up, down):
        f = gate.shape[1]
        nc = f // FFN_CHUNK
        cols = lambda w: w.astype(BF16).reshape(d, nc, FFN_CHUNK).transpose(1, 0, 2)
        return cols(gate), cols(up), down.astype(BF16).reshape(nc, FFN_CHUNK, d)

    bounds = [0]
    for width in (Q_RANK, KV_RANK, MLA_ROPE, hr, hr, hv, hv, d, d):
        bounds.append(bounds[-1] + width)
    assert bounds[-1] == w_in.shape[1]
    w_q, w_kv, w_kr, w_rq, w_rk, w_rv, w_rg, w_ga, w_gb = (
        w_in[:, bounds[i]:bounds[i + 1]].astype(BF16) for i in range(9))

    w_uq3 = w_uq.reshape(Q_RANK, nh, hd)
    zeros = jnp.zeros((MLA_NOPE, KV_RANK), F32)
    w_uk_t = w_uk.transpose(1, 2, 0)
    w_uk_pair = jnp.stack([jnp.block([[w_uk_t[2 * p], zeros], [zeros, w_uk_t[2 * p + 1]]])
                           for p in range(nh // 2)]).astype(BF16)
    w_uv_t = w_uv.transpose(1, 0, 2)
    zv = jnp.zeros((KV_RANK, MLA_V), F32)
    w_uv_pair = jnp.stack([jnp.block([[w_uv_t[2 * p], zv], [zv, w_uv_t[2 * p + 1]]])
                           for p in range(nh // 2)]).astype(BF16)
    row = lambda v: v.reshape(1, -1)
    return dict(
        ffn1_norm=row(ffn1_norm), ffn1=ffn_w(ffn1_gate, ffn1_up, ffn1_down),
        ffn2_norm=row(ffn2_norm), ffn2=ffn_w(ffn2_gate, ffn2_up, ffn2_down), final_norm=row(final_norm),
        mix_norm=row(mix_norm), w_q=w_q, w_kv=w_kv, w_kr=jnp.tile(w_kr, (1, nh)), w_rq=w_rq, w_rk=w_rk, w_rv=w_rv,
        w_rg=w_rg, w_ga=w_ga, w_gb=w_gb, q_norm=row(q_norm), kv_norm=row(kv_norm),
        w_uq_nope=w_uq3[:, :, :MLA_NOPE].reshape(Q_RANK, nh * MLA_NOPE).astype(BF16),
        w_uq_rope=w_uq3[:, :, MLA_NOPE:].reshape(Q_RANK, nh * MLA_ROPE).astype(BF16),
        w_uk_pair=w_uk_pair, w_uv_pair=w_uv_pair, w_uv_all=w_uv.reshape(KV_RANK, nh * MLA_V).astype(BF16),
        ret_gn=row(ret_gn), w_mla_o=w_mla_o.astype(BF16), w_ret_o=w_ret_o.astype(BF16), w_out=w_out.astype(BF16),
    )


def kernel(x_prompt, x_sample, cache_ckv, cache_krope, state_ret, page_table, meta_tokens, ffn1_norm, ffn1_gate, ffn1_up, ffn1_down, mix_norm, w_in, q_norm, kv_norm, w_uq, w_uk, w_uv, w_mla_o, ret_gn, w_ret_o, w_out, ffn2_norm, ffn2_gate, ffn2_up, ffn2_down, final_norm):
    assert ffn1_gate.shape[0] == 1, "single-layer trunk"
    b, t, d = x_prompt.shape
    bd, dec_seq, _ = x_sample.shape
    assert dec_seq == 1
    n_meta = meta_tokens.shape[0]
    past_len = page_table.shape[1] * PAGE_SIZE
    W = _prep_weights(ffn1_norm[0], ffn1_gate[0], ffn1_up[0], ffn1_down[0], mix_norm[0], w_in[0], q_norm[0],
                      kv_norm[0], w_uq[0], w_uk[0], w_uv[0], w_mla_o[0], ret_gn[0], w_ret_o[0], w_out[0],
                      ffn2_norm[0], ffn2_gate[0], ffn2_up[0], ffn2_down[0], final_norm)

    rope_r, rope_q = _rope_tables(n_meta + jnp.arange(t))
    pos_small = jnp.concatenate([jnp.arange(n_meta), jnp.full((bd,), past_len)])
    rope_r_s, rope_q_s = _rope_tables(pos_small)
    x_small = jnp.concatenate([meta_tokens.astype(F32), x_sample.reshape(bd, d)], axis=0)

    h1 = _ffn(x_prompt.reshape(b * t, d), W["ffn1_norm"], *W["ffn1"])
    h1s = _ffn(x_small, W["ffn1_norm"], *W["ffn1"])
    P = _mix_in(h1.reshape(b, t, d), W, rope_r, rope_q)
    S = _mix_in(h1s.reshape(1, n_meta + bd, d), W, rope_r_s, rope_q_s)

    apre = _prompt_attention(P["qlat"], P["qrope"], P["kt"], P["kcat"], S["kt"][0, 0, :, :n_meta],
                             S["kcat"][0, :n_meta, :KV_RANK], W["w_uv_pair"])
    on, s_prompt = _prompt_retention(P["rq"], P["rk"], P["rv"], S["rk"][0, :n_meta], S["rv"][0, :n_meta])

    nh = MLA_HEADS
    qlat_s = S["qlat"][0, :, n_meta:].transpose(1, 0, 2)
    qrope_s = S["qrope"][0, n_meta:].reshape(bd, nh, MLA_ROPE)
    kself = S["kcat"][0, n_meta:, :KV_RANK + MLA_ROPE].reshape(bd, 1, KV_RANK + MLA_ROPE)
    apre_s = _decode_attention(page_table, qlat_s, qrope_s, kself, W["w_uv_all"], cache_ckv[0],
                               cache_krope[0].transpose(0, 2, 1))
    dec = lambda name: S[name][0, n_meta:].reshape(bd, 1, -1)
    on_s, s_sample = _decode_retention(dec("rq"), dec("rk"), dec("rv"), state_ret[0])

    flat = lambda a: a.reshape(b * t, -1)
    h2 = _mix_out(h1, flat(apre), flat(on), flat(P["srg"]), flat(P["sga"]), flat(P["sgb"]), W)
    y_prompt = _ffn(h2, W["ffn2_norm"], *W["ffn2"], final_norm=W["final_norm"])
    tail = lambda name: S[name][0, n_meta:]
    h2s = _mix_out(h1s[n_meta:], apre_s.reshape(bd, -1), on_s.reshape(bd, -1), tail("srg"), tail("sga"), tail("sgb"), W)
    y_sample = _ffn(h2s, W["ffn2_norm"], *W["ffn2"], final_norm=W["final_norm"])

    with_meta = lambda small, main: jnp.concatenate(
        [jnp.broadcast_to(small[:, :n_meta], (b, n_meta, small.shape[2])), main], axis=1)[None]
    return (
        y_prompt.reshape(b, t, d),
        y_sample.reshape(bd, 1, d),
        with_meta(S["ckv"], P["ckv"]),
        with_meta(S["kr"], P["kr"]),
        s_prompt[None],
        S["ckv"][0, n_meta:].reshape(1, bd, 1, KV_RANK),
        S["kr"][0, n_meta:].reshape(1, bd, 1, MLA_ROPE),
        s_sample[None],
    )
```

```python
import functools
import math

import jax
import jax.numpy as jnp
from jax import lax
from jax.experimental import pallas as pl
from jax.experimental.pallas import tpu as pltpu

N_META = 16
PAGE_SIZE = 128
MLA_HEADS = 8
MLA_NOPE = 64
MLA_ROPE = 32
MLA_V = 64
Q_RANK = 384
KV_RANK = 256
MLA_SCALE = (MLA_NOPE + MLA_ROPE) ** -0.5
RET_HEADS = 4
RET_DK = 128
RET_DV = 256
ROPE_BASE = 10000.0
NORM_EPS = 1e-6
RET_LOG_GAMMA = tuple(math.log1p(-(2.0 ** (-5.0 - h))) for h in range(RET_HEADS))

VMEM_LIMIT_BYTES = 56 * 1024 * 1024
FFN_CHUNK = 256
TOKEN_TILE = 512
ATTN_BLOCK = 256
ATTN_GROUP_HEADS = 2
ATTN_ROW_CHUNK = 32
RET_CHUNK = 256
DEC_PAGES_PER_STEP = 16
MASK_VALUE = -1e30

F32 = jnp.float32
BF16 = jnp.bfloat16


def _params(*sem):
    return pltpu.CompilerParams(dimension_semantics=sem, vmem_limit_bytes=VMEM_LIMIT_BYTES)


def _const_spec(shape):
    n = len(shape)
    return pl.BlockSpec(shape, lambda *_: (0,) * n, pipeline_mode=pl.Buffered(1))


def _rms(x, g):
    return x * lax.rsqrt(jnp.mean(x * x, axis=-1, keepdims=True) + NORM_EPS) * g


def _sigmoid(x):
    return 1.0 / (1.0 + jnp.exp(-x))


def _dot(a, b):
    return jnp.dot(a, b, preferred_element_type=F32)


def _dot_nt(a, b):
    return lax.dot_general(a, b, (((1,), (1,)), ((), ())), preferred_element_type=F32)


def _ffn_body(x_ref, g_ref, wg_ref, wu_ref, wd_ref, *rest, final):
    o_ref = rest[-1]
    x = x_ref[...]
    u = _rms(x, g_ref[...]).astype(BF16)
    acc = None
    for c in range(0, wg_ref.shape[1], FFN_CHUNK):
        cols = slice(c, c + FFN_CHUNK)
        gate = _dot(u, wg_ref[:, cols])
        up = _dot(u, wu_ref[:, cols])
        act = (gate * _sigmoid(gate) * up).astype(BF16)
        part = _dot(act, wd_ref[cols, :])
        acc = part if acc is None else acc + part
    y = x + 0.5 * acc
    if final:
        y = _rms(y, rest[0][...])
    o_ref[...] = y


def _ffn(x, norm, wg, wu, wd, final_norm=None):
    m, d = x.shape
    tm = min(m, TOKEN_TILE)
    assert m % tm == 0 and wg.shape[1] % FFN_CHUNK == 0
    final = final_norm is not None
    in_specs = [pl.BlockSpec((tm, d), lambda i: (i, 0)), _const_spec((1, d)),
                _const_spec(wg.shape), _const_spec(wu.shape), _const_spec(wd.shape)]
    args = [x, norm, wg, wu, wd]
    if final:
        in_specs.append(_const_spec((1, d)))
        args.append(final_norm)
    return pl.pallas_call(
        functools.partial(_ffn_body, final=final),
        grid=(m // tm,),
        in_specs=in_specs,
        out_specs=pl.BlockSpec((tm, d), lambda i: (i, 0)),
        out_shape=jax.ShapeDtypeStruct((m, d), F32),
        compiler_params=_params("parallel"),
        name="ffn_final" if final else "ffn",
    )(*args)


def _rope32(x, cos, sin):
    lane = lax.broadcasted_iota(jnp.int32, x.shape, 1)
    width = x.shape[1]
    partner = jnp.where((lane & 31) < 16, pltpu.roll(x, width - 16, axis=1), pltpu.roll(x, 16, axis=1))
    return x * cos + partner * sin


def _mix_in_body(h_ref, g_ref, wq_ref, wkv_ref, wkr_ref, wrq_ref, wrk_ref, wrv_ref, wrg_ref, wga_ref, wgb_ref,
                 qn_ref, kvn_ref, wuqn_ref, wuqr_ref, wuk_ref, rope_r_ref, rope_q_ref,
                 qlat_ref, qrope_ref, ckv_ref, kr_ref, kcat_ref, kt_ref, rq_ref, rk_ref, rv_ref, srg_ref, sga_ref,
                 sgb_ref):
    u = _rms(h_ref[...], g_ref[...]).astype(BF16)
    cos_q, sin_q = rope_q_ref[0], rope_q_ref[1]
    cos_r, sin_r = rope_r_ref[0], rope_r_ref[1]

    cq = _rms(_dot(u, wq_ref[...]), qn_ref[...]).astype(BF16)
    q_nope = (_dot(cq, wuqn_ref[...]) * MLA_SCALE).astype(BF16)
    for p in range(MLA_HEADS // 2):
        pair = _dot(q_nope[:, p * 128:(p + 1) * 128], wuk_ref[p])
        qlat_ref[2 * p] = pair[:, :KV_RANK].astype(BF16)
        qlat_ref[2 * p + 1] = pair[:, KV_RANK:].astype(BF16)
    q_rope = _rope32(_dot(cq, wuqr_ref[...]), cos_q, sin_q) * MLA_SCALE
    qrope_ref[...] = q_rope.astype(BF16)

    ckv = _rms(_dot(u, wkv_ref[...]), kvn_ref[...])
    ckv_ref[...] = ckv
    kcat_ref[:, :KV_RANK] = ckv.astype(BF16)
    kr = _rope32(_dot(u, wkr_ref[...]), cos_q, sin_q)
    kr_ref[...] = kr[:, :MLA_ROPE]
    kcat_ref[:, KV_RANK:] = kr.astype(BF16)
    kt_ref[...] = jnp.concatenate([ckv, kr], axis=1).T.astype(BF16)

    rq = _dot(u, wrq_ref[...])
    rk = _dot(u, wrk_ref[...])
    for hd in range(RET_HEADS):
        sl = slice(hd * RET_DK, (hd + 1) * RET_DK)
        xq, xk = rq[:, sl], rk[:, sl]
        rq_ref[:, sl] = (xq * cos_r + pltpu.roll(xq, RET_DK // 2, axis=1) * sin_r).astype(BF16)
        rk_ref[:, sl] = ((xk * cos_r + pltpu.roll(xk, RET_DK // 2, axis=1) * sin_r) * (RET_DK ** -0.5)).astype(BF16)
    rv_ref[...] = _dot(u, wrv_ref[...]).astype(BF16)

    rg = _dot(u, wrg_ref[...])
    srg_ref[...] = (rg * _sigmoid(rg)).astype(BF16)
    sga_ref[...] = _sigmoid(_dot(u, wga_ref[...])).astype(BF16)
    sgb_ref[...] = _sigmoid(_dot(u, wgb_ref[...])).astype(BF16)


def _mix_in(h, W, rope_r, rope_q):
    b, t, d = h.shape
    tm = min(t, TOKEN_TILE)
    assert t % tm == 0
    hr = RET_HEADS * RET_DK
    hv = RET_HEADS * RET_DV
    qw = MLA_HEADS * MLA_ROPE
    weights = [W["mix_norm"], W["w_q"], W["w_kv"], W["w_kr"], W["w_rq"], W["w_rk"], W["w_rv"], W["w_rg"],
               W["w_ga"], W["w_gb"], W["q_norm"], W["kv_norm"], W["w_uq_nope"], W["w_uq_rope"], W["w_uk_pair"]]
    tok = lambda width: pl.BlockSpec((None, tm, width), lambda bi, ti: (bi, ti, 0))
    in_specs = ([tok(d)] + [_const_spec(w.shape) for w in weights]
                + [pl.BlockSpec((2, tm, RET_DK), lambda bi, ti: (0, ti, 0)),
                   pl.BlockSpec((2, tm, qw), lambda bi, ti: (0, ti, 0))])
    out_specs = [
        pl.BlockSpec((None, MLA_HEADS, tm, KV_RANK), lambda bi, ti: (bi, 0, ti, 0)),
        tok(qw), tok(KV_RANK), tok(MLA_ROPE), tok(2 * KV_RANK),
        pl.BlockSpec((None, 2 * KV_RANK, tm), lambda bi, ti: (bi, 0, ti)),
        tok(hr), tok(hr), tok(hv), tok(hv), tok(d), tok(d),
    ]
    sds = jax.ShapeDtypeStruct
    out_shape = [
        sds((b, MLA_HEADS, t, KV_RANK), BF16), sds((b, t, qw), BF16), sds((b, t, KV_RANK), F32),
        sds((b, t, MLA_ROPE), F32), sds((b, t, 2 * KV_RANK), BF16), sds((b, 2 * KV_RANK, t), BF16),
        sds((b, t, hr), BF16), sds((b, t, hr), BF16),
        sds((b, t, hv), BF16), sds((b, t, hv), BF16), sds((b, t, d), BF16), sds((b, t, d), BF16),
    ]
    names = ("qlat", "qrope", "ckv", "kr", "kcat", "kt", "rq", "rk", "rv", "srg", "sga", "sgb")
    outs = pl.pallas_call(
        _mix_in_body,
        grid=(b, t // tm),
        in_specs=in_specs,
        out_specs=out_specs,
        out_shape=out_shape,
        compiler_params=_params("parallel", "parallel"),
        name="mixer_inputs",
    )(h, *weights, rope_r, rope_q)
    return dict(zip(names, outs))


def _softmax_step(s, k_lat, m_ref, l_ref, acc_ref):
    m_prev = m_ref[...]
    m_new = jnp.maximum(m_prev, jnp.max(s, axis=-1, keepdims=True))
    alpha = jnp.exp(m_prev - m_new)
    p = jnp.exp(s - m_new)
    l_ref[...] = alpha * l_ref[...] + jnp.sum(p, axis=-1, keepdims=True)
    acc_ref[...] = alpha * acc_ref[...] + _dot(p.astype(BF16), k_lat)
    m_ref[...] = m_new


def _lanes(x, width):
    return x if width == x.shape[1] else jnp.concatenate([x] * (width // x.shape[1]), axis=1)


def _attn_body(qlat_ref, qrope_ref, kt_ref, v_ref, wuv_ref, o_ref,
               qcat_ref, s_ref, p_ref, alpha_ref, m_ref, l_ref, acc_ref):
    i = pl.program_id(1)
    blk = qrope_ref.shape[0]
    grp = ATTN_GROUP_HEADS * blk
    n_grp = MLA_HEADS // ATTN_GROUP_HEADS

    lane_head = lax.broadcasted_iota(jnp.int32, (blk, MLA_HEADS * MLA_ROPE), 1) // MLA_ROPE
    qr = qrope_ref[...]
    for hd in range(MLA_HEADS):
        qcat_ref[hd * blk:(hd + 1) * blk, :KV_RANK] = qlat_ref[hd]
        qcat_ref[hd * blk:(hd + 1) * blk, KV_RANK:] = jnp.where(lane_head == hd, qr, jnp.zeros_like(qr))

    def blocks(j, n_blk, first):
        width = n_blk * blk
        v = v_ref[pl.ds(pl.multiple_of(j * blk, blk), width), :]
        if first:
            tok = lax.broadcasted_iota(jnp.int32, (ATTN_ROW_CHUNK, width), 0)
            key = lax.broadcasted_iota(jnp.int32, (ATTN_ROW_CHUNK, width), 1)
        for g in range(n_grp):
            rows = slice(g * grp, (g + 1) * grp)
            for n in range(n_blk):
                s_ref[rows, n * blk:(n + 1) * blk] = _dot(qcat_ref[rows, :], kt_ref[j + n])
        for g in range(n_grp):
            for c in range(g * grp, (g + 1) * grp, ATTN_ROW_CHUNK):
                r = slice(c, c + ATTN_ROW_CHUNK)
                s = s_ref[r, :width]
                if first:
                    s = jnp.where(key <= tok + (c % blk + N_META), s, MASK_VALUE)
                    m_new = jnp.broadcast_to(jnp.max(s, axis=-1, keepdims=True), (ATTN_ROW_CHUNK, 128))
                    p = jnp.exp(s - _lanes(m_new, width))
                    l_ref[r, :] = jnp.broadcast_to(jnp.sum(p, axis=-1, keepdims=True), (ATTN_ROW_CHUNK, 128))
                else:
                    m_prev = m_ref[r, :]
                    m_new = jnp.maximum(m_prev, jnp.max(s, axis=-1, keepdims=True))
                    alpha = jnp.exp(m_prev - m_new)
                    p = jnp.exp(s - _lanes(m_new, width))
                    l_ref[r, :] = alpha * l_ref[r, :] + jnp.sum(p, axis=-1, keepdims=True)
                    alpha_ref[r, :] = alpha
                m_ref[r, :] = m_new
                p_ref[r, :width] = p.astype(BF16)
            rows = slice(g * grp, (g + 1) * grp)
            pv = _dot(p_ref[rows, :width], v)
            acc_ref[rows, :] = pv if first else _lanes(alpha_ref[rows, :], KV_RANK) * acc_ref[rows, :] + pv

    blocks(i, 1, True)

    def two_full_blocks(jj, carry):
        blocks(2 * jj, 2, False)
        return carry

    lax.fori_loop(0, i // 2, two_full_blocks, 0)

    @pl.when(i % 2 == 1)
    def _():
        blocks(i - 1, 1, False)

    tails = [slice((hd + 1) * blk - N_META, (hd + 1) * blk) for hd in range(MLA_HEADS)]
    gather = lambda ref: jnp.concatenate([ref[r, :] for r in tails], axis=0)
    s = _dot(gather(qcat_ref), kt_ref[i + 1][:, :128])
    q_off = lax.broadcasted_iota(jnp.int32, s.shape, 0) & (N_META - 1)
    s = jnp.where(lax.broadcasted_iota(jnp.int32, s.shape, 1) <= q_off, s, MASK_VALUE)
    m_prev = gather(m_ref)
    m_new = jnp.maximum(m_prev, jnp.max(s, axis=-1, keepdims=True))
    alpha = jnp.exp(m_prev - m_new)
    p = jnp.exp(s - m_new)
    l_new = alpha * gather(l_ref) + jnp.sum(p, axis=-1, keepdims=True)
    v = v_ref[pl.ds(pl.multiple_of((i + 1) * blk, blk), 128), :]
    acc_new = _lanes(alpha, KV_RANK) * gather(acc_ref) + _dot(p.astype(BF16), v)
    for hd, r in enumerate(tails):
        part = slice(hd * N_META, (hd + 1) * N_META)
        l_ref[r, :] = l_new[part]
        acc_ref[r, :] = acc_new[part]

    for pr in range(MLA_HEADS // 2):
        halves = []
        for hd in (2 * pr, 2 * pr + 1):
            rows = slice(hd * blk, (hd + 1) * blk)
            halves.append((acc_ref[rows, :] / _lanes(l_ref[rows, :], KV_RANK)).astype(BF16))
        o_ref[:, pr * 2 * MLA_V:(pr + 1) * 2 * MLA_V] = _dot(jnp.concatenate(halves, axis=1), wuv_ref[pr]).astype(BF16)


def _prompt_attention(qlat, qrope, kt_seq, v_seq, wuv_pair):
    b, nh, t, r = qlat.shape
    blk = kt_seq.shape[3]
    nq = t // blk
    assert nq * blk == t and kt_seq.shape[1] == nq + 1 and v_seq.shape[1] == (nq + 1) * blk
    assert blk % 128 == 0 and blk & (blk - 1) == 0 and N_META <= 128
    rows = nh * blk
    return pl.pallas_call(
        _attn_body,
        grid=(b, nq),
        in_specs=[
            pl.BlockSpec((None, nh, blk, r), lambda bi, i: (bi, 0, i, 0)),
            pl.BlockSpec((None, blk, qrope.shape[2]), lambda bi, i: (bi, i, 0)),
            pl.BlockSpec((None,) + kt_seq.shape[1:], lambda bi, i: (bi, 0, 0, 0)),
            pl.BlockSpec((None,) + v_seq.shape[1:], lambda bi, i: (bi, 0, 0)),
            _const_spec(wuv_pair.shape),
        ],
        out_specs=pl.BlockSpec((None, blk, nh * MLA_V), lambda bi, i: (bi, i, 0)),
        out_shape=jax.ShapeDtypeStruct((b, t, nh * MLA_V), BF16),
        scratch_shapes=[
            pltpu.VMEM((rows, 2 * r), BF16),
            pltpu.VMEM((rows, 2 * blk), F32),
            pltpu.VMEM((rows, 2 * blk), BF16),
            pltpu.VMEM((rows, 128), F32),
            pltpu.VMEM((rows, 128), F32),
            pltpu.VMEM((rows, 128), F32),
            pltpu.VMEM((rows, r), F32),
        ],
        compiler_params=_params("parallel", "arbitrary"),
        name="prompt_attention",
    )(qlat, qrope, kt_seq, v_seq, wuv_pair)


def _group_norm(o):
    mu = jnp.mean(o, axis=-1, keepdims=True)
    c = o - mu
    return c * lax.rsqrt(jnp.mean(c * c, axis=-1, keepdims=True) + NORM_EPS)


def _ret_body(rq_ref, rk_ref, rv_ref, kmeta_ref, vmeta_ref, on_ref, sout_ref, s_ref):
    c = pl.program_id(1)
    ch = rq_ref.shape[0]
    n_meta = kmeta_ref.shape[0]

    @pl.when(c == 0)
    def _():
        j = lax.broadcasted_iota(jnp.int32, (n_meta, 1), 0).astype(F32)
        for hd in range(RET_HEADS):
            kd = kmeta_ref[:, hd * RET_DK:(hd + 1) * RET_DK].astype(F32) * jnp.exp(RET_LOG_GAMMA[hd] * (n_meta - 1.0 - j))
            s_ref[hd] = _dot(kd.T.astype(BF16), vmeta_ref[:, hd * RET_DV:(hd + 1) * RET_DV])

    row = lax.broadcasted_iota(jnp.int32, (ch, 1), 0).astype(F32)
    diff = (lax.broadcasted_iota(jnp.int32, (ch, ch), 0) - lax.broadcasted_iota(jnp.int32, (ch, ch), 1)).astype(F32)
    for hd in range(RET_HEADS):
        lg = RET_LOG_GAMMA[hd]
        q = rq_ref[:, hd * RET_DK:(hd + 1) * RET_DK]
        k = rk_ref[:, hd * RET_DK:(hd + 1) * RET_DK]
        v = rv_ref[:, hd * RET_DV:(hd + 1) * RET_DV]
        decay = jnp.where(diff >= 0, jnp.exp(lg * jnp.maximum(diff, 0.0)), 0.0)
        scores = _dot_nt(q, k) * decay
        s_old = s_ref[hd]
        o = _dot(scores.astype(BF16), v) + _dot(q, s_old.astype(BF16)) * jnp.exp(lg * (row + 1.0))
        kd = k.astype(F32) * jnp.exp(lg * (ch - 1.0 - row))
        s_ref[hd] = math.exp(lg * ch) * s_old + _dot(kd.T.astype(BF16), v)
        on_ref[:, hd * RET_DV:(hd + 1) * RET_DV] = _group_norm(o).astype(BF16)

    @pl.when(c == pl.num_programs(1) - 1)
    def _():
        sout_ref[...] = s_ref[...]


def _prompt_retention(rq, rk, rv, kmeta, vmeta):
    b, t, hr = rq.shape
    hv = rv.shape[2]
    ch = min(t, RET_CHUNK)
    assert t % ch == 0
    tok = lambda width: pl.BlockSpec((None, ch, width), lambda bi, ci: (bi, ci, 0))
    return pl.pallas_call(
        _ret_body,
        grid=(b, t // ch),
        in_specs=[tok(hr), tok(hr), tok(hv), _const_spec(kmeta.shape), _const_spec(vmeta.shape)],
        out_specs=[tok(hv), pl.BlockSpec((None, RET_HEADS, RET_DK, RET_DV), lambda bi, ci: (bi, 0, 0, 0))],
        out_shape=[jax.ShapeDtypeStruct((b, t, hv), BF16),
                   jax.ShapeDtypeStruct((b, RET_HEADS, RET_DK, RET_DV), F32)],
        scratch_shapes=[pltpu.VMEM((RET_HEADS, RET_DK, RET_DV), F32)],
        compiler_params=_params("parallel", "arbitrary"),
        name="prompt_retention",
    )(rq, rk, rv, kmeta, vmeta)


def _dec_attn_body(pt_ref, qlat_ref, qrope_ref, kself_ref, wuv_ref, ckv_hbm, kr_hbm, o_ref,
                   kbuf, krbuf, sem_k, sem_r, m_ref, l_ref, acc_ref, *, pages, n_chunks):
    b = pl.program_id(0)
    ql = qlat_ref[...]
    qr = qrope_ref[...]

    def page_copies(page, slot, p):
        lat = pltpu.make_async_copy(ckv_hbm.at[page], kbuf.at[slot, pl.ds(p * PAGE_SIZE, PAGE_SIZE), :], sem_k.at[slot])
        rope = pltpu.make_async_copy(kr_hbm.at[page], krbuf.at[slot, :, pl.ds(p * PAGE_SIZE, PAGE_SIZE)], sem_r.at[slot])
        return lat, rope

    def start_chunk(req, chunk, slot):
        for p in range(pages):
            for cp in page_copies(pt_ref[req, chunk * pages + p], slot, p):
                cp.start()

    def wait_chunk(slot):
        for p in range(pages):
            for cp in page_copies(0, slot, p):
                cp.wait()

    @pl.when(b == 0)
    def _():
        start_chunk(0, 0, 0)

    ks = kself_ref[...].astype(F32)
    qf = jnp.concatenate([ql, qr], axis=1).astype(F32)
    m_ref[...] = jnp.sum(qf * ks, axis=-1, keepdims=True)
    l_ref[...] = jnp.ones(l_ref.shape, F32)
    acc_ref[...] = jnp.broadcast_to(ks[:, :KV_RANK], acc_ref.shape)

    def two_chunks(c2, carry):
        for slot in (0, 1):
            chunk = 2 * c2 + slot
            if slot == 0:
                start_chunk(b, chunk + 1, 1)
            else:
                @pl.when(chunk + 1 < n_chunks)
                def _():
                    start_chunk(b, chunk + 1, 0)

                @pl.when((chunk + 1 == n_chunks) & (b + 1 < pl.num_programs(0)))
                def _():
                    start_chunk(b + 1, 0, 0)
            wait_chunk(slot)
            k_lat = kbuf[slot].astype(BF16)
            k_rope_t = krbuf[slot].astype(BF16)
            _softmax_step(_dot_nt(ql, k_lat) + _dot(qr, k_rope_t), k_lat, m_ref, l_ref, acc_ref)
        return carry

    lax.fori_loop(0, n_chunks // 2, two_chunks, 0)

    o = (acc_ref[...] / l_ref[...]).astype(BF16)
    z = _dot(o, wuv_ref[...])
    head = lax.broadcasted_iota(jnp.int32, z.shape, 0)
    col_head = lax.broadcasted_iota(jnp.int32, z.shape, 1) // MLA_V
    o_ref[...] = jnp.sum(jnp.where(head == col_head, z, 0.0), axis=0, keepdims=True).astype(BF16)


def _decode_attention(page_table, qlat, qrope, kself, wuv_all, cache_ckv, cache_kr_t):
    bd, nh, r = qlat.shape
    n_pages = page_table.shape[1]
    pages = min(n_pages // 2, DEC_PAGES_PER_STEP)
    n_chunks = n_pages // pages
    assert n_chunks * pages == n_pages and n_chunks % 2 == 0

    per_req = lambda shape: pl.BlockSpec((None,) + shape, lambda b, pt: (b, 0, 0))
    grid_spec = pltpu.PrefetchScalarGridSpec(
        num_scalar_prefetch=1,
        grid=(bd,),
        in_specs=[per_req((nh, r)), per_req((nh, MLA_ROPE)), per_req((1, r + MLA_ROPE)),
                  pl.BlockSpec(wuv_all.shape, lambda b, pt: (0, 0)),
                  pl.BlockSpec(memory_space=pl.ANY), pl.BlockSpec(memory_space=pl.ANY)],
        out_specs=per_req((1, nh * MLA_V)),
        scratch_shapes=[
            pltpu.VMEM((2, pages * PAGE_SIZE, r), F32),
            pltpu.VMEM((2, MLA_ROPE, pages * PAGE_SIZE), F32),
            pltpu.SemaphoreType.DMA((2,)),
            pltpu.SemaphoreType.DMA((2,)),
            pltpu.VMEM((nh, 1), F32), pltpu.VMEM((nh, 1), F32), pltpu.VMEM((nh, r), F32),
        ],
    )
    return pl.pallas_call(
        functools.partial(_dec_attn_body, pages=pages, n_chunks=n_chunks),
        grid_spec=grid_spec,
        out_shape=jax.ShapeDtypeStruct((bd, 1, nh * MLA_V), BF16),
        compiler_params=_params("arbitrary"),
        name="decode_attention",
    )(page_table, qlat, qrope, kself, wuv_all, cache_ckv, cache_kr_t)


def _dec_ret_body(rq_ref, rk_ref, rv_ref, s_ref, on_ref, snew_ref):
    eye = (lax.broadcasted_iota(jnp.int32, (RET_DK, RET_DK), 0)
           == lax.broadcasted_iota(jnp.int32, (RET_DK, RET_DK), 1))

    def column(x):
        return jnp.sum(jnp.where(eye, jnp.broadcast_to(x, (RET_DK, RET_DK)), 0.0), axis=1, keepdims=True)

    for hd in range(RET_HEADS):
        gamma = math.exp(RET_LOG_GAMMA[hd])
        q = rq_ref[:, hd * RET_DK:(hd + 1) * RET_DK].astype(F32)
        k = rk_ref[:, hd * RET_DK:(hd + 1) * RET_DK].astype(F32)
        v = rv_ref[:, hd * RET_DV:(hd + 1) * RET_DV].astype(F32)
        s_old = s_ref[hd]
        qs = jnp.sum(column(q) * s_old, axis=0, keepdims=True)
        o = jnp.sum(q * k, axis=-1, keepdims=True) * v + qs * gamma
        snew_ref[hd] = gamma * s_old + column(k) * v
        on_ref[:, hd * RET_DV:(hd + 1) * RET_DV] = _group_norm(o).astype(BF16)


def _decode_retention(rq, rk, rv, state):
    bd = rq.shape[0]
    row = lambda width: pl.BlockSpec((None, 1, width), lambda b: (b, 0, 0))
    st = pl.BlockSpec((None, RET_HEADS, RET_DK, RET_DV), lambda b: (b, 0, 0, 0))
    return pl.pallas_call(
        _dec_ret_body,
        grid=(bd,),
        in_specs=[row(rq.shape[2]), row(rk.shape[2]), row(rv.shape[2]), st],
        out_specs=[row(rv.shape[2]), st],
        out_shape=[jax.ShapeDtypeStruct((bd, 1, rv.shape[2]), BF16), jax.ShapeDtypeStruct(state.shape, F32)],
        compiler_params=_params("parallel"),
        name="decode_retention",
    )(rq, rk, rv, state)


def _mix_out_body(h_ref, a_ref, on_ref, srg_ref, sga_ref, sgb_ref, gn_ref, wa_ref, wr_ref, wo_ref, o_ref):
    a = _dot(a_ref[...], wa_ref[...])
    gated = srg_ref[...].astype(F32) * (on_ref[...].astype(F32) * gn_ref[...])
    r = _dot(gated.astype(BF16), wr_ref[...])
    m = sga_ref[...].astype(F32) * a + sgb_ref[...].astype(F32) * r
    o_ref[...] = h_ref[...] + _dot(m.astype(BF16), wo_ref[...])


def _mix_out(h, apre, on, srg, sga, sgb, W):
    m, d = h.shape
    tm = min(m, TOKEN_TILE)
    assert m % tm == 0
    tok = lambda width: pl.BlockSpec((tm, width), lambda i: (i, 0))
    weights = [W["ret_gn"], W["w_mla_o"], W["w_ret_o"], W["w_out"]]
    return pl.pallas_call(
        _mix_out_body,
        grid=(m // tm,),
        in_specs=[tok(d), tok(apre.shape[1]), tok(on.shape[1]), tok(d), tok(d), tok(d)]
                 + [_const_spec(w.shape) for w in weights],
        out_specs=tok(d),
        out_shape=jax.ShapeDtypeStruct((m, d), F32),
        compiler_params=_params("parallel"),
        name="mixer_output",
    )(h, apre, on, srg, sga, sgb, *weights)


def _rope_tables(pos):
    pos = pos.astype(F32)[:, None]

    def table(width, reps):
        half = width // 2
        inv_freq = ROPE_BASE ** (-jnp.arange(half, dtype=F32) / half)
        ang = pos * inv_freq[None, :]
        cos, sin = jnp.cos(ang), jnp.sin(ang)
        return jnp.stack([jnp.tile(jnp.concatenate([cos, cos], axis=1), (1, reps)),
                          jnp.tile(jnp.concatenate([-sin, sin], axis=1), (1, reps))])

    return table(RET_DK, 1), table(MLA_ROPE, MLA_HEADS)


def _prep_weights(ffn1_norm, ffn1_gate, ffn1_up, ffn1_down, mix_norm, w_in, q_norm, kv_norm, w_uq, w_uk, w_uv,
                  w_mla_o, ret_gn, w_ret_o, w_out, ffn2_norm, ffn2_gate, ffn2_up, ffn2_down, final_norm):
    d = w_in.shape[0]
    hr, hv = RET_HEADS * RET_DK, RET_HEADS * RET_DV
    nh, hd = MLA_HEADS, MLA_NOPE + MLA_ROPE
    bf = lambda *ws: tuple(w.astype(BF16) for w in ws)

    bounds = [0]
    for width in (Q_RANK, KV_RANK, MLA_ROPE, hr, hr, hv, hv, d, d):
        bounds.append(bounds[-1] + width)
    assert bounds[-1] == w_in.shape[1]
    w_q, w_kv, w_kr, w_rq, w_rk, w_rv, w_rg, w_ga, w_gb = (
        w_in[:, bounds[i]:bounds[i + 1]].astype(BF16) for i in range(9))

    w_uq3 = w_uq.reshape(Q_RANK, nh, hd)
    zeros = jnp.zeros((MLA_NOPE, KV_RANK), F32)
    w_uk_t = w_uk.transpose(1, 2, 0)
    w_uk_pair = jnp.stack([jnp.block([[w_uk_t[2 * p], zeros], [zeros, w_uk_t[2 * p + 1]]])
                           for p in range(nh // 2)]).astype(BF16)
    w_uv_t = w_uv.transpose(1, 0, 2)
    zv = jnp.zeros((KV_RANK, MLA_V), F32)
    w_uv_pair = jnp.stack([jnp.block([[w_uv_t[2 * p], zv], [zv, w_uv_t[2 * p + 1]]])
                           for p in range(nh // 2)]).astype(BF16)
    row = lambda v: v.reshape(1, -1)
    return dict(
        ffn1_norm=row(ffn1_norm), ffn1=bf(ffn1_gate, ffn1_up, ffn1_down),
        ffn2_norm=row(ffn2_norm), ffn2=bf(ffn2_gate, ffn2_up, ffn2_down), final_norm=row(final_norm),
        mix_norm=row(mix_norm), w_q=w_q, w_kv=w_kv, w_kr=jnp.tile(w_kr, (1, nh)), w_rq=w_rq, w_rk=w_rk, w_rv=w_rv,
        w_rg=w_rg, w_ga=w_ga, w_gb=w_gb, q_norm=row(q_norm), kv_norm=row(kv_norm),
        w_uq_nope=w_uq3[:, :, :MLA_NOPE].reshape(Q_RANK, nh * MLA_NOPE).astype(BF16),
        w_uq_rope=w_uq3[:, :, MLA_NOPE:].reshape(Q_RANK, nh * MLA_ROPE).astype(BF16),
        w_uk_pair=w_uk_pair, w_uv_pair=w_uv_pair, w_uv_all=w_uv.reshape(KV_RANK, nh * MLA_V).astype(BF16),
        ret_gn=row(ret_gn), w_mla_o=w_mla_o.astype(BF16), w_ret_o=w_ret_o.astype(BF16), w_out=w_out.astype(BF16),
    )


def _key_sequence(kt_tok, v_tok, kt_meta, v_meta, blk):
    b, kw, t = kt_tok.shape
    n_meta = kt_meta.shape[1]
    pad = blk - n_meta
    kt = jnp.concatenate([jnp.broadcast_to(kt_meta[None], (b, kw, n_meta)), kt_tok,
                          jnp.zeros((b, kw, pad), kt_tok.dtype)], axis=2)
    kt = kt.reshape(b, kw, t // blk + 1, blk).transpose(0, 2, 1, 3)
    v = jnp.concatenate([jnp.broadcast_to(v_meta[None], (b,) + v_meta.shape), v_tok,
                         jnp.zeros((b, pad, v_tok.shape[2]), v_tok.dtype)], axis=1)
    return kt, v


def kernel(x_prompt, x_sample, cache_ckv, cache_krope, state_ret, page_table, meta_tokens, ffn1_norm, ffn1_gate, ffn1_up, ffn1_down, mix_norm, w_in, q_norm, kv_norm, w_uq, w_uk, w_uv, w_mla_o, ret_gn, w_ret_o, w_out, ffn2_norm, ffn2_gate, ffn2_up, ffn2_down, final_norm):
    assert ffn1_gate.shape[0] == 1, "single-layer trunk"
    b, t, d = x_prompt.shape
    bd, dec_seq, _ = x_sample.shape
    assert dec_seq == 1
    n_meta = meta_tokens.shape[0]
    assert n_meta == N_META
    past_len = page_table.shape[1] * PAGE_SIZE
    W = _prep_weights(ffn1_norm[0], ffn1_gate[0], ffn1_up[0], ffn1_down[0], mix_norm[0], w_in[0], q_norm[0],
                      kv_norm[0], w_uq[0], w_uk[0], w_uv[0], w_mla_o[0], ret_gn[0], w_ret_o[0], w_out[0],
                      ffn2_norm[0], ffn2_gate[0], ffn2_up[0], ffn2_down[0], final_norm)

    rope_r, rope_q = _rope_tables(n_meta + jnp.arange(t))
    pos_small = jnp.concatenate([jnp.arange(n_meta), jnp.full((bd,), past_len)])
    rope_r_s, rope_q_s = _rope_tables(pos_small)
    x_small = jnp.concatenate([meta_tokens.astype(F32), x_sample.reshape(bd, d)], axis=0)

    h1 = _ffn(x_prompt.reshape(b * t, d), W["ffn1_norm"], *W["ffn1"])
    h1s = _ffn(x_small, W["ffn1_norm"], *W["ffn1"])
    P = _mix_in(h1.reshape(b, t, d), W, rope_r, rope_q)
    S = _mix_in(h1s.reshape(1, n_meta + bd, d), W, rope_r_s, rope_q_s)

    kt_seq, v_seq = _key_sequence(P["kt"], P["kcat"][:, :, :KV_RANK], S["kt"][0, :, :n_meta],
                                  S["kcat"][0, :n_meta, :KV_RANK], min(t, ATTN_BLOCK))
    apre = _prompt_attention(P["qlat"], P["qrope"], kt_seq, v_seq, W["w_uv_pair"])
    on, s_prompt = _prompt_retention(P["rq"], P["rk"], P["rv"], S["rk"][0, :n_meta], S["rv"][0, :n_meta])

    nh = MLA_HEADS
    qlat_s = S["qlat"][0, :, n_meta:].transpose(1, 0, 2)
    qrope_s = S["qrope"][0, n_meta:].reshape(bd, nh, MLA_ROPE)
    kself = S["kcat"][0, n_meta:, :KV_RANK + MLA_ROPE].reshape(bd, 1, KV_RANK + MLA_ROPE)
    apre_s = _decode_attention(page_table, qlat_s, qrope_s, kself, W["w_uv_all"], cache_ckv[0],
                               cache_krope[0].transpose(0, 2, 1))
    dec = lambda name: S[name][0, n_meta:].reshape(bd, 1, -1)
    on_s, s_sample = _decode_retention(dec("rq"), dec("rk"), dec("rv"), state_ret[0])

    flat = lambda a: a.reshape(b * t, -1)
    h2 = _mix_out(h1, flat(apre), flat(on), flat(P["srg"]), flat(P["sga"]), flat(P["sgb"]), W)
    y_prompt = _ffn(h2, W["ffn2_norm"], *W["ffn2"], final_norm=W["final_norm"])
    tail = lambda name: S[name][0, n_meta:]
    h2s = _mix_out(h1s[n_meta:], apre_s.reshape(bd, -1), on_s.reshape(bd, -1), tail("srg"), tail("sga"), tail("sgb"), W)
    y_sample = _ffn(h2s, W["ffn2_norm"], *W["ffn2"], final_norm=W["final_norm"])

    with_meta = lambda small, main: jnp.concatenate(
        [jnp.broadcast_to(small[:, :n_meta], (b, n_meta, small.shape[2])), main], axis=1)[None]
    return (
        y_prompt.reshape(b, t, d),
        y_sample.reshape(bd, 1, d),
        with_meta(S["ckv"], P["ckv"]),
        with_meta(S["kr"], P["kr"]),
        s_prompt[None],
        S["ckv"][0, n_meta:].reshape(1, bd, 1, KV_RANK),
        S["kr"][0, n_meta:].reshape(1, bd, 1, MLA_ROPE),
        s_sample[None],
    )
```

```python
import functools
import math

import jax
import jax.numpy as jnp
from jax import lax
from jax.experimental import pallas as pl
from jax.experimental.pallas import tpu as pltpu

N_META = 16
PAGE_SIZE = 128
MLA_HEADS = 8
MLA_NOPE = 64
MLA_ROPE = 32
MLA_V = 64
Q_RANK = 384
KV_RANK = 256
MLA_SCALE = (MLA_NOPE + MLA_ROPE) ** -0.5
RET_HEADS = 4
RET_DK = 128
RET_DV = 256
ROPE_BASE = 10000.0
NORM_EPS = 1e-6
RET_LOG_GAMMA = tuple(math.log1p(-(2.0 ** (-5.0 - h))) for h in range(RET_HEADS))

VMEM_LIMIT_BYTES = 56 * 1024 * 1024
FFN_CHUNK = 256
TOKEN_TILE = 512
ATTN_BLOCK = 256
ATTN_GROUP_HEADS = 2
ATTN_ROW_CHUNK = 32
RET_CHUNK = 256
DEC_PAGES_PER_STEP = 64
DEC_SUB_PAGES = 16
DEC_SLOTS = 2
MASK_VALUE = -1e30

F32 = jnp.float32
BF16 = jnp.bfloat16


def _params(*sem):
    return pltpu.CompilerParams(dimension_semantics=sem, vmem_limit_bytes=VMEM_LIMIT_BYTES)


def _const_spec(shape):
    n = len(shape)
    return pl.BlockSpec(shape, lambda *_: (0,) * n, pipeline_mode=pl.Buffered(1))


def _rms(x, g):
    return x * lax.rsqrt(jnp.mean(x * x, axis=-1, keepdims=True) + NORM_EPS) * g


def _sigmoid(x):
    return 1.0 / (1.0 + jnp.exp(-x))


def _dot(a, b):
    return jnp.dot(a, b, preferred_element_type=F32)


def _dot_nt(a, b):
    return lax.dot_general(a, b, (((1,), (1,)), ((), ())), preferred_element_type=F32)


def _ffn_body(x_ref, g_ref, wg_ref, wu_ref, wd_ref, *rest, final):
    o_ref = rest[-1]
    x = x_ref[...]
    u = _rms(x, g_ref[...]).astype(BF16)
    acc = None
    for c in range(0, wg_ref.shape[1], FFN_CHUNK):
        cols = slice(c, c + FFN_CHUNK)
        gate = _dot(u, wg_ref[:, cols])
        up = _dot(u, wu_ref[:, cols])
        act = (gate * _sigmoid(gate) * up).astype(BF16)
        part = _dot(act, wd_ref[cols, :])
        acc = part if acc is None else acc + part
    y = x + 0.5 * acc
    if final:
        y = _rms(y, rest[0][...])
    o_ref[...] = y


def _ffn(x, norm, wg, wu, wd, final_norm=None):
    m, d = x.shape
    tm = min(m, TOKEN_TILE)
    assert m % tm == 0 and wg.shape[1] % FFN_CHUNK == 0
    final = final_norm is not None
    in_specs = [pl.BlockSpec((tm, d), lambda i: (i, 0)), _const_spec((1, d)),
                _const_spec(wg.shape), _const_spec(wu.shape), _const_spec(wd.shape)]
    args = [x, norm, wg, wu, wd]
    if final:
        in_specs.append(_const_spec((1, d)))
        args.append(final_norm)
    return pl.pallas_call(
        functools.partial(_ffn_body, final=final),
        grid=(m // tm,),
        in_specs=in_specs,
        out_specs=pl.BlockSpec((tm, d), lambda i: (i, 0)),
        out_shape=jax.ShapeDtypeStruct((m, d), F32),
        compiler_params=_params("parallel"),
        name="ffn_final" if final else "ffn",
    )(*args)


def _rope32(x, cos, sin):
    lane = lax.broadcasted_iota(jnp.int32, x.shape, 1)
    width = x.shape[1]
    partner = jnp.where((lane & 31) < 16, pltpu.roll(x, width - 16, axis=1), pltpu.roll(x, 16, axis=1))
    return x * cos + partner * sin


def _mix_in_body(h_ref, g_ref, wq_ref, wkv_ref, wkr_ref, wrq_ref, wrk_ref, wrv_ref, wrg_ref, wga_ref, wgb_ref,
                 qn_ref, kvn_ref, wuqn_ref, wuqr_ref, wuk_ref, rope_r_ref, rope_q_ref, *rest, shifted):
    if shifted:
        (kprev_ref, vprev_ref, qlat_ref, qrope_ref, ckv_ref, kr_ref, kcat_ref, kt_ref, rq_ref, rk_ref, rv_ref,
         srg_ref, sga_ref, sgb_ref, kts_ref, vs_ref, kcarry_ref, vcarry_ref) = rest
    else:
        (qlat_ref, qrope_ref, ckv_ref, kr_ref, kcat_ref, kt_ref, rq_ref, rk_ref, rv_ref,
         srg_ref, sga_ref, sgb_ref) = rest
    u = _rms(h_ref[...], g_ref[...]).astype(BF16)
    cos_q, sin_q = rope_q_ref[0], rope_q_ref[1]
    cos_r, sin_r = rope_r_ref[0], rope_r_ref[1]

    cq = _rms(_dot(u, wq_ref[...]), qn_ref[...]).astype(BF16)
    q_nope = (_dot(cq, wuqn_ref[...]) * MLA_SCALE).astype(BF16)
    for p in range(MLA_HEADS // 2):
        pair = _dot(q_nope[:, p * 128:(p + 1) * 128], wuk_ref[p])
        qlat_ref[2 * p] = pair[:, :KV_RANK].astype(BF16)
        qlat_ref[2 * p + 1] = pair[:, KV_RANK:].astype(BF16)
    q_rope = _rope32(_dot(cq, wuqr_ref[...]), cos_q, sin_q) * MLA_SCALE
    qrope_ref[...] = q_rope.astype(BF16)

    ckv = _rms(_dot(u, wkv_ref[...]), kvn_ref[...])
    ckv_ref[...] = ckv
    kcat_ref[:, :KV_RANK] = ckv.astype(BF16)
    kr = _rope32(_dot(u, wkr_ref[...]), cos_q, sin_q)
    kr_ref[...] = kr[:, :MLA_ROPE]
    kcat_ref[:, KV_RANK:] = kr.astype(BF16)
    k_t = jnp.concatenate([ckv, kr], axis=1).T
    kt_ref[...] = k_t.astype(BF16)
    if shifted:
        @pl.when(pl.program_id(1) == 0)
        def _():
            kcarry_ref[...] = kprev_ref[...]
            vcarry_ref[...] = vprev_ref[...]

        keep = k_t.shape[1] - N_META
        k_sh = jnp.concatenate([kcarry_ref[...], k_t[:, :keep]], axis=1).astype(BF16)
        blk = kts_ref.shape[2]
        for j in range(kts_ref.shape[0]):
            kts_ref[j] = k_sh[:, j * blk:(j + 1) * blk]
        vs_ref[...] = jnp.concatenate([vcarry_ref[...], ckv[:keep]], axis=0).astype(BF16)
        kcarry_ref[...] = k_t[:, keep:]
        vcarry_ref[...] = ckv[keep:]

    rq = _dot(u, wrq_ref[...])
    rk = _dot(u, wrk_ref[...])
    for hd in range(RET_HEADS):
        sl = slice(hd * RET_DK, (hd + 1) * RET_DK)
        xq, xk = rq[:, sl], rk[:, sl]
        rq_ref[:, sl] = (xq * cos_r + pltpu.roll(xq, RET_DK // 2, axis=1) * sin_r).astype(BF16)
        rk_ref[:, sl] = ((xk * cos_r + pltpu.roll(xk, RET_DK // 2, axis=1) * sin_r) * (RET_DK ** -0.5)).astype(BF16)
    rv_ref[...] = _dot(u, wrv_ref[...]).astype(BF16)

    rg = _dot(u, wrg_ref[...])
    srg_ref[...] = (rg * _sigmoid(rg)).astype(BF16)
    sga_ref[...] = _sigmoid(_dot(u, wga_ref[...])).astype(BF16)
    sgb_ref[...] = _sigmoid(_dot(u, wgb_ref[...])).astype(BF16)


def _mix_in(h, W, rope_r, rope_q, prev_keys=None):
    b, t, d = h.shape
    tm = min(t, TOKEN_TILE)
    assert t % tm == 0
    hr = RET_HEADS * RET_DK
    hv = RET_HEADS * RET_DV
    qw = MLA_HEADS * MLA_ROPE
    shifted = prev_keys is not None
    weights = [W["mix_norm"], W["w_q"], W["w_kv"], W["w_kr"], W["w_rq"], W["w_rk"], W["w_rv"], W["w_rg"],
               W["w_ga"], W["w_gb"], W["q_norm"], W["kv_norm"], W["w_uq_nope"], W["w_uq_rope"], W["w_uk_pair"]]
    tok = lambda width: pl.BlockSpec((None, tm, width), lambda bi, ti: (bi, ti, 0))
    in_specs = ([tok(d)] + [_const_spec(w.shape) for w in weights]
                + [pl.BlockSpec((2, tm, RET_DK), lambda bi, ti: (0, ti, 0)),
                   pl.BlockSpec((2, tm, qw), lambda bi, ti: (0, ti, 0))])
    out_specs = [
        pl.BlockSpec((None, MLA_HEADS, tm, KV_RANK), lambda bi, ti: (bi, 0, ti, 0)),
        tok(qw), tok(KV_RANK), tok(MLA_ROPE), tok(2 * KV_RANK),
        pl.BlockSpec((None, 2 * KV_RANK, tm), lambda bi, ti: (bi, 0, ti)),
        tok(hr), tok(hr), tok(hv), tok(hv), tok(d), tok(d),
    ]
    sds = jax.ShapeDtypeStruct
    out_shape = [
        sds((b, MLA_HEADS, t, KV_RANK), BF16), sds((b, t, qw), BF16), sds((b, t, KV_RANK), F32),
        sds((b, t, MLA_ROPE), F32), sds((b, t, 2 * KV_RANK), BF16), sds((b, 2 * KV_RANK, t), BF16),
        sds((b, t, hr), BF16), sds((b, t, hr), BF16),
        sds((b, t, hv), BF16), sds((b, t, hv), BF16), sds((b, t, d), BF16), sds((b, t, d), BF16),
    ]
    names = ["qlat", "qrope", "ckv", "kr", "kcat", "kt", "rq", "rk", "rv", "srg", "sga", "sgb"]
    extra_in, scratch = [], []
    if shifted:
        blk = ATTN_BLOCK
        assert tm % blk == 0 and N_META % 8 == 0
        extra_in = list(prev_keys)
        in_specs += [_const_spec(a.shape) for a in extra_in]
        out_specs += [pl.BlockSpec((None, tm // blk, 2 * KV_RANK, blk), lambda bi, ti: (bi, ti, 0, 0)), tok(KV_RANK)]
        out_shape += [sds((b, t // blk, 2 * KV_RANK, blk), BF16), sds((b, t, KV_RANK), BF16)]
        names += ["kts", "vs"]
        scratch = [pltpu.VMEM((2 * KV_RANK, N_META), F32), pltpu.VMEM((N_META, KV_RANK), F32)]
    outs = pl.pallas_call(
        functools.partial(_mix_in_body, shifted=shifted),
        grid=(b, t // tm),
        in_specs=in_specs,
        out_specs=out_specs,
        out_shape=out_shape,
        scratch_shapes=scratch,
        compiler_params=_params("parallel", "arbitrary"),
        name="mixer_inputs",
    )(h, *weights, rope_r, rope_q, *extra_in)
    return dict(zip(names, outs))


def _softmax_step(s, k_lat, m_ref, l_ref, acc_ref):
    m_prev = m_ref[...]
    m_new = jnp.maximum(m_prev, jnp.max(s, axis=-1, keepdims=True))
    alpha = jnp.exp(m_prev - m_new)
    p = jnp.exp(s - m_new)
    l_ref[...] = alpha * l_ref[...] + jnp.sum(p, axis=-1, keepdims=True)
    acc_ref[...] = alpha * acc_ref[...] + _dot(p.astype(BF16), k_lat)
    m_ref[...] = m_new


def _lanes(x, width):
    return x if width == x.shape[1] else jnp.concatenate([x] * (width // x.shape[1]), axis=1)


def _attn_body(qlat_ref, qrope_ref, kt_ref, v_ref, ktail_ref, vtail_ref, wuv_ref, o_ref,
               qcat_ref, s_ref, p_ref, alpha_ref, m_ref, l_ref, acc_ref):
    i = pl.program_id(1)
    blk = qrope_ref.shape[0]
    grp = ATTN_GROUP_HEADS * blk
    n_grp = MLA_HEADS // ATTN_GROUP_HEADS

    lane_head = lax.broadcasted_iota(jnp.int32, (blk, MLA_HEADS * MLA_ROPE), 1) // MLA_ROPE
    qr = qrope_ref[...]
    for hd in range(MLA_HEADS):
        qcat_ref[hd * blk:(hd + 1) * blk, :KV_RANK] = qlat_ref[hd]
        qcat_ref[hd * blk:(hd + 1) * blk, KV_RANK:] = jnp.where(lane_head == hd, qr, jnp.zeros_like(qr))

    def blocks(j, n_blk, first):
        width = n_blk * blk
        v = v_ref[pl.ds(pl.multiple_of(j * blk, blk), width), :]
        if first:
            tok = lax.broadcasted_iota(jnp.int32, (ATTN_ROW_CHUNK, width), 0)
            key = lax.broadcasted_iota(jnp.int32, (ATTN_ROW_CHUNK, width), 1)
        for g in range(n_grp):
            rows = slice(g * grp, (g + 1) * grp)
            for n in range(n_blk):
                s_ref[rows, n * blk:(n + 1) * blk] = _dot(qcat_ref[rows, :], kt_ref[j + n])
        for g in range(n_grp):
            for c in range(g * grp, (g + 1) * grp, ATTN_ROW_CHUNK):
                r = slice(c, c + ATTN_ROW_CHUNK)
                s = s_ref[r, :width]
                if first:
                    s = jnp.where(key <= tok + (c % blk + N_META), s, MASK_VALUE)
                    m_new = jnp.broadcast_to(jnp.max(s, axis=-1, keepdims=True), (ATTN_ROW_CHUNK, 128))
                    p = jnp.exp(s - _lanes(m_new, width))
                    l_ref[r, :] = jnp.broadcast_to(jnp.sum(p, axis=-1, keepdims=True), (ATTN_ROW_CHUNK, 128))
                else:
                    m_prev = m_ref[r, :]
                    m_new = jnp.maximum(m_prev, jnp.max(s, axis=-1, keepdims=True))
                    alpha = jnp.exp(m_prev - m_new)
                    p = jnp.exp(s - _lanes(m_new, width))
                    l_ref[r, :] = alpha * l_ref[r, :] + jnp.sum(p, axis=-1, keepdims=True)
                    alpha_ref[r, :] = alpha
                m_ref[r, :] = m_new
                p_ref[r, :width] = p.astype(BF16)
            rows = slice(g * grp, (g + 1) * grp)
            pv = _dot(p_ref[rows, :width], v)
            acc_ref[rows, :] = pv if first else _lanes(alpha_ref[rows, :], KV_RANK) * acc_ref[rows, :] + pv

    blocks(i, 1, True)

    def two_full_blocks(jj, carry):
        blocks(2 * jj, 2, False)
        return carry

    lax.fori_loop(0, i // 2, two_full_blocks, 0)

    @pl.when(i % 2 == 1)
    def _():
        blocks(i - 1, 1, False)

    tails = [slice((hd + 1) * blk - N_META, (hd + 1) * blk) for hd in range(MLA_HEADS)]
    gather = lambda ref: jnp.concatenate([ref[r, :] for r in tails], axis=0)
    s = _dot(gather(qcat_ref), ktail_ref[...])
    q_off = lax.broadcasted_iota(jnp.int32, s.shape, 0) & (N_META - 1)
    col = lax.broadcasted_iota(jnp.int32, s.shape, 1) - (128 - N_META)
    s = jnp.where((col >= 0) & (col <= q_off), s, MASK_VALUE)
    m_prev = gather(m_ref)
    m_new = jnp.maximum(m_prev, jnp.max(s, axis=-1, keepdims=True))
    alpha = jnp.exp(m_prev - m_new)
    p = jnp.exp(s - m_new)
    l_new = alpha * gather(l_ref) + jnp.sum(p, axis=-1, keepdims=True)
    acc_new = _lanes(alpha, KV_RANK) * gather(acc_ref) + _dot(p.astype(BF16), vtail_ref[...])
    for hd, r in enumerate(tails):
        part = slice(hd * N_META, (hd + 1) * N_META)
        l_ref[r, :] = l_new[part]
        acc_ref[r, :] = acc_new[part]

    for pr in range(MLA_HEADS // 2):
        halves = []
        for hd in (2 * pr, 2 * pr + 1):
            rows = slice(hd * blk, (hd + 1) * blk)
            halves.append((acc_ref[rows, :] / _lanes(l_ref[rows, :], KV_RANK)).astype(BF16))
        o_ref[:, pr * 2 * MLA_V:(pr + 1) * 2 * MLA_V] = _dot(jnp.concatenate(halves, axis=1), wuv_ref[pr]).astype(BF16)


def _prompt_attention(qlat, qrope, kt_seq, v_seq, kt_tok, kcat, wuv_pair):
    b, nh, t, r = qlat.shape
    nq, blk = kt_seq.shape[1], kt_seq.shape[3]
    assert nq * blk == t and v_seq.shape[1] == t
    assert blk % 128 == 0 and blk & (blk - 1) == 0 and N_META <= 128
    rows = nh * blk
    last = blk // 128
    return pl.pallas_call(
        _attn_body,
        grid=(b, nq),
        in_specs=[
            pl.BlockSpec((None, nh, blk, r), lambda bi, i: (bi, 0, i, 0)),
            pl.BlockSpec((None, blk, qrope.shape[2]), lambda bi, i: (bi, i, 0)),
            pl.BlockSpec((None,) + kt_seq.shape[1:], lambda bi, i: (bi, 0, 0, 0)),
            pl.BlockSpec((None,) + v_seq.shape[1:], lambda bi, i: (bi, 0, 0)),
            pl.BlockSpec((None, kt_tok.shape[1], 128), lambda bi, i: (bi, 0, (i + 1) * last - 1)),
            pl.BlockSpec((None, 128, r), lambda bi, i: (bi, (i + 1) * last - 1, 0)),
            _const_spec(wuv_pair.shape),
        ],
        out_specs=pl.BlockSpec((None, blk, nh * MLA_V), lambda bi, i: (bi, i, 0)),
        out_shape=jax.ShapeDtypeStruct((b, t, nh * MLA_V), BF16),
        scratch_shapes=[
            pltpu.VMEM((rows, 2 * r), BF16),
            pltpu.VMEM((rows, 2 * blk), F32),
            pltpu.VMEM((rows, 2 * blk), BF16),
            pltpu.VMEM((rows, 128), F32),
            pltpu.VMEM((rows, 128), F32),
            pltpu.VMEM((rows, 128), F32),
            pltpu.VMEM((rows, r), F32),
        ],
        compiler_params=_params("parallel", "arbitrary"),
        name="prompt_attention",
    )(qlat, qrope, kt_seq, v_seq, kt_tok, kcat, wuv_pair)


def _group_norm(o):
    mu = jnp.mean(o, axis=-1, keepdims=True)
    c = o - mu
    return c * lax.rsqrt(jnp.mean(c * c, axis=-1, keepdims=True) + NORM_EPS)


def _ret_body(rq_ref, rk_ref, rv_ref, kmeta_ref, vmeta_ref, on_ref, sout_ref, s_ref):
    c = pl.program_id(1)
    ch = rq_ref.shape[0]
    n_meta = kmeta_ref.shape[0]

    @pl.when(c == 0)
    def _():
        j = lax.broadcasted_iota(jnp.int32, (n_meta, 1), 0).astype(F32)
        for hd in range(RET_HEADS):
            kd = kmeta_ref[:, hd * RET_DK:(hd + 1) * RET_DK].astype(F32) * jnp.exp(RET_LOG_GAMMA[hd] * (n_meta - 1.0 - j))
            s_ref[hd] = _dot(kd.T.astype(BF16), vmeta_ref[:, hd * RET_DV:(hd + 1) * RET_DV])

    row = lax.broadcasted_iota(jnp.int32, (ch, 1), 0).astype(F32)
    diff = (lax.broadcasted_iota(jnp.int32, (ch, ch), 0) - lax.broadcasted_iota(jnp.int32, (ch, ch), 1)).astype(F32)
    for hd in range(RET_HEADS):
        lg = RET_LOG_GAMMA[hd]
        q = rq_ref[:, hd * RET_DK:(hd + 1) * RET_DK]
        k = rk_ref[:, hd * RET_DK:(hd + 1) * RET_DK]
        v = rv_ref[:, hd * RET_DV:(hd + 1) * RET_DV]
        decay = jnp.where(diff >= 0, jnp.exp(lg * jnp.maximum(diff, 0.0)), 0.0)
        scores = _dot_nt(q, k) * decay
        s_old = s_ref[hd]
        o = _dot(scores.astype(BF16), v) + _dot(q, s_old.astype(BF16)) * jnp.exp(lg * (row + 1.0))
        kd = k.astype(F32) * jnp.exp(lg * (ch - 1.0 - row))
        s_ref[hd] = math.exp(lg * ch) * s_old + _dot(kd.T.astype(BF16), v)
        on_ref[:, hd * RET_DV:(hd + 1) * RET_DV] = _group_norm(o).astype(BF16)

    @pl.when(c == pl.num_programs(1) - 1)
    def _():
        sout_ref[...] = s_ref[...]


def _prompt_retention(rq, rk, rv, kmeta, vmeta):
    b, t, hr = rq.shape
    hv = rv.shape[2]
    ch = min(t, RET_CHUNK)
    assert t % ch == 0
    tok = lambda width: pl.BlockSpec((None, ch, width), lambda bi, ci: (bi, ci, 0))
    return pl.pallas_call(
        _ret_body,
        grid=(b, t // ch),
        in_specs=[tok(hr), tok(hr), tok(hv), _const_spec(kmeta.shape), _const_spec(vmeta.shape)],
        out_specs=[tok(hv), pl.BlockSpec((None, RET_HEADS, RET_DK, RET_DV), lambda bi, ci: (bi, 0, 0, 0))],
        out_shape=[jax.ShapeDtypeStruct((b, t, hv), BF16),
                   jax.ShapeDtypeStruct((b, RET_HEADS, RET_DK, RET_DV), F32)],
        scratch_shapes=[pltpu.VMEM((RET_HEADS, RET_DK, RET_DV), F32)],
        compiler_params=_params("parallel", "arbitrary"),
        name="prompt_retention",
    )(rq, rk, rv, kmeta, vmeta)


def _dec_attn_body(pt_ref, qlat_ref, qrope_ref, kself_ref, wuv_ref, ckv_hbm, kr_hbm, o_ref,
                   kbuf, krbuf, sem_k, sem_r, m_ref, l_ref, acc_ref, *, pages, n_chunks):
    b = pl.program_id(0)
    n_req = pl.num_programs(0)
    ahead = DEC_SLOTS - 1
    ql = qlat_ref[...]
    qr = qrope_ref[...]

    def page_copies(page, slot, p):
        lat = pltpu.make_async_copy(ckv_hbm.at[page], kbuf.at[slot, p], sem_k.at[slot])
        rope = pltpu.make_async_copy(kr_hbm.at[page], krbuf.at[slot, p], sem_r.at[slot])
        return lat, rope

    def start_chunk(req, chunk, slot):
        def one(p, carry):
            for cp in page_copies(pt_ref[req, chunk * pages + p], slot, p):
                cp.start()
            return carry
        lax.fori_loop(0, pages, one, 0)

    def wait_chunk(slot):
        def one(p, carry):
            for cp in page_copies(0, slot, p):
                cp.wait()
            return carry
        lax.fori_loop(0, pages, one, 0)

    @pl.when(b == 0)
    def _():
        for g in range(ahead):
            start_chunk(0, g, g)

    ks = kself_ref[...].astype(F32)
    qf = jnp.concatenate([ql, qr], axis=1).astype(F32)
    m_ref[...] = jnp.sum(qf * ks, axis=-1, keepdims=True)
    l_ref[...] = jnp.ones(l_ref.shape, F32)
    acc_ref[...] = jnp.broadcast_to(ks[:, :KV_RANK], acc_ref.shape)

    sub = min(pages, DEC_SUB_PAGES)
    n_sub = pages // sub
    sub_keys = sub * PAGE_SIZE

    def ring_round(rnd, carry):
        for slot in range(DEC_SLOTS):
            chunk = rnd * DEC_SLOTS + slot
            nxt = chunk + ahead
            wrap = nxt >= n_chunks
            nxt_req = jnp.where(wrap, b + 1, b)

            @pl.when(nxt_req < n_req)
            def _():
                start_chunk(nxt_req, jnp.where(wrap, nxt - n_chunks, nxt), (slot + ahead) % DEC_SLOTS)

            wait_chunk(slot)
            k_lat, scores = [], []
            for q in range(n_sub):
                k_q = kbuf[slot, q * sub:(q + 1) * sub].reshape(sub_keys, KV_RANK).astype(BF16)
                kr_q = jnp.concatenate([krbuf[slot, q * sub + j] for j in range(sub)], axis=1).astype(BF16)
                k_lat.append(k_q)
                scores.append(_dot_nt(ql, k_q) + _dot(qr, kr_q))
            s = jnp.concatenate(scores, axis=1)
            m_prev = m_ref[...]
            m_new = jnp.maximum(m_prev, jnp.max(s, axis=-1, keepdims=True))
            alpha = jnp.exp(m_prev - m_new)
            p = jnp.exp(s - m_new)
            l_ref[...] = alpha * l_ref[...] + jnp.sum(p, axis=-1, keepdims=True)
            p = p.astype(BF16)
            pv = _dot(p[:, :sub_keys], k_lat[0])
            for q in range(1, n_sub):
                pv = pv + _dot(p[:, q * sub_keys:(q + 1) * sub_keys], k_lat[q])
            acc_ref[...] = alpha * acc_ref[...] + pv
            m_ref[...] = m_new
        return carry

    lax.fori_loop(0, n_chunks // DEC_SLOTS, ring_round, 0)

    o = (acc_ref[...] / l_ref[...]).astype(BF16)
    z = _dot(o, wuv_ref[...])
    head = lax.broadcasted_iota(jnp.int32, z.shape, 0)
    col_head = lax.broadcasted_iota(jnp.int32, z.shape, 1) // MLA_V
    o_ref[...] = jnp.sum(jnp.where(head == col_head, z, 0.0), axis=0, keepdims=True).astype(BF16)


def _decode_attention(page_table, qlat, qrope, kself, wuv_all, cache_ckv, cache_kr_t):
    bd, nh, r = qlat.shape
    n_pages = page_table.shape[1]
    pages = min(n_pages // DEC_SLOTS, DEC_PAGES_PER_STEP)
    n_chunks = n_pages // pages
    assert n_chunks * pages == n_pages and n_chunks % DEC_SLOTS == 0 and pages % min(pages, DEC_SUB_PAGES) == 0

    per_req = lambda shape: pl.BlockSpec((None,) + shape, lambda b, pt: (b, 0, 0))
    grid_spec = pltpu.PrefetchScalarGridSpec(
        num_scalar_prefetch=1,
        grid=(bd,),
        in_specs=[per_req((nh, r)), per_req((nh, MLA_ROPE)), per_req((1, r + MLA_ROPE)),
                  pl.BlockSpec(wuv_all.shape, lambda b, pt: (0, 0)),
                  pl.BlockSpec(memory_space=pl.ANY), pl.BlockSpec(memory_space=pl.ANY)],
        out_specs=per_req((1, nh * MLA_V)),
        scratch_shapes=[
            pltpu.VMEM((DEC_SLOTS, pages, PAGE_SIZE, r), F32),
            pltpu.VMEM((DEC_SLOTS, pages, MLA_ROPE, PAGE_SIZE), F32),
            pltpu.SemaphoreType.DMA((DEC_SLOTS,)),
            pltpu.SemaphoreType.DMA((DEC_SLOTS,)),
            pltpu.VMEM((nh, 1), F32), pltpu.VMEM((nh, 1), F32), pltpu.VMEM((nh, r), F32),
        ],
    )
    return pl.pallas_call(
        functools.partial(_dec_attn_body, pages=pages, n_chunks=n_chunks),
        grid_spec=grid_spec,
        out_shape=jax.ShapeDtypeStruct((bd, 1, nh * MLA_V), BF16),
        compiler_params=_params("arbitrary"),
        name="decode_attention",
    )(page_table, qlat, qrope, kself, wuv_all, cache_ckv, cache_kr_t)


def _dec_ret_body(rq_ref, rk_ref, rv_ref, s_ref, on_ref, snew_ref):
    eye = (lax.broadcasted_iota(jnp.int32, (RET_DK, RET_DK), 0)
           == lax.broadcasted_iota(jnp.int32, (RET_DK, RET_DK), 1))

    def column(x):
        return jnp.sum(jnp.where(eye, jnp.broadcast_to(x, (RET_DK, RET_DK)), 0.0), axis=1, keepdims=True)

    for hd in range(RET_HEADS):
        gamma = math.exp(RET_LOG_GAMMA[hd])
        q = rq_ref[:, hd * RET_DK:(hd + 1) * RET_DK].astype(F32)
        k = rk_ref[:, hd * RET_DK:(hd + 1) * RET_DK].astype(F32)
        v = rv_ref[:, hd * RET_DV:(hd + 1) * RET_DV].astype(F32)
        s_old = s_ref[hd]
        qs = jnp.sum(column(q) * s_old, axis=0, keepdims=True)
        o = jnp.sum(q * k, axis=-1, keepdims=True) * v + qs * gamma
        snew_ref[hd] = gamma * s_old + column(k) * v
        on_ref[:, hd * RET_DV:(hd + 1) * RET_DV] = _group_norm(o).astype(BF16)


def _decode_retention(rq, rk, rv, state):
    bd = rq.shape[0]
    row = lambda width: pl.BlockSpec((None, 1, width), lambda b: (b, 0, 0))
    st = pl.BlockSpec((None, RET_HEADS, RET_DK, RET_DV), lambda b: (b, 0, 0, 0))
    return pl.pallas_call(
        _dec_ret_body,
        grid=(bd,),
        in_specs=[row(rq.shape[2]), row(rk.shape[2]), row(rv.shape[2]), st],
        out_specs=[row(rv.shape[2]), st],
        out_shape=[jax.ShapeDtypeStruct((bd, 1, rv.shape[2]), BF16), jax.ShapeDtypeStruct(state.shape, F32)],
        compiler_params=_params("parallel"),
        name="decode_retention",
    )(rq, rk, rv, state)


def _mix_out_body(h_ref, a_ref, on_ref, srg_ref, sga_ref, sgb_ref, gn_ref, wa_ref, wr_ref, wo_ref, o_ref):
    a = _dot(a_ref[...], wa_ref[...])
    gated = srg_ref[...].astype(F32) * (on_ref[...].astype(F32) * gn_ref[...])
    r = _dot(gated.astype(BF16), wr_ref[...])
    m = sga_ref[...].astype(F32) * a + sgb_ref[...].astype(F32) * r
    o_ref[...] = h_ref[...] + _dot(m.astype(BF16), wo_ref[...])


def _mix_out(h, apre, on, srg, sga, sgb, W):
    m, d = h.shape
    tm = min(m, TOKEN_TILE)
    assert m % tm == 0
    tok = lambda width: pl.BlockSpec((tm, width), lambda i: (i, 0))
    weights = [W["ret_gn"], W["w_mla_o"], W["w_ret_o"], W["w_out"]]
    return pl.pallas_call(
        _mix_out_body,
        grid=(m // tm,),
        in_specs=[tok(d), tok(apre.shape[1]), tok(on.shape[1]), tok(d), tok(d), tok(d)]
                 + [_const_spec(w.shape) for w in weights],
        out_specs=tok(d),
        out_shape=jax.ShapeDtypeStruct((m, d), F32),
        compiler_params=_params("parallel"),
        name="mixer_output",
    )(h, apre, on, srg, sga, sgb, *weights)


def _rope_tables(pos):
    pos = pos.astype(F32)[:, None]

    def table(width, reps):
        half = width // 2
        inv_freq = ROPE_BASE ** (-jnp.arange(half, dtype=F32) / half)
        ang = pos * inv_freq[None, :]
        cos, sin = jnp.cos(ang), jnp.sin(ang)
        return jnp.stack([jnp.tile(jnp.concatenate([cos, cos], axis=1), (1, reps)),
                          jnp.tile(jnp.concatenate([-sin, sin], axis=1), (1, reps))])

    return table(RET_DK, 1), table(MLA_ROPE, MLA_HEADS)


def _prep_weights(ffn1_norm, ffn1_gate, ffn1_up, ffn1_down, mix_norm, w_in, q_norm, kv_norm, w_uq, w_uk, w_uv,
                  w_mla_o, ret_gn, w_ret_o, w_out, ffn2_norm, ffn2_gate, ffn2_up, ffn2_down, final_norm):
    d = w_in.shape[0]
    hr, hv = RET_HEADS * RET_DK, RET_HEADS * RET_DV
    nh, hd = MLA_HEADS, MLA_NOPE + MLA_ROPE
    bf = lambda *ws: tuple(w.astype(BF16) for w in ws)

    bounds = [0]
    for width in (Q_RANK, KV_RANK, MLA_ROPE, hr, hr, hv, hv, d, d):
        bounds.append(bounds[-1] + width)
    assert bounds[-1] == w_in.shape[1]
    w_q, w_kv, w_kr, w_rq, w_rk, w_rv, w_rg, w_ga, w_gb = (
        w_in[:, bounds[i]:bounds[i + 1]].astype(BF16) for i in range(9))

    w_uq3 = w_uq.reshape(Q_RANK, nh, hd)
    zeros = jnp.zeros((MLA_NOPE, KV_RANK), F32)
    w_uk_t = w_uk.transpose(1, 2, 0)
    w_uk_pair = jnp.stack([jnp.block([[w_uk_t[2 * p], zeros], [zeros, w_uk_t[2 * p + 1]]])
                           for p in range(nh // 2)]).astype(BF16)
    w_uv_t = w_uv.transpose(1, 0, 2)
    zv = jnp.zeros((KV_RANK, MLA_V), F32)
    w_uv_pair = jnp.stack([jnp.block([[w_uv_t[2 * p], zv], [zv, w_uv_t[2 * p + 1]]])
                           for p in range(nh // 2)]).astype(BF16)
    row = lambda v: v.reshape(1, -1)
    return dict(
        ffn1_norm=row(ffn1_norm), ffn1=bf(ffn1_gate, ffn1_up, ffn1_down),
        ffn2_norm=row(ffn2_norm), ffn2=bf(ffn2_gate, ffn2_up, ffn2_down), final_norm=row(final_norm),
        mix_norm=row(mix_norm), w_q=w_q, w_kv=w_kv, w_kr=jnp.tile(w_kr, (1, nh)), w_rq=w_rq, w_rk=w_rk, w_rv=w_rv,
        w_rg=w_rg, w_ga=w_ga, w_gb=w_gb, q_norm=row(q_norm), kv_norm=row(kv_norm),
        w_uq_nope=w_uq3[:, :, :MLA_NOPE].reshape(Q_RANK, nh * MLA_NOPE).astype(BF16),
        w_uq_rope=w_uq3[:, :, MLA_NOPE:].reshape(Q_RANK, nh * MLA_ROPE).astype(BF16),
        w_uk_pair=w_uk_pair, w_uv_pair=w_uv_pair, w_uv_all=w_uv.reshape(KV_RANK, nh * MLA_V).astype(BF16),
        ret_gn=row(ret_gn), w_mla_o=w_mla_o.astype(BF16), w_ret_o=w_ret_o.astype(BF16), w_out=w_out.astype(BF16),
    )


def kernel(x_prompt, x_sample, cache_ckv, cache_krope, state_ret, page_table, meta_tokens, ffn1_norm, ffn1_gate, ffn1_up, ffn1_down, mix_norm, w_in, q_norm, kv_norm, w_uq, w_uk, w_uv, w_mla_o, ret_gn, w_ret_o, w_out, ffn2_norm, ffn2_gate, ffn2_up, ffn2_down, final_norm):
    assert ffn1_gate.shape[0] == 1, "single-layer trunk"
    b, t, d = x_prompt.shape
    bd, dec_seq, _ = x_sample.shape
    assert dec_seq == 1
    n_meta = meta_tokens.shape[0]
    assert n_meta == N_META
    past_len = page_table.shape[1] * PAGE_SIZE
    W = _prep_weights(ffn1_norm[0], ffn1_gate[0], ffn1_up[0], ffn1_down[0], mix_norm[0], w_in[0], q_norm[0],
                      kv_norm[0], w_uq[0], w_uk[0], w_uv[0], w_mla_o[0], ret_gn[0], w_ret_o[0], w_out[0],
                      ffn2_norm[0], ffn2_gate[0], ffn2_up[0], ffn2_down[0], final_norm)

    rope_r, rope_q = _rope_tables(n_meta + jnp.arange(t))
    pos_small = jnp.concatenate([jnp.arange(n_meta), jnp.full((bd,), past_len)])
    rope_r_s, rope_q_s = _rope_tables(pos_small)
    x_small = jnp.concatenate([meta_tokens.astype(F32), x_sample.reshape(bd, d)], axis=0)

    h1 = _ffn(x_prompt.reshape(b * t, d), W["ffn1_norm"], *W["ffn1"])
    h1s = _ffn(x_small, W["ffn1_norm"], *W["ffn1"])
    S = _mix_in(h1s.reshape(1, n_meta + bd, d), W, rope_r_s, rope_q_s)
    meta_keys = (S["kt"][0, :, :n_meta].astype(F32), S["kcat"][0, :n_meta, :KV_RANK].astype(F32))
    P = _mix_in(h1.reshape(b, t, d), W, rope_r, rope_q, prev_keys=meta_keys)

    apre = _prompt_attention(P["qlat"], P["qrope"], P["kts"], P["vs"], P["kt"], P["kcat"], W["w_uv_pair"])
    on, s_prompt = _prompt_retention(P["rq"], P["rk"], P["rv"], S["rk"][0, :n_meta], S["rv"][0, :n_meta])

    nh = MLA_HEADS
    qlat_s = S["qlat"][0, :, n_meta:].transpose(1, 0, 2)
    qrope_s = S["qrope"][0, n_meta:].reshape(bd, nh, MLA_ROPE)
    kself = S["kcat"][0, n_meta:, :KV_RANK + MLA_ROPE].reshape(bd, 1, KV_RANK + MLA_ROPE)
    apre_s = _decode_attention(page_table, qlat_s, qrope_s, kself, W["w_uv_all"], cache_ckv[0],
                               cache_krope[0].transpose(0, 2, 1))
    dec = lambda name: S[name][0, n_meta:].reshape(bd, 1, -1)
    on_s, s_sample = _decode_retention(dec("rq"), dec("rk"), dec("rv"), state_ret[0])

    flat = lambda a: a.reshape(b * t, -1)
    h2 = _mix_out(h1, flat(apre), flat(on), flat(P["srg"]), flat(P["sga"]), flat(P["sgb"]), W)
    y_prompt = _ffn(h2, W["ffn2_norm"], *W["ffn2"], final_norm=W["final_norm"])
    tail = lambda name: S[name][0, n_meta:]
    h2s = _mix_out(h1s[n_meta:], apre_s.reshape(bd, -1), on_s.reshape(bd, -1), tail("srg"), tail("sga"), tail("sgb"), W)
    y_sample = _ffn(h2s, W["ffn2_norm"], *W["ffn2"], final_norm=W["final_norm"])

    with_meta = lambda small, main: jnp.concatenate(
        [jnp.broadcast_to(small[:, :n_meta], (b, n_meta, small.shape[2])), main], axis=1)[None]
    return (
        y_prompt.reshape(b, t, d),
        y_sample.reshape(bd, 1, d),
        with_meta(S["ckv"], P["ckv"]),
        with_meta(S["kr"], P["kr"]),
        s_prompt[None],
        S["ckv"][0, n_meta:].reshape(1, bd, 1, KV_RANK),
        S["kr"][0, n_meta:].reshape(1, bd, 1, MLA_ROPE),
        s_sample[None],
    )
```

```python
import functools
import math

import jax
import jax.numpy as jnp
from jax import lax
from jax.experimental import pallas as pl
from jax.experimental.pallas import tpu as pltpu

N_META = 16
PAGE_SIZE = 128
MLA_HEADS = 8
MLA_NOPE = 64
MLA_ROPE = 32
MLA_V = 64
Q_RANK = 384
KV_RANK = 256
MLA_SCALE = (MLA_NOPE + MLA_ROPE) ** -0.5
RET_HEADS = 4
RET_DK = 128
RET_DV = 256
ROPE_BASE = 10000.0
NORM_EPS = 1e-6
RET_LOG_GAMMA = tuple(math.log1p(-(2.0 ** (-5.0 - h))) for h in range(RET_HEADS))

VMEM_LIMIT_BYTES = 56 * 1024 * 1024
FFN_CHUNK = 256
TOKEN_TILE = 512
ATTN_BLOCK = 256
ATTN_GROUP_HEADS = 2
ATTN_ROW_CHUNK = 32
RET_CHUNK = 256
DEC_PAGES_PER_STEP = 64
DEC_SUB_PAGES = 16
DEC_SLOTS = 2
MASK_VALUE = -1e30

F32 = jnp.float32
BF16 = jnp.bfloat16


def _params(*sem):
    return pltpu.CompilerParams(dimension_semantics=sem, vmem_limit_bytes=VMEM_LIMIT_BYTES)


def _const_spec(shape):
    n = len(shape)
    return pl.BlockSpec(shape, lambda *_: (0,) * n, pipeline_mode=pl.Buffered(1))


def _rms(x, g):
    return x * lax.rsqrt(jnp.mean(x * x, axis=-1, keepdims=True) + NORM_EPS) * g


def _sigmoid(x):
    return 1.0 / (1.0 + jnp.exp(-x))


def _dot(a, b):
    return jnp.dot(a, b, preferred_element_type=F32)


def _dot_nt(a, b):
    return lax.dot_general(a, b, (((1,), (1,)), ((), ())), preferred_element_type=F32)


def _ffn_body(x_ref, g_ref, wg_ref, wu_ref, wd_ref, *rest, final):
    o_ref = rest[-1]
    x = x_ref[...]
    u = _rms(x, g_ref[...]).astype(BF16)
    acc = None
    for c in range(0, wg_ref.shape[1], FFN_CHUNK):
        cols = slice(c, c + FFN_CHUNK)
        gate = _dot(u, wg_ref[:, cols].astype(BF16))
        up = _dot(u, wu_ref[:, cols].astype(BF16))
        act = (gate * _sigmoid(gate) * up).astype(BF16)
        part = _dot(act, wd_ref[cols, :].astype(BF16))
        acc = part if acc is None else acc + part
    y = x + 0.5 * acc
    if final:
        y = _rms(y, rest[0][...])
    o_ref[...] = y


def _ffn(x, norm, wg, wu, wd, final_norm=None):
    m, d = x.shape
    tm = min(m, TOKEN_TILE)
    assert m % tm == 0 and wg.shape[1] % FFN_CHUNK == 0
    final = final_norm is not None
    in_specs = [pl.BlockSpec((tm, d), lambda i: (i, 0)), _const_spec((1, d)),
                _const_spec(wg.shape), _const_spec(wu.shape), _const_spec(wd.shape)]
    args = [x, norm, wg, wu, wd]
    if final:
        in_specs.append(_const_spec((1, d)))
        args.append(final_norm)
    return pl.pallas_call(
        functools.partial(_ffn_body, final=final),
        grid=(m // tm,),
        in_specs=in_specs,
        out_specs=pl.BlockSpec((tm, d), lambda i: (i, 0)),
        out_shape=jax.ShapeDtypeStruct((m, d), F32),
        compiler_params=_params("parallel"),
        name="ffn_final" if final else "ffn",
    )(*args)


def _rope32(x, cos, sin):
    lane = lax.broadcasted_iota(jnp.int32, x.shape, 1)
    width = x.shape[1]
    partner = jnp.where((lane & 31) < 16, pltpu.roll(x, width - 16, axis=1), pltpu.roll(x, 16, axis=1))
    return x * cos + partner * sin


def _mix_in_body(h_ref, g_ref, wq_ref, wkv_ref, wkr_ref, wrq_ref, wrk_ref, wrv_ref, wrg_ref, wga_ref, wgb_ref,
                 qn_ref, kvn_ref, wuqn_ref, wuqr_ref, wuk_ref, rope_r_ref, rope_q_ref, *rest, shifted):
    if shifted:
        (kprev_ref, vprev_ref, qlat_ref, qrope_ref, ckv_ref, kr_ref, kcat_ref, kt_ref, rq_ref, rk_ref, rv_ref,
         srg_ref, sga_ref, sgb_ref, kts_ref, vs_ref, kcarry_ref, vcarry_ref) = rest
    else:
        (qlat_ref, qrope_ref, ckv_ref, kr_ref, kcat_ref, kt_ref, rq_ref, rk_ref, rv_ref,
         srg_ref, sga_ref, sgb_ref) = rest
    if shifted:
        @pl.when(pl.program_id(1) == 0)
        def _():
            kcarry_ref[...] = kprev_ref[...]
            vcarry_ref[...] = vprev_ref[...]

    u = _rms(h_ref[...], g_ref[...]).astype(BF16)
    cos_q, sin_q = rope_q_ref[0], rope_q_ref[1]
    cos_r, sin_r = rope_r_ref[0], rope_r_ref[1]

    cq = _rms(_dot(u, wq_ref[...]), qn_ref[...]).astype(BF16)
    q_nope = (_dot(cq, wuqn_ref[...]) * MLA_SCALE).astype(BF16)
    for p in range(MLA_HEADS // 2):
        pair = _dot(q_nope[:, p * 128:(p + 1) * 128], wuk_ref[p])
        qlat_ref[2 * p] = pair[:, :KV_RANK].astype(BF16)
        qlat_ref[2 * p + 1] = pair[:, KV_RANK:].astype(BF16)
    q_rope = _rope32(_dot(cq, wuqr_ref[...]), cos_q, sin_q) * MLA_SCALE
    qrope_ref[...] = q_rope.astype(BF16)

    ckv = _rms(_dot(u, wkv_ref[...]), kvn_ref[...])
    ckv_ref[...] = ckv
    kcat_ref[:, :KV_RANK] = ckv.astype(BF16)
    kr = _rope32(_dot(u, wkr_ref[...]), cos_q, sin_q)
    kr_ref[...] = kr[:, :MLA_ROPE]
    kcat_ref[:, KV_RANK:] = kr.astype(BF16)
    k_t = jnp.concatenate([ckv, kr], axis=1).T
    kt_ref[...] = k_t.astype(BF16)
    if shifted:
        keep = k_t.shape[1] - N_META
        k_sh = jnp.concatenate([kcarry_ref[...], k_t[:, :keep]], axis=1).astype(BF16)
        blk = kts_ref.shape[2]
        for j in range(kts_ref.shape[0]):
            kts_ref[j] = k_sh[:, j * blk:(j + 1) * blk]
        vs_ref[...] = jnp.concatenate([vcarry_ref[...], ckv[:keep]], axis=0).astype(BF16)
        kcarry_ref[...] = k_t[:, keep:]
        vcarry_ref[...] = ckv[keep:]

    rq = _dot(u, wrq_ref[...])
    rk = _dot(u, wrk_ref[...])
    for hd in range(RET_HEADS):
        sl = slice(hd * RET_DK, (hd + 1) * RET_DK)
        xq, xk = rq[:, sl], rk[:, sl]
        rq_ref[:, sl] = (xq * cos_r + pltpu.roll(xq, RET_DK // 2, axis=1) * sin_r).astype(BF16)
        rk_ref[:, sl] = ((xk * cos_r + pltpu.roll(xk, RET_DK // 2, axis=1) * sin_r) * (RET_DK ** -0.5)).astype(BF16)
    rv_ref[...] = _dot(u, wrv_ref[...]).astype(BF16)

    rg = _dot(u, wrg_ref[...])
    srg_ref[...] = (rg * _sigmoid(rg)).astype(BF16)
    sga_ref[...] = _sigmoid(_dot(u, wga_ref[...])).astype(BF16)
    sgb_ref[...] = _sigmoid(_dot(u, wgb_ref[...])).astype(BF16)


def _mix_in(h, W, rope_r, rope_q, prev_keys=None):
    b, t, d = h.shape
    tm = min(t, TOKEN_TILE)
    assert t % tm == 0
    hr = RET_HEADS * RET_DK
    hv = RET_HEADS * RET_DV
    qw = MLA_HEADS * MLA_ROPE
    shifted = prev_keys is not None
    weights = [W["mix_norm"], W["w_q"], W["w_kv"], W["w_kr"], W["w_rq"], W["w_rk"], W["w_rv"], W["w_rg"],
               W["w_ga"], W["w_gb"], W["q_norm"], W["kv_norm"], W["w_uq_nope"], W["w_uq_rope"], W["w_uk_pair"]]
    tok = lambda width: pl.BlockSpec((None, tm, width), lambda bi, ti: (bi, ti, 0))
    in_specs = ([tok(d)] + [_const_spec(w.shape) for w in weights]
                + [pl.BlockSpec((2, tm, RET_DK), lambda bi, ti: (0, ti, 0)),
                   pl.BlockSpec((2, tm, qw), lambda bi, ti: (0, ti, 0))])
    out_specs = [
        pl.BlockSpec((None, MLA_HEADS, tm, KV_RANK), lambda bi, ti: (bi, 0, ti, 0)),
        tok(qw), tok(KV_RANK), tok(MLA_ROPE), tok(2 * KV_RANK),
        pl.BlockSpec((None, 2 * KV_RANK, tm), lambda bi, ti: (bi, 0, ti)),
        tok(hr), tok(hr), tok(hv), tok(hv), tok(d), tok(d),
    ]
    sds = jax.ShapeDtypeStruct
    out_shape = [
        sds((b, MLA_HEADS, t, KV_RANK), BF16), sds((b, t, qw), BF16), sds((b, t, KV_RANK), F32),
        sds((b, t, MLA_ROPE), F32), sds((b, t, 2 * KV_RANK), BF16), sds((b, 2 * KV_RANK, t), BF16),
        sds((b, t, hr), BF16), sds((b, t, hr), BF16),
        sds((b, t, hv), BF16), sds((b, t, hv), BF16), sds((b, t, d), BF16), sds((b, t, d), BF16),
    ]
    names = ["qlat", "qrope", "ckv", "kr", "kcat", "kt", "rq", "rk", "rv", "srg", "sga", "sgb"]
    extra_in, scratch = [], []
    if shifted:
        blk = ATTN_BLOCK
        assert tm % blk == 0 and N_META % 8 == 0
        extra_in = list(prev_keys)
        in_specs += [_const_spec(a.shape) for a in extra_in]
        out_specs += [pl.BlockSpec((None, tm // blk, 2 * KV_RANK, blk), lambda bi, ti: (bi, ti, 0, 0)), tok(KV_RANK)]
        out_shape += [sds((b, t // blk, 2 * KV_RANK, blk), BF16), sds((b, t, KV_RANK), BF16)]
        names += ["kts", "vs"]
        scratch = [pltpu.VMEM((2 * KV_RANK, N_META), F32), pltpu.VMEM((N_META, KV_RANK), F32)]
    outs = pl.pallas_call(
        functools.partial(_mix_in_body, shifted=shifted),
        grid=(b, t // tm),
        in_specs=in_specs,
        out_specs=out_specs,
        out_shape=out_shape,
        scratch_shapes=scratch,
        compiler_params=_params("parallel", "arbitrary"),
        name="mixer_inputs",
    )(h, *weights, rope_r, rope_q, *extra_in)
    return dict(zip(names, outs))


def _softmax_step(s, k_lat, m_ref, l_ref, acc_ref):
    m_prev = m_ref[...]
    m_new = jnp.maximum(m_prev, jnp.max(s, axis=-1, keepdims=True))
    alpha = jnp.exp(m_prev - m_new)
    p = jnp.exp(s - m_new)
    l_ref[...] = alpha * l_ref[...] + jnp.sum(p, axis=-1, keepdims=True)
    acc_ref[...] = alpha * acc_ref[...] + _dot(p.astype(BF16), k_lat)
    m_ref[...] = m_new


def _lanes(x, width):
    return x if width == x.shape[1] else jnp.concatenate([x] * (width // x.shape[1]), axis=1)


def _attn_body(qlat_ref, qrope_ref, kt_ref, v_ref, ktail_ref, vtail_ref, wuv_ref, o_ref,
               qcat_ref, s_ref, p_ref, alpha_ref, m_ref, l_ref, acc_ref):
    i = pl.program_id(1)
    blk = qrope_ref.shape[0]
    grp = ATTN_GROUP_HEADS * blk
    n_grp = MLA_HEADS // ATTN_GROUP_HEADS

    lane_head = lax.broadcasted_iota(jnp.int32, (blk, MLA_HEADS * MLA_ROPE), 1) // MLA_ROPE
    qr = qrope_ref[...]
    for hd in range(MLA_HEADS):
        qcat_ref[hd * blk:(hd + 1) * blk, :KV_RANK] = qlat_ref[hd]
        qcat_ref[hd * blk:(hd + 1) * blk, KV_RANK:] = jnp.where(lane_head == hd, qr, jnp.zeros_like(qr))

    def blocks(j, n_blk, first):
        width = n_blk * blk
        v = v_ref[pl.ds(pl.multiple_of(j * blk, blk), width), :]
        if first:
            tok = lax.broadcasted_iota(jnp.int32, (ATTN_ROW_CHUNK, width), 0)
            key = lax.broadcasted_iota(jnp.int32, (ATTN_ROW_CHUNK, width), 1)
        for g in range(n_grp):
            rows = slice(g * grp, (g + 1) * grp)
            for n in range(n_blk):
                s_ref[rows, n * blk:(n + 1) * blk] = _dot(qcat_ref[rows, :], kt_ref[j + n])
        for g in range(n_grp):
            for c in range(g * grp, (g + 1) * grp, ATTN_ROW_CHUNK):
                r = slice(c, c + ATTN_ROW_CHUNK)
                s = s_ref[r, :width]
                if first:
                    s = jnp.where(key <= tok + (c % blk + N_META), s, MASK_VALUE)
                    m_new = jnp.broadcast_to(jnp.max(s, axis=-1, keepdims=True), (ATTN_ROW_CHUNK, 128))
                    p = jnp.exp(s - _lanes(m_new, width))
                    l_ref[r, :] = jnp.broadcast_to(jnp.sum(p, axis=-1, keepdims=True), (ATTN_ROW_CHUNK, 128))
                else:
                    m_prev = m_ref[r, :]
                    m_new = jnp.maximum(m_prev, jnp.max(s, axis=-1, keepdims=True))
                    alpha = jnp.exp(m_prev - m_new)
                    p = jnp.exp(s - _lanes(m_new, width))
                    l_ref[r, :] = alpha * l_ref[r, :] + jnp.sum(p, axis=-1, keepdims=True)
                    alpha_ref[r, :] = alpha
                m_ref[r, :] = m_new
                p_ref[r, :width] = p.astype(BF16)
            rows = slice(g * grp, (g + 1) * grp)
            pv = _dot(p_ref[rows, :width], v)
            acc_ref[rows, :] = pv if first else _lanes(alpha_ref[rows, :], KV_RANK) * acc_ref[rows, :] + pv

    blocks(i, 1, True)

    def two_full_blocks(jj, carry):
        blocks(2 * jj, 2, False)
        return carry

    lax.fori_loop(0, i // 2, two_full_blocks, 0)

    @pl.when(i % 2 == 1)
    def _():
        blocks(i - 1, 1, False)

    tails = [slice((hd + 1) * blk - N_META, (hd + 1) * blk) for hd in range(MLA_HEADS)]
    gather = lambda ref: jnp.concatenate([ref[r, :] for r in tails], axis=0)
    s = _dot(gather(qcat_ref), ktail_ref[...])
    q_off = lax.broadcasted_iota(jnp.int32, s.shape, 0) & (N_META - 1)
    col = lax.broadcasted_iota(jnp.int32, s.shape, 1) - (128 - N_META)
    s = jnp.where((col >= 0) & (col <= q_off), s, MASK_VALUE)
    m_prev = gather(m_ref)
    m_new = jnp.maximum(m_prev, jnp.max(s, axis=-1, keepdims=True))
    alpha = jnp.exp(m_prev - m_new)
    p = jnp.exp(s - m_new)
    l_new = alpha * gather(l_ref) + jnp.sum(p, axis=-1, keepdims=True)
    acc_new = _lanes(alpha, KV_RANK) * gather(acc_ref) + _dot(p.astype(BF16), vtail_ref[...])
    for hd, r in enumerate(tails):
        part = slice(hd * N_META, (hd + 1) * N_META)
        l_ref[r, :] = l_new[part]
        acc_ref[r, :] = acc_new[part]

    for pr in range(MLA_HEADS // 2):
        halves = []
        for hd in (2 * pr, 2 * pr + 1):
            rows = slice(hd * blk, (hd + 1) * blk)
            halves.append((acc_ref[rows, :] / _lanes(l_ref[rows, :], KV_RANK)).astype(BF16))
        o_ref[:, pr * 2 * MLA_V:(pr + 1) * 2 * MLA_V] = _dot(jnp.concatenate(halves, axis=1), wuv_ref[pr]).astype(BF16)


def _prompt_attention(qlat, qrope, kt_seq, v_seq, kt_tok, kcat, wuv_pair):
    b, nh, t, r = qlat.shape
    nq, blk = kt_seq.shape[1], kt_seq.shape[3]
    assert nq * blk == t and v_seq.shape[1] == t
    assert blk % 128 == 0 and blk & (blk - 1) == 0 and N_META <= 128
    rows = nh * blk
    last = blk // 128
    return pl.pallas_call(
        _attn_body,
        grid=(b, nq),
        in_specs=[
            pl.BlockSpec((None, nh, blk, r), lambda bi, i: (bi, 0, i, 0)),
            pl.BlockSpec((None, blk, qrope.shape[2]), lambda bi, i: (bi, i, 0)),
            pl.BlockSpec((None,) + kt_seq.shape[1:], lambda bi, i: (bi, 0, 0, 0)),
            pl.BlockSpec((None,) + v_seq.shape[1:], lambda bi, i: (bi, 0, 0)),
            pl.BlockSpec((None, kt_tok.shape[1], 128), lambda bi, i: (bi, 0, (i + 1) * last - 1)),
            pl.BlockSpec((None, 128, r), lambda bi, i: (bi, (i + 1) * last - 1, 0)),
            _const_spec(wuv_pair.shape),
        ],
        out_specs=pl.BlockSpec((None, blk, nh * MLA_V), lambda bi, i: (bi, i, 0)),
        out_shape=jax.ShapeDtypeStruct((b, t, nh * MLA_V), BF16),
        scratch_shapes=[
            pltpu.VMEM((rows, 2 * r), BF16),
            pltpu.VMEM((rows, 2 * blk), F32),
            pltpu.VMEM((rows, 2 * blk), BF16),
            pltpu.VMEM((rows, 128), F32),
            pltpu.VMEM((rows, 128), F32),
            pltpu.VMEM((rows, 128), F32),
            pltpu.VMEM((rows, r), F32),
        ],
        compiler_params=_params("parallel", "arbitrary"),
        name="prompt_attention",
    )(qlat, qrope, kt_seq, v_seq, kt_tok, kcat, wuv_pair)


def _group_norm(o):
    mu = jnp.mean(o, axis=-1, keepdims=True)
    c = o - mu
    return c * lax.rsqrt(jnp.mean(c * c, axis=-1, keepdims=True) + NORM_EPS)


def _ret_body(rq_ref, rk_ref, rv_ref, kmeta_ref, vmeta_ref, on_ref, sout_ref, s_ref):
    c = pl.program_id(1)
    ch = rq_ref.shape[0]
    n_meta = kmeta_ref.shape[0]

    @pl.when(c == 0)
    def _():
        j = lax.broadcasted_iota(jnp.int32, (n_meta, 1), 0).astype(F32)
        for hd in range(RET_HEADS):
            kd = kmeta_ref[:, hd * RET_DK:(hd + 1) * RET_DK].astype(F32) * jnp.exp(RET_LOG_GAMMA[hd] * (n_meta - 1.0 - j))
            s_ref[hd] = _dot(kd.T.astype(BF16), vmeta_ref[:, hd * RET_DV:(hd + 1) * RET_DV])

    row = lax.broadcasted_iota(jnp.int32, (ch, 1), 0).astype(F32)
    diff = (lax.broadcasted_iota(jnp.int32, (ch, ch), 0) - lax.broadcasted_iota(jnp.int32, (ch, ch), 1)).astype(F32)
    for hd in range(RET_HEADS):
        lg = RET_LOG_GAMMA[hd]
        q = rq_ref[:, hd * RET_DK:(hd + 1) * RET_DK]
        k = rk_ref[:, hd * RET_DK:(hd + 1) * RET_DK]
        v = rv_ref[:, hd * RET_DV:(hd + 1) * RET_DV]
        decay = jnp.where(diff >= 0, jnp.exp(lg * jnp.maximum(diff, 0.0)), 0.0)
        scores = _dot_nt(q, k) * decay
        s_old = s_ref[hd]
        o = _dot(scores.astype(BF16), v) + _dot(q, s_old.astype(BF16)) * jnp.exp(lg * (row + 1.0))
        kd = k.astype(F32) * jnp.exp(lg * (ch - 1.0 - row))
        s_ref[hd] = math.exp(lg * ch) * s_old + _dot(kd.T.astype(BF16), v)
        on_ref[:, hd * RET_DV:(hd + 1) * RET_DV] = _group_norm(o).astype(BF16)

    @pl.when(c == pl.num_programs(1) - 1)
    def _():
        sout_ref[...] = s_ref[...]


def _prompt_retention(rq, rk, rv, kmeta, vmeta):
    b, t, hr = rq.shape
    hv = rv.shape[2]
    ch = min(t, RET_CHUNK)
    assert t % ch == 0
    tok = lambda width: pl.BlockSpec((None, ch, width), lambda bi, ci: (bi, ci, 0))
    return pl.pallas_call(
        _ret_body,
        grid=(b, t // ch),
        in_specs=[tok(hr), tok(hr), tok(hv), _const_spec(kmeta.shape), _const_spec(vmeta.shape)],
        out_specs=[tok(hv), pl.BlockSpec((None, RET_HEADS, RET_DK, RET_DV), lambda bi, ci: (bi, 0, 0, 0))],
        out_shape=[jax.ShapeDtypeStruct((b, t, hv), BF16),
                   jax.ShapeDtypeStruct((b, RET_HEADS, RET_DK, RET_DV), F32)],
        scratch_shapes=[pltpu.VMEM((RET_HEADS, RET_DK, RET_DV), F32)],
        compiler_params=_params("parallel", "arbitrary"),
        name="prompt_retention",
    )(rq, rk, rv, kmeta, vmeta)


def _dec_attn_body(pt_ref, qlat_ref, qrope_ref, kself_ref, wuv_ref, ckv_hbm, kr_hbm, o_ref,
                   kbuf, krbuf, sem_k, sem_r, m_ref, l_ref, acc_ref, *, pages, n_chunks):
    b = pl.program_id(0)
    n_req = pl.num_programs(0)
    ahead = DEC_SLOTS - 1
    ql = qlat_ref[...]
    qr = qrope_ref[...]

    def page_copies(page, slot, p):
        lat = pltpu.make_async_copy(ckv_hbm.at[page], kbuf.at[slot, p], sem_k.at[slot])
        rope = pltpu.make_async_copy(kr_hbm.at[page], krbuf.at[slot, p], sem_r.at[slot])
        return lat, rope

    def start_chunk(req, chunk, slot):
        def one(p, carry):
            for cp in page_copies(pt_ref[req, chunk * pages + p], slot, p):
                cp.start()
            return carry
        lax.fori_loop(0, pages, one, 0)

    def wait_chunk(slot):
        def one(p, carry):
            for cp in page_copies(0, slot, p):
                cp.wait()
            return carry
        lax.fori_loop(0, pages, one, 0)

    @pl.when(b == 0)
    def _():
        for g in range(ahead):
            start_chunk(0, g, g)

    ks = kself_ref[...].astype(F32)
    qf = jnp.concatenate([ql, qr], axis=1).astype(F32)
    m_ref[...] = jnp.sum(qf * ks, axis=-1, keepdims=True)
    l_ref[...] = jnp.ones(l_ref.shape, F32)
    acc_ref[...] = jnp.broadcast_to(ks[:, :KV_RANK], acc_ref.shape)

    sub = min(pages, DEC_SUB_PAGES)
    n_sub = pages // sub
    sub_keys = sub * PAGE_SIZE

    def ring_round(rnd, carry):
        for slot in range(DEC_SLOTS):
            chunk = rnd * DEC_SLOTS + slot
            nxt = chunk + ahead
            wrap = nxt >= n_chunks
            nxt_req = jnp.where(wrap, b + 1, b)

            @pl.when(nxt_req < n_req)
            def _():
                start_chunk(nxt_req, jnp.where(wrap, nxt - n_chunks, nxt), (slot + ahead) % DEC_SLOTS)

            wait_chunk(slot)
            k_lat, scores = [], []
            for q in range(n_sub):
                k_q = kbuf[slot, q * sub:(q + 1) * sub].reshape(sub_keys, KV_RANK).astype(BF16)
                kr_q = jnp.concatenate([krbuf[slot, q * sub + j] for j in range(sub)], axis=1).astype(BF16)
                k_lat.append(k_q)
                scores.append(_dot_nt(ql, k_q) + _dot(qr, kr_q))
            s = jnp.concatenate(scores, axis=1)
            m_prev = m_ref[...]
            m_new = jnp.maximum(m_prev, jnp.max(s, axis=-1, keepdims=True))
            alpha = jnp.exp(m_prev - m_new)
            p = jnp.exp(s - m_new)
            l_ref[...] = alpha * l_ref[...] + jnp.sum(p, axis=-1, keepdims=True)
            p = p.astype(BF16)
            pv = _dot(p[:, :sub_keys], k_lat[0])
            for q in range(1, n_sub):
                pv = pv + _dot(p[:, q * sub_keys:(q + 1) * sub_keys], k_lat[q])
            acc_ref[...] = alpha * acc_ref[...] + pv
            m_ref[...] = m_new
        return carry

    lax.fori_loop(0, n_chunks // DEC_SLOTS, ring_round, 0)

    o = (acc_ref[...] / l_ref[...]).astype(BF16)
    z = _dot(o, wuv_ref[...])
    head = lax.broadcasted_iota(jnp.int32, z.shape, 0)
    col_head = lax.broadcasted_iota(jnp.int32, z.shape, 1) // MLA_V
    o_ref[...] = jnp.sum(jnp.where(head == col_head, z, 0.0), axis=0, keepdims=True).astype(BF16)


def _decode_attention(page_table, qlat, qrope, kself, wuv_all, cache_ckv, cache_kr_t):
    bd, nh, r = qlat.shape
    n_pages = page_table.shape[1]
    pages = min(n_pages // DEC_SLOTS, DEC_PAGES_PER_STEP)
    n_chunks = n_pages // pages
    assert n_chunks * pages == n_pages and n_chunks % DEC_SLOTS == 0 and pages % min(pages, DEC_SUB_PAGES) == 0

    per_req = lambda shape: pl.BlockSpec((None,) + shape, lambda b, pt: (b, 0, 0))
    grid_spec = pltpu.PrefetchScalarGridSpec(
        num_scalar_prefetch=1,
        grid=(bd,),
        in_specs=[per_req((nh, r)), per_req((nh, MLA_ROPE)), per_req((1, r + MLA_ROPE)),
                  pl.BlockSpec(wuv_all.shape, lambda b, pt: (0, 0)),
                  pl.BlockSpec(memory_space=pl.ANY), pl.BlockSpec(memory_space=pl.ANY)],
        out_specs=per_req((1, nh * MLA_V)),
        scratch_shapes=[
            pltpu.VMEM((DEC_SLOTS, pages, PAGE_SIZE, r), F32),
            pltpu.VMEM((DEC_SLOTS, pages, MLA_ROPE, PAGE_SIZE), F32),
            pltpu.SemaphoreType.DMA((DEC_SLOTS,)),
            pltpu.SemaphoreType.DMA((DEC_SLOTS,)),
            pltpu.VMEM((nh, 1), F32), pltpu.VMEM((nh, 1), F32), pltpu.VMEM((nh, r), F32),
        ],
    )
    return pl.pallas_call(
        functools.partial(_dec_attn_body, pages=pages, n_chunks=n_chunks),
        grid_spec=grid_spec,
        out_shape=jax.ShapeDtypeStruct((bd, 1, nh * MLA_V), BF16),
        compiler_params=_params("arbitrary"),
        name="decode_attention",
    )(page_table, qlat, qrope, kself, wuv_all, cache_ckv, cache_kr_t)


def _dec_ret_body(rq_ref, rk_ref, rv_ref, s_ref, on_ref, snew_ref):
    eye = (lax.broadcasted_iota(jnp.int32, (RET_DK, RET_DK), 0)
           == lax.broadcasted_iota(jnp.int32, (RET_DK, RET_DK), 1))

    def column(x):
        return jnp.sum(jnp.where(eye, jnp.broadcast_to(x, (RET_DK, RET_DK)), 0.0), axis=1, keepdims=True)

    for hd in range(RET_HEADS):
        gamma = math.exp(RET_LOG_GAMMA[hd])
        q = rq_ref[:, hd * RET_DK:(hd + 1) * RET_DK].astype(F32)
        k = rk_ref[:, hd * RET_DK:(hd + 1) * RET_DK].astype(F32)
        v = rv_ref[:, hd * RET_DV:(hd + 1) * RET_DV].astype(F32)
        s_old = s_ref[hd]
        qs = jnp.sum(column(q) * s_old, axis=0, keepdims=True)
        o = jnp.sum(q * k, axis=-1, keepdims=True) * v + qs * gamma
        snew_ref[hd] = gamma * s_old + column(k) * v
        on_ref[:, hd * RET_DV:(hd + 1) * RET_DV] = _group_norm(o).astype(BF16)


def _decode_retention(rq, rk, rv, state):
    bd = rq.shape[0]
    row = lambda width: pl.BlockSpec((None, 1, width), lambda b: (b, 0, 0))
    st = pl.BlockSpec((None, RET_HEADS, RET_DK, RET_DV), lambda b: (b, 0, 0, 0))
    return pl.pallas_call(
        _dec_ret_body,
        grid=(bd,),
        in_specs=[row(rq.shape[2]), row(rk.shape[2]), row(rv.shape[2]), st],
        out_specs=[row(rv.shape[2]), st],
        out_shape=[jax.ShapeDtypeStruct((bd, 1, rv.shape[2]), BF16), jax.ShapeDtypeStruct(state.shape, F32)],
        compiler_params=_params("parallel"),
        name="decode_retention",
    )(rq, rk, rv, state)


def _mix_out_body(h_ref, a_ref, on_ref, srg_ref, sga_ref, sgb_ref, gn_ref, wa_ref, wr_ref, wo_ref, o_ref):
    a = _dot(a_ref[...], wa_ref[...])
    gated = srg_ref[...].astype(F32) * (on_ref[...].astype(F32) * gn_ref[...])
    r = _dot(gated.astype(BF16), wr_ref[...])
    m = sga_ref[...].astype(F32) * a + sgb_ref[...].astype(F32) * r
    o_ref[...] = h_ref[...] + _dot(m.astype(BF16), wo_ref[...])


def _mix_out(h, apre, on, srg, sga, sgb, W):
    m, d = h.shape
    tm = min(m, TOKEN_TILE)
    assert m % tm == 0
    tok = lambda width: pl.BlockSpec((tm, width), lambda i: (i, 0))
    weights = [W["ret_gn"], W["w_mla_o"], W["w_ret_o"], W["w_out"]]
    return pl.pallas_call(
        _mix_out_body,
        grid=(m // tm,),
        in_specs=[tok(d), tok(apre.shape[1]), tok(on.shape[1]), tok(d), tok(d), tok(d)]
                 + [_const_spec(w.shape) for w in weights],
        out_specs=tok(d),
        out_shape=jax.ShapeDtypeStruct((m, d), F32),
        compiler_params=_params("parallel"),
        name="mixer_output",
    )(h, apre, on, srg, sga, sgb, *weights)


def _rope_tables(pos):
    pos = pos.astype(F32)[:, None]

    def table(width, reps):
        half = width // 2
        inv_freq = ROPE_BASE ** (-jnp.arange(half, dtype=F32) / half)
        ang = pos * inv_freq[None, :]
        cos, sin = jnp.cos(ang), jnp.sin(ang)
        return jnp.stack([jnp.tile(jnp.concatenate([cos, cos], axis=1), (1, reps)),
                          jnp.tile(jnp.concatenate([-sin, sin], axis=1), (1, reps))])

    return table(RET_DK, 1), table(MLA_ROPE, MLA_HEADS)


def _prep_weights(ffn1_norm, ffn1_gate, ffn1_up, ffn1_down, mix_norm, w_in, q_norm, kv_norm, w_uq, w_uk, w_uv,
                  w_mla_o, ret_gn, w_ret_o, w_out, ffn2_norm, ffn2_gate, ffn2_up, ffn2_down, final_norm):
    d = w_in.shape[0]
    hr, hv = RET_HEADS * RET_DK, RET_HEADS * RET_DV
    nh, hd = MLA_HEADS, MLA_NOPE + MLA_ROPE

    bounds = [0]
    for width in (Q_RANK, KV_RANK, MLA_ROPE, hr, hr, hv, hv, d, d):
        bounds.append(bounds[-1] + width)
    assert bounds[-1] == w_in.shape[1]
    w_q, w_kv, w_kr, w_rq, w_rk, w_rv, w_rg, w_ga, w_gb = (
        w_in[:, bounds[i]:bounds[i + 1]].astype(BF16) for i in range(9))

    w_uq3 = w_uq.reshape(Q_RANK, nh, hd)
    zeros = jnp.zeros((MLA_NOPE, KV_RANK), F32)
    w_uk_t = w_uk.transpose(1, 2, 0)
    w_uk_pair = jnp.stack([jnp.block([[w_uk_t[2 * p], zeros], [zeros, w_uk_t[2 * p + 1]]])
                           for p in range(nh // 2)]).astype(BF16)
    w_uv_t = w_uv.transpose(1, 0, 2)
    zv = jnp.zeros((KV_RANK, MLA_V), F32)
    w_uv_pair = jnp.stack([jnp.block([[w_uv_t[2 * p], zv], [zv, w_uv_t[2 * p + 1]]])
                           for p in range(nh // 2)]).astype(BF16)
    row = lambda v: v.reshape(1, -1)
    return dict(
        ffn1_norm=row(ffn1_norm), ffn1=(ffn1_gate, ffn1_up, ffn1_down),
        ffn2_norm=row(ffn2_norm), ffn2=(ffn2_gate, ffn2_up, ffn2_down), final_norm=row(final_norm),
        mix_norm=row(mix_norm), w_q=w_q, w_kv=w_kv, w_kr=jnp.tile(w_kr, (1, nh)), w_rq=w_rq, w_rk=w_rk, w_rv=w_rv,
        w_rg=w_rg, w_ga=w_ga, w_gb=w_gb, q_norm=row(q_norm), kv_norm=row(kv_norm),
        w_uq_nope=w_uq3[:, :, :MLA_NOPE].reshape(Q_RANK, nh * MLA_NOPE).astype(BF16),
        w_uq_rope=w_uq3[:, :, MLA_NOPE:].reshape(Q_RANK, nh * MLA_ROPE).astype(BF16),
        w_uk_pair=w_uk_pair, w_uv_pair=w_uv_pair, w_uv_all=w_uv.reshape(KV_RANK, nh * MLA_V).astype(BF16),
        ret_gn=row(ret_gn), w_mla_o=w_mla_o.astype(BF16), w_ret_o=w_ret_o.astype(BF16), w_out=w_out.astype(BF16),
    )


def kernel(x_prompt, x_sample, cache_ckv, cache_krope, state_ret, page_table, meta_tokens, ffn1_norm, ffn1_gate, ffn1_up, ffn1_down, mix_norm, w_in, q_norm, kv_norm, w_uq, w_uk, w_uv, w_mla_o, ret_gn, w_ret_o, w_out, ffn2_norm, ffn2_gate, ffn2_up, ffn2_down, final_norm):
    assert ffn1_gate.shape[0] == 1, "single-layer trunk"
    b, t, d = x_prompt.shape
    bd, dec_seq, _ = x_sample.shape
    assert dec_seq == 1
    n_meta = meta_tokens.shape[0]
    assert n_meta == N_META
    past_len = page_table.shape[1] * PAGE_SIZE
    W = _prep_weights(ffn1_norm[0], ffn1_gate[0], ffn1_up[0], ffn1_down[0], mix_norm[0], w_in[0], q_norm[0],
                      kv_norm[0], w_uq[0], w_uk[0], w_uv[0], w_mla_o[0], ret_gn[0], w_ret_o[0], w_out[0],
                      ffn2_norm[0], ffn2_gate[0], ffn2_up[0], ffn2_down[0], final_norm)

    rope_r, rope_q = _rope_tables(n_meta + jnp.arange(t))
    pos_small = jnp.concatenate([jnp.arange(n_meta), jnp.full((bd,), past_len)])
    rope_r_s, rope_q_s = _rope_tables(pos_small)
    x_small = jnp.concatenate([meta_tokens.astype(F32), x_sample.reshape(bd, d)], axis=0)

    h1 = _ffn(x_prompt.reshape(b * t, d), W["ffn1_norm"], *W["ffn1"])
    h1s = _ffn(x_small, W["ffn1_norm"], *W["ffn1"])
    S = _mix_in(h1s.reshape(1, n_meta + bd, d), W, rope_r_s, rope_q_s)
    meta_keys = (S["kt"][0, :, :n_meta].astype(F32), S["kcat"][0, :n_meta, :KV_RANK].astype(F32))
    P = _mix_in(h1.reshape(b, t, d), W, rope_r, rope_q, prev_keys=meta_keys)

    apre = _prompt_attention(P["qlat"], P["qrope"], P["kts"], P["vs"], P["kt"], P["kcat"], W["w_uv_pair"])
    on, s_prompt = _prompt_retention(P["rq"], P["rk"], P["rv"], S["rk"][0, :n_meta], S["rv"][0, :n_meta])

    nh = MLA_HEADS
    qlat_s = S["qlat"][0, :, n_meta:].transpose(1, 0, 2)
    qrope_s = S["qrope"][0, n_meta:].reshape(bd, nh, MLA_ROPE)
    kself = S["kcat"][0, n_meta:, :KV_RANK + MLA_ROPE].reshape(bd, 1, KV_RANK + MLA_ROPE)
    apre_s = _decode_attention(page_table, qlat_s, qrope_s, kself, W["w_uv_all"], cache_ckv[0],
                               cache_krope[0].transpose(0, 2, 1))
    dec = lambda name: S[name][0, n_meta:].reshape(bd, 1, -1)
    on_s, s_sample = _decode_retention(dec("rq"), dec("rk"), dec("rv"), state_ret[0])

    flat = lambda a: a.reshape(b * t, -1)
    h2 = _mix_out(h1, flat(apre), flat(on), flat(P["srg"]), flat(P["sga"]), flat(P["sgb"]), W)
    y_prompt = _ffn(h2, W["ffn2_norm"], *W["ffn2"], final_norm=W["final_norm"])
    tail = lambda name: S[name][0, n_meta:]
    h2s = _mix_out(h1s[n_meta:], apre_s.reshape(bd, -1), on_s.reshape(bd, -1), tail("srg"), tail("sga"), tail("sgb"), W)
    y_sample = _ffn(h2s, W["ffn2_norm"], *W["ffn2"], final_norm=W["final_norm"])

    with_meta = lambda small, main: jnp.concatenate(
        [jnp.broadcast_to(small[:, :n_meta], (b, n_meta, small.shape[2])), main], axis=1)[None]
    return (
        y_prompt.reshape(b, t, d),
        y_sample.reshape(bd, 1, d),
        with_meta(S["ckv"], P["ckv"]),
        with_meta(S["kr"], P["kr"]),
        s_prompt[None],
        S["ckv"][0, n_meta:].reshape(1, bd, 1, KV_RANK),
        S["kr"][0, n_meta:].reshape(1, bd, 1, MLA_ROPE),
        s_sample[None],
    )
```

```python
import functools
import math

import jax
import jax.numpy as jnp
from jax import lax
from jax.experimental import pallas as pl
from jax.experimental.pallas import tpu as pltpu

N_META = 16
PAGE_SIZE = 128
MLA_HEADS = 8
MLA_NOPE = 64
MLA_ROPE = 32
MLA_V = 64
Q_RANK = 384
KV_RANK = 256
MLA_SCALE = (MLA_NOPE + MLA_ROPE) ** -0.5
RET_HEADS = 4
RET_DK = 128
RET_DV = 256
ROPE_BASE = 10000.0
NORM_EPS = 1e-6
RET_LOG_GAMMA = tuple(math.log1p(-(2.0 ** (-5.0 - h))) for h in range(RET_HEADS))

VMEM_LIMIT_BYTES = 56 * 1024 * 1024
FFN_CHUNK = 256
TOKEN_TILE = 512
ATTN_BLOCK = 256
ATTN_GROUP_HEADS = 2
ATTN_ROW_CHUNK = 32
RET_CHUNK = 256
DEC_PAGES_PER_STEP = 64
DEC_SUB_PAGES = 16
DEC_SLOTS = 2
DEC_RET_REQUESTS = 4
MASK_VALUE = -1e30

F32 = jnp.float32
BF16 = jnp.bfloat16


def _params(*sem):
    return pltpu.CompilerParams(dimension_semantics=sem, vmem_limit_bytes=VMEM_LIMIT_BYTES)


def _const_spec(shape):
    n = len(shape)
    return pl.BlockSpec(shape, lambda *_: (0,) * n, pipeline_mode=pl.Buffered(1))


def _rms(x, g):
    return x * lax.rsqrt(jnp.mean(x * x, axis=-1, keepdims=True) + NORM_EPS) * g


def _sigmoid(x):
    return 1.0 / (1.0 + jnp.exp(-x))


def _dot(a, b):
    return jnp.dot(a, b, preferred_element_type=F32)


def _dot_nt(a, b):
    return lax.dot_general(a, b, (((1,), (1,)), ((), ())), preferred_element_type=F32)


def _ffn_body(x_ref, xs_ref, g_ref, wg_ref, wu_ref, wd_ref, *rest, final):
    o_ref, os_ref = rest[-2:]

    def apply(src_ref, dst_ref):
        x = src_ref[...]
        u = _rms(x, g_ref[...]).astype(BF16)
        acc = None
        for c in range(0, wg_ref.shape[1], FFN_CHUNK):
            cols = slice(c, c + FFN_CHUNK)
            gate = _dot(u, wg_ref[:, cols].astype(BF16))
            up = _dot(u, wu_ref[:, cols].astype(BF16))
            act = (gate * _sigmoid(gate) * up).astype(BF16)
            part = _dot(act, wd_ref[cols, :].astype(BF16))
            acc = part if acc is None else acc + part
        y = x + 0.5 * acc
        if final:
            y = _rms(y, rest[0][...])
        dst_ref[...] = y

    last = pl.num_programs(0) - 1
    pl.when(pl.program_id(0) < last)(functools.partial(apply, x_ref, o_ref))
    pl.when(pl.program_id(0) == last)(functools.partial(apply, xs_ref, os_ref))


def _ffn(x, x_small, norm, wg, wu, wd, final_norm=None):
    m, d = x.shape
    tm = min(m, TOKEN_TILE)
    assert m % tm == 0 and wg.shape[1] % FFN_CHUNK == 0
    n = m // tm
    final = final_norm is not None
    main = pl.BlockSpec((tm, d), lambda i: (jnp.minimum(i, n - 1), 0))
    in_specs = [main, _const_spec(x_small.shape), _const_spec((1, d)),
                _const_spec(wg.shape), _const_spec(wu.shape), _const_spec(wd.shape)]
    args = [x, x_small, norm, wg, wu, wd]
    if final:
        in_specs.append(_const_spec((1, d)))
        args.append(final_norm)
    return pl.pallas_call(
        functools.partial(_ffn_body, final=final),
        grid=(n + 1,),
        in_specs=in_specs,
        out_specs=[main, pl.BlockSpec(x_small.shape, lambda i: (0, 0))],
        out_shape=[jax.ShapeDtypeStruct((m, d), F32), jax.ShapeDtypeStruct(x_small.shape, F32)],
        compiler_params=_params("arbitrary"),
        name="ffn_final" if final else "ffn",
    )(*args)


def _rope32(x, cos, sin):
    lane = lax.broadcasted_iota(jnp.int32, x.shape, 1)
    width = x.shape[1]
    partner = jnp.where((lane & 31) < 16, pltpu.roll(x, width - 16, axis=1), pltpu.roll(x, 16, axis=1))
    return x * cos + partner * sin


def _mix_in_body(h_ref, g_ref, wq_ref, wkv_ref, wkr_ref, wrq_ref, wrk_ref, wrv_ref, wrg_ref, wga_ref, wgb_ref,
                 qn_ref, kvn_ref, wuqn_ref, wuqr_ref, wuk_ref, rope_r_ref, rope_q_ref, *rest, shifted):
    if shifted:
        (kprev_ref, vprev_ref, qlat_ref, qrope_ref, ckv_ref, kr_ref, kcat_ref, kt_ref, rq_ref, rk_ref, rv_ref,
         srg_ref, sga_ref, sgb_ref, kts_ref, vs_ref, kcarry_ref, vcarry_ref) = rest
    else:
        (qlat_ref, qrope_ref, ckv_ref, kr_ref, kcat_ref, kt_ref, rq_ref, rk_ref, rv_ref,
         srg_ref, sga_ref, sgb_ref) = rest
    if shifted:
        @pl.when(pl.program_id(1) == 0)
        def _():
            kcarry_ref[...] = kprev_ref[...]
            vcarry_ref[...] = vprev_ref[...]

    u = _rms(h_ref[...], g_ref[...]).astype(BF16)
    cos_q, sin_q = rope_q_ref[0], rope_q_ref[1]
    cos_r, sin_r = rope_r_ref[0], rope_r_ref[1]

    cq = _rms(_dot(u, wq_ref[...]), qn_ref[...]).astype(BF16)
    q_nope = (_dot(cq, wuqn_ref[...]) * MLA_SCALE).astype(BF16)
    for p in range(MLA_HEADS // 2):
        pair = _dot(q_nope[:, p * 128:(p + 1) * 128], wuk_ref[p])
        qlat_ref[2 * p] = pair[:, :KV_RANK].astype(BF16)
        qlat_ref[2 * p + 1] = pair[:, KV_RANK:].astype(BF16)
    q_rope = _rope32(_dot(cq, wuqr_ref[...]), cos_q, sin_q) * MLA_SCALE
    qrope_ref[...] = q_rope.astype(BF16)

    ckv = _rms(_dot(u, wkv_ref[...]), kvn_ref[...])
    ckv_ref[...] = ckv
    kcat_ref[:, :KV_RANK] = ckv.astype(BF16)
    kr = _rope32(_dot(u, wkr_ref[...]), cos_q, sin_q)
    kr_ref[...] = kr[:, :MLA_ROPE]
    kcat_ref[:, KV_RANK:] = kr.astype(BF16)
    k_t = jnp.concatenate([ckv, kr], axis=1).T
    kt_ref[...] = k_t.astype(BF16)
    if shifted:
        keep = k_t.shape[1] - N_META
        k_sh = jnp.concatenate([kcarry_ref[...], k_t[:, :keep]], axis=1).astype(BF16)
        blk = kts_ref.shape[2]
        for j in range(kts_ref.shape[0]):
            kts_ref[j] = k_sh[:, j * blk:(j + 1) * blk]
        vs_ref[...] = jnp.concatenate([vcarry_ref[...], ckv[:keep]], axis=0).astype(BF16)
        kcarry_ref[...] = k_t[:, keep:]
        vcarry_ref[...] = ckv[keep:]

    rq = _dot(u, wrq_ref[...])
    rk = _dot(u, wrk_ref[...])
    for hd in range(RET_HEADS):
        sl = slice(hd * RET_DK, (hd + 1) * RET_DK)
        xq, xk = rq[:, sl], rk[:, sl]
        rq_ref[:, sl] = (xq * cos_r + pltpu.roll(xq, RET_DK // 2, axis=1) * sin_r).astype(BF16)
        rk_ref[:, sl] = ((xk * cos_r + pltpu.roll(xk, RET_DK // 2, axis=1) * sin_r) * (RET_DK ** -0.5)).astype(BF16)
    rv_ref[...] = _dot(u, wrv_ref[...]).astype(BF16)

    rg = _dot(u, wrg_ref[...])
    srg_ref[...] = (rg * _sigmoid(rg)).astype(BF16)
    sga_ref[...] = _sigmoid(_dot(u, wga_ref[...])).astype(BF16)
    sgb_ref[...] = _sigmoid(_dot(u, wgb_ref[...])).astype(BF16)


def _mix_in(h, W, rope_r, rope_q, prev_keys=None):
    b, t, d = h.shape
    tm = min(t, TOKEN_TILE)
    assert t % tm == 0
    hr = RET_HEADS * RET_DK
    hv = RET_HEADS * RET_DV
    qw = MLA_HEADS * MLA_ROPE
    shifted = prev_keys is not None
    weights = [W["mix_norm"], W["w_q"], W["w_kv"], W["w_kr"], W["w_rq"], W["w_rk"], W["w_rv"], W["w_rg"],
               W["w_ga"], W["w_gb"], W["q_norm"], W["kv_norm"], W["w_uq_nope"], W["w_uq_rope"], W["w_uk_pair"]]
    tok = lambda width: pl.BlockSpec((None, tm, width), lambda bi, ti: (bi, ti, 0))
    in_specs = ([tok(d)] + [_const_spec(w.shape) for w in weights]
                + [pl.BlockSpec((2, tm, RET_DK), lambda bi, ti: (0, ti, 0)),
                   pl.BlockSpec((2, tm, qw), lambda bi, ti: (0, ti, 0))])
    out_specs = [
        pl.BlockSpec((None, MLA_HEADS, tm, KV_RANK), lambda bi, ti: (bi, 0, ti, 0)),
        tok(qw), tok(KV_RANK), tok(MLA_ROPE), tok(2 * KV_RANK),
        pl.BlockSpec((None, 2 * KV_RANK, tm), lambda bi, ti: (bi, 0, ti)),
        tok(hr), tok(hr), tok(hv), tok(hv), tok(d), tok(d),
    ]
    sds = jax.ShapeDtypeStruct
    out_shape = [
        sds((b, MLA_HEADS, t, KV_RANK), BF16), sds((b, t, qw), BF16), sds((b, t, KV_RANK), F32),
        sds((b, t, MLA_ROPE), F32), sds((b, t, 2 * KV_RANK), BF16), sds((b, 2 * KV_RANK, t), BF16),
        sds((b, t, hr), BF16), sds((b, t, hr), BF16),
        sds((b, t, hv), BF16), sds((b, t, hv), BF16), sds((b, t, d), BF16), sds((b, t, d), BF16),
    ]
    names = ["qlat", "qrope", "ckv", "kr", "kcat", "kt", "rq", "rk", "rv", "srg", "sga", "sgb"]
    extra_in, scratch = [], []
    if shifted:
        blk = ATTN_BLOCK
        assert tm % blk == 0 and N_META % 8 == 0
        extra_in = list(prev_keys)
        in_specs += [_const_spec(a.shape) for a in extra_in]
        out_specs += [pl.BlockSpec((None, tm // blk, 2 * KV_RANK, blk), lambda bi, ti: (bi, ti, 0, 0)), tok(KV_RANK)]
        out_shape += [sds((b, t // blk, 2 * KV_RANK, blk), BF16), sds((b, t, KV_RANK), BF16)]
        names += ["kts", "vs"]
        scratch = [pltpu.VMEM((2 * KV_RANK, N_META), F32), pltpu.VMEM((N_META, KV_RANK), F32)]
    outs = pl.pallas_call(
        functools.partial(_mix_in_body, shifted=shifted),
        grid=(b, t // tm),
        in_specs=in_specs,
        out_specs=out_specs,
        out_shape=out_shape,
        scratch_shapes=scratch,
        compiler_params=_params("parallel", "arbitrary"),
        name="mixer_inputs",
    )(h, *weights, rope_r, rope_q, *extra_in)
    return dict(zip(names, outs))


def _softmax_step(s, k_lat, m_ref, l_ref, acc_ref):
    m_prev = m_ref[...]
    m_new = jnp.maximum(m_prev, jnp.max(s, axis=-1, keepdims=True))
    alpha = jnp.exp(m_prev - m_new)
    p = jnp.exp(s - m_new)
    l_ref[...] = alpha * l_ref[...] + jnp.sum(p, axis=-1, keepdims=True)
    acc_ref[...] = alpha * acc_ref[...] + _dot(p.astype(BF16), k_lat)
    m_ref[...] = m_new


def _lanes(x, width):
    return x if width == x.shape[1] else jnp.concatenate([x] * (width // x.shape[1]), axis=1)


def _attn_body(qlat_ref, qrope_ref, kt_ref, v_ref, ktail_ref, vtail_ref, wuv_ref, o_ref,
               qcat_ref, s_ref, p_ref, alpha_ref, m_ref, l_ref, acc_ref):
    blk = qrope_ref.shape[0]
    grp = ATTN_GROUP_HEADS * blk
    n_grp = MLA_HEADS // ATTN_GROUP_HEADS

    def blocks(j, n_blk, first):
        width = n_blk * blk
        v = v_ref[j * blk:j * blk + width, :]
        if first:
            tok = lax.broadcasted_iota(jnp.int32, (ATTN_ROW_CHUNK, width), 0)
            key = lax.broadcasted_iota(jnp.int32, (ATTN_ROW_CHUNK, width), 1)
        for g in range(n_grp):
            rows = slice(g * grp, (g + 1) * grp)
            for n in range(n_blk):
                s_ref[rows, n * blk:(n + 1) * blk] = _dot(qcat_ref[rows, :], kt_ref[j + n])
        for g in range(n_grp):
            for c in range(g * grp, (g + 1) * grp, ATTN_ROW_CHUNK):
                r = slice(c, c + ATTN_ROW_CHUNK)
                s = s_ref[r, :width]
                if first:
                    s = jnp.where(key <= tok + (c % blk + N_META), s, MASK_VALUE)
                    m_new = jnp.broadcast_to(jnp.max(s, axis=-1, keepdims=True), (ATTN_ROW_CHUNK, 128))
                    p = jnp.exp(s - _lanes(m_new, width))
                    l_ref[r, :] = jnp.broadcast_to(jnp.sum(p, axis=-1, keepdims=True), (ATTN_ROW_CHUNK, 128))
                else:
                    m_prev = m_ref[r, :]
                    m_new = jnp.maximum(m_prev, jnp.max(s, axis=-1, keepdims=True))
                    alpha = jnp.exp(m_prev - m_new)
                    p = jnp.exp(s - _lanes(m_new, width))
                    l_ref[r, :] = alpha * l_ref[r, :] + jnp.sum(p, axis=-1, keepdims=True)
                    alpha_ref[r, :] = alpha
                m_ref[r, :] = m_new
                p_ref[r, :width] = p.astype(BF16)
            rows = slice(g * grp, (g + 1) * grp)
            pv = _dot(p_ref[rows, :width], v)
            acc_ref[rows, :] = pv if first else _lanes(alpha_ref[rows, :], KV_RANK) * acc_ref[rows, :] + pv

    def tile(i):
        lane_head = lax.broadcasted_iota(jnp.int32, (blk, MLA_HEADS * MLA_ROPE), 1) // MLA_ROPE
        qr = qrope_ref[...]
        for hd in range(MLA_HEADS):
            qcat_ref[hd * blk:(hd + 1) * blk, :KV_RANK] = qlat_ref[hd]
            qcat_ref[hd * blk:(hd + 1) * blk, KV_RANK:] = jnp.where(lane_head == hd, qr, jnp.zeros_like(qr))

        blocks(i, 1, True)
        for jj in range(i // 2):
            blocks(2 * jj, 2, False)
        if i % 2 == 1:
            blocks(i - 1, 1, False)

        tails = [slice((hd + 1) * blk - N_META, (hd + 1) * blk) for hd in range(MLA_HEADS)]
        gather = lambda ref: jnp.concatenate([ref[r, :] for r in tails], axis=0)
        s = _dot(gather(qcat_ref), ktail_ref[...])
        q_off = lax.broadcasted_iota(jnp.int32, s.shape, 0) & (N_META - 1)
        col = lax.broadcasted_iota(jnp.int32, s.shape, 1) - (128 - N_META)
        s = jnp.where((col >= 0) & (col <= q_off), s, MASK_VALUE)
        m_prev = gather(m_ref)
        m_new = jnp.maximum(m_prev, jnp.max(s, axis=-1, keepdims=True))
        alpha = jnp.exp(m_prev - m_new)
        p = jnp.exp(s - m_new)
        l_new = alpha * gather(l_ref) + jnp.sum(p, axis=-1, keepdims=True)
        acc_new = _lanes(alpha, KV_RANK) * gather(acc_ref) + _dot(p.astype(BF16), vtail_ref[...])
        for hd, r in enumerate(tails):
            part = slice(hd * N_META, (hd + 1) * N_META)
            l_ref[r, :] = l_new[part]
            acc_ref[r, :] = acc_new[part]

        for pr in range(MLA_HEADS // 2):
            halves = []
            for hd in (2 * pr, 2 * pr + 1):
                rows = slice(hd * blk, (hd + 1) * blk)
                halves.append((acc_ref[rows, :] / _lanes(l_ref[rows, :], KV_RANK)).astype(BF16))
            o_ref[:, pr * 2 * MLA_V:(pr + 1) * 2 * MLA_V] = _dot(jnp.concatenate(halves, axis=1), wuv_ref[pr]).astype(BF16)

    for i in range(kt_ref.shape[0]):
        pl.when(pl.program_id(1) == i)(functools.partial(tile, i))


def _prompt_attention(qlat, qrope, kt_seq, v_seq, kt_tok, kcat, wuv_pair):
    b, nh, t, r = qlat.shape
    nq, blk = kt_seq.shape[1], kt_seq.shape[3]
    assert nq * blk == t and v_seq.shape[1] == t
    assert blk % 128 == 0 and blk & (blk - 1) == 0 and N_META <= 128
    rows = nh * blk
    last = blk // 128
    return pl.pallas_call(
        _attn_body,
        grid=(b, nq),
        in_specs=[
            pl.BlockSpec((None, nh, blk, r), lambda bi, i: (bi, 0, i, 0)),
            pl.BlockSpec((None, blk, qrope.shape[2]), lambda bi, i: (bi, i, 0)),
            pl.BlockSpec((None,) + kt_seq.shape[1:], lambda bi, i: (bi, 0, 0, 0)),
            pl.BlockSpec((None,) + v_seq.shape[1:], lambda bi, i: (bi, 0, 0)),
            pl.BlockSpec((None, kt_tok.shape[1], 128), lambda bi, i: (bi, 0, (i + 1) * last - 1)),
            pl.BlockSpec((None, 128, r), lambda bi, i: (bi, (i + 1) * last - 1, 0)),
            _const_spec(wuv_pair.shape),
        ],
        out_specs=pl.BlockSpec((None, blk, nh * MLA_V), lambda bi, i: (bi, i, 0)),
        out_shape=jax.ShapeDtypeStruct((b, t, nh * MLA_V), BF16),
        scratch_shapes=[
            pltpu.VMEM((rows, 2 * r), BF16),
            pltpu.VMEM((rows, 2 * blk), F32),
            pltpu.VMEM((rows, 2 * blk), BF16),
            pltpu.VMEM((rows, 128), F32),
            pltpu.VMEM((rows, 128), F32),
            pltpu.VMEM((rows, 128), F32),
            pltpu.VMEM((rows, r), F32),
        ],
        compiler_params=_params("parallel", "arbitrary"),
        name="prompt_attention",
    )(qlat, qrope, kt_seq, v_seq, kt_tok, kcat, wuv_pair)


def _group_norm(o):
    mu = jnp.mean(o, axis=-1, keepdims=True)
    c = o - mu
    return c * lax.rsqrt(jnp.mean(c * c, axis=-1, keepdims=True) + NORM_EPS)


def _ret_body(rq_ref, rk_ref, rv_ref, kmeta_ref, vmeta_ref, on_ref, sout_ref, s_ref, decay_ref):
    c = pl.program_id(1)
    ch = rq_ref.shape[0]
    n_meta = kmeta_ref.shape[0]

    @pl.when(c == 0)
    def _():
        j = lax.broadcasted_iota(jnp.int32, (n_meta, 1), 0).astype(F32)
        diff = (lax.broadcasted_iota(jnp.int32, (ch, ch), 0) - lax.broadcasted_iota(jnp.int32, (ch, ch), 1)).astype(F32)
        for hd in range(RET_HEADS):
            kd = kmeta_ref[:, hd * RET_DK:(hd + 1) * RET_DK].astype(F32) * jnp.exp(RET_LOG_GAMMA[hd] * (n_meta - 1.0 - j))
            s_ref[hd] = _dot(kd.T.astype(BF16), vmeta_ref[:, hd * RET_DV:(hd + 1) * RET_DV])
            decay_ref[hd] = jnp.where(diff >= 0, jnp.exp(RET_LOG_GAMMA[hd] * jnp.maximum(diff, 0.0)), 0.0)

    row = lax.broadcasted_iota(jnp.int32, (ch, 1), 0).astype(F32)
    for hd in range(RET_HEADS):
        lg = RET_LOG_GAMMA[hd]
        q = rq_ref[:, hd * RET_DK:(hd + 1) * RET_DK]
        k = rk_ref[:, hd * RET_DK:(hd + 1) * RET_DK]
        v = rv_ref[:, hd * RET_DV:(hd + 1) * RET_DV]
        scores = _dot_nt(q, k) * decay_ref[hd]
        s_old = s_ref[hd]
        o = _dot(scores.astype(BF16), v) + _dot(q, s_old.astype(BF16)) * jnp.exp(lg * (row + 1.0))
        kd = k.astype(F32) * jnp.exp(lg * (ch - 1.0 - row))
        s_ref[hd] = math.exp(lg * ch) * s_old + _dot(kd.T.astype(BF16), v)
        on_ref[:, hd * RET_DV:(hd + 1) * RET_DV] = _group_norm(o).astype(BF16)

    @pl.when(c == pl.num_programs(1) - 1)
    def _():
        sout_ref[...] = s_ref[...]


def _prompt_retention(rq, rk, rv, kmeta, vmeta):
    b, t, hr = rq.shape
    hv = rv.shape[2]
    ch = min(t, RET_CHUNK)
    assert t % ch == 0
    tok = lambda width: pl.BlockSpec((None, ch, width), lambda bi, ci: (bi, ci, 0))
    return pl.pallas_call(
        _ret_body,
        grid=(b, t // ch),
        in_specs=[tok(hr), tok(hr), tok(hv), _const_spec(kmeta.shape), _const_spec(vmeta.shape)],
        out_specs=[tok(hv), pl.BlockSpec((None, RET_HEADS, RET_DK, RET_DV), lambda bi, ci: (bi, 0, 0, 0))],
        out_shape=[jax.ShapeDtypeStruct((b, t, hv), BF16),
                   jax.ShapeDtypeStruct((b, RET_HEADS, RET_DK, RET_DV), F32)],
        scratch_shapes=[pltpu.VMEM((RET_HEADS, RET_DK, RET_DV), F32), pltpu.VMEM((RET_HEADS, ch, ch), F32)],
        compiler_params=_params("parallel", "arbitrary"),
        name="prompt_retention",
    )(rq, rk, rv, kmeta, vmeta)


def _dec_attn_body(pt_ref, qlat_ref, qrope_ref, kself_ref, wuv_ref, ckv_hbm, kr_hbm, o_ref,
                   kbuf, krbuf, sem_k, sem_r, m_ref, l_ref, acc_ref, *, pages, n_chunks):
    b = pl.program_id(0)
    n_req = pl.num_programs(0)
    ahead = DEC_SLOTS - 1
    ql = qlat_ref[...]
    qr = qrope_ref[...]

    def page_copies(page, slot, p):
        lat = pltpu.make_async_copy(ckv_hbm.at[page], kbuf.at[slot, p], sem_k.at[slot])
        rope = pltpu.make_async_copy(kr_hbm.at[page], krbuf.at[slot, p], sem_r.at[slot])
        return lat, rope

    def start_chunk(req, chunk, slot):
        def one(p, carry):
            for cp in page_copies(pt_ref[req, chunk * pages + p], slot, p):
                cp.start()
            return carry
        lax.fori_loop(0, pages, one, 0)

    def wait_chunk(slot):
        def one(p, carry):
            for cp in page_copies(0, slot, p):
                cp.wait()
            return carry
        lax.fori_loop(0, pages, one, 0)

    @pl.when(b == 0)
    def _():
        for g in range(ahead):
            start_chunk(0, g, g)

    ks = kself_ref[...].astype(F32)
    qf = jnp.concatenate([ql, qr], axis=1).astype(F32)
    m_ref[...] = jnp.sum(qf * ks, axis=-1, keepdims=True)
    l_ref[...] = jnp.ones(l_ref.shape, F32)
    acc_ref[...] = jnp.broadcast_to(ks[:, :KV_RANK], acc_ref.shape)

    sub = min(pages, DEC_SUB_PAGES)
    n_sub = pages // sub
    sub_keys = sub * PAGE_SIZE

    def ring_round(rnd, carry):
        for slot in range(DEC_SLOTS):
            chunk = rnd * DEC_SLOTS + slot
            nxt = chunk + ahead
            wrap = nxt >= n_chunks
            nxt_req = jnp.where(wrap, b + 1, b)

            @pl.when(nxt_req < n_req)
            def _():
                start_chunk(nxt_req, jnp.where(wrap, nxt - n_chunks, nxt), (slot + ahead) % DEC_SLOTS)

            wait_chunk(slot)
            k_lat, scores = [], []
            for q in range(n_sub):
                k_q = kbuf[slot, q * sub:(q + 1) * sub].reshape(sub_keys, KV_RANK).astype(BF16)
                kr_q = jnp.concatenate([krbuf[slot, q * sub + j] for j in range(sub)], axis=1).astype(BF16)
                k_lat.append(k_q)
                scores.append(_dot_nt(ql, k_q) + _dot(qr, kr_q))
            s = jnp.concatenate(scores, axis=1)
            m_prev = m_ref[...]
            m_new = jnp.maximum(m_prev, jnp.max(s, axis=-1, keepdims=True))
            alpha = jnp.exp(m_prev - m_new)
            p = jnp.exp(s - m_new)
            l_ref[...] = alpha * l_ref[...] + jnp.sum(p, axis=-1, keepdims=True)
            p = p.astype(BF16)
            pv = _dot(p[:, :sub_keys], k_lat[0])
            for q in range(1, n_sub):
                pv = pv + _dot(p[:, q * sub_keys:(q + 1) * sub_keys], k_lat[q])
            acc_ref[...] = alpha * acc_ref[...] + pv
            m_ref[...] = m_new
        return carry

    lax.fori_loop(0, n_chunks // DEC_SLOTS, ring_round, 0)

    o = (acc_ref[...] / l_ref[...]).astype(BF16)
    z = _dot(o, wuv_ref[...])
    head = lax.broadcasted_iota(jnp.int32, z.shape, 0)
    col_head = lax.broadcasted_iota(jnp.int32, z.shape, 1) // MLA_V
    o_ref[...] = jnp.sum(jnp.where(head == col_head, z, 0.0), axis=0, keepdims=True).astype(BF16)


def _decode_attention(page_table, qlat, qrope, kself, wuv_all, cache_ckv, cache_kr_t):
    bd, nh, r = qlat.shape
    n_pages = page_table.shape[1]
    pages = min(n_pages // DEC_SLOTS, DEC_PAGES_PER_STEP)
    n_chunks = n_pages // pages
    assert n_chunks * pages == n_pages and n_chunks % DEC_SLOTS == 0 and pages % min(pages, DEC_SUB_PAGES) == 0

    per_req = lambda shape: pl.BlockSpec((None,) + shape, lambda b, pt: (b, 0, 0))
    grid_spec = pltpu.PrefetchScalarGridSpec(
        num_scalar_prefetch=1,
        grid=(bd,),
        in_specs=[per_req((nh, r)), per_req((nh, MLA_ROPE)), per_req((1, r + MLA_ROPE)),
                  pl.BlockSpec(wuv_all.shape, lambda b, pt: (0, 0)),
                  pl.BlockSpec(memory_space=pl.ANY), pl.BlockSpec(memory_space=pl.ANY)],
        out_specs=per_req((1, nh * MLA_V)),
        scratch_shapes=[
            pltpu.VMEM((DEC_SLOTS, pages, PAGE_SIZE, r), F32),
            pltpu.VMEM((DEC_SLOTS, pages, MLA_ROPE, PAGE_SIZE), F32),
            pltpu.SemaphoreType.DMA((DEC_SLOTS,)),
            pltpu.SemaphoreType.DMA((DEC_SLOTS,)),
            pltpu.VMEM((nh, 1), F32), pltpu.VMEM((nh, 1), F32), pltpu.VMEM((nh, r), F32),
        ],
    )
    return pl.pallas_call(
        functools.partial(_dec_attn_body, pages=pages, n_chunks=n_chunks),
        grid_spec=grid_spec,
        out_shape=jax.ShapeDtypeStruct((bd, 1, nh * MLA_V), BF16),
        compiler_params=_params("arbitrary"),
        name="decode_attention",
    )(page_table, qlat, qrope, kself, wuv_all, cache_ckv, cache_kr_t)


def _dec_ret_body(rq_ref, rk_ref, rv_ref, s_ref, on_ref, snew_ref):
    eye = (lax.broadcasted_iota(jnp.int32, (RET_DK, RET_DK), 0)
           == lax.broadcasted_iota(jnp.int32, (RET_DK, RET_DK), 1))

    def column(x):
        return jnp.sum(jnp.where(eye, jnp.broadcast_to(x, (RET_DK, RET_DK)), 0.0), axis=1, keepdims=True)

    for req in range(rq_ref.shape[0]):
        for hd in range(RET_HEADS):
            gamma = math.exp(RET_LOG_GAMMA[hd])
            q = rq_ref[req, :, hd * RET_DK:(hd + 1) * RET_DK].astype(F32)
            k = rk_ref[req, :, hd * RET_DK:(hd + 1) * RET_DK].astype(F32)
            v = rv_ref[req, :, hd * RET_DV:(hd + 1) * RET_DV].astype(F32)
            s_old = s_ref[req, hd]
            qs = jnp.sum(column(q) * s_old, axis=0, keepdims=True)
            o = jnp.sum(q * k, axis=-1, keepdims=True) * v + qs * gamma
            snew_ref[req, hd] = gamma * s_old + column(k) * v
            on_ref[req, :, hd * RET_DV:(hd + 1) * RET_DV] = _group_norm(o).astype(BF16)


def _decode_retention(rq, rk, rv, state):
    bd = rq.shape[0]
    per = math.gcd(bd, DEC_RET_REQUESTS)
    row = lambda width: pl.BlockSpec((per, 1, width), lambda b: (b, 0, 0))
    st = pl.BlockSpec((per, RET_HEADS, RET_DK, RET_DV), lambda b: (b, 0, 0, 0))
    return pl.pallas_call(
        _dec_ret_body,
        grid=(bd // per,),
        in_specs=[row(rq.shape[2]), row(rk.shape[2]), row(rv.shape[2]), st],
        out_specs=[row(rv.shape[2]), st],
        out_shape=[jax.ShapeDtypeStruct((bd, 1, rv.shape[2]), BF16), jax.ShapeDtypeStruct(state.shape, F32)],
        compiler_params=_params("parallel"),
        name="decode_retention",
    )(rq, rk, rv, state)


def _mix_out_body(h_ref, a_ref, on_ref, srg_ref, sga_ref, sgb_ref, gn_ref, wa_ref, wr_ref, wo_ref, o_ref):
    a = _dot(a_ref[...], wa_ref[...])
    gated = srg_ref[...].astype(F32) * (on_ref[...].astype(F32) * gn_ref[...])
    r = _dot(gated.astype(BF16), wr_ref[...])
    m = sga_ref[...].astype(F32) * a + sgb_ref[...].astype(F32) * r
    o_ref[...] = h_ref[...] + _dot(m.astype(BF16), wo_ref[...])


def _mix_out(h, apre, on, srg, sga, sgb, W):
    m, d = h.shape
    tm = min(m, TOKEN_TILE)
    assert m % tm == 0
    tok = lambda width: pl.BlockSpec((tm, width), lambda i: (i, 0))
    weights = [W["ret_gn"], W["w_mla_o"], W["w_ret_o"], W["w_out"]]
    return pl.pallas_call(
        _mix_out_body,
        grid=(m // tm,),
        in_specs=[tok(d), tok(apre.shape[1]), tok(on.shape[1]), tok(d), tok(d), tok(d)]
                 + [_const_spec(w.shape) for w in weights],
        out_specs=tok(d),
        out_shape=jax.ShapeDtypeStruct((m, d), F32),
        compiler_params=_params("parallel"),
        name="mixer_output",
    )(h, apre, on, srg, sga, sgb, *weights)


def _rope_tables(pos):
    pos = pos.astype(F32)[:, None]

    def table(width, reps):
        half = width // 2
        inv_freq = ROPE_BASE ** (-jnp.arange(half, dtype=F32) / half)
        ang = pos * inv_freq[None, :]
        cos, sin = jnp.cos(ang), jnp.sin(ang)
        return jnp.stack([jnp.tile(jnp.concatenate([cos, cos], axis=1), (1, reps)),
                          jnp.tile(jnp.concatenate([-sin, sin], axis=1), (1, reps))])

    return table(RET_DK, 1), table(MLA_ROPE, MLA_HEADS)


def _prep_weights(ffn1_norm, ffn1_gate, ffn1_up, ffn1_down, mix_norm, w_in, q_norm, kv_norm, w_uq, w_uk, w_uv,
                  w_mla_o, ret_gn, w_ret_o, w_out, ffn2_norm, ffn2_gate, ffn2_up, ffn2_down, final_norm):
    d = w_in.shape[0]
    hr, hv = RET_HEADS * RET_DK, RET_HEADS * RET_DV
    nh, hd = MLA_HEADS, MLA_NOPE + MLA_ROPE

    bounds = [0]
    for width in (Q_RANK, KV_RANK, MLA_ROPE, hr, hr, hv, hv, d, d):
        bounds.append(bounds[-1] + width)
    assert bounds[-1] == w_in.shape[1]
    w_q, w_kv, w_kr, w_rq, w_rk, w_rv, w_rg, w_ga, w_gb = (
        w_in[:, bounds[i]:bounds[i + 1]].astype(BF16) for i in range(9))

    w_uq3 = w_uq.reshape(Q_RANK, nh, hd)
    zeros = jnp.zeros((MLA_NOPE, KV_RANK), F32)
    w_uk_t = w_uk.transpose(1, 2, 0)
    w_uk_pair = jnp.stack([jnp.block([[w_uk_t[2 * p], zeros], [zeros, w_uk_t[2 * p + 1]]])
                           for p in range(nh // 2)]).astype(BF16)
    w_uv_t = w_uv.transpose(1, 0, 2)
    zv = jnp.zeros((KV_RANK, MLA_V), F32)
    w_uv_pair = jnp.stack([jnp.block([[w_uv_t[2 * p], zv], [zv, w_uv_t[2 * p + 1]]])
                           for p in range(nh // 2)]).astype(BF16)
    row = lambda v: v.reshape(1, -1)
    return dict(
        ffn1_norm=row(ffn1_norm), ffn1=(ffn1_gate, ffn1_up, ffn1_down),
        ffn2_norm=row(ffn2_norm), ffn2=(ffn2_gate, ffn2_up, ffn2_down), final_norm=row(final_norm),
        mix_norm=row(mix_norm), w_q=w_q, w_kv=w_kv, w_kr=jnp.tile(w_kr, (1, nh)), w_rq=w_rq, w_rk=w_rk, w_rv=w_rv,
        w_rg=w_rg, w_ga=w_ga, w_gb=w_gb, q_norm=row(q_norm), kv_norm=row(kv_norm),
        w_uq_nope=w_uq3[:, :, :MLA_NOPE].reshape(Q_RANK, nh * MLA_NOPE).astype(BF16),
        w_uq_rope=w_uq3[:, :, MLA_NOPE:].reshape(Q_RANK, nh * MLA_ROPE).astype(BF16),
        w_uk_pair=w_uk_pair, w_uv_pair=w_uv_pair, w_uv_all=w_uv.reshape(KV_RANK, nh * MLA_V).astype(BF16),
        ret_gn=row(ret_gn), w_mla_o=w_mla_o.astype(BF16), w_ret_o=w_ret_o.astype(BF16), w_out=w_out.astype(BF16),
    )


def kernel(x_prompt, x_sample, cache_ckv, cache_krope, state_ret, page_table, meta_tokens, ffn1_norm, ffn1_gate, ffn1_up, ffn1_down, mix_norm, w_in, q_norm, kv_norm, w_uq, w_uk, w_uv, w_mla_o, ret_gn, w_ret_o, w_out, ffn2_norm, ffn2_gate, ffn2_up, ffn2_down, final_norm):
    assert ffn1_gate.shape[0] == 1, "single-layer trunk"
    b, t, d = x_prompt.shape
    bd, dec_seq, _ = x_sample.shape
    assert dec_seq == 1
    n_meta = meta_tokens.shape[0]
    assert n_meta == N_META
    past_len = page_table.shape[1] * PAGE_SIZE
    W = _prep_weights(ffn1_norm[0], ffn1_gate[0], ffn1_up[0], ffn1_down[0], mix_norm[0], w_in[0], q_norm[0],
                      kv_norm[0], w_uq[0], w_uk[0], w_uv[0], w_mla_o[0], ret_gn[0], w_ret_o[0], w_out[0],
                      ffn2_norm[0], ffn2_gate[0], ffn2_up[0], ffn2_down[0], final_norm)

    rope_r, rope_q = _rope_tables(n_meta + jnp.arange(t))
    pos_small = jnp.concatenate([jnp.arange(n_meta), jnp.full((bd,), past_len)])
    rope_r_s, rope_q_s = _rope_tables(pos_small)
    x_small = jnp.concatenate([meta_tokens.astype(F32), x_sample.reshape(bd, d)], axis=0)

    h1, h1s = _ffn(x_prompt.reshape(b * t, d), x_small, W["ffn1_norm"], *W["ffn1"])
    S = _mix_in(h1s.reshape(1, n_meta + bd, d), W, rope_r_s, rope_q_s)
    meta_keys = (S["kt"][0, :, :n_meta].astype(F32), S["kcat"][0, :n_meta, :KV_RANK].astype(F32))
    P = _mix_in(h1.reshape(b, t, d), W, rope_r, rope_q, prev_keys=meta_keys)

    apre = _prompt_attention(P["qlat"], P["qrope"], P["kts"], P["vs"], P["kt"], P["kcat"], W["w_uv_pair"])
    on, s_prompt = _prompt_retention(P["rq"], P["rk"], P["rv"], S["rk"][0, :n_meta], S["rv"][0, :n_meta])

    nh = MLA_HEADS
    qlat_s = S["qlat"][0, :, n_meta:].transpose(1, 0, 2)
    qrope_s = S["qrope"][0, n_meta:].reshape(bd, nh, MLA_ROPE)
    kself = S["kcat"][0, n_meta:, :KV_RANK + MLA_ROPE].reshape(bd, 1, KV_RANK + MLA_ROPE)
    apre_s = _decode_attention(page_table, qlat_s, qrope_s, kself, W["w_uv_all"], cache_ckv[0],
                               cache_krope[0].transpose(0, 2, 1))
    dec = lambda name: S[name][0, n_meta:].reshape(bd, 1, -1)
    on_s, s_sample = _decode_retention(dec("rq"), dec("rk"), dec("rv"), state_ret[0])

    flat = lambda a: a.reshape(b * t, -1)
    h2 = _mix_out(h1, flat(apre), flat(on), flat(P["srg"]), flat(P["sga"]), flat(P["sgb"]), W)
    tail = lambda name: S[name][0, n_meta:]
    h2s = _mix_out(h1s[n_meta:], apre_s.reshape(bd, -1), on_s.reshape(bd, -1), tail("srg"), tail("sga"), tail("sgb"), W)
    y_prompt, y_sample = _ffn(h2, h2s, W["ffn2_norm"], *W["ffn2"], final_norm=W["final_norm"])

    with_meta = lambda small, main: jnp.concatenate(
        [jnp.broadcast_to(small[:, :n_meta], (b, n_meta, small.shape[2])), main], axis=1)[None]
    return (
        y_prompt.reshape(b, t, d),
        y_sample.reshape(bd, 1, d),
        with_meta(S["ckv"], P["ckv"]),
        with_meta(S["kr"], P["kr"]),
        s_prompt[None],
        S["ckv"][0, n_meta:].reshape(1, bd, 1, KV_RANK),
        S["kr"][0, n_meta:].reshape(1, bd, 1, MLA_ROPE),
        s_sample[None],
    )
```

```python
import functools
import math

import jax
import jax.numpy as jnp
import numpy as np
from jax import lax
from jax.experimental import pallas as pl
from jax.experimental.pallas import tpu as pltpu

N_META = 16
PAGE_SIZE = 128
MLA_HEADS = 8
MLA_NOPE = 64
MLA_ROPE = 32
MLA_V = 64
Q_RANK = 384
KV_RANK = 256
MLA_SCALE = (MLA_NOPE + MLA_ROPE) ** -0.5
RET_HEADS = 4
RET_DK = 128
RET_DV = 256
ROPE_BASE = 10000.0
NORM_EPS = 1e-6
RET_LOG_GAMMA = tuple(math.log1p(-(2.0 ** (-5.0 - h))) for h in range(RET_HEADS))

VMEM_LIMIT_BYTES = 56 * 1024 * 1024
FFN_CHUNK = 256
TOKEN_TILE = 512
ATTN_BLOCK = 256
ATTN_GROUP_HEADS = 2
ATTN_ROW_CHUNK = 32
RET_CHUNK = 256
DEC_PAGES_PER_STEP = 64
DEC_SUB_PAGES = 16
DEC_SLOTS = 2
DEC_RET_REQUESTS = 4
MASK_VALUE = -1e30

F32 = jnp.float32
BF16 = jnp.bfloat16


def _params(*sem):
    return pltpu.CompilerParams(dimension_semantics=sem, vmem_limit_bytes=VMEM_LIMIT_BYTES)


def _const_spec(shape):
    n = len(shape)
    return pl.BlockSpec(shape, lambda *_: (0,) * n, pipeline_mode=pl.Buffered(1))


def _rms(x, g):
    return x * lax.rsqrt(jnp.mean(x * x, axis=-1, keepdims=True) + NORM_EPS) * g


def _sigmoid(x):
    return 1.0 / (1.0 + jnp.exp(-x))


def _dot(a, b):
    return jnp.dot(a, b, preferred_element_type=F32)


def _dot_nt(a, b):
    return lax.dot_general(a, b, (((1,), (1,)), ((), ())), preferred_element_type=F32)


def _ffn_body(x_ref, xs_ref, g_ref, wg_ref, wu_ref, wd_ref, *rest, final):
    o_ref, os_ref = rest[-2:]

    def apply(src_ref, dst_ref):
        x = src_ref[...]
        u = _rms(x, g_ref[...]).astype(BF16)
        acc = None
        for c in range(0, wg_ref.shape[1], FFN_CHUNK):
            cols = slice(c, c + FFN_CHUNK)
            gate = _dot(u, wg_ref[:, cols].astype(BF16))
            up = _dot(u, wu_ref[:, cols].astype(BF16))
            act = (gate * _sigmoid(gate) * up).astype(BF16)
            part = _dot(act, wd_ref[cols, :].astype(BF16))
            acc = part if acc is None else acc + part
        y = x + 0.5 * acc
        if final:
            y = _rms(y, rest[0][...])
        dst_ref[...] = y

    last = pl.num_programs(0) - 1
    pl.when(pl.program_id(0) < last)(functools.partial(apply, x_ref, o_ref))
    pl.when(pl.program_id(0) == last)(functools.partial(apply, xs_ref, os_ref))


def _ffn(x, x_small, norm, wg, wu, wd, final_norm=None):
    m, d = x.shape
    tm = min(m, TOKEN_TILE)
    assert m % tm == 0 and wg.shape[1] % FFN_CHUNK == 0
    n = m // tm
    final = final_norm is not None
    main = pl.BlockSpec((tm, d), lambda i: (jnp.minimum(i, n - 1), 0))
    in_specs = [main, _const_spec(x_small.shape), _const_spec((1, d)),
                _const_spec(wg.shape), _const_spec(wu.shape), _const_spec(wd.shape)]
    args = [x, x_small, norm, wg, wu, wd]
    if final:
        in_specs.append(_const_spec((1, d)))
        args.append(final_norm)
    return pl.pallas_call(
        functools.partial(_ffn_body, final=final),
        grid=(n + 1,),
        in_specs=in_specs,
        out_specs=[main, pl.BlockSpec(x_small.shape, lambda i: (0, 0))],
        out_shape=[jax.ShapeDtypeStruct((m, d), F32), jax.ShapeDtypeStruct(x_small.shape, F32)],
        compiler_params=_params("arbitrary"),
        name="ffn_final" if final else "ffn",
    )(*args)


def _rope32(x, cos, sin):
    lane = lax.broadcasted_iota(jnp.int32, x.shape, 1)
    width = x.shape[1]
    partner = jnp.where((lane & 31) < 16, pltpu.roll(x, width - 16, axis=1), pltpu.roll(x, 16, axis=1))
    return x * cos + partner * sin


def _mix_in_body(h_ref, g_ref, win_ref, qn_ref, kvn_ref, wuqn_ref, wuqr_ref, wuk_ref, rope_r_ref, rope_q_ref,
                 *rest, shifted, cols):
    wq_ref, wkv_ref, wkr_ref, wrq_ref, wrk_ref, wrv_ref, wrg_ref, wga_ref, wgb_ref = (
        win_ref.at[:, lo:hi] for lo, hi in cols)
    if shifted:
        (kprev_ref, vprev_ref, qlat_ref, qrope_ref, ckv_ref, kr_ref, kcat_ref, kt_ref, rq_ref, rk_ref, rv_ref,
         srg_ref, sga_ref, sgb_ref, kts_ref, vs_ref, kcarry_ref, vcarry_ref) = rest
    else:
        (qlat_ref, qrope_ref, ckv_ref, kr_ref, kcat_ref, kt_ref, rq_ref, rk_ref, rv_ref,
         srg_ref, sga_ref, sgb_ref) = rest
    if shifted:
        @pl.when(pl.program_id(1) == 0)
        def _():
            kcarry_ref[...] = kprev_ref[...]
            vcarry_ref[...] = vprev_ref[...]

    u = _rms(h_ref[...], g_ref[...]).astype(BF16)
    cos_q, sin_q = rope_q_ref[0], rope_q_ref[1]
    cos_r, sin_r = rope_r_ref[0], rope_r_ref[1]

    cq = _rms(_dot(u, wq_ref[...]), qn_ref[...]).astype(BF16)
    q_nope = (_dot(cq, wuqn_ref[...]) * MLA_SCALE).astype(BF16)
    for p in range(MLA_HEADS // 2):
        pair = _dot(q_nope[:, p * 128:(p + 1) * 128], wuk_ref[p])
        qlat_ref[2 * p] = pair[:, :KV_RANK].astype(BF16)
        qlat_ref[2 * p + 1] = pair[:, KV_RANK:].astype(BF16)
    q_rope = _rope32(_dot(cq, wuqr_ref[...]), cos_q, sin_q) * MLA_SCALE
    qrope_ref[...] = q_rope.astype(BF16)

    ckv = _rms(_dot(u, wkv_ref[...]), kvn_ref[...])
    ckv_ref[...] = ckv
    kcat_ref[:, :KV_RANK] = ckv.astype(BF16)
    kr = _rope32(_dot(u, wkr_ref[...]), cos_q, sin_q)
    kr_ref[...] = kr[:, :MLA_ROPE]
    kcat_ref[:, KV_RANK:] = kr.astype(BF16)
    k_t = jnp.concatenate([ckv, kr], axis=1).T
    kt_ref[...] = k_t.astype(BF16)
    if shifted:
        keep = k_t.shape[1] - N_META
        k_sh = jnp.concatenate([kcarry_ref[...], k_t[:, :keep]], axis=1).astype(BF16)
        blk = kts_ref.shape[2]
        for j in range(kts_ref.shape[0]):
            kts_ref[j] = k_sh[:, j * blk:(j + 1) * blk]
        vs_ref[...] = jnp.concatenate([vcarry_ref[...], ckv[:keep]], axis=0).astype(BF16)
        kcarry_ref[...] = k_t[:, keep:]
        vcarry_ref[...] = ckv[keep:]

    rq = _dot(u, wrq_ref[...])
    rk = _dot(u, wrk_ref[...])
    for hd in range(RET_HEADS):
        sl = slice(hd * RET_DK, (hd + 1) * RET_DK)
        xq, xk = rq[:, sl], rk[:, sl]
        rq_ref[:, sl] = (xq * cos_r + pltpu.roll(xq, RET_DK // 2, axis=1) * sin_r).astype(BF16)
        rk_ref[:, sl] = ((xk * cos_r + pltpu.roll(xk, RET_DK // 2, axis=1) * sin_r) * (RET_DK ** -0.5)).astype(BF16)
    rv_ref[...] = _dot(u, wrv_ref[...]).astype(BF16)

    rg = _dot(u, wrg_ref[...])
    srg_ref[...] = (rg * _sigmoid(rg)).astype(BF16)
    sga_ref[...] = _sigmoid(_dot(u, wga_ref[...])).astype(BF16)
    sgb_ref[...] = _sigmoid(_dot(u, wgb_ref[...])).astype(BF16)


def _mix_in(h, W, rope_r, rope_q, prev_keys=None):
    b, t, d = h.shape
    tm = min(t, TOKEN_TILE)
    assert t % tm == 0
    hr = RET_HEADS * RET_DK
    hv = RET_HEADS * RET_DV
    qw = MLA_HEADS * MLA_ROPE
    shifted = prev_keys is not None
    weights = [W["mix_norm"], W["w_in"], W["q_norm"], W["kv_norm"], W["w_uq_nope"], W["w_uq_rope"], W["w_uk_pair"]]
    tok = lambda width: pl.BlockSpec((None, tm, width), lambda bi, ti: (bi, ti, 0))
    in_specs = ([tok(d)] + [_const_spec(w.shape) for w in weights]
                + [pl.BlockSpec((2, tm, RET_DK), lambda bi, ti: (0, ti, 0)),
                   pl.BlockSpec((2, tm, qw), lambda bi, ti: (0, ti, 0))])
    out_specs = [
        pl.BlockSpec((None, MLA_HEADS, tm, KV_RANK), lambda bi, ti: (bi, 0, ti, 0)),
        tok(qw), tok(KV_RANK), tok(MLA_ROPE), tok(2 * KV_RANK),
        pl.BlockSpec((None, 2 * KV_RANK, tm), lambda bi, ti: (bi, 0, ti)),
        tok(hr), tok(hr), tok(hv), tok(hv), tok(d), tok(d),
    ]
    sds = jax.ShapeDtypeStruct
    out_shape = [
        sds((b, MLA_HEADS, t, KV_RANK), BF16), sds((b, t, qw), BF16), sds((b, t, KV_RANK), F32),
        sds((b, t, MLA_ROPE), F32), sds((b, t, 2 * KV_RANK), BF16), sds((b, 2 * KV_RANK, t), BF16),
        sds((b, t, hr), BF16), sds((b, t, hr), BF16),
        sds((b, t, hv), BF16), sds((b, t, hv), BF16), sds((b, t, d), BF16), sds((b, t, d), BF16),
    ]
    names = ["qlat", "qrope", "ckv", "kr", "kcat", "kt", "rq", "rk", "rv", "srg", "sga", "sgb"]
    extra_in, scratch = [], []
    if shifted:
        blk = ATTN_BLOCK
        assert tm % blk == 0 and N_META % 8 == 0
        extra_in = list(prev_keys)
        in_specs += [_const_spec(a.shape) for a in extra_in]
        out_specs += [pl.BlockSpec((None, tm // blk, 2 * KV_RANK, blk), lambda bi, ti: (bi, ti, 0, 0)), tok(KV_RANK)]
        out_shape += [sds((b, t // blk, 2 * KV_RANK, blk), BF16), sds((b, t, KV_RANK), BF16)]
        names += ["kts", "vs"]
        scratch = [pltpu.VMEM((2 * KV_RANK, N_META), F32), pltpu.VMEM((N_META, KV_RANK), F32)]
    outs = pl.pallas_call(
        functools.partial(_mix_in_body, shifted=shifted, cols=W["w_in_cols"]),
        grid=(b, t // tm),
        in_specs=in_specs,
        out_specs=out_specs,
        out_shape=out_shape,
        scratch_shapes=scratch,
        compiler_params=_params("parallel", "arbitrary"),
        name="mixer_inputs",
    )(h, *weights, rope_r, rope_q, *extra_in)
    return dict(zip(names, outs))


def _softmax_step(s, k_lat, m_ref, l_ref, acc_ref):
    m_prev = m_ref[...]
    m_new = jnp.maximum(m_prev, jnp.max(s, axis=-1, keepdims=True))
    alpha = jnp.exp(m_prev - m_new)
    p = jnp.exp(s - m_new)
    l_ref[...] = alpha * l_ref[...] + jnp.sum(p, axis=-1, keepdims=True)
    acc_ref[...] = alpha * acc_ref[...] + _dot(p.astype(BF16), k_lat)
    m_ref[...] = m_new


def _lanes(x, width):
    return x if width == x.shape[1] else jnp.concatenate([x] * (width // x.shape[1]), axis=1)


def _attn_body(qlat_ref, qrope_ref, kt_ref, v_ref, ktail_ref, vtail_ref, wuv_ref, o_ref,
               qcat_ref, s_ref, p_ref, alpha_ref, m_ref, l_ref, acc_ref):
    blk = qrope_ref.shape[0]
    grp = ATTN_GROUP_HEADS * blk
    n_grp = MLA_HEADS // ATTN_GROUP_HEADS

    def blocks(j, n_blk, first):
        width = n_blk * blk
        v = v_ref[pl.ds(pl.multiple_of(j * blk, blk), width), :]
        if first:
            tok = lax.broadcasted_iota(jnp.int32, (ATTN_ROW_CHUNK, width), 0)
            key = lax.broadcasted_iota(jnp.int32, (ATTN_ROW_CHUNK, width), 1)
        for g in range(n_grp):
            rows = slice(g * grp, (g + 1) * grp)
            for n in range(n_blk):
                s_ref[rows, n * blk:(n + 1) * blk] = _dot(qcat_ref[rows, :], kt_ref[j + n])
        for g in range(n_grp):
            for c in range(g * grp, (g + 1) * grp, ATTN_ROW_CHUNK):
                r = slice(c, c + ATTN_ROW_CHUNK)
                s = s_ref[r, :width]
                if first:
                    s = jnp.where(key <= tok + (c % blk + N_META), s, MASK_VALUE)
                    m_new = jnp.broadcast_to(jnp.max(s, axis=-1, keepdims=True), (ATTN_ROW_CHUNK, 128))
                    p = jnp.exp(s - _lanes(m_new, width))
                    l_ref[r, :] = jnp.broadcast_to(jnp.sum(p, axis=-1, keepdims=True), (ATTN_ROW_CHUNK, 128))
                else:
                    m_prev = m_ref[r, :]
                    m_new = jnp.maximum(m_prev, jnp.max(s, axis=-1, keepdims=True))
                    alpha = jnp.exp(m_prev - m_new)
                    p = jnp.exp(s - _lanes(m_new, width))
                    l_ref[r, :] = alpha * l_ref[r, :] + jnp.sum(p, axis=-1, keepdims=True)
                    alpha_ref[r, :] = alpha
                m_ref[r, :] = m_new
                p_ref[r, :width] = p.astype(BF16)
            rows = slice(g * grp, (g + 1) * grp)
            pv = _dot(p_ref[rows, :width], v)
            acc_ref[rows, :] = pv if first else _lanes(alpha_ref[rows, :], KV_RANK) * acc_ref[rows, :] + pv

    i = pl.program_id(1)
    lane_head = lax.broadcasted_iota(jnp.int32, (blk, MLA_HEADS * MLA_ROPE), 1) // MLA_ROPE
    qr = qrope_ref[...]
    for hd in range(MLA_HEADS):
        qcat_ref[hd * blk:(hd + 1) * blk, :KV_RANK] = qlat_ref[hd]
        qcat_ref[hd * blk:(hd + 1) * blk, KV_RANK:] = jnp.where(lane_head == hd, qr, jnp.zeros_like(qr))

    blocks(i, 1, True)

    def two_full_blocks(jj, carry):
        blocks(2 * jj, 2, False)
        return carry

    lax.fori_loop(0, i // 2, two_full_blocks, 0)

    @pl.when(i % 2 == 1)
    def _():
        blocks(i - 1, 1, False)

    tails = [slice((hd + 1) * blk - N_META, (hd + 1) * blk) for hd in range(MLA_HEADS)]
    gather = lambda ref: jnp.concatenate([ref[r, :] for r in tails], axis=0)
    s = _dot(gather(qcat_ref), ktail_ref[...])
    q_off = lax.broadcasted_iota(jnp.int32, s.shape, 0) & (N_META - 1)
    col = lax.broadcasted_iota(jnp.int32, s.shape, 1) - (128 - N_META)
    s = jnp.where((col >= 0) & (col <= q_off), s, MASK_VALUE)
    m_prev = gather(m_ref)
    m_new = jnp.maximum(m_prev, jnp.max(s, axis=-1, keepdims=True))
    alpha = jnp.exp(m_prev - m_new)
    p = jnp.exp(s - m_new)
    l_new = alpha * gather(l_ref) + jnp.sum(p, axis=-1, keepdims=True)
    acc_new = _lanes(alpha, KV_RANK) * gather(acc_ref) + _dot(p.astype(BF16), vtail_ref[...])
    for hd, r in enumerate(tails):
        part = slice(hd * N_META, (hd + 1) * N_META)
        l_ref[r, :] = l_new[part]
        acc_ref[r, :] = acc_new[part]

    for pr in range(MLA_HEADS // 2):
        halves = []
        for hd in (2 * pr, 2 * pr + 1):
            rows = slice(hd * blk, (hd + 1) * blk)
            halves.append((acc_ref[rows, :] / _lanes(l_ref[rows, :], KV_RANK)).astype(BF16))
        o_ref[:, pr * 2 * MLA_V:(pr + 1) * 2 * MLA_V] = _dot(jnp.concatenate(halves, axis=1), wuv_ref[pr]).astype(BF16)


def _prompt_attention(qlat, qrope, kt_seq, v_seq, kt_tok, kcat, wuv_pair):
    b, nh, t, r = qlat.shape
    nq, blk = kt_seq.shape[1], kt_seq.shape[3]
    assert nq * blk == t and v_seq.shape[1] == t
    assert blk % 128 == 0 and blk & (blk - 1) == 0 and N_META <= 128
    rows = nh * blk
    last = blk // 128
    return pl.pallas_call(
        _attn_body,
        grid=(b, nq),
        in_specs=[
            pl.BlockSpec((None, nh, blk, r), lambda bi, i: (bi, 0, i, 0)),
            pl.BlockSpec((None, blk, qrope.shape[2]), lambda bi, i: (bi, i, 0)),
            pl.BlockSpec((None,) + kt_seq.shape[1:], lambda bi, i: (bi, 0, 0, 0)),
            pl.BlockSpec((None,) + v_seq.shape[1:], lambda bi, i: (bi, 0, 0)),
            pl.BlockSpec((None, kt_tok.shape[1], 128), lambda bi, i: (bi, 0, (i + 1) * last - 1)),
            pl.BlockSpec((None, 128, r), lambda bi, i: (bi, (i + 1) * last - 1, 0)),
            _const_spec(wuv_pair.shape),
        ],
        out_specs=pl.BlockSpec((None, blk, nh * MLA_V), lambda bi, i: (bi, i, 0)),
        out_shape=jax.ShapeDtypeStruct((b, t, nh * MLA_V), BF16),
        scratch_shapes=[
            pltpu.VMEM((rows, 2 * r), BF16),
            pltpu.VMEM((rows, 2 * blk), F32),
            pltpu.VMEM((rows, 2 * blk), BF16),
            pltpu.VMEM((rows, 128), F32),
            pltpu.VMEM((rows, 128), F32),
            pltpu.VMEM((rows, 128), F32),
            pltpu.VMEM((rows, r), F32),
        ],
        compiler_params=_params("parallel", "arbitrary"),
        name="prompt_attention",
    )(qlat, qrope, kt_seq, v_seq, kt_tok, kcat, wuv_pair)


def _group_norm(o):
    mu = jnp.mean(o, axis=-1, keepdims=True)
    c = o - mu
    return c * lax.rsqrt(jnp.mean(c * c, axis=-1, keepdims=True) + NORM_EPS)


def _ret_body(rq_ref, rk_ref, rv_ref, kmeta_ref, vmeta_ref, on_ref, sout_ref, s_ref, decay_ref):
    c = pl.program_id(1)
    ch = rq_ref.shape[0]
    n_meta = kmeta_ref.shape[0]

    @pl.when(c == 0)
    def _():
        j = lax.broadcasted_iota(jnp.int32, (n_meta, 1), 0).astype(F32)
        diff = (lax.broadcasted_iota(jnp.int32, (ch, ch), 0) - lax.broadcasted_iota(jnp.int32, (ch, ch), 1)).astype(F32)
        for hd in range(RET_HEADS):
            kd = kmeta_ref[:, hd * RET_DK:(hd + 1) * RET_DK].astype(F32) * jnp.exp(RET_LOG_GAMMA[hd] * (n_meta - 1.0 - j))
            s_ref[hd] = _dot(kd.T.astype(BF16), vmeta_ref[:, hd * RET_DV:(hd + 1) * RET_DV])
            decay_ref[hd] = jnp.where(diff >= 0, jnp.exp(RET_LOG_GAMMA[hd] * jnp.maximum(diff, 0.0)), 0.0)

    row = lax.broadcasted_iota(jnp.int32, (ch, 1), 0).astype(F32)
    for hd in range(RET_HEADS):
        lg = RET_LOG_GAMMA[hd]
        q = rq_ref[:, hd * RET_DK:(hd + 1) * RET_DK]
        k = rk_ref[:, hd * RET_DK:(hd + 1) * RET_DK]
        v = rv_ref[:, hd * RET_DV:(hd + 1) * RET_DV]
        scores = _dot_nt(q, k) * decay_ref[hd]
        s_old = s_ref[hd]
        o = _dot(scores.astype(BF16), v) + _dot(q, s_old.astype(BF16)) * jnp.exp(lg * (row + 1.0))
        kd = k.astype(F32) * jnp.exp(lg * (ch - 1.0 - row))
        s_ref[hd] = math.exp(lg * ch) * s_old + _dot(kd.T.astype(BF16), v)
        on_ref[:, hd * RET_DV:(hd + 1) * RET_DV] = _group_norm(o).astype(BF16)

    @pl.when(c == pl.num_programs(1) - 1)
    def _():
        sout_ref[...] = s_ref[...]


def _prompt_retention(rq, rk, rv, kmeta, vmeta):
    b, t, hr = rq.shape
    hv = rv.shape[2]
    ch = min(t, RET_CHUNK)
    assert t % ch == 0
    tok = lambda width: pl.BlockSpec((None, ch, width), lambda bi, ci: (bi, ci, 0))
    return pl.pallas_call(
        _ret_body,
        grid=(b, t // ch),
        in_specs=[tok(hr), tok(hr), tok(hv), _const_spec(kmeta.shape), _const_spec(vmeta.shape)],
        out_specs=[tok(hv), pl.BlockSpec((None, RET_HEADS, RET_DK, RET_DV), lambda bi, ci: (bi, 0, 0, 0))],
        out_shape=[jax.ShapeDtypeStruct((b, t, hv), BF16),
                   jax.ShapeDtypeStruct((b, RET_HEADS, RET_DK, RET_DV), F32)],
        scratch_shapes=[pltpu.VMEM((RET_HEADS, RET_DK, RET_DV), F32), pltpu.VMEM((RET_HEADS, ch, ch), F32)],
        compiler_params=_params("parallel", "arbitrary"),
        name="prompt_retention",
    )(rq, rk, rv, kmeta, vmeta)


def _dec_attn_body(pt_ref, qlat_ref, qrope_ref, kself_ref, wuv_ref, ckv_hbm, kr_hbm, o_ref,
                   kbuf, krbuf, sem_k, sem_r, m_ref, l_ref, acc_ref, *, pages, n_chunks):
    b = pl.program_id(0)
    n_req = pl.num_programs(0)
    ahead = DEC_SLOTS - 1
    ql = qlat_ref[...]
    qr = qrope_ref[...]

    def page_copies(page, slot, p):
        lat = pltpu.make_async_copy(ckv_hbm.at[page], kbuf.at[slot, p], sem_k.at[slot])
        rope = pltpu.make_async_copy(kr_hbm.at[page], krbuf.at[slot, p], sem_r.at[slot])
        return lat, rope

    def start_chunk(req, chunk, slot):
        def one(p, carry):
            for cp in page_copies(pt_ref[req, chunk * pages + p], slot, p):
                cp.start()
            return carry
        lax.fori_loop(0, pages, one, 0)

    def wait_chunk(slot):
        def one(p, carry):
            for cp in page_copies(0, slot, p):
                cp.wait()
            return carry
        lax.fori_loop(0, pages, one, 0)

    @pl.when(b == 0)
    def _():
        for g in range(ahead):
            start_chunk(0, g, g)

    ks = kself_ref[...].astype(F32)
    qf = jnp.concatenate([ql, qr], axis=1).astype(F32)
    m_ref[...] = jnp.sum(qf * ks, axis=-1, keepdims=True)
    l_ref[...] = jnp.ones(l_ref.shape, F32)
    acc_ref[...] = jnp.broadcast_to(ks[:, :KV_RANK], acc_ref.shape)

    sub = min(pages, DEC_SUB_PAGES)
    n_sub = pages // sub
    sub_keys = sub * PAGE_SIZE

    def ring_round(rnd, carry):
        for slot in range(DEC_SLOTS):
            chunk = rnd * DEC_SLOTS + slot
            nxt = chunk + ahead
            wrap = nxt >= n_chunks
            nxt_req = jnp.where(wrap, b + 1, b)

            @pl.when(nxt_req < n_req)
            def _():
                start_chunk(nxt_req, jnp.where(wrap, nxt - n_chunks, nxt), (slot + ahead) % DEC_SLOTS)

            wait_chunk(slot)
            k_lat, scores = [], []
            for q in range(n_sub):
                k_q = kbuf[slot, q * sub:(q + 1) * sub].reshape(sub_keys, KV_RANK).astype(BF16)
                kr_q = jnp.concatenate([krbuf[slot, q * sub + j] for j in range(sub)], axis=1).astype(BF16)
                k_lat.append(k_q)
                scores.append(_dot_nt(ql, k_q) + _dot(qr, kr_q))
            s = jnp.concatenate(scores, axis=1)
            m_prev = m_ref[...]
            m_new = jnp.maximum(m_prev, jnp.max(s, axis=-1, keepdims=True))
            alpha = jnp.exp(m_prev - m_new)
            p = jnp.exp(s - m_new)
            l_ref[...] = alpha * l_ref[...] + jnp.sum(p, axis=-1, keepdims=True)
            p = p.astype(BF16)
            pv = _dot(p[:, :sub_keys], k_lat[0])
            for q in range(1, n_sub):
                pv = pv + _dot(p[:, q * sub_keys:(q + 1) * sub_keys], k_lat[q])
            acc_ref[...] = alpha * acc_ref[...] + pv
            m_ref[...] = m_new
        return carry

    lax.fori_loop(0, n_chunks // DEC_SLOTS, ring_round, 0)

    o = (acc_ref[...] / l_ref[...]).astype(BF16)
    z = _dot(o, wuv_ref[...])
    head = lax.broadcasted_iota(jnp.int32, z.shape, 0)
    col_head = lax.broadcasted_iota(jnp.int32, z.shape, 1) // MLA_V
    o_ref[...] = jnp.sum(jnp.where(head == col_head, z, 0.0), axis=0, keepdims=True).astype(BF16)


def _decode_attention(page_table, qlat, qrope, kself, wuv_all, cache_ckv, cache_kr_t):
    bd, nh, r = qlat.shape
    n_pages = page_table.shape[1]
    pages = min(n_pages // DEC_SLOTS, DEC_PAGES_PER_STEP)
    n_chunks = n_pages // pages
    assert n_chunks * pages == n_pages and n_chunks % DEC_SLOTS == 0 and pages % min(pages, DEC_SUB_PAGES) == 0

    per_req = lambda shape: pl.BlockSpec((None,) + shape, lambda b, pt: (b, 0, 0))
    grid_spec = pltpu.PrefetchScalarGridSpec(
        num_scalar_prefetch=1,
        grid=(bd,),
        in_specs=[per_req((nh, r)), per_req((nh, MLA_ROPE)), per_req((1, r + MLA_ROPE)),
                  pl.BlockSpec(wuv_all.shape, lambda b, pt: (0, 0)),
                  pl.BlockSpec(memory_space=pl.ANY), pl.BlockSpec(memory_space=pl.ANY)],
        out_specs=per_req((1, nh * MLA_V)),
        scratch_shapes=[
            pltpu.VMEM((DEC_SLOTS, pages, PAGE_SIZE, r), F32),
            pltpu.VMEM((DEC_SLOTS, pages, MLA_ROPE, PAGE_SIZE), F32),
            pltpu.SemaphoreType.DMA((DEC_SLOTS,)),
            pltpu.SemaphoreType.DMA((DEC_SLOTS,)),
            pltpu.VMEM((nh, 1), F32), pltpu.VMEM((nh, 1), F32), pltpu.VMEM((nh, r), F32),
        ],
    )
    return pl.pallas_call(
        functools.partial(_dec_attn_body, pages=pages, n_chunks=n_chunks),
        grid_spec=grid_spec,
        out_shape=jax.ShapeDtypeStruct((bd, 1, nh * MLA_V), BF16),
        compiler_params=_params("arbitrary"),
        name="decode_attention",
    )(page_table, qlat, qrope, kself, wuv_all, cache_ckv, cache_kr_t)


def _dec_ret_body(rq_ref, rk_ref, rv_ref, s_ref, on_ref, snew_ref):
    eye = (lax.broadcasted_iota(jnp.int32, (RET_DK, RET_DK), 0)
           == lax.broadcasted_iota(jnp.int32, (RET_DK, RET_DK), 1))

    def column(x):
        return jnp.sum(jnp.where(eye, jnp.broadcast_to(x, (RET_DK, RET_DK)), 0.0), axis=1, keepdims=True)

    for req in range(rq_ref.shape[0]):
        for hd in range(RET_HEADS):
            gamma = math.exp(RET_LOG_GAMMA[hd])
            q = rq_ref[req, :, hd * RET_DK:(hd + 1) * RET_DK].astype(F32)
            k = rk_ref[req, :, hd * RET_DK:(hd + 1) * RET_DK].astype(F32)
            v = rv_ref[req, :, hd * RET_DV:(hd + 1) * RET_DV].astype(F32)
            s_old = s_ref[req, hd]
            qs = jnp.sum(column(q) * s_old, axis=0, keepdims=True)
            o = jnp.sum(q * k, axis=-1, keepdims=True) * v + qs * gamma
            snew_ref[req, hd] = gamma * s_old + column(k) * v
            on_ref[req, :, hd * RET_DV:(hd + 1) * RET_DV] = _group_norm(o).astype(BF16)


def _decode_retention(rq, rk, rv, state):
    bd = rq.shape[0]
    per = math.gcd(bd, DEC_RET_REQUESTS)
    row = lambda width: pl.BlockSpec((per, 1, width), lambda b: (b, 0, 0))
    st = pl.BlockSpec((per, RET_HEADS, RET_DK, RET_DV), lambda b: (b, 0, 0, 0))
    return pl.pallas_call(
        _dec_ret_body,
        grid=(bd // per,),
        in_specs=[row(rq.shape[2]), row(rk.shape[2]), row(rv.shape[2]), st],
        out_specs=[row(rv.shape[2]), st],
        out_shape=[jax.ShapeDtypeStruct((bd, 1, rv.shape[2]), BF16), jax.ShapeDtypeStruct(state.shape, F32)],
        compiler_params=_params("parallel"),
        name="decode_retention",
    )(rq, rk, rv, state)


def _mix_out_body(h_ref, a_ref, on_ref, srg_ref, sga_ref, sgb_ref, gn_ref, wa_ref, wr_ref, wo_ref, o_ref):
    a = _dot(a_ref[...], wa_ref[...])
    gated = srg_ref[...].astype(F32) * (on_ref[...].astype(F32) * gn_ref[...])
    r = _dot(gated.astype(BF16), wr_ref[...])
    m = sga_ref[...].astype(F32) * a + sgb_ref[...].astype(F32) * r
    o_ref[...] = h_ref[...] + _dot(m.astype(BF16), wo_ref[...])


def _mix_out(h, apre, on, srg, sga, sgb, W):
    m, d = h.shape
    tm = min(m, TOKEN_TILE)
    assert m % tm == 0
    tok = lambda width: pl.BlockSpec((tm, width), lambda i: (i, 0))
    weights = [W["ret_gn"], W["w_mla_o"], W["w_ret_o"], W["w_out"]]
    return pl.pallas_call(
        _mix_out_body,
        grid=(m // tm,),
        in_specs=[tok(d), tok(apre.shape[1]), tok(on.shape[1]), tok(d), tok(d), tok(d)]
                 + [_const_spec(w.shape) for w in weights],
        out_specs=tok(d),
        out_shape=jax.ShapeDtypeStruct((m, d), F32),
        compiler_params=_params("parallel"),
        name="mixer_output",
    )(h, apre, on, srg, sga, sgb, *weights)


def _rope_tables(pos):
    pos = np.asarray(pos, np.float64)[:, None]

    def table(width, reps):
        half = width // 2
        inv_freq = ROPE_BASE ** (-np.arange(half, dtype=np.float64) / half)
        ang = pos * inv_freq[None, :]
        cos, sin = np.cos(ang), np.sin(ang)
        return np.stack([np.tile(np.concatenate([cos, cos], axis=1), (1, reps)),
                         np.tile(np.concatenate([-sin, sin], axis=1), (1, reps))]).astype(np.float32)

    return table(RET_DK, 1), table(MLA_ROPE, MLA_HEADS)


def _prep_weights(ffn1_norm, ffn1_gate, ffn1_up, ffn1_down, mix_norm, w_in, q_norm, kv_norm, w_uq, w_uk, w_uv,
                  w_mla_o, ret_gn, w_ret_o, w_out, ffn2_norm, ffn2_gate, ffn2_up, ffn2_down, final_norm):
    d = w_in.shape[0]
    hr, hv = RET_HEADS * RET_DK, RET_HEADS * RET_DV
    nh, hd = MLA_HEADS, MLA_NOPE + MLA_ROPE

    bounds = [0]
    for width in (Q_RANK, KV_RANK, MLA_ROPE, hr, hr, hv, hv, d, d):
        bounds.append(bounds[-1] + width)
    assert bounds[-1] == w_in.shape[1]
    pieces = [w_in[:, bounds[i]:bounds[i + 1]] for i in range(9)]
    pieces[2] = jnp.tile(pieces[2], (1, nh))
    w_in_cols, lo = [], 0
    for piece in pieces:
        assert lo % 128 == 0
        w_in_cols.append((lo, lo + piece.shape[1]))
        lo += piece.shape[1]
    w_in_packed = jnp.concatenate(pieces, axis=1).astype(BF16)

    w_uq3 = w_uq.reshape(Q_RANK, nh, hd)
    zeros = jnp.zeros((MLA_NOPE, KV_RANK), F32)
    w_uk_t = w_uk.transpose(1, 2, 0)
    w_uk_pair = jnp.stack([jnp.block([[w_uk_t[2 * p], zeros], [zeros, w_uk_t[2 * p + 1]]])
                           for p in range(nh // 2)]).astype(BF16)
    w_uv_t = w_uv.transpose(1, 0, 2)
    zv = jnp.zeros((KV_RANK, MLA_V), F32)
    w_uv_pair = jnp.stack([jnp.block([[w_uv_t[2 * p], zv], [zv, w_uv_t[2 * p + 1]]])
                           for p in range(nh // 2)]).astype(BF16)
    row = lambda v: v.reshape(1, -1)
    return dict(
        ffn1_norm=row(ffn1_norm), ffn1=(ffn1_gate, ffn1_up, ffn1_down),
        ffn2_norm=row(ffn2_norm), ffn2=(ffn2_gate, ffn2_up, ffn2_down), final_norm=row(final_norm),
        mix_norm=row(mix_norm), w_in=w_in_packed, w_in_cols=tuple(w_in_cols), q_norm=row(q_norm), kv_norm=row(kv_norm),
        w_uq_nope=w_uq3[:, :, :MLA_NOPE].reshape(Q_RANK, nh * MLA_NOPE).astype(BF16),
        w_uq_rope=w_uq3[:, :, MLA_NOPE:].reshape(Q_RANK, nh * MLA_ROPE).astype(BF16),
        w_uk_pair=w_uk_pair, w_uv_pair=w_uv_pair, w_uv_all=w_uv.reshape(KV_RANK, nh * MLA_V).astype(BF16),
        ret_gn=row(ret_gn), w_mla_o=w_mla_o.astype(BF16), w_ret_o=w_ret_o.astype(BF16), w_out=w_out.astype(BF16),
    )


def kernel(x_prompt, x_sample, cache_ckv, cache_krope, state_ret, page_table, meta_tokens, ffn1_norm, ffn1_gate, ffn1_up, ffn1_down, mix_norm, w_in, q_norm, kv_norm, w_uq, w_uk, w_uv, w_mla_o, ret_gn, w_ret_o, w_out, ffn2_norm, ffn2_gate, ffn2_up, ffn2_down, final_norm):
    assert ffn1_gate.shape[0] == 1, "single-layer trunk"
    b, t, d = x_prompt.shape
    bd, dec_seq, _ = x_sample.shape
    assert dec_seq == 1
    n_meta = meta_tokens.shape[0]
    assert n_meta == N_META
    past_len = page_table.shape[1] * PAGE_SIZE
    W = _prep_weights(ffn1_norm[0], ffn1_gate[0], ffn1_up[0], ffn1_down[0], mix_norm[0], w_in[0], q_norm[0],
                      kv_norm[0], w_uq[0], w_uk[0], w_uv[0], w_mla_o[0], ret_gn[0], w_ret_o[0], w_out[0],
                      ffn2_norm[0], ffn2_gate[0], ffn2_up[0], ffn2_down[0], final_norm)

    rope_r, rope_q = _rope_tables(n_meta + np.arange(t))
    rope_r_s, rope_q_s = _rope_tables(np.concatenate([np.arange(n_meta), np.full((bd,), past_len)]))
    x_small = jnp.concatenate([meta_tokens.astype(F32), x_sample.reshape(bd, d)], axis=0)

    h1, h1s = _ffn(x_prompt.reshape(b * t, d), x_small, W["ffn1_norm"], *W["ffn1"])
    S = _mix_in(h1s.reshape(1, n_meta + bd, d), W, rope_r_s, rope_q_s)
    meta_keys = (S["kt"][0, :, :n_meta].astype(F32), S["kcat"][0, :n_meta, :KV_RANK].astype(F32))
    P = _mix_in(h1.reshape(b, t, d), W, rope_r, rope_q, prev_keys=meta_keys)

    apre = _prompt_attention(P["qlat"], P["qrope"], P["kts"], P["vs"], P["kt"], P["kcat"], W["w_uv_pair"])
    on, s_prompt = _prompt_retention(P["rq"], P["rk"], P["rv"], S["rk"][0, :n_meta], S["rv"][0, :n_meta])

    nh = MLA_HEADS
    qlat_s = S["qlat"][0, :, n_meta:].transpose(1, 0, 2)
    qrope_s = S["qrope"][0, n_meta:].reshape(bd, nh, MLA_ROPE)
    kself = S["kcat"][0, n_meta:, :KV_RANK + MLA_ROPE].reshape(bd, 1, KV_RANK + MLA_ROPE)
    apre_s = _decode_attention(page_table, qlat_s, qrope_s, kself, W["w_uv_all"], cache_ckv[0],
                               cache_krope[0].transpose(0, 2, 1))
    dec = lambda name: S[name][0, n_meta:].reshape(bd, 1, -1)
    on_s, s_sample = _decode_retention(dec("rq"), dec("rk"), dec("rv"), state_ret[0])

    flat = lambda a: a.reshape(b * t, -1)
    h2 = _mix_out(h1, flat(apre), flat(on), flat(P["srg"]), flat(P["sga"]), flat(P["sgb"]), W)
    tail = lambda name: S[name][0, n_meta:]
    h2s = _mix_out(h1s[n_meta:], apre_s.reshape(bd, -1), on_s.reshape(bd, -1), tail("srg"), tail("sga"), tail("sgb"), W)
    y_prompt, y_sample = _ffn(h2, h2s, W["ffn2_norm"], *W["ffn2"], final_norm=W["final_norm"])

    with_meta = lambda small, main: jnp.concatenate(
        [jnp.broadcast_to(small[:, :n_meta], (b, n_meta, small.shape[2])), main], axis=1)[None]
    return (
        y_prompt.reshape(b, t, d),
        y_sample.reshape(bd, 1, d),
        with_meta(S["ckv"], P["ckv"]),
        with_meta(S["kr"], P["kr"]),
        s_prompt[None],
        S["ckv"][0, n_meta:].reshape(1, bd, 1, KV_RANK),
        S["kr"][0, n_meta:].reshape(1, bd, 1, MLA_ROPE),
        s_sample[None],
    )
```

```python
import functools
import math

import jax
import jax.numpy as jnp
import numpy as np
from jax import lax
from jax.experimental import pallas as pl
from jax.experimental.pallas import tpu as pltpu

N_META = 16
PAGE_SIZE = 128
MLA_HEADS = 8
MLA_NOPE = 64
MLA_ROPE = 32
MLA_V = 64
Q_RANK = 384
KV_RANK = 256
MLA_SCALE = (MLA_NOPE + MLA_ROPE) ** -0.5
RET_HEADS = 4
RET_DK = 128
RET_DV = 256
ROPE_BASE = 10000.0
NORM_EPS = 1e-6
RET_LOG_GAMMA = tuple(math.log1p(-(2.0 ** (-5.0 - h))) for h in range(RET_HEADS))

VMEM_LIMIT_BYTES = 56 * 1024 * 1024
FFN_CHUNK = 256
TOKEN_TILE = 512
ATTN_BLOCK = 256
ATTN_GROUP_HEADS = 1
ATTN_ROW_CHUNK = 32
RET_CHUNK = 256
DEC_PAGES_PER_STEP = 64
DEC_SUB_PAGES = 16
DEC_SLOTS = 2
DEC_RET_REQUESTS = 4
MASK_VALUE = -1e30

F32 = jnp.float32
BF16 = jnp.bfloat16


def _params(*sem):
    return pltpu.CompilerParams(dimension_semantics=sem, vmem_limit_bytes=VMEM_LIMIT_BYTES)


def _const_spec(shape):
    n = len(shape)
    return pl.BlockSpec(shape, lambda *_: (0,) * n, pipeline_mode=pl.Buffered(1))


def _rms(x, g):
    return x * lax.rsqrt(jnp.mean(x * x, axis=-1, keepdims=True) + NORM_EPS) * g


def _sigmoid(x):
    return 1.0 / (1.0 + jnp.exp(-x))


def _dot(a, b):
    return jnp.dot(a, b, preferred_element_type=F32)


def _dot_nt(a, b):
    return lax.dot_general(a, b, (((1,), (1,)), ((), ())), preferred_element_type=F32)


def _ffn_body(x_ref, xs_ref, g_ref, wg_ref, wu_ref, wd_ref, *rest, final):
    o_ref, os_ref = rest[-2:]

    def apply(src_ref, dst_ref):
        x = src_ref[...]
        u = _rms(x, g_ref[...]).astype(BF16)
        acc = None
        for c in range(0, wg_ref.shape[1], FFN_CHUNK):
            cols = slice(c, c + FFN_CHUNK)
            gate = _dot(u, wg_ref[:, cols].astype(BF16))
            up = _dot(u, wu_ref[:, cols].astype(BF16))
            act = (gate * _sigmoid(gate) * up).astype(BF16)
            part = _dot(act, wd_ref[cols, :].astype(BF16))
            acc = part if acc is None else acc + part
        y = x + 0.5 * acc
        if final:
            y = _rms(y, rest[0][...])
        dst_ref[...] = y

    last = pl.num_programs(0) - 1
    pl.when(pl.program_id(0) < last)(functools.partial(apply, x_ref, o_ref))
    pl.when(pl.program_id(0) == last)(functools.partial(apply, xs_ref, os_ref))


def _ffn(x, x_small, norm, wg, wu, wd, final_norm=None):
    m, d = x.shape
    tm = min(m, TOKEN_TILE)
    assert m % tm == 0 and wg.shape[1] % FFN_CHUNK == 0
    n = m // tm
    final = final_norm is not None
    main = pl.BlockSpec((tm, d), lambda i: (jnp.minimum(i, n - 1), 0))
    in_specs = [main, _const_spec(x_small.shape), _const_spec((1, d)),
                _const_spec(wg.shape), _const_spec(wu.shape), _const_spec(wd.shape)]
    args = [x, x_small, norm, wg, wu, wd]
    if final:
        in_specs.append(_const_spec((1, d)))
        args.append(final_norm)
    return pl.pallas_call(
        functools.partial(_ffn_body, final=final),
        grid=(n + 1,),
        in_specs=in_specs,
        out_specs=[main, pl.BlockSpec(x_small.shape, lambda i: (0, 0))],
        out_shape=[jax.ShapeDtypeStruct((m, d), F32), jax.ShapeDtypeStruct(x_small.shape, F32)],
        compiler_params=_params("arbitrary"),
        name="ffn_final" if final else "ffn",
    )(*args)


def _rope32(x, cos, sin):
    lane = lax.broadcasted_iota(jnp.int32, x.shape, 1)
    width = x.shape[1]
    partner = jnp.where((lane & 31) < 16, pltpu.roll(x, width - 16, axis=1), pltpu.roll(x, 16, axis=1))
    return x * cos + partner * sin


def _mix_in_body(h_ref, g_ref, win_ref, qn_ref, kvn_ref, wuqn_ref, wuqr_ref, wuk_ref, rope_r_ref, rope_q_ref,
                 *rest, shifted, cols):
    def project(x, piece, reps=1):
        lo, hi = cols[piece]
        w = win_ref[lo:hi, :]
        return _dot_nt(x, w if reps == 1 else jnp.concatenate([w] * reps, axis=0))

    if shifted:
        (kprev_ref, vprev_ref, qlat_ref, qrope_ref, ckv_ref, kr_ref, kcat_ref, kt_ref, rq_ref, rk_ref, rv_ref,
         srg_ref, sga_ref, sgb_ref, kts_ref, vs_ref, kcarry_ref, vcarry_ref) = rest
    else:
        (qlat_ref, qrope_ref, ckv_ref, kr_ref, kcat_ref, kt_ref, rq_ref, rk_ref, rv_ref,
         srg_ref, sga_ref, sgb_ref) = rest
    if shifted:
        @pl.when(pl.program_id(1) == 0)
        def _():
            kcarry_ref[...] = kprev_ref[...]
            vcarry_ref[...] = vprev_ref[...]

    u = _rms(h_ref[...], g_ref[...]).astype(BF16)
    cos_q, sin_q = rope_q_ref[0], rope_q_ref[1]
    cos_r, sin_r = rope_r_ref[0], rope_r_ref[1]

    cq = _rms(project(u, 0), qn_ref[...]).astype(BF16)
    q_nope = (_dot(cq, wuqn_ref[...]) * MLA_SCALE).astype(BF16)
    for p in range(MLA_HEADS // 2):
        pair = _dot(q_nope[:, p * 128:(p + 1) * 128], wuk_ref[p])
        qlat_ref[2 * p] = pair[:, :KV_RANK].astype(BF16)
        qlat_ref[2 * p + 1] = pair[:, KV_RANK:].astype(BF16)
    q_rope = _rope32(_dot(cq, wuqr_ref[...]), cos_q, sin_q) * MLA_SCALE
    qrope_ref[...] = q_rope.astype(BF16)

    ckv = _rms(project(u, 1), kvn_ref[...])
    ckv_ref[...] = ckv
    kcat_ref[:, :KV_RANK] = ckv.astype(BF16)
    kr = _rope32(project(u, 2, reps=MLA_HEADS), cos_q, sin_q)
    kr_ref[...] = kr[:, :MLA_ROPE]
    kcat_ref[:, KV_RANK:] = kr.astype(BF16)
    k_t = jnp.concatenate([ckv, kr], axis=1).T
    kt_ref[...] = k_t.astype(BF16)
    if shifted:
        keep = k_t.shape[1] - N_META
        k_sh = jnp.concatenate([kcarry_ref[...], k_t[:, :keep]], axis=1).astype(BF16)
        blk = kts_ref.shape[2]
        for j in range(kts_ref.shape[0]):
            kts_ref[j] = k_sh[:, j * blk:(j + 1) * blk]
        vs_ref[...] = jnp.concatenate([vcarry_ref[...], ckv[:keep]], axis=0).astype(BF16)
        kcarry_ref[...] = k_t[:, keep:]
        vcarry_ref[...] = ckv[keep:]

    rq = project(u, 3)
    rk = project(u, 4)
    for hd in range(RET_HEADS):
        sl = slice(hd * RET_DK, (hd + 1) * RET_DK)
        xq, xk = rq[:, sl], rk[:, sl]
        rq_ref[:, sl] = (xq * cos_r + pltpu.roll(xq, RET_DK // 2, axis=1) * sin_r).astype(BF16)
        rk_ref[:, sl] = ((xk * cos_r + pltpu.roll(xk, RET_DK // 2, axis=1) * sin_r) * (RET_DK ** -0.5)).astype(BF16)
    rv_ref[...] = project(u, 5).astype(BF16)

    rg = project(u, 6)
    srg_ref[...] = (rg * _sigmoid(rg)).astype(BF16)
    sga_ref[...] = _sigmoid(project(u, 7)).astype(BF16)
    sgb_ref[...] = _sigmoid(project(u, 8)).astype(BF16)


def _mix_in(h, W, rope_r, rope_q, prev_keys=None):
    b, t, d = h.shape
    tm = min(t, TOKEN_TILE)
    assert t % tm == 0
    hr = RET_HEADS * RET_DK
    hv = RET_HEADS * RET_DV
    qw = MLA_HEADS * MLA_ROPE
    shifted = prev_keys is not None
    weights = [W["mix_norm"], W["w_in"], W["q_norm"], W["kv_norm"], W["w_uq_nope"], W["w_uq_rope"], W["w_uk_pair"]]
    tok = lambda width: pl.BlockSpec((None, tm, width), lambda bi, ti: (bi, ti, 0))
    in_specs = ([tok(d)] + [_const_spec(w.shape) for w in weights]
                + [pl.BlockSpec((2, tm, RET_DK), lambda bi, ti: (0, ti, 0)),
                   pl.BlockSpec((2, tm, qw), lambda bi, ti: (0, ti, 0))])
    out_specs = [
        pl.BlockSpec((None, MLA_HEADS, tm, KV_RANK), lambda bi, ti: (bi, 0, ti, 0)),
        tok(qw), tok(KV_RANK), tok(MLA_ROPE), tok(2 * KV_RANK),
        pl.BlockSpec((None, 2 * KV_RANK, tm), lambda bi, ti: (bi, 0, ti)),
        tok(hr), tok(hr), tok(hv), tok(hv), tok(d), tok(d),
    ]
    sds = jax.ShapeDtypeStruct
    out_shape = [
        sds((b, MLA_HEADS, t, KV_RANK), BF16), sds((b, t, qw), BF16), sds((b, t, KV_RANK), F32),
        sds((b, t, MLA_ROPE), F32), sds((b, t, 2 * KV_RANK), BF16), sds((b, 2 * KV_RANK, t), BF16),
        sds((b, t, hr), BF16), sds((b, t, hr), BF16),
        sds((b, t, hv), BF16), sds((b, t, hv), BF16), sds((b, t, d), BF16), sds((b, t, d), BF16),
    ]
    names = ["qlat", "qrope", "ckv", "kr", "kcat", "kt", "rq", "rk", "rv", "srg", "sga", "sgb"]
    extra_in, scratch = [], []
    if shifted:
        blk = ATTN_BLOCK
        assert tm % blk == 0 and N_META % 8 == 0
        extra_in = list(prev_keys)
        in_specs += [_const_spec(a.shape) for a in extra_in]
        out_specs += [pl.BlockSpec((None, tm // blk, 2 * KV_RANK, blk), lambda bi, ti: (bi, ti, 0, 0)), tok(KV_RANK)]
        out_shape += [sds((b, t // blk, 2 * KV_RANK, blk), BF16), sds((b, t, KV_RANK), BF16)]
        names += ["kts", "vs"]
        scratch = [pltpu.VMEM((2 * KV_RANK, N_META), F32), pltpu.VMEM((N_META, KV_RANK), F32)]
    outs = pl.pallas_call(
        functools.partial(_mix_in_body, shifted=shifted, cols=W["w_in_cols"]),
        grid=(b, t // tm),
        in_specs=in_specs,
        out_specs=out_specs,
        out_shape=out_shape,
        scratch_shapes=scratch,
        compiler_params=_params("parallel", "arbitrary"),
        name="mixer_inputs",
    )(h, *weights, rope_r, rope_q, *extra_in)
    return dict(zip(names, outs))


def _softmax_step(s, k_lat, m_ref, l_ref, acc_ref):
    m_prev = m_ref[...]
    m_new = jnp.maximum(m_prev, jnp.max(s, axis=-1, keepdims=True))
    alpha = jnp.exp(m_prev - m_new)
    p = jnp.exp(s - m_new)
    l_ref[...] = alpha * l_ref[...] + jnp.sum(p, axis=-1, keepdims=True)
    acc_ref[...] = alpha * acc_ref[...] + _dot(p.astype(BF16), k_lat)
    m_ref[...] = m_new


def _lanes(x, width):
    return x if width == x.shape[1] else jnp.concatenate([x] * (width // x.shape[1]), axis=1)


def _attn_body(qlat_ref, qrope_ref, kt_ref, v_ref, ktail_ref, vtail_ref, wuv_ref, o_ref,
               qcat_ref, s_ref, p_ref, alpha_ref, m_ref, l_ref, acc_ref):
    blk = qrope_ref.shape[0]
    grp = ATTN_GROUP_HEADS * blk
    n_grp = MLA_HEADS // ATTN_GROUP_HEADS

    def blocks(j, n_blk, first):
        width = n_blk * blk
        v = v_ref[pl.ds(pl.multiple_of(j * blk, blk), width), :]
        if first:
            tok = lax.broadcasted_iota(jnp.int32, (ATTN_ROW_CHUNK, width), 0)
            key = lax.broadcasted_iota(jnp.int32, (ATTN_ROW_CHUNK, width), 1)
        for g in range(n_grp):
            rows = slice(g * grp, (g + 1) * grp)
            for n in range(n_blk):
                s_ref[rows, n * blk:(n + 1) * blk] = _dot(qcat_ref[rows, :], kt_ref[j + n])
        for g in range(n_grp):
            for c in range(g * grp, (g + 1) * grp, ATTN_ROW_CHUNK):
                r = slice(c, c + ATTN_ROW_CHUNK)
                s = s_ref[r, :width]
                if first:
                    s = jnp.where(key <= tok + (c % blk + N_META), s, MASK_VALUE)
                    m_new = jnp.broadcast_to(jnp.max(s, axis=-1, keepdims=True), (ATTN_ROW_CHUNK, 128))
                    p = jnp.exp(s - _lanes(m_new, width))
                    l_ref[r, :] = jnp.broadcast_to(jnp.sum(p, axis=-1, keepdims=True), (ATTN_ROW_CHUNK, 128))
                else:
                    m_prev = m_ref[r, :]
                    m_new = jnp.maximum(m_prev, jnp.max(s, axis=-1, keepdims=True))
                    alpha = jnp.exp(m_prev - m_new)
                    p = jnp.exp(s - _lanes(m_new, width))
                    l_ref[r, :] = alpha * l_ref[r, :] + jnp.sum(p, axis=-1, keepdims=True)
                    alpha_ref[r, :] = alpha
                m_ref[r, :] = m_new
                p_ref[r, :width] = p.astype(BF16)
            rows = slice(g * grp, (g + 1) * grp)
            pv = _dot(p_ref[rows, :width], v)
            acc_ref[rows, :] = pv if first else _lanes(alpha_ref[rows, :], KV_RANK) * acc_ref[rows, :] + pv

    i = pl.program_id(1)
    lane_head = lax.broadcasted_iota(jnp.int32, (blk, MLA_HEADS * MLA_ROPE), 1) // MLA_ROPE
    qr = qrope_ref[...]
    for hd in range(MLA_HEADS):
        qcat_ref[hd * blk:(hd + 1) * blk, :KV_RANK] = qlat_ref[hd]
        qcat_ref[hd * blk:(hd + 1) * blk, KV_RANK:] = jnp.where(lane_head == hd, qr, jnp.zeros_like(qr))

    blocks(i, 1, True)

    def two_full_blocks(jj, carry):
        blocks(2 * jj, 2, False)
        return carry

    lax.fori_loop(0, i // 2, two_full_blocks, 0)

    @pl.when(i % 2 == 1)
    def _():
        blocks(i - 1, 1, False)

    tails = [slice((hd + 1) * blk - N_META, (hd + 1) * blk) for hd in range(MLA_HEADS)]
    gather = lambda ref: jnp.concatenate([ref[r, :] for r in tails], axis=0)
    s = _dot(gather(qcat_ref), ktail_ref[...])
    q_off = lax.broadcasted_iota(jnp.int32, s.shape, 0) & (N_META - 1)
    col = lax.broadcasted_iota(jnp.int32, s.shape, 1) - (128 - N_META)
    s = jnp.where((col >= 0) & (col <= q_off), s, MASK_VALUE)
    m_prev = gather(m_ref)
    m_new = jnp.maximum(m_prev, jnp.max(s, axis=-1, keepdims=True))
    alpha = jnp.exp(m_prev - m_new)
    p = jnp.exp(s - m_new)
    l_new = alpha * gather(l_ref) + jnp.sum(p, axis=-1, keepdims=True)
    acc_new = _lanes(alpha, KV_RANK) * gather(acc_ref) + _dot(p.astype(BF16), vtail_ref[...])
    for hd, r in enumerate(tails):
        part = slice(hd * N_META, (hd + 1) * N_META)
        l_ref[r, :] = l_new[part]
        acc_ref[r, :] = acc_new[part]

    for pr in range(MLA_HEADS // 2):
        halves = []
        for hd in (2 * pr, 2 * pr + 1):
            rows = slice(hd * blk, (hd + 1) * blk)
            halves.append((acc_ref[rows, :] / _lanes(l_ref[rows, :], KV_RANK)).astype(BF16))
        o_ref[:, pr * 2 * MLA_V:(pr + 1) * 2 * MLA_V] = _dot(jnp.concatenate(halves, axis=1), wuv_ref[pr]).astype(BF16)


def _prompt_attention(qlat, qrope, kt_seq, v_seq, kt_tok, kcat, wuv_pair):
    b, nh, t, r = qlat.shape
    nq, blk = kt_seq.shape[1], kt_seq.shape[3]
    assert nq * blk == t and v_seq.shape[1] == t
    assert blk % 128 == 0 and blk & (blk - 1) == 0 and N_META <= 128
    rows = nh * blk
    last = blk // 128
    return pl.pallas_call(
        _attn_body,
        grid=(b, nq),
        in_specs=[
            pl.BlockSpec((None, nh, blk, r), lambda bi, i: (bi, 0, i, 0)),
            pl.BlockSpec((None, blk, qrope.shape[2]), lambda bi, i: (bi, i, 0)),
            pl.BlockSpec((None,) + kt_seq.shape[1:], lambda bi, i: (bi, 0, 0, 0)),
            pl.BlockSpec((None,) + v_seq.shape[1:], lambda bi, i: (bi, 0, 0)),
            pl.BlockSpec((None, kt_tok.shape[1], 128), lambda bi, i: (bi, 0, (i + 1) * last - 1)),
            pl.BlockSpec((None, 128, r), lambda bi, i: (bi, (i + 1) * last - 1, 0)),
            _const_spec(wuv_pair.shape),
        ],
        out_specs=pl.BlockSpec((None, blk, nh * MLA_V), lambda bi, i: (bi, i, 0)),
        out_shape=jax.ShapeDtypeStruct((b, t, nh * MLA_V), BF16),
        scratch_shapes=[
            pltpu.VMEM((rows, 2 * r), BF16),
            pltpu.VMEM((rows, 2 * blk), F32),
            pltpu.VMEM((rows, 2 * blk), BF16),
            pltpu.VMEM((rows, 128), F32),
            pltpu.VMEM((rows, 128), F32),
            pltpu.VMEM((rows, 128), F32),
            pltpu.VMEM((rows, r), F32),
        ],
        compiler_params=_params("parallel", "arbitrary"),
        name="prompt_attention",
    )(qlat, qrope, kt_seq, v_seq, kt_tok, kcat, wuv_pair)


def _group_norm(o):
    mu = jnp.mean(o, axis=-1, keepdims=True)
    c = o - mu
    return c * lax.rsqrt(jnp.mean(c * c, axis=-1, keepdims=True) + NORM_EPS)


def _ret_body(rq_ref, rk_ref, rv_ref, kmeta_ref, vmeta_ref, on_ref, sout_ref, s_ref, decay_ref):
    c = pl.program_id(1)
    ch = rq_ref.shape[0]
    n_meta = kmeta_ref.shape[0]

    @pl.when(c == 0)
    def _():
        j = lax.broadcasted_iota(jnp.int32, (n_meta, 1), 0).astype(F32)
        diff = (lax.broadcasted_iota(jnp.int32, (ch, ch), 0) - lax.broadcasted_iota(jnp.int32, (ch, ch), 1)).astype(F32)
        for hd in range(RET_HEADS):
            kd = kmeta_ref[:, hd * RET_DK:(hd + 1) * RET_DK].astype(F32) * jnp.exp(RET_LOG_GAMMA[hd] * (n_meta - 1.0 - j))
            s_ref[hd] = _dot(kd.T.astype(BF16), vmeta_ref[:, hd * RET_DV:(hd + 1) * RET_DV])
            decay_ref[hd] = jnp.where(diff >= 0, jnp.exp(RET_LOG_GAMMA[hd] * jnp.maximum(diff, 0.0)), 0.0)

    row = lax.broadcasted_iota(jnp.int32, (ch, 1), 0).astype(F32)
    for hd in range(RET_HEADS):
        lg = RET_LOG_GAMMA[hd]
        q = rq_ref[:, hd * RET_DK:(hd + 1) * RET_DK]
        k = rk_ref[:, hd * RET_DK:(hd + 1) * RET_DK]
        v = rv_ref[:, hd * RET_DV:(hd + 1) * RET_DV]
        scores = _dot_nt(q, k) * decay_ref[hd]
        s_old = s_ref[hd]
        o = _dot(scores.astype(BF16), v) + _dot(q, s_old.astype(BF16)) * jnp.exp(lg * (row + 1.0))
        kd = k.astype(F32) * jnp.exp(lg * (ch - 1.0 - row))
        s_ref[hd] = math.exp(lg * ch) * s_old + _dot(kd.T.astype(BF16), v)
        on_ref[:, hd * RET_DV:(hd + 1) * RET_DV] = _group_norm(o).astype(BF16)

    @pl.when(c == pl.num_programs(1) - 1)
    def _():
        sout_ref[...] = s_ref[...]


def _prompt_retention(rq, rk, rv, kmeta, vmeta):
    b, t, hr = rq.shape
    hv = rv.shape[2]
    ch = min(t, RET_CHUNK)
    assert t % ch == 0
    tok = lambda width: pl.BlockSpec((None, ch, width), lambda bi, ci: (bi, ci, 0))
    return pl.pallas_call(
        _ret_body,
        grid=(b, t // ch),
        in_specs=[tok(hr), tok(hr), tok(hv), _const_spec(kmeta.shape), _const_spec(vmeta.shape)],
        out_specs=[tok(hv), pl.BlockSpec((None, RET_HEADS, RET_DK, RET_DV), lambda bi, ci: (bi, 0, 0, 0))],
        out_shape=[jax.ShapeDtypeStruct((b, t, hv), BF16),
                   jax.ShapeDtypeStruct((b, RET_HEADS, RET_DK, RET_DV), F32)],
        scratch_shapes=[pltpu.VMEM((RET_HEADS, RET_DK, RET_DV), F32), pltpu.VMEM((RET_HEADS, ch, ch), F32)],
        compiler_params=_params("parallel", "arbitrary"),
        name="prompt_retention",
    )(rq, rk, rv, kmeta, vmeta)


def _dec_attn_body(pt_ref, qlat_ref, qrope_ref, kself_ref, wuv_ref, ckv_hbm, kr_hbm, o_ref,
                   kbuf, krbuf, sem_k, sem_r, m_ref, l_ref, acc_ref, *, pages, n_chunks):
    b = pl.program_id(0)
    n_req = pl.num_programs(0)
    ahead = DEC_SLOTS - 1
    ql = qlat_ref[...]
    qr = qrope_ref[...]

    def page_copies(page, slot, p):
        lat = pltpu.make_async_copy(ckv_hbm.at[page], kbuf.at[slot, p], sem_k.at[slot])
        rope = pltpu.make_async_copy(kr_hbm.at[page], krbuf.at[slot, p], sem_r.at[slot])
        return lat, rope

    def start_chunk(req, chunk, slot):
        def one(p, carry):
            for cp in page_copies(pt_ref[req, chunk * pages + p], slot, p):
                cp.start()
            return carry
        lax.fori_loop(0, pages, one, 0)

    def wait_chunk(slot):
        def one(p, carry):
            for cp in page_copies(0, slot, p):
                cp.wait()
            return carry
        lax.fori_loop(0, pages, one, 0)

    @pl.when(b == 0)
    def _():
        for g in range(ahead):
            start_chunk(0, g, g)

    ks = kself_ref[...].astype(F32)
    qf = jnp.concatenate([ql, qr], axis=1).astype(F32)
    m_ref[...] = jnp.sum(qf * ks, axis=-1, keepdims=True)
    l_ref[...] = jnp.ones(l_ref.shape, F32)
    acc_ref[...] = jnp.broadcast_to(ks[:, :KV_RANK], acc_ref.shape)

    sub = min(pages, DEC_SUB_PAGES)
    n_sub = pages // sub
    sub_keys = sub * PAGE_SIZE

    def ring_round(rnd, carry):
        for slot in range(DEC_SLOTS):
            chunk = rnd * DEC_SLOTS + slot
            nxt = chunk + ahead
            wrap = nxt >= n_chunks
            nxt_req = jnp.where(wrap, b + 1, b)

            @pl.when(nxt_req < n_req)
            def _():
                start_chunk(nxt_req, jnp.where(wrap, nxt - n_chunks, nxt), (slot + ahead) % DEC_SLOTS)

            wait_chunk(slot)
            k_lat, scores = [], []
            for q in range(n_sub):
                k_q = kbuf[slot, q * sub:(q + 1) * sub].reshape(sub_keys, KV_RANK).astype(BF16)
                kr_q = jnp.concatenate([krbuf[slot, q * sub + j] for j in range(sub)], axis=1).astype(BF16)
                k_lat.append(k_q)
                scores.append(_dot_nt(ql, k_q) + _dot(qr, kr_q))
            s = jnp.concatenate(scores, axis=1)
            m_prev = m_ref[...]
            m_new = jnp.maximum(m_prev, jnp.max(s, axis=-1, keepdims=True))
            alpha = jnp.exp(m_prev - m_new)
            p = jnp.exp(s - m_new)
            l_ref[...] = alpha * l_ref[...] + jnp.sum(p, axis=-1, keepdims=True)
            p = p.astype(BF16)
            pv = _dot(p[:, :sub_keys], k_lat[0])
            for q in range(1, n_sub):
                pv = pv + _dot(p[:, q * sub_keys:(q + 1) * sub_keys], k_lat[q])
            acc_ref[...] = alpha * acc_ref[...] + pv
            m_ref[...] = m_new
        return carry

    lax.fori_loop(0, n_chunks // DEC_SLOTS, ring_round, 0)

    o = (acc_ref[...] / l_ref[...]).astype(BF16)
    z = _dot(o, wuv_ref[...])
    head = lax.broadcasted_iota(jnp.int32, z.shape, 0)
    col_head = lax.broadcasted_iota(jnp.int32, z.shape, 1) // MLA_V
    o_ref[...] = jnp.sum(jnp.where(head == col_head, z, 0.0), axis=0, keepdims=True).astype(BF16)


def _decode_attention(page_table, qlat, qrope, kself, wuv_all, cache_ckv, cache_kr_t):
    bd, nh, r = qlat.shape
    n_pages = page_table.shape[1]
    pages = min(n_pages // DEC_SLOTS, DEC_PAGES_PER_STEP)
    n_chunks = n_pages // pages
    assert n_chunks * pages == n_pages and n_chunks % DEC_SLOTS == 0 and pages % min(pages, DEC_SUB_PAGES) == 0

    per_req = lambda shape: pl.BlockSpec((None,) + shape, lambda b, pt: (b, 0, 0))
    grid_spec = pltpu.PrefetchScalarGridSpec(
        num_scalar_prefetch=1,
        grid=(bd,),
        in_specs=[per_req((nh, r)), per_req((nh, MLA_ROPE)), per_req((1, r + MLA_ROPE)),
                  pl.BlockSpec(wuv_all.shape, lambda b, pt: (0, 0)),
                  pl.BlockSpec(memory_space=pl.ANY), pl.BlockSpec(memory_space=pl.ANY)],
        out_specs=per_req((1, nh * MLA_V)),
        scratch_shapes=[
            pltpu.VMEM((DEC_SLOTS, pages, PAGE_SIZE, r), F32),
            pltpu.VMEM((DEC_SLOTS, pages, MLA_ROPE, PAGE_SIZE), F32),
            pltpu.SemaphoreType.DMA((DEC_SLOTS,)),
            pltpu.SemaphoreType.DMA((DEC_SLOTS,)),
            pltpu.VMEM((nh, 1), F32), pltpu.VMEM((nh, 1), F32), pltpu.VMEM((nh, r), F32),
        ],
    )
    return pl.pallas_call(
        functools.partial(_dec_attn_body, pages=pages, n_chunks=n_chunks),
        grid_spec=grid_spec,
        out_shape=jax.ShapeDtypeStruct((bd, 1, nh * MLA_V), BF16),
        compiler_params=_params("arbitrary"),
        name="decode_attention",
    )(page_table, qlat, qrope, kself, wuv_all, cache_ckv, cache_kr_t)


def _dec_ret_body(rq_ref, rk_ref, rv_ref, s_ref, on_ref, snew_ref):
    eye = (lax.broadcasted_iota(jnp.int32, (RET_DK, RET_DK), 0)
           == lax.broadcasted_iota(jnp.int32, (RET_DK, RET_DK), 1))

    def column(x):
        return jnp.sum(jnp.where(eye, jnp.broadcast_to(x, (RET_DK, RET_DK)), 0.0), axis=1, keepdims=True)

    for req in range(rq_ref.shape[0]):
        for hd in range(RET_HEADS):
            gamma = math.exp(RET_LOG_GAMMA[hd])
            q = rq_ref[req, :, hd * RET_DK:(hd + 1) * RET_DK].astype(F32)
            k = rk_ref[req, :, hd * RET_DK:(hd + 1) * RET_DK].astype(F32)
            v = rv_ref[req, :, hd * RET_DV:(hd + 1) * RET_DV].astype(F32)
            s_old = s_ref[req, hd]
            qs = jnp.sum(column(q) * s_old, axis=0, keepdims=True)
            o = jnp.sum(q * k, axis=-1, keepdims=True) * v + qs * gamma
            snew_ref[req, hd] = gamma * s_old + column(k) * v
            on_ref[req, :, hd * RET_DV:(hd + 1) * RET_DV] = _group_norm(o).astype(BF16)


def _decode_retention(rq, rk, rv, state):
    bd = rq.shape[0]
    per = math.gcd(bd, DEC_RET_REQUESTS)
    row = lambda width: pl.BlockSpec((per, 1, width), lambda b: (b, 0, 0))
    st = pl.BlockSpec((per, RET_HEADS, RET_DK, RET_DV), lambda b: (b, 0, 0, 0))
    return pl.pallas_call(
        _dec_ret_body,
        grid=(bd // per,),
        in_specs=[row(rq.shape[2]), row(rk.shape[2]), row(rv.shape[2]), st],
        out_specs=[row(rv.shape[2]), st],
        out_shape=[jax.ShapeDtypeStruct((bd, 1, rv.shape[2]), BF16), jax.ShapeDtypeStruct(state.shape, F32)],
        compiler_params=_params("parallel"),
        name="decode_retention",
    )(rq, rk, rv, state)


def _mix_out_body(h_ref, a_ref, on_ref, srg_ref, sga_ref, sgb_ref, gn_ref, wa_ref, wr_ref, wo_ref, o_ref):
    a = _dot(a_ref[...], wa_ref[...])
    gated = srg_ref[...].astype(F32) * (on_ref[...].astype(F32) * gn_ref[...])
    r = _dot(gated.astype(BF16), wr_ref[...])
    m = sga_ref[...].astype(F32) * a + sgb_ref[...].astype(F32) * r
    o_ref[...] = h_ref[...] + _dot(m.astype(BF16), wo_ref[...])


def _mix_out(h, apre, on, srg, sga, sgb, W):
    m, d = h.shape
    tm = min(m, TOKEN_TILE)
    assert m % tm == 0
    tok = lambda width: pl.BlockSpec((tm, width), lambda i: (i, 0))
    weights = [W["ret_gn"], W["w_mla_o"], W["w_ret_o"], W["w_out"]]
    return pl.pallas_call(
        _mix_out_body,
        grid=(m // tm,),
        in_specs=[tok(d), tok(apre.shape[1]), tok(on.shape[1]), tok(d), tok(d), tok(d)]
                 + [_const_spec(w.shape) for w in weights],
        out_specs=tok(d),
        out_shape=jax.ShapeDtypeStruct((m, d), F32),
        compiler_params=_params("parallel"),
        name="mixer_output",
    )(h, apre, on, srg, sga, sgb, *weights)


def _rope_tables(pos):
    pos = np.asarray(pos, np.float64)[:, None]

    def table(width, reps):
        half = width // 2
        inv_freq = ROPE_BASE ** (-np.arange(half, dtype=np.float64) / half)
        ang = pos * inv_freq[None, :]
        cos, sin = np.cos(ang), np.sin(ang)
        return np.stack([np.tile(np.concatenate([cos, cos], axis=1), (1, reps)),
                         np.tile(np.concatenate([-sin, sin], axis=1), (1, reps))]).astype(np.float32)

    return table(RET_DK, 1), table(MLA_ROPE, MLA_HEADS)


def _prep_weights(ffn1_norm, ffn1_gate, ffn1_up, ffn1_down, mix_norm, w_in, q_norm, kv_norm, w_uq, w_uk, w_uv,
                  w_mla_o, ret_gn, w_ret_o, w_out, ffn2_norm, ffn2_gate, ffn2_up, ffn2_down, final_norm):
    d = w_in.shape[0]
    hr, hv = RET_HEADS * RET_DK, RET_HEADS * RET_DV
    nh, hd = MLA_HEADS, MLA_NOPE + MLA_ROPE

    bounds = [0]
    for width in (Q_RANK, KV_RANK, MLA_ROPE, hr, hr, hv, hv, d, d):
        bounds.append(bounds[-1] + width)
    assert bounds[-1] == w_in.shape[1] and all(bd % 16 == 0 for bd in bounds)
    w_in_cols = [(bounds[i], bounds[i + 1]) for i in range(9)]
    w_in_packed = w_in.T.astype(BF16)

    w_uq3 = w_uq.reshape(Q_RANK, nh, hd)
    zeros = jnp.zeros((MLA_NOPE, KV_RANK), F32)
    w_uk_t = w_uk.transpose(1, 2, 0)
    w_uk_pair = jnp.stack([jnp.block([[w_uk_t[2 * p], zeros], [zeros, w_uk_t[2 * p + 1]]])
                           for p in range(nh // 2)]).astype(BF16)
    w_uv_t = w_uv.transpose(1, 0, 2)
    zv = jnp.zeros((KV_RANK, MLA_V), F32)
    w_uv_pair = jnp.stack([jnp.block([[w_uv_t[2 * p], zv], [zv, w_uv_t[2 * p + 1]]])
                           for p in range(nh // 2)]).astype(BF16)
    row = lambda v: v.reshape(1, -1)
    return dict(
        ffn1_norm=row(ffn1_norm), ffn1=(ffn1_gate, ffn1_up, ffn1_down),
        ffn2_norm=row(ffn2_norm), ffn2=(ffn2_gate, ffn2_up, ffn2_down), final_norm=row(final_norm),
        mix_norm=row(mix_norm), w_in=w_in_packed, w_in_cols=tuple(w_in_cols), q_norm=row(q_norm), kv_norm=row(kv_norm),
        w_uq_nope=w_uq3[:, :, :MLA_NOPE].reshape(Q_RANK, nh * MLA_NOPE).astype(BF16),
        w_uq_rope=w_uq3[:, :, MLA_NOPE:].reshape(Q_RANK, nh * MLA_ROPE).astype(BF16),
        w_uk_pair=w_uk_pair, w_uv_pair=w_uv_pair, w_uv_all=w_uv.reshape(KV_RANK, nh * MLA_V).astype(BF16),
        ret_gn=row(ret_gn), w_mla_o=w_mla_o.astype(BF16), w_ret_o=w_ret_o.astype(BF16), w_out=w_out.astype(BF16),
    )


def kernel(x_prompt, x_sample, cache_ckv, cache_krope, state_ret, page_table, meta_tokens, ffn1_norm, ffn1_gate, ffn1_up, ffn1_down, mix_norm, w_in, q_norm, kv_norm, w_uq, w_uk, w_uv, w_mla_o, ret_gn, w_ret_o, w_out, ffn2_norm, ffn2_gate, ffn2_up, ffn2_down, final_norm):
    assert ffn1_gate.shape[0] == 1, "single-layer trunk"
    b, t, d = x_prompt.shape
    bd, dec_seq, _ = x_sample.shape
    assert dec_seq == 1
    n_meta = meta_tokens.shape[0]
    assert n_meta == N_META
    past_len = page_table.shape[1] * PAGE_SIZE
    W = _prep_weights(ffn1_norm[0], ffn1_gate[0], ffn1_up[0], ffn1_down[0], mix_norm[0], w_in[0], q_norm[0],
                      kv_norm[0], w_uq[0], w_uk[0], w_uv[0], w_mla_o[0], ret_gn[0], w_ret_o[0], w_out[0],
                      ffn2_norm[0], ffn2_gate[0], ffn2_up[0], ffn2_down[0], final_norm)

    rope_r, rope_q = _rope_tables(n_meta + np.arange(t))
    rope_r_s, rope_q_s = _rope_tables(np.concatenate([np.arange(n_meta), np.full((bd,), past_len)]))
    x_small = jnp.concatenate([meta_tokens.astype(F32), x_sample.reshape(bd, d)], axis=0)

    h1, h1s = _ffn(x_prompt.reshape(b * t, d), x_small, W["ffn1_norm"], *W["ffn1"])
    S = _mix_in(h1s.reshape(1, n_meta + bd, d), W, rope_r_s, rope_q_s)
    meta_keys = (S["kt"][0, :, :n_meta].astype(F32), S["kcat"][0, :n_meta, :KV_RANK].astype(F32))
    P = _mix_in(h1.reshape(b, t, d), W, rope_r, rope_q, prev_keys=meta_keys)

    apre = _prompt_attention(P["qlat"], P["qrope"], P["kts"], P["vs"], P["kt"], P["kcat"], W["w_uv_pair"])
    on, s_prompt = _prompt_retention(P["rq"], P["rk"], P["rv"], S["rk"][0, :n_meta], S["rv"][0, :n_meta])

    nh = MLA_HEADS
    qlat_s = S["qlat"][0, :, n_meta:].transpose(1, 0, 2)
    qrope_s = S["qrope"][0, n_meta:].reshape(bd, nh, MLA_ROPE)
    kself = S["kcat"][0, n_meta:, :KV_RANK + MLA_ROPE].reshape(bd, 1, KV_RANK + MLA_ROPE)
    apre_s = _decode_attention(page_table, qlat_s, qrope_s, kself, W["w_uv_all"], cache_ckv[0],
                               cache_krope[0].transpose(0, 2, 1))
    dec = lambda name: S[name][0, n_meta:].reshape(bd, 1, -1)
    on_s, s_sample = _decode_retention(dec("rq"), dec("rk"), dec("rv"), state_ret[0])

    flat = lambda a: a.reshape(b * t, -1)
    h2 = _mix_out(h1, flat(apre), flat(on), flat(P["srg"]), flat(P["sga"]), flat(P["sgb"]), W)
    tail = lambda name: S[name][0, n_meta:]
    h2s = _mix_out(h1s[n_meta:], apre_s.reshape(bd, -1), on_s.reshape(bd, -1), tail("srg"), tail("sga"), tail("sgb"), W)
    y_prompt, y_sample = _ffn(h2, h2s, W["ffn2_norm"], *W["ffn2"], final_norm=W["final_norm"])

    with_meta = lambda small, main: jnp.concatenate(
        [jnp.broadcast_to(small[:, :n_meta], (b, n_meta, small.shape[2])), main], axis=1)[None]
    return (
        y_prompt.reshape(b, t, d),
        y_sample.reshape(bd, 1, d),
        with_meta(S["ckv"], P["ckv"]),
        with_meta(S["kr"], P["kr"]),
        s_prompt[None],
        S["ckv"][0, n_meta:].reshape(1, bd, 1, KV_RANK),
        S["kr"][0, n_meta:].reshape(1, bd, 1, MLA_ROPE),
        s_sample[None],
    )
```

```python
import functools
import math

import jax
import jax.numpy as jnp
import numpy as np
from jax import lax
from jax.experimental import pallas as pl
from jax.experimental.pallas import tpu as pltpu

N_META = 16
PAGE_SIZE = 128
MLA_HEADS = 8
MLA_NOPE = 64
MLA_ROPE = 32
MLA_V = 64
Q_RANK = 384
KV_RANK = 256
MLA_SCALE = (MLA_NOPE + MLA_ROPE) ** -0.5
RET_HEADS = 4
RET_DK = 128
RET_DV = 256
ROPE_BASE = 10000.0
NORM_EPS = 1e-6
RET_LOG_GAMMA = tuple(math.log1p(-(2.0 ** (-5.0 - h))) for h in range(RET_HEADS))

VMEM_LIMIT_BYTES = 56 * 1024 * 1024
FFN_CHUNK = 256
TOKEN_TILE = 512
ATTN_BLOCK = 256
ATTN_GROUP_HEADS = 1
ATTN_ROW_CHUNK = 32
RET_CHUNK = 256
RET_SEQS = 4
DEC_PAGES_PER_STEP = 64
DEC_SUB_PAGES = 16
DEC_SLOTS = 2
DEC_RET_REQUESTS = 4
MASK_VALUE = -1e30

F32 = jnp.float32
BF16 = jnp.bfloat16


def _params(*sem):
    return pltpu.CompilerParams(dimension_semantics=sem, vmem_limit_bytes=VMEM_LIMIT_BYTES)


def _const_spec(shape):
    n = len(shape)
    return pl.BlockSpec(shape, lambda *_: (0,) * n, pipeline_mode=pl.Buffered(1))


def _rms(x, g):
    return x * lax.rsqrt(jnp.mean(x * x, axis=-1, keepdims=True) + NORM_EPS) * g


def _sigmoid(x):
    return 1.0 / (1.0 + jnp.exp(-x))


def _dot(a, b):
    return jnp.dot(a, b, preferred_element_type=F32)


def _dot_nt(a, b):
    return lax.dot_general(a, b, (((1,), (1,)), ((), ())), preferred_element_type=F32)


def _ffn_body(x_ref, xs_ref, g_ref, wg_ref, wu_ref, wd_ref, *rest, final):
    o_ref, os_ref = rest[-2:]

    def apply(src_ref, dst_ref):
        x = src_ref[...]
        u = _rms(x, g_ref[...]).astype(BF16)
        acc = None
        for c in range(0, wg_ref.shape[1], FFN_CHUNK):
            cols = slice(c, c + FFN_CHUNK)
            gate = _dot(u, wg_ref[:, cols].astype(BF16))
            up = _dot(u, wu_ref[:, cols].astype(BF16))
            act = (gate * _sigmoid(gate) * up).astype(BF16)
            part = _dot(act, wd_ref[cols, :].astype(BF16))
            acc = part if acc is None else acc + part
        y = x + 0.5 * acc
        if final:
            y = _rms(y, rest[0][...])
        dst_ref[...] = y

    last = pl.num_programs(0) - 1
    pl.when(pl.program_id(0) < last)(functools.partial(apply, x_ref, o_ref))
    pl.when(pl.program_id(0) == last)(functools.partial(apply, xs_ref, os_ref))


def _ffn(x, x_small, norm, wg, wu, wd, final_norm=None):
    m, d = x.shape
    tm = min(m, TOKEN_TILE)
    assert m % tm == 0 and wg.shape[1] % FFN_CHUNK == 0
    n = m // tm
    final = final_norm is not None
    main = pl.BlockSpec((tm, d), lambda i: (jnp.minimum(i, n - 1), 0))
    in_specs = [main, _const_spec(x_small.shape), _const_spec((1, d)),
                _const_spec(wg.shape), _const_spec(wu.shape), _const_spec(wd.shape)]
    args = [x, x_small, norm, wg, wu, wd]
    if final:
        in_specs.append(_const_spec((1, d)))
        args.append(final_norm)
    return pl.pallas_call(
        functools.partial(_ffn_body, final=final),
        grid=(n + 1,),
        in_specs=in_specs,
        out_specs=[main, pl.BlockSpec(x_small.shape, lambda i: (0, 0))],
        out_shape=[jax.ShapeDtypeStruct((m, d), F32), jax.ShapeDtypeStruct(x_small.shape, F32)],
        compiler_params=_params("arbitrary"),
        name="ffn_final" if final else "ffn",
    )(*args)


def _rope32(x, cos, sin):
    lane = lax.broadcasted_iota(jnp.int32, x.shape, 1)
    width = x.shape[1]
    partner = jnp.where((lane & 31) < 16, pltpu.roll(x, width - 16, axis=1), pltpu.roll(x, 16, axis=1))
    return x * cos + partner * sin


def _mix_in_body(h_ref, g_ref, win_ref, qn_ref, kvn_ref, wuqn_ref, wuqr_ref, wuk_ref, rope_r_ref, rope_q_ref,
                 *rest, shifted, cols):
    def project(x, piece, reps=1):
        lo, hi = cols[piece]
        w = win_ref[lo:hi, :]
        return _dot_nt(x, w if reps == 1 else jnp.concatenate([w] * reps, axis=0))

    if shifted:
        (kprev_ref, vprev_ref, qlat_ref, qrope_ref, ckv_ref, kr_ref, kcat_ref, kt_ref, rq_ref, rk_ref, rv_ref,
         srg_ref, sga_ref, sgb_ref, kts_ref, vs_ref, kcarry_ref, vcarry_ref) = rest
    else:
        (qlat_ref, qrope_ref, ckv_ref, kr_ref, kcat_ref, kt_ref, rq_ref, rk_ref, rv_ref,
         srg_ref, sga_ref, sgb_ref) = rest
    if shifted:
        @pl.when(pl.program_id(1) == 0)
        def _():
            kcarry_ref[...] = kprev_ref[...]
            vcarry_ref[...] = vprev_ref[...]

    u = _rms(h_ref[...], g_ref[...]).astype(BF16)
    cos_q, sin_q = rope_q_ref[0], rope_q_ref[1]
    cos_r, sin_r = rope_r_ref[0], rope_r_ref[1]

    cq = _rms(project(u, 0), qn_ref[...]).astype(BF16)
    q_nope = (_dot(cq, wuqn_ref[...]) * MLA_SCALE).astype(BF16)
    for p in range(MLA_HEADS // 2):
        pair = _dot(q_nope[:, p * 128:(p + 1) * 128], wuk_ref[p])
        qlat_ref[2 * p] = pair[:, :KV_RANK].astype(BF16)
        qlat_ref[2 * p + 1] = pair[:, KV_RANK:].astype(BF16)
    q_rope = _rope32(_dot(cq, wuqr_ref[...]), cos_q, sin_q) * MLA_SCALE
    qrope_ref[...] = q_rope.astype(BF16)

    ckv = _rms(project(u, 1), kvn_ref[...])
    ckv_ref[...] = ckv
    kcat_ref[:, :KV_RANK] = ckv.astype(BF16)
    kr = _rope32(project(u, 2, reps=MLA_HEADS), cos_q, sin_q)
    kr_ref[...] = kr[:, :MLA_ROPE]
    kcat_ref[:, KV_RANK:] = kr.astype(BF16)
    k_t = jnp.concatenate([ckv, kr], axis=1).T
    kt_ref[...] = k_t.astype(BF16)
    if shifted:
        keep = k_t.shape[1] - N_META
        k_sh = jnp.concatenate([kcarry_ref[...], k_t[:, :keep]], axis=1).astype(BF16)
        blk = kts_ref.shape[2]
        for j in range(kts_ref.shape[0]):
            kts_ref[j] = k_sh[:, j * blk:(j + 1) * blk]
        vs_ref[...] = jnp.concatenate([vcarry_ref[...], ckv[:keep]], axis=0).astype(BF16)
        kcarry_ref[...] = k_t[:, keep:]
        vcarry_ref[...] = ckv[keep:]

    rq = project(u, 3)
    rk = project(u, 4)
    for hd in range(RET_HEADS):
        sl = slice(hd * RET_DK, (hd + 1) * RET_DK)
        xq, xk = rq[:, sl], rk[:, sl]
        rq_ref[:, sl] = (xq * cos_r + pltpu.roll(xq, RET_DK // 2, axis=1) * sin_r).astype(BF16)
        rk_ref[:, sl] = ((xk * cos_r + pltpu.roll(xk, RET_DK // 2, axis=1) * sin_r) * (RET_DK ** -0.5)).astype(BF16)
    rv_ref[...] = project(u, 5).astype(BF16)

    rg = project(u, 6)
    srg_ref[...] = (rg * _sigmoid(rg)).astype(BF16)
    sga_ref[...] = _sigmoid(project(u, 7)).astype(BF16)
    sgb_ref[...] = _sigmoid(project(u, 8)).astype(BF16)


def _mix_in(h, W, rope_r, rope_q, prev_keys=None):
    b, t, d = h.shape
    tm = min(t, TOKEN_TILE)
    assert t % tm == 0
    hr = RET_HEADS * RET_DK
    hv = RET_HEADS * RET_DV
    qw = MLA_HEADS * MLA_ROPE
    shifted = prev_keys is not None
    weights = [W["mix_norm"], W["w_in"], W["q_norm"], W["kv_norm"], W["w_uq_nope"], W["w_uq_rope"], W["w_uk_pair"]]
    tok = lambda width: pl.BlockSpec((None, tm, width), lambda bi, ti: (bi, ti, 0))
    in_specs = ([tok(d)] + [_const_spec(w.shape) for w in weights]
                + [pl.BlockSpec((2, tm, RET_DK), lambda bi, ti: (0, ti, 0)),
                   pl.BlockSpec((2, tm, qw), lambda bi, ti: (0, ti, 0))])
    out_specs = [
        pl.BlockSpec((None, MLA_HEADS, tm, KV_RANK), lambda bi, ti: (bi, 0, ti, 0)),
        tok(qw), tok(KV_RANK), tok(MLA_ROPE), tok(2 * KV_RANK),
        pl.BlockSpec((None, 2 * KV_RANK, tm), lambda bi, ti: (bi, 0, ti)),
        tok(hr), tok(hr), tok(hv), tok(hv), tok(d), tok(d),
    ]
    sds = jax.ShapeDtypeStruct
    out_shape = [
        sds((b, MLA_HEADS, t, KV_RANK), BF16), sds((b, t, qw), BF16), sds((b, t, KV_RANK), F32),
        sds((b, t, MLA_ROPE), F32), sds((b, t, 2 * KV_RANK), BF16), sds((b, 2 * KV_RANK, t), BF16),
        sds((b, t, hr), BF16), sds((b, t, hr), BF16),
        sds((b, t, hv), BF16), sds((b, t, hv), BF16), sds((b, t, d), BF16), sds((b, t, d), BF16),
    ]
    names = ["qlat", "qrope", "ckv", "kr", "kcat", "kt", "rq", "rk", "rv", "srg", "sga", "sgb"]
    extra_in, scratch = [], []
    if shifted:
        blk = ATTN_BLOCK
        assert tm % blk == 0 and N_META % 8 == 0
        extra_in = list(prev_keys)
        in_specs += [_const_spec(a.shape) for a in extra_in]
        out_specs += [pl.BlockSpec((None, tm // blk, 2 * KV_RANK, blk), lambda bi, ti: (bi, ti, 0, 0)), tok(KV_RANK)]
        out_shape += [sds((b, t // blk, 2 * KV_RANK, blk), BF16), sds((b, t, KV_RANK), BF16)]
        names += ["kts", "vs"]
        scratch = [pltpu.VMEM((2 * KV_RANK, N_META), F32), pltpu.VMEM((N_META, KV_RANK), F32)]
    outs = pl.pallas_call(
        functools.partial(_mix_in_body, shifted=shifted, cols=W["w_in_cols"]),
        grid=(b, t // tm),
        in_specs=in_specs,
        out_specs=out_specs,
        out_shape=out_shape,
        scratch_shapes=scratch,
        compiler_params=_params("parallel", "arbitrary"),
        name="mixer_inputs",
    )(h, *weights, rope_r, rope_q, *extra_in)
    return dict(zip(names, outs))


def _softmax_step(s, k_lat, m_ref, l_ref, acc_ref):
    m_prev = m_ref[...]
    m_new = jnp.maximum(m_prev, jnp.max(s, axis=-1, keepdims=True))
    alpha = jnp.exp(m_prev - m_new)
    p = jnp.exp(s - m_new)
    l_ref[...] = alpha * l_ref[...] + jnp.sum(p, axis=-1, keepdims=True)
    acc_ref[...] = alpha * acc_ref[...] + _dot(p.astype(BF16), k_lat)
    m_ref[...] = m_new


def _lanes(x, width):
    return x if width == x.shape[1] else jnp.concatenate([x] * (width // x.shape[1]), axis=1)


def _attn_body(qlat_ref, qrope_ref, kt_ref, v_ref, ktail_ref, vtail_ref, wuv_ref, o_ref,
               qcat_ref, s_ref, p_ref, alpha_ref, m_ref, l_ref, acc_ref):
    blk = qrope_ref.shape[0]
    grp = ATTN_GROUP_HEADS * blk
    n_grp = MLA_HEADS // ATTN_GROUP_HEADS

    def blocks(j, n_blk, first):
        width = n_blk * blk
        v = v_ref[pl.ds(pl.multiple_of(j * blk, blk), width), :]
        if first:
            tok = lax.broadcasted_iota(jnp.int32, (ATTN_ROW_CHUNK, width), 0)
            key = lax.broadcasted_iota(jnp.int32, (ATTN_ROW_CHUNK, width), 1)
        for g in range(n_grp):
            rows = slice(g * grp, (g + 1) * grp)
            for n in range(n_blk):
                s_ref[rows, n * blk:(n + 1) * blk] = _dot(qcat_ref[rows, :], kt_ref[j + n])
        for g in range(n_grp):
            for c in range(g * grp, (g + 1) * grp, ATTN_ROW_CHUNK):
                r = slice(c, c + ATTN_ROW_CHUNK)
                s = s_ref[r, :width]
                if first:
                    s = jnp.where(key <= tok + (c % blk + N_META), s, MASK_VALUE)
                    m_new = jnp.broadcast_to(jnp.max(s, axis=-1, keepdims=True), (ATTN_ROW_CHUNK, 128))
                    p = jnp.exp(s - _lanes(m_new, width))
                    l_ref[r, :] = jnp.broadcast_to(jnp.sum(p, axis=-1, keepdims=True), (ATTN_ROW_CHUNK, 128))
                else:
                    m_prev = m_ref[r, :]
                    m_new = jnp.maximum(m_prev, jnp.max(s, axis=-1, keepdims=True))
                    alpha = jnp.exp(m_prev - m_new)
                    p = jnp.exp(s - _lanes(m_new, width))
                    l_ref[r, :] = alpha * l_ref[r, :] + jnp.sum(p, axis=-1, keepdims=True)
                    alpha_ref[r, :] = alpha
                m_ref[r, :] = m_new
                p_ref[r, :width] = p.astype(BF16)
            rows = slice(g * grp, (g + 1) * grp)
            pv = _dot(p_ref[rows, :width], v)
            acc_ref[rows, :] = pv if first else _lanes(alpha_ref[rows, :], KV_RANK) * acc_ref[rows, :] + pv

    i = pl.program_id(1)
    lane_head = lax.broadcasted_iota(jnp.int32, (blk, MLA_HEADS * MLA_ROPE), 1) // MLA_ROPE
    qr = qrope_ref[...]
    for hd in range(MLA_HEADS):
        qcat_ref[hd * blk:(hd + 1) * blk, :KV_RANK] = qlat_ref[hd]
        qcat_ref[hd * blk:(hd + 1) * blk, KV_RANK:] = jnp.where(lane_head == hd, qr, jnp.zeros_like(qr))

    blocks(i, 1, True)

    def two_full_blocks(jj, carry):
        blocks(2 * jj, 2, False)
        return carry

    lax.fori_loop(0, i // 2, two_full_blocks, 0)

    @pl.when(i % 2 == 1)
    def _():
        blocks(i - 1, 1, False)

    tails = [slice((hd + 1) * blk - N_META, (hd + 1) * blk) for hd in range(MLA_HEADS)]
    gather = lambda ref: jnp.concatenate([ref[r, :] for r in tails], axis=0)
    s = _dot(gather(qcat_ref), ktail_ref[...])
    q_off = lax.broadcasted_iota(jnp.int32, s.shape, 0) & (N_META - 1)
    col = lax.broadcasted_iota(jnp.int32, s.shape, 1) - (128 - N_META)
    s = jnp.where((col >= 0) & (col <= q_off), s, MASK_VALUE)
    m_prev = gather(m_ref)
    m_new = jnp.maximum(m_prev, jnp.max(s, axis=-1, keepdims=True))
    alpha = jnp.exp(m_prev - m_new)
    p = jnp.exp(s - m_new)
    l_new = alpha * gather(l_ref) + jnp.sum(p, axis=-1, keepdims=True)
    acc_new = _lanes(alpha, KV_RANK) * gather(acc_ref) + _dot(p.astype(BF16), vtail_ref[...])
    for hd, r in enumerate(tails):
        part = slice(hd * N_META, (hd + 1) * N_META)
        l_ref[r, :] = l_new[part]
        acc_ref[r, :] = acc_new[part]

    for pr in range(MLA_HEADS // 2):
        halves = []
        for hd in (2 * pr, 2 * pr + 1):
            rows = slice(hd * blk, (hd + 1) * blk)
            halves.append((acc_ref[rows, :] / _lanes(l_ref[rows, :], KV_RANK)).astype(BF16))
        o_ref[:, pr * 2 * MLA_V:(pr + 1) * 2 * MLA_V] = _dot(jnp.concatenate(halves, axis=1), wuv_ref[pr]).astype(BF16)


def _prompt_attention(qlat, qrope, kt_seq, v_seq, kt_tok, kcat, wuv_pair):
    b, nh, t, r = qlat.shape
    nq, blk = kt_seq.shape[1], kt_seq.shape[3]
    assert nq * blk == t and v_seq.shape[1] == t
    assert blk % 128 == 0 and blk & (blk - 1) == 0 and N_META <= 128
    rows = nh * blk
    last = blk // 128
    return pl.pallas_call(
        _attn_body,
        grid=(b, nq),
        in_specs=[
            pl.BlockSpec((None, nh, blk, r), lambda bi, i: (bi, 0, i, 0)),
            pl.BlockSpec((None, blk, qrope.shape[2]), lambda bi, i: (bi, i, 0)),
            pl.BlockSpec((None,) + kt_seq.shape[1:], lambda bi, i: (bi, 0, 0, 0)),
            pl.BlockSpec((None,) + v_seq.shape[1:], lambda bi, i: (bi, 0, 0)),
            pl.BlockSpec((None, kt_tok.shape[1], 128), lambda bi, i: (bi, 0, (i + 1) * last - 1)),
            pl.BlockSpec((None, 128, r), lambda bi, i: (bi, (i + 1) * last - 1, 0)),
            _const_spec(wuv_pair.shape),
        ],
        out_specs=pl.BlockSpec((None, blk, nh * MLA_V), lambda bi, i: (bi, i, 0)),
        out_shape=jax.ShapeDtypeStruct((b, t, nh * MLA_V), BF16),
        scratch_shapes=[
            pltpu.VMEM((rows, 2 * r), BF16),
            pltpu.VMEM((rows, 2 * blk), F32),
            pltpu.VMEM((rows, 2 * blk), BF16),
            pltpu.VMEM((rows, 128), F32),
            pltpu.VMEM((rows, 128), F32),
            pltpu.VMEM((rows, 128), F32),
            pltpu.VMEM((rows, r), F32),
        ],
        compiler_params=_params("parallel", "arbitrary"),
        name="prompt_attention",
    )(qlat, qrope, kt_seq, v_seq, kt_tok, kcat, wuv_pair)


def _group_norm(o):
    mu = jnp.mean(o, axis=-1, keepdims=True)
    c = o - mu
    return c * lax.rsqrt(jnp.mean(c * c, axis=-1, keepdims=True) + NORM_EPS)


def _ret_body(rq_ref, rk_ref, rv_ref, kmeta_ref, vmeta_ref, on_ref, sout_ref, s_ref, decay_ref):
    c = pl.program_id(1)
    n_seq, ch = rq_ref.shape[:2]
    n_meta = kmeta_ref.shape[0]

    @pl.when(c == 0)
    def _():
        j = lax.broadcasted_iota(jnp.int32, (n_meta, 1), 0).astype(F32)
        diff = (lax.broadcasted_iota(jnp.int32, (ch, ch), 0) - lax.broadcasted_iota(jnp.int32, (ch, ch), 1)).astype(F32)
        for hd in range(RET_HEADS):
            kd = kmeta_ref[:, hd * RET_DK:(hd + 1) * RET_DK].astype(F32) * jnp.exp(RET_LOG_GAMMA[hd] * (n_meta - 1.0 - j))
            s_meta = _dot(kd.T.astype(BF16), vmeta_ref[:, hd * RET_DV:(hd + 1) * RET_DV])
            for sq in range(n_seq):
                s_ref[sq, hd] = s_meta
            decay_ref[hd] = jnp.where(diff >= 0, jnp.exp(RET_LOG_GAMMA[hd] * jnp.maximum(diff, 0.0)), 0.0)

    row = lax.broadcasted_iota(jnp.int32, (ch, 1), 0).astype(F32)
    for hd in range(RET_HEADS):
        lg = RET_LOG_GAMMA[hd]
        q_decay = jnp.exp(lg * (row + 1.0))
        k_decay = jnp.exp(lg * (ch - 1.0 - row))
        for sq in range(n_seq):
            q = rq_ref[sq, :, hd * RET_DK:(hd + 1) * RET_DK]
            k = rk_ref[sq, :, hd * RET_DK:(hd + 1) * RET_DK]
            v = rv_ref[sq, :, hd * RET_DV:(hd + 1) * RET_DV]
            scores = _dot_nt(q, k) * decay_ref[hd]
            s_old = s_ref[sq, hd]
            o = _dot(scores.astype(BF16), v) + _dot(q, s_old.astype(BF16)) * q_decay
            kd = k.astype(F32) * k_decay
            s_ref[sq, hd] = math.exp(lg * ch) * s_old + _dot(kd.T.astype(BF16), v)
            on_ref[sq, :, hd * RET_DV:(hd + 1) * RET_DV] = _group_norm(o).astype(BF16)

    @pl.when(c == pl.num_programs(1) - 1)
    def _():
        sout_ref[...] = s_ref[...]


def _prompt_retention(rq, rk, rv, kmeta, vmeta):
    b, t, hr = rq.shape
    hv = rv.shape[2]
    ch = min(t, RET_CHUNK)
    n_seq = math.gcd(b, RET_SEQS)
    assert t % ch == 0
    tok = lambda width: pl.BlockSpec((n_seq, ch, width), lambda bi, ci: (bi, ci, 0))
    return pl.pallas_call(
        _ret_body,
        grid=(b // n_seq, t // ch),
        in_specs=[tok(hr), tok(hr), tok(hv), _const_spec(kmeta.shape), _const_spec(vmeta.shape)],
        out_specs=[tok(hv), pl.BlockSpec((n_seq, RET_HEADS, RET_DK, RET_DV), lambda bi, ci: (bi, 0, 0, 0))],
        out_shape=[jax.ShapeDtypeStruct((b, t, hv), BF16),
                   jax.ShapeDtypeStruct((b, RET_HEADS, RET_DK, RET_DV), F32)],
        scratch_shapes=[pltpu.VMEM((n_seq, RET_HEADS, RET_DK, RET_DV), F32), pltpu.VMEM((RET_HEADS, ch, ch), F32)],
        compiler_params=_params("parallel", "arbitrary"),
        name="prompt_retention",
    )(rq, rk, rv, kmeta, vmeta)


def _dec_attn_body(pt_ref, qlat_ref, qrope_ref, kself_ref, wuv_ref, ckv_hbm, kr_hbm, o_ref,
                   kbuf, krbuf, sem_k, sem_r, m_ref, l_ref, acc_ref, *, pages, n_chunks):
    b = pl.program_id(0)
    n_req = pl.num_programs(0)
    ahead = DEC_SLOTS - 1
    ql = qlat_ref[...]
    qr = qrope_ref[...]

    def page_copies(page, slot, p):
        lat = pltpu.make_async_copy(ckv_hbm.at[page], kbuf.at[slot, p], sem_k.at[slot])
        rope = pltpu.make_async_copy(kr_hbm.at[page], krbuf.at[slot, p], sem_r.at[slot])
        return lat, rope

    def start_chunk(req, chunk, slot):
        def one(p, carry):
            for cp in page_copies(pt_ref[req, chunk * pages + p], slot, p):
                cp.start()
            return carry
        lax.fori_loop(0, pages, one, 0)

    def wait_chunk(slot):
        def one(p, carry):
            for cp in page_copies(0, slot, p):
                cp.wait()
            return carry
        lax.fori_loop(0, pages, one, 0)

    @pl.when(b == 0)
    def _():
        for g in range(ahead):
            start_chunk(0, g, g)

    ks = kself_ref[...].astype(F32)
    qf = jnp.concatenate([ql, qr], axis=1).astype(F32)
    m_ref[...] = jnp.sum(qf * ks, axis=-1, keepdims=True)
    l_ref[...] = jnp.ones(l_ref.shape, F32)
    acc_ref[...] = jnp.broadcast_to(ks[:, :KV_RANK], acc_ref.shape)

    sub = min(pages, DEC_SUB_PAGES)
    n_sub = pages // sub
    sub_keys = sub * PAGE_SIZE

    def ring_round(rnd, carry):
        for slot in range(DEC_SLOTS):
            chunk = rnd * DEC_SLOTS + slot
            nxt = chunk + ahead
            wrap = nxt >= n_chunks
            nxt_req = jnp.where(wrap, b + 1, b)

            @pl.when(nxt_req < n_req)
            def _():
                start_chunk(nxt_req, jnp.where(wrap, nxt - n_chunks, nxt), (slot + ahead) % DEC_SLOTS)

            wait_chunk(slot)
            k_lat, scores = [], []
            for q in range(n_sub):
                k_q = kbuf[slot, q * sub:(q + 1) * sub].reshape(sub_keys, KV_RANK).astype(BF16)
                kr_q = jnp.concatenate([krbuf[slot, q * sub + j] for j in range(sub)], axis=1).astype(BF16)
                k_lat.append(k_q)
                scores.append(_dot_nt(ql, k_q) + _dot(qr, kr_q))
            s = jnp.concatenate(scores, axis=1)
            m_prev = m_ref[...]
            m_new = jnp.maximum(m_prev, jnp.max(s, axis=-1, keepdims=True))
            alpha = jnp.exp(m_prev - m_new)
            p = jnp.exp(s - m_new)
            l_ref[...] = alpha * l_ref[...] + jnp.sum(p, axis=-1, keepdims=True)
            p = p.astype(BF16)
            pv = _dot(p[:, :sub_keys], k_lat[0])
            for q in range(1, n_sub):
                pv = pv + _dot(p[:, q * sub_keys:(q + 1) * sub_keys], k_lat[q])
            acc_ref[...] = alpha * acc_ref[...] + pv
            m_ref[...] = m_new
        return carry

    lax.fori_loop(0, n_chunks // DEC_SLOTS, ring_round, 0)

    o = (acc_ref[...] / l_ref[...]).astype(BF16)
    z = _dot(o, wuv_ref[...])
    head = lax.broadcasted_iota(jnp.int32, z.shape, 0)
    col_head = lax.broadcasted_iota(jnp.int32, z.shape, 1) // MLA_V
    o_ref[...] = jnp.sum(jnp.where(head == col_head, z, 0.0), axis=0, keepdims=True).astype(BF16)


def _decode_attention(page_table, qlat, qrope, kself, wuv_all, cache_ckv, cache_kr_t):
    bd, nh, r = qlat.shape
    n_pages = page_table.shape[1]
    pages = min(n_pages // DEC_SLOTS, DEC_PAGES_PER_STEP)
    n_chunks = n_pages // pages
    assert n_chunks * pages == n_pages and n_chunks % DEC_SLOTS == 0 and pages % min(pages, DEC_SUB_PAGES) == 0

    per_req = lambda shape: pl.BlockSpec((None,) + shape, lambda b, pt: (b, 0, 0))
    grid_spec = pltpu.PrefetchScalarGridSpec(
        num_scalar_prefetch=1,
        grid=(bd,),
        in_specs=[per_req((nh, r)), per_req((nh, MLA_ROPE)), per_req((1, r + MLA_ROPE)),
                  pl.BlockSpec(wuv_all.shape, lambda b, pt: (0, 0)),
                  pl.BlockSpec(memory_space=pl.ANY), pl.BlockSpec(memory_space=pl.ANY)],
        out_specs=per_req((1, nh * MLA_V)),
        scratch_shapes=[
            pltpu.VMEM((DEC_SLOTS, pages, PAGE_SIZE, r), F32),
            pltpu.VMEM((DEC_SLOTS, pages, MLA_ROPE, PAGE_SIZE), F32),
            pltpu.SemaphoreType.DMA((DEC_SLOTS,)),
            pltpu.SemaphoreType.DMA((DEC_SLOTS,)),
            pltpu.VMEM((nh, 1), F32), pltpu.VMEM((nh, 1), F32), pltpu.VMEM((nh, r), F32),
        ],
    )
    return pl.pallas_call(
        functools.partial(_dec_attn_body, pages=pages, n_chunks=n_chunks),
        grid_spec=grid_spec,
        out_shape=jax.ShapeDtypeStruct((bd, 1, nh * MLA_V), BF16),
        compiler_params=_params("arbitrary"),
        name="decode_attention",
    )(page_table, qlat, qrope, kself, wuv_all, cache_ckv, cache_kr_t)


def _dec_ret_body(rq_ref, rk_ref, rv_ref, s_ref, on_ref, snew_ref):
    eye = (lax.broadcasted_iota(jnp.int32, (RET_DK, RET_DK), 0)
           == lax.broadcasted_iota(jnp.int32, (RET_DK, RET_DK), 1))

    def column(x):
        return jnp.sum(jnp.where(eye, jnp.broadcast_to(x, (RET_DK, RET_DK)), 0.0), axis=1, keepdims=True)

    for req in range(rq_ref.shape[0]):
        for hd in range(RET_HEADS):
            gamma = math.exp(RET_LOG_GAMMA[hd])
            q = rq_ref[req, :, hd * RET_DK:(hd + 1) * RET_DK].astype(F32)
            k = rk_ref[req, :, hd * RET_DK:(hd + 1) * RET_DK].astype(F32)
            v = rv_ref[req, :, hd * RET_DV:(hd + 1) * RET_DV].astype(F32)
            s_old = s_ref[req, hd]
            qs = jnp.sum(column(q) * s_old, axis=0, keepdims=True)
            o = jnp.sum(q * k, axis=-1, keepdims=True) * v + qs * gamma
            snew_ref[req, hd] = gamma * s_old + column(k) * v
            on_ref[req, :, hd * RET_DV:(hd + 1) * RET_DV] = _group_norm(o).astype(BF16)


def _decode_retention(rq, rk, rv, state):
    bd = rq.shape[0]
    per = math.gcd(bd, DEC_RET_REQUESTS)
    row = lambda width: pl.BlockSpec((per, 1, width), lambda b: (b, 0, 0))
    st = pl.BlockSpec((per, RET_HEADS, RET_DK, RET_DV), lambda b: (b, 0, 0, 0))
    return pl.pallas_call(
        _dec_ret_body,
        grid=(bd // per,),
        in_specs=[row(rq.shape[2]), row(rk.shape[2]), row(rv.shape[2]), st],
        out_specs=[row(rv.shape[2]), st],
        out_shape=[jax.ShapeDtypeStruct((bd, 1, rv.shape[2]), BF16), jax.ShapeDtypeStruct(state.shape, F32)],
        compiler_params=_params("parallel"),
        name="decode_retention",
    )(rq, rk, rv, state)


def _mix_out_body(h_ref, a_ref, on_ref, srg_ref, sga_ref, sgb_ref, gn_ref, wa_ref, wr_ref, wo_ref, o_ref):
    a = _dot(a_ref[...], wa_ref[...])
    gated = srg_ref[...].astype(F32) * (on_ref[...].astype(F32) * gn_ref[...])
    r = _dot(gated.astype(BF16), wr_ref[...])
    m = sga_ref[...].astype(F32) * a + sgb_ref[...].astype(F32) * r
    o_ref[...] = h_ref[...] + _dot(m.astype(BF16), wo_ref[...])


def _mix_out(h, apre, on, srg, sga, sgb, W):
    m, d = h.shape
    tm = min(m, TOKEN_TILE)
    assert m % tm == 0
    tok = lambda width: pl.BlockSpec((tm, width), lambda i: (i, 0))
    weights = [W["ret_gn"], W["w_mla_o"], W["w_ret_o"], W["w_out"]]
    return pl.pallas_call(
        _mix_out_body,
        grid=(m // tm,),
        in_specs=[tok(d), tok(apre.shape[1]), tok(on.shape[1]), tok(d), tok(d), tok(d)]
                 + [_const_spec(w.shape) for w in weights],
        out_specs=tok(d),
        out_shape=jax.ShapeDtypeStruct((m, d), F32),
        compiler_params=_params("parallel"),
        name="mixer_output",
    )(h, apre, on, srg, sga, sgb, *weights)


def _rope_tables(pos):
    pos = np.asarray(pos, np.float64)[:, None]

    def table(width, reps):
        half = width // 2
        inv_freq = ROPE_BASE ** (-np.arange(half, dtype=np.float64) / half)
        ang = pos * inv_freq[None, :]
        cos, sin = np.cos(ang), np.sin(ang)
        return np.stack([np.tile(np.concatenate([cos, cos], axis=1), (1, reps)),
                         np.tile(np.concatenate([-sin, sin], axis=1), (1, reps))]).astype(np.float32)

    return table(RET_DK, 1), table(MLA_ROPE, MLA_HEADS)


def _prep_weights(ffn1_norm, ffn1_gate, ffn1_up, ffn1_down, mix_norm, w_in, q_norm, kv_norm, w_uq, w_uk, w_uv,
                  w_mla_o, ret_gn, w_ret_o, w_out, ffn2_norm, ffn2_gate, ffn2_up, ffn2_down, final_norm):
    d = w_in.shape[0]
    hr, hv = RET_HEADS * RET_DK, RET_HEADS * RET_DV
    nh, hd = MLA_HEADS, MLA_NOPE + MLA_ROPE

    bounds = [0]
    for width in (Q_RANK, KV_RANK, MLA_ROPE, hr, hr, hv, hv, d, d):
        bounds.append(bounds[-1] + width)
    assert bounds[-1] == w_in.shape[1] and all(bd % 16 == 0 for bd in bounds)
    w_in_cols = [(bounds[i], bounds[i + 1]) for i in range(9)]
    w_in_packed = w_in.T.astype(BF16)

    w_uq3 = w_uq.reshape(Q_RANK, nh, hd)
    zeros = jnp.zeros((MLA_NOPE, KV_RANK), F32)
    w_uk_t = w_uk.transpose(1, 2, 0)
    w_uk_pair = jnp.stack([jnp.block([[w_uk_t[2 * p], zeros], [zeros, w_uk_t[2 * p + 1]]])
                           for p in range(nh // 2)]).astype(BF16)
    w_uv_t = w_uv.transpose(1, 0, 2)
    zv = jnp.zeros((KV_RANK, MLA_V), F32)
    w_uv_pair = jnp.stack([jnp.block([[w_uv_t[2 * p], zv], [zv, w_uv_t[2 * p + 1]]])
                           for p in range(nh // 2)]).astype(BF16)
    row = lambda v: v.reshape(1, -1)
    return dict(
        ffn1_norm=row(ffn1_norm), ffn1=(ffn1_gate, ffn1_up, ffn1_down),
        ffn2_norm=row(ffn2_norm), ffn2=(ffn2_gate, ffn2_up, ffn2_down), final_norm=row(final_norm),
        mix_norm=row(mix_norm), w_in=w_in_packed, w_in_cols=tuple(w_in_cols), q_norm=row(q_norm), kv_norm=row(kv_norm),
        w_uq_nope=w_uq3[:, :, :MLA_NOPE].reshape(Q_RANK, nh * MLA_NOPE).astype(BF16),
        w_uq_rope=w_uq3[:, :, MLA_NOPE:].reshape(Q_RANK, nh * MLA_ROPE).astype(BF16),
        w_uk_pair=w_uk_pair, w_uv_pair=w_uv_pair, w_uv_all=w_uv.reshape(KV_RANK, nh * MLA_V).astype(BF16),
        ret_gn=row(ret_gn), w_mla_o=w_mla_o.astype(BF16), w_ret_o=w_ret_o.astype(BF16), w_out=w_out.astype(BF16),
    )


def kernel(x_prompt, x_sample, cache_ckv, cache_krope, state_ret, page_table, meta_tokens, ffn1_norm, ffn1_gate, ffn1_up, ffn1_down, mix_norm, w_in, q_norm, kv_norm, w_uq, w_uk, w_uv, w_mla_o, ret_gn, w_ret_o, w_out, ffn2_norm, ffn2_gate, ffn2_up, ffn2_down, final_norm):
    assert ffn1_gate.shape[0] == 1, "single-layer trunk"
    b, t, d = x_prompt.shape
    bd, dec_seq, _ = x_sample.shape
    assert dec_seq == 1
    n_meta = meta_tokens.shape[0]
    assert n_meta == N_META
    past_len = page_table.shape[1] * PAGE_SIZE
    W = _prep_weights(ffn1_norm[0], ffn1_gate[0], ffn1_up[0], ffn1_down[0], mix_norm[0], w_in[0], q_norm[0],
                      kv_norm[0], w_uq[0], w_uk[0], w_uv[0], w_mla_o[0], ret_gn[0], w_ret_o[0], w_out[0],
                      ffn2_norm[0], ffn2_gate[0], ffn2_up[0], ffn2_down[0], final_norm)

    rope_r, rope_q = _rope_tables(n_meta + np.arange(t))
    rope_r_s, rope_q_s = _rope_tables(np.concatenate([np.arange(n_meta), np.full((bd,), past_len)]))
    x_small = jnp.concatenate([meta_tokens.astype(F32), x_sample.reshape(bd, d)], axis=0)

    h1, h1s = _ffn(x_prompt.reshape(b * t, d), x_small, W["ffn1_norm"], *W["ffn1"])
    S = _mix_in(h1s.reshape(1, n_meta + bd, d), W, rope_r_s, rope_q_s)
    meta_keys = (S["kt"][0, :, :n_meta].astype(F32), S["kcat"][0, :n_meta, :KV_RANK].astype(F32))
    P = _mix_in(h1.reshape(b, t, d), W, rope_r, rope_q, prev_keys=meta_keys)

    apre = _prompt_attention(P["qlat"], P["qrope"], P["kts"], P["vs"], P["kt"], P["kcat"], W["w_uv_pair"])
    on, s_prompt = _prompt_retention(P["rq"], P["rk"], P["rv"], S["rk"][0, :n_meta], S["rv"][0, :n_meta])

    nh = MLA_HEADS
    qlat_s = S["qlat"][0, :, n_meta:].transpose(1, 0, 2)
    qrope_s = S["qrope"][0, n_meta:].reshape(bd, nh, MLA_ROPE)
    kself = S["kcat"][0, n_meta:, :KV_RANK + MLA_ROPE].reshape(bd, 1, KV_RANK + MLA_ROPE)
    apre_s = _decode_attention(page_table, qlat_s, qrope_s, kself, W["w_uv_all"], cache_ckv[0],
                               cache_krope[0].transpose(0, 2, 1))
    dec = lambda name: S[name][0, n_meta:].reshape(bd, 1, -1)
    on_s, s_sample = _decode_retention(dec("rq"), dec("rk"), dec("rv"), state_ret[0])

    flat = lambda a: a.reshape(b * t, -1)
    h2 = _mix_out(h1, flat(apre), flat(on), flat(P["srg"]), flat(P["sga"]), flat(P["sgb"]), W)
    tail = lambda name: S[name][0, n_meta:]
    h2s = _mix_out(h1s[n_meta:], apre_s.reshape(bd, -1), on_s.reshape(bd, -1), tail("srg"), tail("sga"), tail("sgb"), W)
    y_prompt, y_sample = _ffn(h2, h2s, W["ffn2_norm"], *W["ffn2"], final_norm=W["final_norm"])

    with_meta = lambda small, main: jnp.concatenate(
        [jnp.broadcast_to(small[:, :n_meta], (b, n_meta, small.shape[2])), main], axis=1)[None]
    return (
        y_prompt.reshape(b, t, d),
        y_sample.reshape(bd, 1, d),
        with_meta(S["ckv"], P["ckv"]),
        with_meta(S["kr"], P["kr"]),
        s_prompt[None],
        S["ckv"][0, n_meta:].reshape(1, bd, 1, KV_RANK),
        S["kr"][0, n_meta:].reshape(1, bd, 1, MLA_ROPE),
        s_sample[None],
    )
```

```python
import functools
import math

import jax
import jax.numpy as jnp
import numpy as np
from jax import lax
from jax.experimental import pallas as pl
from jax.experimental.pallas import tpu as pltpu

N_META = 16
PAGE_SIZE = 128
MLA_HEADS = 8
MLA_NOPE = 64
MLA_ROPE = 32
MLA_V = 64
Q_RANK = 384
KV_RANK = 256
MLA_SCALE = (MLA_NOPE + MLA_ROPE) ** -0.5
Q_SCALE = MLA_SCALE * math.log2(math.e)
RET_HEADS = 4
RET_DK = 128
RET_DV = 256
ROPE_BASE = 10000.0
NORM_EPS = 1e-6
RET_LOG_GAMMA = tuple(math.log1p(-(2.0 ** (-5.0 - h))) for h in range(RET_HEADS))

VMEM_LIMIT_BYTES = 56 * 1024 * 1024
FFN_CHUNK = 256
TOKEN_TILE = 512
MIX_OUT_TILE = 1024
ATTN_BLOCK = 256
ATTN_GROUP_HEADS = 1
ATTN_ROW_CHUNK = 32
RET_CHUNK = 256
RET_SEQS = 8
DEC_PAGES_PER_STEP = 64
DEC_SUB_PAGES = 16
DEC_SLOTS = 2
DEC_RET_REQUESTS = 4
MASK_VALUE = -1e30

F32 = jnp.float32
BF16 = jnp.bfloat16


def _params(*sem):
    return pltpu.CompilerParams(dimension_semantics=sem, vmem_limit_bytes=VMEM_LIMIT_BYTES)


def _const_spec(shape):
    n = len(shape)
    return pl.BlockSpec(shape, lambda *_: (0,) * n, pipeline_mode=pl.Buffered(1))


def _rms(x, g):
    return x * lax.rsqrt(jnp.mean(x * x, axis=-1, keepdims=True) + NORM_EPS) * g


def _sigmoid(x):
    return 1.0 / (1.0 + jnp.exp(-x))


def _dot(a, b):
    return jnp.dot(a, b, preferred_element_type=F32)


def _dot_nt(a, b):
    return lax.dot_general(a, b, (((1,), (1,)), ((), ())), preferred_element_type=F32)


def _ffn_body(x_ref, xs_ref, g_ref, wg_ref, wu_ref, wd_ref, *rest, final):
    o_ref, os_ref = rest[-2:]

    def apply(src_ref, dst_ref):
        x = src_ref[...]
        u = _rms(x, g_ref[...]).astype(BF16)
        acc = None
        for c in range(0, wg_ref.shape[1], FFN_CHUNK):
            cols = slice(c, c + FFN_CHUNK)
            gate = _dot(u, wg_ref[:, cols].astype(BF16))
            up = _dot(u, wu_ref[:, cols].astype(BF16))
            act = (gate * _sigmoid(gate) * up).astype(BF16)
            part = _dot(act, wd_ref[cols, :].astype(BF16))
            acc = part if acc is None else acc + part
        y = x + 0.5 * acc
        if final:
            y = _rms(y, rest[0][...])
        dst_ref[...] = y

    last = pl.num_programs(0) - 1
    pl.when(pl.program_id(0) < last)(functools.partial(apply, x_ref, o_ref))
    pl.when(pl.program_id(0) == last)(functools.partial(apply, xs_ref, os_ref))


def _ffn(x, x_small, norm, wg, wu, wd, final_norm=None):
    m, d = x.shape
    tm = min(m, TOKEN_TILE)
    assert m % tm == 0 and wg.shape[1] % FFN_CHUNK == 0
    n = m // tm
    final = final_norm is not None
    main = pl.BlockSpec((tm, d), lambda i: (jnp.minimum(i, n - 1), 0))
    in_specs = [main, _const_spec(x_small.shape), _const_spec((1, d)),
                _const_spec(wg.shape), _const_spec(wu.shape), _const_spec(wd.shape)]
    args = [x, x_small, norm, wg, wu, wd]
    if final:
        in_specs.append(_const_spec((1, d)))
        args.append(final_norm)
    return pl.pallas_call(
        functools.partial(_ffn_body, final=final),
        grid=(n + 1,),
        in_specs=in_specs,
        out_specs=[main, pl.BlockSpec(x_small.shape, lambda i: (0, 0))],
        out_shape=[jax.ShapeDtypeStruct((m, d), F32), jax.ShapeDtypeStruct(x_small.shape, F32)],
        compiler_params=_params("arbitrary"),
        name="ffn_final" if final else "ffn",
    )(*args)


def _rope32(x, cos, sin):
    lane = lax.broadcasted_iota(jnp.int32, x.shape, 1)
    width = x.shape[1]
    partner = jnp.where((lane & 31) < 16, pltpu.roll(x, width - 16, axis=1), pltpu.roll(x, 16, axis=1))
    return x * cos + partner * sin


def _mix_in_body(h_ref, g_ref, win_ref, qn_ref, kvn_ref, wuqn_ref, wuqr_ref, wuk_ref, rope_r_ref, rope_q_ref,
                 *rest, shifted, cols):
    def project(x, piece, reps=1):
        lo, hi = cols[piece]
        w = win_ref[lo:hi, :]
        return _dot_nt(x, w if reps == 1 else jnp.concatenate([w] * reps, axis=0))

    if shifted:
        (kprev_ref, vprev_ref, qlat_ref, qrope_ref, ckv_ref, kr_ref, kcat_ref, kt_ref, rq_ref, rk_ref, rv_ref,
         srg_ref, sga_ref, sgb_ref, kts_ref, vs_ref, kcarry_ref, vcarry_ref) = rest
    else:
        (qlat_ref, qrope_ref, ckv_ref, kr_ref, kcat_ref, kt_ref, rq_ref, rk_ref, rv_ref,
         srg_ref, sga_ref, sgb_ref) = rest
    if shifted:
        @pl.when(pl.program_id(1) == 0)
        def _():
            kcarry_ref[...] = kprev_ref[...]
            vcarry_ref[...] = vprev_ref[...]

    u = _rms(h_ref[...], g_ref[...]).astype(BF16)
    cos_q, sin_q = rope_q_ref[0], rope_q_ref[1]
    cos_r, sin_r = rope_r_ref[0], rope_r_ref[1]

    cq = _rms(project(u, 0), qn_ref[...]).astype(BF16)
    q_nope = (_dot(cq, wuqn_ref[...]) * Q_SCALE).astype(BF16)
    for p in range(MLA_HEADS // 2):
        pair = _dot(q_nope[:, p * 128:(p + 1) * 128], wuk_ref[p])
        qlat_ref[2 * p] = pair[:, :KV_RANK].astype(BF16)
        qlat_ref[2 * p + 1] = pair[:, KV_RANK:].astype(BF16)
    q_rope = _rope32(_dot(cq, wuqr_ref[...]), cos_q, sin_q) * Q_SCALE
    qrope_ref[...] = q_rope.astype(BF16)

    ckv = _rms(project(u, 1), kvn_ref[...])
    ckv_ref[...] = ckv
    kcat_ref[:, :KV_RANK] = ckv.astype(BF16)
    kr = _rope32(project(u, 2, reps=MLA_HEADS), cos_q, sin_q)
    kr_ref[...] = kr[:, :MLA_ROPE]
    kcat_ref[:, KV_RANK:] = kr.astype(BF16)
    k_t = jnp.concatenate([ckv, kr], axis=1).T
    kt_ref[...] = k_t.astype(BF16)
    if shifted:
        keep = k_t.shape[1] - N_META
        k_sh = jnp.concatenate([kcarry_ref[...], k_t[:, :keep]], axis=1).astype(BF16)
        blk = kts_ref.shape[2]
        for j in range(kts_ref.shape[0]):
            kts_ref[j] = k_sh[:, j * blk:(j + 1) * blk]
        vs_ref[...] = jnp.concatenate([vcarry_ref[...], ckv[:keep]], axis=0).astype(BF16)
        kcarry_ref[...] = k_t[:, keep:]
        vcarry_ref[...] = ckv[keep:]

    rq = project(u, 3)
    rk = project(u, 4)
    for hd in range(RET_HEADS):
        sl = slice(hd * RET_DK, (hd + 1) * RET_DK)
        xq, xk = rq[:, sl], rk[:, sl]
        rq_ref[:, sl] = (xq * cos_r + pltpu.roll(xq, RET_DK // 2, axis=1) * sin_r).astype(BF16)
        rk_ref[:, sl] = ((xk * cos_r + pltpu.roll(xk, RET_DK // 2, axis=1) * sin_r) * (RET_DK ** -0.5)).astype(BF16)
    rv_ref[...] = project(u, 5).astype(BF16)

    rg = project(u, 6)
    srg_ref[...] = (rg * _sigmoid(rg)).astype(BF16)
    sga_ref[...] = _sigmoid(project(u, 7)).astype(BF16)
    sgb_ref[...] = _sigmoid(project(u, 8)).astype(BF16)


def _mix_in(h, W, rope_r, rope_q, prev_keys=None):
    b, t, d = h.shape
    tm = min(t, TOKEN_TILE)
    assert t % tm == 0
    hr = RET_HEADS * RET_DK
    hv = RET_HEADS * RET_DV
    qw = MLA_HEADS * MLA_ROPE
    shifted = prev_keys is not None
    weights = [W["mix_norm"], W["w_in"], W["q_norm"], W["kv_norm"], W["w_uq_nope"], W["w_uq_rope"], W["w_uk_pair"]]
    tok = lambda width: pl.BlockSpec((None, tm, width), lambda bi, ti: (bi, ti, 0))
    in_specs = ([tok(d)] + [_const_spec(w.shape) for w in weights]
                + [pl.BlockSpec((2, tm, RET_DK), lambda bi, ti: (0, ti, 0)),
                   pl.BlockSpec((2, tm, qw), lambda bi, ti: (0, ti, 0))])
    out_specs = [
        pl.BlockSpec((None, MLA_HEADS, tm, KV_RANK), lambda bi, ti: (bi, 0, ti, 0)),
        tok(qw), tok(KV_RANK), tok(MLA_ROPE), tok(2 * KV_RANK),
        pl.BlockSpec((None, 2 * KV_RANK, tm), lambda bi, ti: (bi, 0, ti)),
        tok(hr), tok(hr), tok(hv), tok(hv), tok(d), tok(d),
    ]
    sds = jax.ShapeDtypeStruct
    out_shape = [
        sds((b, MLA_HEADS, t, KV_RANK), BF16), sds((b, t, qw), BF16), sds((b, t, KV_RANK), F32),
        sds((b, t, MLA_ROPE), F32), sds((b, t, 2 * KV_RANK), BF16), sds((b, 2 * KV_RANK, t), BF16),
        sds((b, t, hr), BF16), sds((b, t, hr), BF16),
        sds((b, t, hv), BF16), sds((b, t, hv), BF16), sds((b, t, d), BF16), sds((b, t, d), BF16),
    ]
    names = ["qlat", "qrope", "ckv", "kr", "kcat", "kt", "rq", "rk", "rv", "srg", "sga", "sgb"]
    extra_in, scratch = [], []
    if shifted:
        blk = ATTN_BLOCK
        assert tm % blk == 0 and N_META % 8 == 0
        extra_in = list(prev_keys)
        in_specs += [_const_spec(a.shape) for a in extra_in]
        out_specs += [pl.BlockSpec((None, tm // blk, 2 * KV_RANK, blk), lambda bi, ti: (bi, ti, 0, 0)), tok(KV_RANK)]
        out_shape += [sds((b, t // blk, 2 * KV_RANK, blk), BF16), sds((b, t, KV_RANK), BF16)]
        names += ["kts", "vs"]
        scratch = [pltpu.VMEM((2 * KV_RANK, N_META), F32), pltpu.VMEM((N_META, KV_RANK), F32)]
    outs = pl.pallas_call(
        functools.partial(_mix_in_body, shifted=shifted, cols=W["w_in_cols"]),
        grid=(b, t // tm),
        in_specs=in_specs,
        out_specs=out_specs,
        out_shape=out_shape,
        scratch_shapes=scratch,
        compiler_params=_params("parallel", "arbitrary"),
        name="mixer_inputs",
    )(h, *weights, rope_r, rope_q, *extra_in)
    return dict(zip(names, outs))


def _lanes(x, width):
    return x if width == x.shape[1] else jnp.concatenate([x] * (width // x.shape[1]), axis=1)


def _attn_body(qlat_ref, qrope_ref, kt_ref, v_ref, ktail_ref, vtail_ref, wuv_ref, o_ref,
               qcat_ref, s_ref, p_ref, alpha_ref, m_ref, l_ref, acc_ref):
    blk = qrope_ref.shape[0]
    grp = ATTN_GROUP_HEADS * blk
    n_grp = MLA_HEADS // ATTN_GROUP_HEADS

    def blocks(j, n_blk, first):
        width = n_blk * blk
        v = v_ref[pl.ds(pl.multiple_of(j * blk, blk), width), :]
        if first:
            tok = lax.broadcasted_iota(jnp.int32, (ATTN_ROW_CHUNK, width), 0)
            key = lax.broadcasted_iota(jnp.int32, (ATTN_ROW_CHUNK, width), 1)
        for g in range(n_grp):
            rows = slice(g * grp, (g + 1) * grp)
            for n in range(n_blk):
                s_ref[rows, n * blk:(n + 1) * blk] = _dot(qcat_ref[rows, :], kt_ref[j + n])
        for g in range(n_grp):
            for c in range(g * grp, (g + 1) * grp, ATTN_ROW_CHUNK):
                r = slice(c, c + ATTN_ROW_CHUNK)
                s = s_ref[r, :width]
                if first:
                    s = jnp.where(key <= tok + (c % blk + N_META), s, MASK_VALUE)
                    m_new = jnp.broadcast_to(jnp.max(s, axis=-1, keepdims=True), (ATTN_ROW_CHUNK, 128))
                    p = jnp.exp2(s - _lanes(m_new, width))
                    l_ref[r, :] = jnp.broadcast_to(jnp.sum(p, axis=-1, keepdims=True), (ATTN_ROW_CHUNK, 128))
                else:
                    m_prev = m_ref[r, :]
                    m_new = jnp.maximum(m_prev, jnp.max(s, axis=-1, keepdims=True))
                    alpha = jnp.exp2(m_prev - m_new)
                    p = jnp.exp2(s - _lanes(m_new, width))
                    l_ref[r, :] = alpha * l_ref[r, :] + jnp.sum(p, axis=-1, keepdims=True)
                    alpha_ref[r, :] = alpha
                m_ref[r, :] = m_new
                p_ref[r, :width] = p.astype(BF16)
            rows = slice(g * grp, (g + 1) * grp)
            pv = _dot(p_ref[rows, :width], v)
            acc_ref[rows, :] = pv if first else _lanes(alpha_ref[rows, :], KV_RANK) * acc_ref[rows, :] + pv

    i = pl.program_id(1)
    lane_head = lax.broadcasted_iota(jnp.int32, (blk, MLA_HEADS * MLA_ROPE), 1) // MLA_ROPE
    qr = qrope_ref[...]
    for hd in range(MLA_HEADS):
        qcat_ref[hd * blk:(hd + 1) * blk, :KV_RANK] = qlat_ref[hd]
        qcat_ref[hd * blk:(hd + 1) * blk, KV_RANK:] = jnp.where(lane_head == hd, qr, jnp.zeros_like(qr))

    blocks(i, 1, True)

    def two_full_blocks(jj, carry):
        blocks(2 * jj, 2, False)
        return carry

    lax.fori_loop(0, i // 2, two_full_blocks, 0)

    @pl.when(i % 2 == 1)
    def _():
        blocks(i - 1, 1, False)

    tails = [slice((hd + 1) * blk - N_META, (hd + 1) * blk) for hd in range(MLA_HEADS)]
    gather = lambda ref: jnp.concatenate([ref[r, :] for r in tails], axis=0)
    s = _dot(gather(qcat_ref), ktail_ref[...])
    q_off = lax.broadcasted_iota(jnp.int32, s.shape, 0) & (N_META - 1)
    col = lax.broadcasted_iota(jnp.int32, s.shape, 1) - (128 - N_META)
    s = jnp.where((col >= 0) & (col <= q_off), s, MASK_VALUE)
    m_prev = gather(m_ref)
    m_new = jnp.maximum(m_prev, jnp.max(s, axis=-1, keepdims=True))
    alpha = jnp.exp2(m_prev - m_new)
    p = jnp.exp2(s - m_new)
    l_new = alpha * gather(l_ref) + jnp.sum(p, axis=-1, keepdims=True)
    acc_new = _lanes(alpha, KV_RANK) * gather(acc_ref) + _dot(p.astype(BF16), vtail_ref[...])
    for hd, r in enumerate(tails):
        part = slice(hd * N_META, (hd + 1) * N_META)
        l_ref[r, :] = l_new[part]
        acc_ref[r, :] = acc_new[part]

    for pr in range(MLA_HEADS // 2):
        halves = []
        for hd in (2 * pr, 2 * pr + 1):
            rows = slice(hd * blk, (hd + 1) * blk)
            halves.append((acc_ref[rows, :] / _lanes(l_ref[rows, :], KV_RANK)).astype(BF16))
        o_ref[:, pr * 2 * MLA_V:(pr + 1) * 2 * MLA_V] = _dot(jnp.concatenate(halves, axis=1), wuv_ref[pr]).astype(BF16)


def _prompt_attention(qlat, qrope, kt_seq, v_seq, kt_tok, kcat, wuv_pair):
    b, nh, t, r = qlat.shape
    nq, blk = kt_seq.shape[1], kt_seq.shape[3]
    assert nq * blk == t and v_seq.shape[1] == t
    assert blk % 128 == 0 and blk & (blk - 1) == 0 and N_META <= 128
    rows = nh * blk
    last = blk // 128
    return pl.pallas_call(
        _attn_body,
        grid=(b, nq),
        in_specs=[
            pl.BlockSpec((None, nh, blk, r), lambda bi, i: (bi, 0, i, 0)),
            pl.BlockSpec((None, blk, qrope.shape[2]), lambda bi, i: (bi, i, 0)),
            pl.BlockSpec((None,) + kt_seq.shape[1:], lambda bi, i: (bi, 0, 0, 0)),
            pl.BlockSpec((None,) + v_seq.shape[1:], lambda bi, i: (bi, 0, 0)),
            pl.BlockSpec((None, kt_tok.shape[1], 128), lambda bi, i: (bi, 0, (i + 1) * last - 1)),
            pl.BlockSpec((None, 128, r), lambda bi, i: (bi, (i + 1) * last - 1, 0)),
            _const_spec(wuv_pair.shape),
        ],
        out_specs=pl.BlockSpec((None, blk, nh * MLA_V), lambda bi, i: (bi, i, 0)),
        out_shape=jax.ShapeDtypeStruct((b, t, nh * MLA_V), BF16),
        scratch_shapes=[
            pltpu.VMEM((rows, 2 * r), BF16),
            pltpu.VMEM((rows, 2 * blk), F32),
            pltpu.VMEM((rows, 2 * blk), BF16),
            pltpu.VMEM((rows, 128), F32),
            pltpu.VMEM((rows, 128), F32),
            pltpu.VMEM((rows, 128), F32),
            pltpu.VMEM((rows, r), F32),
        ],
        compiler_params=_params("parallel", "arbitrary"),
        name="prompt_attention",
    )(qlat, qrope, kt_seq, v_seq, kt_tok, kcat, wuv_pair)


def _group_norm(o):
    mu = jnp.mean(o, axis=-1, keepdims=True)
    c = o - mu
    return c * lax.rsqrt(jnp.mean(c * c, axis=-1, keepdims=True) + NORM_EPS)


def _ret_body(rq_ref, rk_ref, rv_ref, kmeta_ref, vmeta_ref, on_ref, sout_ref, s_ref, decay_ref):
    c = pl.program_id(1)
    n_seq, ch = rq_ref.shape[:2]
    n_meta = kmeta_ref.shape[0]

    @pl.when(c == 0)
    def _():
        j = lax.broadcasted_iota(jnp.int32, (n_meta, 1), 0).astype(F32)
        diff = (lax.broadcasted_iota(jnp.int32, (ch, ch), 0) - lax.broadcasted_iota(jnp.int32, (ch, ch), 1)).astype(F32)
        for hd in range(RET_HEADS):
            kd = kmeta_ref[:, hd * RET_DK:(hd + 1) * RET_DK].astype(F32) * jnp.exp(RET_LOG_GAMMA[hd] * (n_meta - 1.0 - j))
            s_meta = _dot(kd.T.astype(BF16), vmeta_ref[:, hd * RET_DV:(hd + 1) * RET_DV])
            for sq in range(n_seq):
                s_ref[sq, hd] = s_meta
            decay_ref[hd] = jnp.where(diff >= 0, jnp.exp(RET_LOG_GAMMA[hd] * jnp.maximum(diff, 0.0)), 0.0)

    row = lax.broadcasted_iota(jnp.int32, (ch, 1), 0).astype(F32)
    for hd in range(RET_HEADS):
        lg = RET_LOG_GAMMA[hd]
        q_decay = jnp.exp(lg * (row + 1.0))
        k_decay = jnp.exp(lg * (ch - 1.0 - row))
        for sq in range(n_seq):
            q = rq_ref[sq, :, hd * RET_DK:(hd + 1) * RET_DK]
            k = rk_ref[sq, :, hd * RET_DK:(hd + 1) * RET_DK]
            v = rv_ref[sq, :, hd * RET_DV:(hd + 1) * RET_DV]
            scores = _dot_nt(q, k) * decay_ref[hd]
            s_old = s_ref[sq, hd]
            o = _dot(scores.astype(BF16), v) + _dot(q, s_old.astype(BF16)) * q_decay
            kd = k.astype(F32) * k_decay
            s_ref[sq, hd] = math.exp(lg * ch) * s_old + _dot(kd.T.astype(BF16), v)
            on_ref[sq, :, hd * RET_DV:(hd + 1) * RET_DV] = _group_norm(o).astype(BF16)

    @pl.when(c == pl.num_programs(1) - 1)
    def _():
        sout_ref[...] = s_ref[...]


def _prompt_retention(rq, rk, rv, kmeta, vmeta):
    b, t, hr = rq.shape
    hv = rv.shape[2]
    ch = min(t, RET_CHUNK)
    n_seq = math.gcd(b, RET_SEQS)
    assert t % ch == 0
    tok = lambda width: pl.BlockSpec((n_seq, ch, width), lambda bi, ci: (bi, ci, 0))
    return pl.pallas_call(
        _ret_body,
        grid=(b // n_seq, t // ch),
        in_specs=[tok(hr), tok(hr), tok(hv), _const_spec(kmeta.shape), _const_spec(vmeta.shape)],
        out_specs=[tok(hv), pl.BlockSpec((n_seq, RET_HEADS, RET_DK, RET_DV), lambda bi, ci: (bi, 0, 0, 0))],
        out_shape=[jax.ShapeDtypeStruct((b, t, hv), BF16),
                   jax.ShapeDtypeStruct((b, RET_HEADS, RET_DK, RET_DV), F32)],
        scratch_shapes=[pltpu.VMEM((n_seq, RET_HEADS, RET_DK, RET_DV), F32), pltpu.VMEM((RET_HEADS, ch, ch), F32)],
        compiler_params=_params("parallel", "arbitrary"),
        name="prompt_retention",
    )(rq, rk, rv, kmeta, vmeta)


def _dec_attn_body(pt_ref, qlat_ref, qrope_ref, kself_ref, wuv_ref, ckv_hbm, kr_hbm, o_ref,
                   kbuf, krbuf, sem_k, sem_r, m_ref, l_ref, acc_ref, *, pages, n_chunks):
    b = pl.program_id(0)
    n_req = pl.num_programs(0)
    ahead = DEC_SLOTS - 1
    ql = qlat_ref[...]
    qr = qrope_ref[...]

    def page_copies(page, slot, p):
        lat = pltpu.make_async_copy(ckv_hbm.at[page], kbuf.at[slot, p], sem_k.at[slot])
        rope = pltpu.make_async_copy(kr_hbm.at[page], krbuf.at[slot, p], sem_r.at[slot])
        return lat, rope

    def start_chunk(req, chunk, slot):
        def one(p, carry):
            for cp in page_copies(pt_ref[req, chunk * pages + p], slot, p):
                cp.start()
            return carry
        lax.fori_loop(0, pages, one, 0)

    def wait_chunk(slot):
        def one(p, carry):
            for cp in page_copies(0, slot, p):
                cp.wait()
            return carry
        lax.fori_loop(0, pages, one, 0)

    @pl.when(b == 0)
    def _():
        for g in range(ahead):
            start_chunk(0, g, g)

    ks = kself_ref[...].astype(F32)
    qf = jnp.concatenate([ql, qr], axis=1).astype(F32)
    m_ref[...] = jnp.sum(qf * ks, axis=-1, keepdims=True)
    l_ref[...] = jnp.ones(l_ref.shape, F32)
    acc_ref[...] = jnp.broadcast_to(ks[:, :KV_RANK], acc_ref.shape)

    sub = min(pages, DEC_SUB_PAGES)
    n_sub = pages // sub
    sub_keys = sub * PAGE_SIZE

    def ring_round(rnd, carry):
        for slot in range(DEC_SLOTS):
            chunk = rnd * DEC_SLOTS + slot
            nxt = chunk + ahead
            wrap = nxt >= n_chunks
            nxt_req = jnp.where(wrap, b + 1, b)

            @pl.when(nxt_req < n_req)
            def _():
                start_chunk(nxt_req, jnp.where(wrap, nxt - n_chunks, nxt), (slot + ahead) % DEC_SLOTS)

            wait_chunk(slot)
            k_lat, scores = [], []
            for q in range(n_sub):
                k_q = kbuf[slot, q * sub:(q + 1) * sub].reshape(sub_keys, KV_RANK).astype(BF16)
                kr_q = jnp.concatenate([krbuf[slot, q * sub + j] for j in range(sub)], axis=1).astype(BF16)
                k_lat.append(k_q)
                scores.append(_dot_nt(ql, k_q) + _dot(qr, kr_q))
            s = jnp.concatenate(scores, axis=1)
            m_prev = m_ref[...]
            m_new = jnp.maximum(m_prev, jnp.max(s, axis=-1, keepdims=True))
            alpha = jnp.exp2(m_prev - m_new)
            p = jnp.exp2(s - m_new)
            l_ref[...] = alpha * l_ref[...] + jnp.sum(p, axis=-1, keepdims=True)
            p = p.astype(BF16)
            pv = _dot(p[:, :sub_keys], k_lat[0])
            for q in range(1, n_sub):
                pv = pv + _dot(p[:, q * sub_keys:(q + 1) * sub_keys], k_lat[q])
            acc_ref[...] = alpha * acc_ref[...] + pv
            m_ref[...] = m_new
        return carry

    lax.fori_loop(0, n_chunks // DEC_SLOTS, ring_round, 0)

    o = (acc_ref[...] / l_ref[...]).astype(BF16)
    z = _dot(o, wuv_ref[...])
    head = lax.broadcasted_iota(jnp.int32, z.shape, 0)
    col_head = lax.broadcasted_iota(jnp.int32, z.shape, 1) // MLA_V
    o_ref[...] = jnp.sum(jnp.where(head == col_head, z, 0.0), axis=0, keepdims=True).astype(BF16)


def _decode_attention(page_table, qlat, qrope, kself, wuv_all, cache_ckv, cache_kr_t):
    bd, nh, r = qlat.shape
    n_pages = page_table.shape[1]
    pages = min(n_pages // DEC_SLOTS, DEC_PAGES_PER_STEP)
    n_chunks = n_pages // pages
    assert n_chunks * pages == n_pages and n_chunks % DEC_SLOTS == 0 and pages % min(pages, DEC_SUB_PAGES) == 0

    per_req = lambda shape: pl.BlockSpec((None,) + shape, lambda b, pt: (b, 0, 0))
    grid_spec = pltpu.PrefetchScalarGridSpec(
        num_scalar_prefetch=1,
        grid=(bd,),
        in_specs=[per_req((nh, r)), per_req((nh, MLA_ROPE)), per_req((1, r + MLA_ROPE)),
                  pl.BlockSpec(wuv_all.shape, lambda b, pt: (0, 0)),
                  pl.BlockSpec(memory_space=pl.ANY), pl.BlockSpec(memory_space=pl.ANY)],
        out_specs=per_req((1, nh * MLA_V)),
        scratch_shapes=[
            pltpu.VMEM((DEC_SLOTS, pages, PAGE_SIZE, r), F32),
            pltpu.VMEM((DEC_SLOTS, pages, MLA_ROPE, PAGE_SIZE), F32),
            pltpu.SemaphoreType.DMA((DEC_SLOTS,)),
            pltpu.SemaphoreType.DMA((DEC_SLOTS,)),
            pltpu.VMEM((nh, 1), F32), pltpu.VMEM((nh, 1), F32), pltpu.VMEM((nh, r), F32),
        ],
    )
    return pl.pallas_call(
        functools.partial(_dec_attn_body, pages=pages, n_chunks=n_chunks),
        grid_spec=grid_spec,
        out_shape=jax.ShapeDtypeStruct((bd, 1, nh * MLA_V), BF16),
        compiler_params=_params("arbitrary"),
        name="decode_attention",
    )(page_table, qlat, qrope, kself, wuv_all, cache_ckv, cache_kr_t)


def _dec_ret_body(rq_ref, rk_ref, rv_ref, s_ref, on_ref, snew_ref):
    eye = (lax.broadcasted_iota(jnp.int32, (RET_DK, RET_DK), 0)
           == lax.broadcasted_iota(jnp.int32, (RET_DK, RET_DK), 1))

    def column(x):
        return jnp.sum(jnp.where(eye, jnp.broadcast_to(x, (RET_DK, RET_DK)), 0.0), axis=1, keepdims=True)

    for req in range(rq_ref.shape[0]):
        for hd in range(RET_HEADS):
            gamma = math.exp(RET_LOG_GAMMA[hd])
            q = rq_ref[req, :, hd * RET_DK:(hd + 1) * RET_DK].astype(F32)
            k = rk_ref[req, :, hd * RET_DK:(hd + 1) * RET_DK].astype(F32)
            v = rv_ref[req, :, hd * RET_DV:(hd + 1) * RET_DV].astype(F32)
            s_old = s_ref[req, hd]
            qs = jnp.sum(column(q) * s_old, axis=0, keepdims=True)
            o = jnp.sum(q * k, axis=-1, keepdims=True) * v + qs * gamma
            snew_ref[req, hd] = gamma * s_old + column(k) * v
            on_ref[req, :, hd * RET_DV:(hd + 1) * RET_DV] = _group_norm(o).astype(BF16)


def _decode_retention(rq, rk, rv, state):
    bd = rq.shape[0]
    per = math.gcd(bd, DEC_RET_REQUESTS)
    row = lambda width: pl.BlockSpec((per, 1, width), lambda b: (b, 0, 0))
    st = pl.BlockSpec((per, RET_HEADS, RET_DK, RET_DV), lambda b: (b, 0, 0, 0))
    return pl.pallas_call(
        _dec_ret_body,
        grid=(bd // per,),
        in_specs=[row(rq.shape[2]), row(rk.shape[2]), row(rv.shape[2]), st],
        out_specs=[row(rv.shape[2]), st],
        out_shape=[jax.ShapeDtypeStruct((bd, 1, rv.shape[2]), BF16), jax.ShapeDtypeStruct(state.shape, F32)],
        compiler_params=_params("parallel"),
        name="decode_retention",
    )(rq, rk, rv, state)


def _mix_out_body(h_ref, a_ref, on_ref, srg_ref, sga_ref, sgb_ref, gn_ref, wa_ref, wr_ref, wo_ref, o_ref):
    a = _dot(a_ref[...], wa_ref[...])
    gated = srg_ref[...].astype(F32) * (on_ref[...].astype(F32) * gn_ref[...])
    r = _dot(gated.astype(BF16), wr_ref[...])
    m = sga_ref[...].astype(F32) * a + sgb_ref[...].astype(F32) * r
    o_ref[...] = h_ref[...] + _dot(m.astype(BF16), wo_ref[...])


def _mix_out(h, apre, on, srg, sga, sgb, W):
    m, d = h.shape
    tm = min(m, MIX_OUT_TILE)
    assert m % tm == 0
    tok = lambda width: pl.BlockSpec((tm, width), lambda i: (i, 0))
    weights = [W["ret_gn"], W["w_mla_o"], W["w_ret_o"], W["w_out"]]
    return pl.pallas_call(
        _mix_out_body,
        grid=(m // tm,),
        in_specs=[tok(d), tok(apre.shape[1]), tok(on.shape[1]), tok(d), tok(d), tok(d)]
                 + [_const_spec(w.shape) for w in weights],
        out_specs=tok(d),
        out_shape=jax.ShapeDtypeStruct((m, d), F32),
        compiler_params=_params("parallel"),
        name="mixer_output",
    )(h, apre, on, srg, sga, sgb, *weights)


def _rope_tables(pos):
    pos = np.asarray(pos, np.float64)[:, None]

    def table(width, reps):
        half = width // 2
        inv_freq = ROPE_BASE ** (-np.arange(half, dtype=np.float64) / half)
        ang = pos * inv_freq[None, :]
        cos, sin = np.cos(ang), np.sin(ang)
        return np.stack([np.tile(np.concatenate([cos, cos], axis=1), (1, reps)),
                         np.tile(np.concatenate([-sin, sin], axis=1), (1, reps))]).astype(np.float32)

    return table(RET_DK, 1), table(MLA_ROPE, MLA_HEADS)


def _prep_weights(ffn1_norm, ffn1_gate, ffn1_up, ffn1_down, mix_norm, w_in, q_norm, kv_norm, w_uq, w_uk, w_uv,
                  w_mla_o, ret_gn, w_ret_o, w_out, ffn2_norm, ffn2_gate, ffn2_up, ffn2_down, final_norm):
    d = w_in.shape[0]
    hr, hv = RET_HEADS * RET_DK, RET_HEADS * RET_DV
    nh, hd = MLA_HEADS, MLA_NOPE + MLA_ROPE

    bounds = [0]
    for width in (Q_RANK, KV_RANK, MLA_ROPE, hr, hr, hv, hv, d, d):
        bounds.append(bounds[-1] + width)
    assert bounds[-1] == w_in.shape[1] and all(bd % 16 == 0 for bd in bounds)
    w_in_cols = [(bounds[i], bounds[i + 1]) for i in range(9)]
    w_in_packed = w_in.T.astype(BF16)

    w_uq3 = w_uq.reshape(Q_RANK, nh, hd)
    zeros = jnp.zeros((MLA_NOPE, KV_RANK), F32)
    w_uk_t = w_uk.transpose(1, 2, 0)
    w_uk_pair = jnp.stack([jnp.block([[w_uk_t[2 * p], zeros], [zeros, w_uk_t[2 * p + 1]]])
                           for p in range(nh // 2)]).astype(BF16)
    w_uv_t = w_uv.transpose(1, 0, 2)
    zv = jnp.zeros((KV_RANK, MLA_V), F32)
    w_uv_pair = jnp.stack([jnp.block([[w_uv_t[2 * p], zv], [zv, w_uv_t[2 * p + 1]]])
                           for p in range(nh // 2)]).astype(BF16)
    row = lambda v: v.reshape(1, -1)
    return dict(
        ffn1_norm=row(ffn1_norm), ffn1=(ffn1_gate, ffn1_up, ffn1_down),
        ffn2_norm=row(ffn2_norm), ffn2=(ffn2_gate, ffn2_up, ffn2_down), final_norm=row(final_norm),
        mix_norm=row(mix_norm), w_in=w_in_packed, w_in_cols=tuple(w_in_cols), q_norm=row(q_norm), kv_norm=row(kv_norm),
        w_uq_nope=w_uq3[:, :, :MLA_NOPE].reshape(Q_RANK, nh * MLA_NOPE).astype(BF16),
        w_uq_rope=w_uq3[:, :, MLA_NOPE:].reshape(Q_RANK, nh * MLA_ROPE).astype(BF16),
        w_uk_pair=w_uk_pair, w_uv_pair=w_uv_pair, w_uv_all=w_uv.reshape(KV_RANK, nh * MLA_V).astype(BF16),
        ret_gn=row(ret_gn), w_mla_o=w_mla_o.astype(BF16), w_ret_o=w_ret_o.astype(BF16), w_out=w_out.astype(BF16),
    )


def kernel(x_prompt, x_sample, cache_ckv, cache_krope, state_ret, page_table, meta_tokens, ffn1_norm, ffn1_gate, ffn1_up, ffn1_down, mix_norm, w_in, q_norm, kv_norm, w_uq, w_uk, w_uv, w_mla_o, ret_gn, w_ret_o, w_out, ffn2_norm, ffn2_gate, ffn2_up, ffn2_down, final_norm):
    assert ffn1_gate.shape[0] == 1, "single-layer trunk"
    b, t, d = x_prompt.shape
    bd, dec_seq, _ = x_sample.shape
    assert dec_seq == 1
    n_meta = meta_tokens.shape[0]
    assert n_meta == N_META
    past_len = page_table.shape[1] * PAGE_SIZE
    W = _prep_weights(ffn1_norm[0], ffn1_gate[0], ffn1_up[0], ffn1_down[0], mix_norm[0], w_in[0], q_norm[0],
                      kv_norm[0], w_uq[0], w_uk[0], w_uv[0], w_mla_o[0], ret_gn[0], w_ret_o[0], w_out[0],
                      ffn2_norm[0], ffn2_gate[0], ffn2_up[0], ffn2_down[0], final_norm)

    rope_r, rope_q = _rope_tables(n_meta + np.arange(t))
    rope_r_s, rope_q_s = _rope_tables(np.concatenate([np.arange(n_meta), np.full((bd,), past_len)]))
    x_small = jnp.concatenate([meta_tokens.astype(F32), x_sample.reshape(bd, d)], axis=0)

    h1, h1s = _ffn(x_prompt.reshape(b * t, d), x_small, W["ffn1_norm"], *W["ffn1"])
    S = _mix_in(h1s.reshape(1, n_meta + bd, d), W, rope_r_s, rope_q_s)
    meta_keys = (S["kt"][0, :, :n_meta].astype(F32), S["kcat"][0, :n_meta, :KV_RANK].astype(F32))
    P = _mix_in(h1.reshape(b, t, d), W, rope_r, rope_q, prev_keys=meta_keys)

    apre = _prompt_attention(P["qlat"], P["qrope"], P["kts"], P["vs"], P["kt"], P["kcat"], W["w_uv_pair"])
    on, s_prompt = _prompt_retention(P["rq"], P["rk"], P["rv"], S["rk"][0, :n_meta], S["rv"][0, :n_meta])

    nh = MLA_HEADS
    qlat_s = S["qlat"][0, :, n_meta:].transpose(1, 0, 2)
    qrope_s = S["qrope"][0, n_meta:].reshape(bd, nh, MLA_ROPE)
    kself = S["kcat"][0, n_meta:, :KV_RANK + MLA_ROPE].reshape(bd, 1, KV_RANK + MLA_ROPE)
    apre_s = _decode_attention(page_table, qlat_s, qrope_s, kself, W["w_uv_all"], cache_ckv[0],
                               cache_krope[0].transpose(0, 2, 1))
    dec = lambda name: S[name][0, n_meta:].reshape(bd, 1, -1)
    on_s, s_sample = _decode_retention(dec("rq"), dec("rk"), dec("rv"), state_ret[0])

    flat = lambda a: a.reshape(b * t, -1)
    h2 = _mix_out(h1, flat(apre), flat(on), flat(P["srg"]), flat(P["sga"]), flat(P["sgb"]), W)
    tail = lambda name: S[name][0, n_meta:]
    h2s = _mix_out(h1s[n_meta:], apre_s.reshape(bd, -1), on_s.reshape(bd, -1), tail("srg"), tail("sga"), tail("sgb"), W)
    y_prompt, y_sample = _ffn(h2, h2s, W["ffn2_norm"], *W["ffn2"], final_norm=W["final_norm"])

    with_meta = lambda small, main: jnp.concatenate(
        [jnp.broadcast_to(small[:, :n_meta], (b, n_meta, small.shape[2])), main], axis=1)[None]
    return (
        y_prompt.reshape(b, t, d),
        y_sample.reshape(bd, 1, d),
        with_meta(S["ckv"], P["ckv"]),
        with_meta(S["kr"], P["kr"]),
        s_prompt[None],
        S["ckv"][0, n_meta:].reshape(1, bd, 1, KV_RANK),
        S["kr"][0, n_meta:].reshape(1, bd, 1, MLA_ROPE),
        s_sample[None],
    )
```

```python
import functools
import math

import jax
import jax.numpy as jnp
import numpy as np
from jax import lax
from jax.experimental import pallas as pl
from jax.experimental.pallas import tpu as pltpu

N_META = 16
PAGE_SIZE = 128
MLA_HEADS = 8
MLA_NOPE = 64
MLA_ROPE = 32
MLA_V = 64
Q_RANK = 384
KV_RANK = 256
MLA_SCALE = (MLA_NOPE + MLA_ROPE) ** -0.5
Q_SCALE = MLA_SCALE * math.log2(math.e)
RET_HEADS = 4
RET_DK = 128
RET_DV = 256
ROPE_BASE = 10000.0
NORM_EPS = 1e-6
RET_LOG_GAMMA = tuple(math.log1p(-(2.0 ** (-5.0 - h))) for h in range(RET_HEADS))

VMEM_LIMIT_BYTES = 56 * 1024 * 1024
FFN_CHUNK = 256
TOKEN_TILE = 512
MIX_OUT_TILE = 1024
ATTN_BLOCK = 256
ATTN_ROW_CHUNK = 32
RET_CHUNK = 256
RET_SEQS = 8
DEC_PAGES_PER_STEP = 32
DEC_SUB_PAGES = 16
DEC_SLOTS = 4
DEC_RET_REQUESTS = 4
MASK_VALUE = -1e30

F32 = jnp.float32
BF16 = jnp.bfloat16


def _params(*sem):
    return pltpu.CompilerParams(dimension_semantics=sem, vmem_limit_bytes=VMEM_LIMIT_BYTES)


def _const_spec(shape):
    n = len(shape)
    return pl.BlockSpec(shape, lambda *_: (0,) * n, pipeline_mode=pl.Buffered(1))


def _rms(x, g):
    return x * lax.rsqrt(jnp.mean(x * x, axis=-1, keepdims=True) + NORM_EPS) * g


def _sigmoid(x):
    return 1.0 / (1.0 + jnp.exp(-x))


def _dot(a, b):
    return jnp.dot(a, b, preferred_element_type=F32)


def _dot_nt(a, b):
    return lax.dot_general(a, b, (((1,), (1,)), ((), ())), preferred_element_type=F32)


def _ffn_body(x_ref, xs_ref, g_ref, wg_ref, wu_ref, wd_ref, *rest, final):
    o_ref, os_ref = rest[-2:]

    def apply(src_ref, dst_ref):
        x = src_ref[...]
        u = _rms(x, g_ref[...]).astype(BF16)
        acc = None
        for c in range(0, wg_ref.shape[1], FFN_CHUNK):
            cols = slice(c, c + FFN_CHUNK)
            gate = _dot(u, wg_ref[:, cols].astype(BF16))
            up = _dot(u, wu_ref[:, cols].astype(BF16))
            act = (gate * _sigmoid(gate) * up).astype(BF16)
            part = _dot(act, wd_ref[cols, :].astype(BF16))
            acc = part if acc is None else acc + part
        y = x + 0.5 * acc
        if final:
            y = _rms(y, rest[0][...])
        dst_ref[...] = y

    last = pl.num_programs(0) - 1
    pl.when(pl.program_id(0) < last)(functools.partial(apply, x_ref, o_ref))
    pl.when(pl.program_id(0) == last)(functools.partial(apply, xs_ref, os_ref))


def _ffn(x, x_small, norm, wg, wu, wd, final_norm=None):
    m, d = x.shape
    tm = min(m, TOKEN_TILE)
    assert m % tm == 0 and wg.shape[1] % FFN_CHUNK == 0
    n = m // tm
    final = final_norm is not None
    main = pl.BlockSpec((tm, d), lambda i: (jnp.minimum(i, n - 1), 0))
    in_specs = [main, _const_spec(x_small.shape), _const_spec((1, d)),
                _const_spec(wg.shape), _const_spec(wu.shape), _const_spec(wd.shape)]
    args = [x, x_small, norm, wg, wu, wd]
    if final:
        in_specs.append(_const_spec((1, d)))
        args.append(final_norm)
    return pl.pallas_call(
        functools.partial(_ffn_body, final=final),
        grid=(n + 1,),
        in_specs=in_specs,
        out_specs=[main, pl.BlockSpec(x_small.shape, lambda i: (0, 0))],
        out_shape=[jax.ShapeDtypeStruct((m, d), F32), jax.ShapeDtypeStruct(x_small.shape, F32)],
        compiler_params=_params("arbitrary"),
        name="ffn_final" if final else "ffn",
    )(*args)


def _rope32(x, cos, sin):
    lane = lax.broadcasted_iota(jnp.int32, x.shape, 1)
    width = x.shape[1]
    partner = jnp.where((lane & 31) < 16, pltpu.roll(x, width - 16, axis=1), pltpu.roll(x, 16, axis=1))
    return x * cos + partner * sin


def _mix_in_body(h_ref, g_ref, win_ref, qn_ref, kvn_ref, wuqn_ref, wuqr_ref, wuk_ref, rope_r_ref, rope_q_ref,
                 *rest, shifted, cols):
    def project(x, piece, reps=1):
        lo, hi = cols[piece]
        w = win_ref[lo:hi, :]
        return _dot_nt(x, w if reps == 1 else jnp.concatenate([w] * reps, axis=0))

    if shifted:
        (kprev_ref, vprev_ref, qcat_ref, qrope_ref, ckv_ref, kr_ref, kcat_ref, kt_ref, rq_ref, rk_ref, rv_ref,
         srg_ref, sga_ref, sgb_ref, kts_ref, vs_ref, kcarry_ref, vcarry_ref) = rest
    else:
        (qcat_ref, qrope_ref, ckv_ref, kr_ref, kcat_ref, kt_ref, rq_ref, rk_ref, rv_ref,
         srg_ref, sga_ref, sgb_ref) = rest
    if shifted:
        @pl.when(pl.program_id(1) == 0)
        def _():
            kcarry_ref[...] = kprev_ref[...]
            vcarry_ref[...] = vprev_ref[...]

    u = _rms(h_ref[...], g_ref[...]).astype(BF16)
    cos_q, sin_q = rope_q_ref[0], rope_q_ref[1]
    cos_r, sin_r = rope_r_ref[0], rope_r_ref[1]

    cq = _rms(project(u, 0), qn_ref[...]).astype(BF16)
    q_nope = (_dot(cq, wuqn_ref[...]) * Q_SCALE).astype(BF16)
    for p in range(MLA_HEADS // 2):
        pair = _dot(q_nope[:, p * 128:(p + 1) * 128], wuk_ref[p])
        qcat_ref[2 * p, :, :KV_RANK] = pair[:, :KV_RANK].astype(BF16)
        qcat_ref[2 * p + 1, :, :KV_RANK] = pair[:, KV_RANK:].astype(BF16)
    q_rope = (_rope32(_dot(cq, wuqr_ref[...]), cos_q, sin_q) * Q_SCALE).astype(BF16)
    qrope_ref[...] = q_rope
    lane_head = lax.broadcasted_iota(jnp.int32, q_rope.shape, 1) // MLA_ROPE
    for hd in range(MLA_HEADS):
        qcat_ref[hd, :, KV_RANK:] = jnp.where(lane_head == hd, q_rope, jnp.zeros_like(q_rope))

    ckv = _rms(project(u, 1), kvn_ref[...])
    ckv_ref[...] = ckv
    kcat_ref[:, :KV_RANK] = ckv.astype(BF16)
    kr = _rope32(project(u, 2, reps=MLA_HEADS), cos_q, sin_q)
    kr_ref[...] = kr[:, :MLA_ROPE]
    kcat_ref[:, KV_RANK:] = kr.astype(BF16)
    k_t = jnp.concatenate([ckv, kr], axis=1).T
    kt_ref[...] = k_t.astype(BF16)
    if shifted:
        keep = k_t.shape[1] - N_META
        k_sh = jnp.concatenate([kcarry_ref[...], k_t[:, :keep]], axis=1).astype(BF16)
        blk = kts_ref.shape[2]
        for j in range(kts_ref.shape[0]):
            kts_ref[j] = k_sh[:, j * blk:(j + 1) * blk]
        vs_ref[...] = jnp.concatenate([vcarry_ref[...], ckv[:keep]], axis=0).astype(BF16)
        kcarry_ref[...] = k_t[:, keep:]
        vcarry_ref[...] = ckv[keep:]

    rq = project(u, 3)
    rk = project(u, 4)
    for hd in range(RET_HEADS):
        sl = slice(hd * RET_DK, (hd + 1) * RET_DK)
        xq, xk = rq[:, sl], rk[:, sl]
        rq_ref[:, sl] = (xq * cos_r + pltpu.roll(xq, RET_DK // 2, axis=1) * sin_r).astype(BF16)
        rk_ref[:, sl] = ((xk * cos_r + pltpu.roll(xk, RET_DK // 2, axis=1) * sin_r) * (RET_DK ** -0.5)).astype(BF16)
    rv_ref[...] = project(u, 5).astype(BF16)

    rg = project(u, 6)
    srg_ref[...] = (rg * _sigmoid(rg)).astype(BF16)
    sga_ref[...] = _sigmoid(project(u, 7)).astype(BF16)
    sgb_ref[...] = _sigmoid(project(u, 8)).astype(BF16)


def _mix_in(h, W, rope_r, rope_q, prev_keys=None):
    b, t, d = h.shape
    tm = min(t, TOKEN_TILE)
    assert t % tm == 0
    hr = RET_HEADS * RET_DK
    hv = RET_HEADS * RET_DV
    qw = MLA_HEADS * MLA_ROPE
    shifted = prev_keys is not None
    weights = [W["mix_norm"], W["w_in"], W["q_norm"], W["kv_norm"], W["w_uq_nope"], W["w_uq_rope"], W["w_uk_pair"]]
    tok = lambda width: pl.BlockSpec((None, tm, width), lambda bi, ti: (bi, ti, 0))
    in_specs = ([tok(d)] + [_const_spec(w.shape) for w in weights]
                + [pl.BlockSpec((2, tm, RET_DK), lambda bi, ti: (0, ti, 0)),
                   pl.BlockSpec((2, tm, qw), lambda bi, ti: (0, ti, 0))])
    out_specs = [
        pl.BlockSpec((None, MLA_HEADS, tm, 2 * KV_RANK), lambda bi, ti: (bi, 0, ti, 0)),
        tok(qw), tok(KV_RANK), tok(MLA_ROPE), tok(2 * KV_RANK),
        pl.BlockSpec((None, 2 * KV_RANK, tm), lambda bi, ti: (bi, 0, ti)),
        tok(hr), tok(hr), tok(hv), tok(hv), tok(d), tok(d),
    ]
    sds = jax.ShapeDtypeStruct
    out_shape = [
        sds((b, MLA_HEADS, t, 2 * KV_RANK), BF16), sds((b, t, qw), BF16), sds((b, t, KV_RANK), F32),
        sds((b, t, MLA_ROPE), F32), sds((b, t, 2 * KV_RANK), BF16), sds((b, 2 * KV_RANK, t), BF16),
        sds((b, t, hr), BF16), sds((b, t, hr), BF16),
        sds((b, t, hv), BF16), sds((b, t, hv), BF16), sds((b, t, d), BF16), sds((b, t, d), BF16),
    ]
    names = ["qcat", "qrope", "ckv", "kr", "kcat", "kt", "rq", "rk", "rv", "srg", "sga", "sgb"]
    extra_in, scratch = [], []
    if shifted:
        blk = ATTN_BLOCK
        assert tm % blk == 0 and N_META % 8 == 0
        extra_in = list(prev_keys)
        in_specs += [_const_spec(a.shape) for a in extra_in]
        out_specs += [pl.BlockSpec((None, tm // blk, 2 * KV_RANK, blk), lambda bi, ti: (bi, ti, 0, 0)), tok(KV_RANK)]
        out_shape += [sds((b, t // blk, 2 * KV_RANK, blk), BF16), sds((b, t, KV_RANK), BF16)]
        names += ["kts", "vs"]
        scratch = [pltpu.VMEM((2 * KV_RANK, N_META), F32), pltpu.VMEM((N_META, KV_RANK), F32)]
    outs = pl.pallas_call(
        functools.partial(_mix_in_body, shifted=shifted, cols=W["w_in_cols"]),
        grid=(b, t // tm),
        in_specs=in_specs,
        out_specs=out_specs,
        out_shape=out_shape,
        scratch_shapes=scratch,
        compiler_params=_params("parallel", "arbitrary"),
        name="mixer_inputs",
    )(h, *weights, rope_r, rope_q, *extra_in)
    return dict(zip(names, outs))


def _lanes(x, width):
    return x if width == x.shape[1] else jnp.concatenate([x] * (width // x.shape[1]), axis=1)


def _attn_body(qcat_ref, kt_ref, v_ref, ktail_ref, vtail_ref, wuv_ref, o_ref,
               s_ref, p_ref, alpha_ref, m_ref, l_ref, acc_ref):
    n_grp, blk = qcat_ref.shape[:2]
    grp = blk

    def blocks(j, n_blk, first):
        width = n_blk * blk
        v = v_ref[pl.ds(pl.multiple_of(j * blk, blk), width), :]
        if first:
            tok = lax.broadcasted_iota(jnp.int32, (ATTN_ROW_CHUNK, width), 0)
            key = lax.broadcasted_iota(jnp.int32, (ATTN_ROW_CHUNK, width), 1)
        for g in range(n_grp):
            rows = slice(g * grp, (g + 1) * grp)
            for n in range(n_blk):
                s_ref[rows, n * blk:(n + 1) * blk] = _dot(qcat_ref[g], kt_ref[j + n])
        for g in range(n_grp):
            for c in range(g * grp, (g + 1) * grp, ATTN_ROW_CHUNK):
                r = slice(c, c + ATTN_ROW_CHUNK)
                s = s_ref[r, :width]
                if first:
                    s = jnp.where(key <= tok + (c % blk + N_META), s, MASK_VALUE)
                    m_new = jnp.broadcast_to(jnp.max(s, axis=-1, keepdims=True), (ATTN_ROW_CHUNK, 128))
                    p = jnp.exp2(s - _lanes(m_new, width))
                    l_ref[r, :] = jnp.broadcast_to(jnp.sum(p, axis=-1, keepdims=True), (ATTN_ROW_CHUNK, 128))
                else:
                    m_prev = m_ref[r, :]
                    m_new = jnp.maximum(m_prev, jnp.max(s, axis=-1, keepdims=True))
                    alpha = jnp.exp2(m_prev - m_new)
                    p = jnp.exp2(s - _lanes(m_new, width))
                    l_ref[r, :] = alpha * l_ref[r, :] + jnp.sum(p, axis=-1, keepdims=True)
                    alpha_ref[r, :] = alpha
                m_ref[r, :] = m_new
                p_ref[r, :width] = p.astype(BF16)
            rows = slice(g * grp, (g + 1) * grp)
            pv = _dot(p_ref[rows, :width], v)
            acc_ref[rows, :] = pv if first else _lanes(alpha_ref[rows, :], KV_RANK) * acc_ref[rows, :] + pv

    i = pl.program_id(1)
    blocks(i, 1, True)

    def two_full_blocks(jj, carry):
        blocks(2 * jj, 2, False)
        return carry

    lax.fori_loop(0, i // 2, two_full_blocks, 0)

    @pl.when(i % 2 == 1)
    def _():
        blocks(i - 1, 1, False)

    tails = [slice((hd + 1) * blk - N_META, (hd + 1) * blk) for hd in range(MLA_HEADS)]
    gather = lambda ref: jnp.concatenate([ref[r, :] for r in tails], axis=0)
    q_tail = jnp.concatenate([qcat_ref[hd, blk - N_META:, :] for hd in range(MLA_HEADS)], axis=0)
    s = _dot(q_tail, ktail_ref[...])
    q_off = lax.broadcasted_iota(jnp.int32, s.shape, 0) & (N_META - 1)
    col = lax.broadcasted_iota(jnp.int32, s.shape, 1) - (128 - N_META)
    s = jnp.where((col >= 0) & (col <= q_off), s, MASK_VALUE)
    m_prev = gather(m_ref)
    m_new = jnp.maximum(m_prev, jnp.max(s, axis=-1, keepdims=True))
    alpha = jnp.exp2(m_prev - m_new)
    p = jnp.exp2(s - m_new)
    l_new = alpha * gather(l_ref) + jnp.sum(p, axis=-1, keepdims=True)
    acc_new = _lanes(alpha, KV_RANK) * gather(acc_ref) + _dot(p.astype(BF16), vtail_ref[...])
    for hd, r in enumerate(tails):
        part = slice(hd * N_META, (hd + 1) * N_META)
        l_ref[r, :] = l_new[part]
        acc_ref[r, :] = acc_new[part]

    for pr in range(MLA_HEADS // 2):
        halves = []
        for hd in (2 * pr, 2 * pr + 1):
            rows = slice(hd * blk, (hd + 1) * blk)
            halves.append((acc_ref[rows, :] / _lanes(l_ref[rows, :], KV_RANK)).astype(BF16))
        o_ref[:, pr * 2 * MLA_V:(pr + 1) * 2 * MLA_V] = _dot(jnp.concatenate(halves, axis=1), wuv_ref[pr]).astype(BF16)


def _prompt_attention(qcat, kt_seq, v_seq, kt_tok, kcat, wuv_pair):
    b, nh, t, _ = qcat.shape
    r = v_seq.shape[2]
    nq, blk = kt_seq.shape[1], kt_seq.shape[3]
    assert nq * blk == t and v_seq.shape[1] == t
    assert blk % 128 == 0 and blk & (blk - 1) == 0 and N_META <= 128
    rows = nh * blk
    last = blk // 128
    return pl.pallas_call(
        _attn_body,
        grid=(b, nq),
        in_specs=[
            pl.BlockSpec((None, nh, blk, 2 * r), lambda bi, i: (bi, 0, i, 0)),
            pl.BlockSpec((None,) + kt_seq.shape[1:], lambda bi, i: (bi, 0, 0, 0)),
            pl.BlockSpec((None,) + v_seq.shape[1:], lambda bi, i: (bi, 0, 0)),
            pl.BlockSpec((None, kt_tok.shape[1], 128), lambda bi, i: (bi, 0, (i + 1) * last - 1)),
            pl.BlockSpec((None, 128, r), lambda bi, i: (bi, (i + 1) * last - 1, 0)),
            _const_spec(wuv_pair.shape),
        ],
        out_specs=pl.BlockSpec((None, blk, nh * MLA_V), lambda bi, i: (bi, i, 0)),
        out_shape=jax.ShapeDtypeStruct((b, t, nh * MLA_V), BF16),
        scratch_shapes=[
            pltpu.VMEM((rows, 2 * blk), F32),
            pltpu.VMEM((rows, 2 * blk), BF16),
            pltpu.VMEM((rows, 128), F32),
            pltpu.VMEM((rows, 128), F32),
            pltpu.VMEM((rows, 128), F32),
            pltpu.VMEM((rows, r), F32),
        ],
        compiler_params=_params("parallel", "arbitrary"),
        name="prompt_attention",
    )(qcat, kt_seq, v_seq, kt_tok, kcat, wuv_pair)


def _group_norm(o):
    mu = jnp.mean(o, axis=-1, keepdims=True)
    c = o - mu
    return c * lax.rsqrt(jnp.mean(c * c, axis=-1, keepdims=True) + NORM_EPS)


def _ret_body(rq_ref, rk_ref, rv_ref, kmeta_ref, vmeta_ref, on_ref, sout_ref, s_ref, decay_ref):
    c = pl.program_id(1)
    n_seq, ch = rq_ref.shape[:2]
    n_meta = kmeta_ref.shape[0]

    @pl.when(c == 0)
    def _():
        j = lax.broadcasted_iota(jnp.int32, (n_meta, 1), 0).astype(F32)
        diff = (lax.broadcasted_iota(jnp.int32, (ch, ch), 0) - lax.broadcasted_iota(jnp.int32, (ch, ch), 1)).astype(F32)
        for hd in range(RET_HEADS):
            kd = kmeta_ref[:, hd * RET_DK:(hd + 1) * RET_DK].astype(F32) * jnp.exp(RET_LOG_GAMMA[hd] * (n_meta - 1.0 - j))
            s_meta = _dot(kd.T.astype(BF16), vmeta_ref[:, hd * RET_DV:(hd + 1) * RET_DV])
            for sq in range(n_seq):
                s_ref[sq, hd] = s_meta
            decay_ref[hd] = jnp.where(diff >= 0, jnp.exp(RET_LOG_GAMMA[hd] * jnp.maximum(diff, 0.0)), 0.0)

    row = lax.broadcasted_iota(jnp.int32, (ch, 1), 0).astype(F32)
    for hd in range(RET_HEADS):
        lg = RET_LOG_GAMMA[hd]
        q_decay = jnp.exp(lg * (row + 1.0))
        k_decay = jnp.exp(lg * (ch - 1.0 - row))
        for sq in range(n_seq):
            q = rq_ref[sq, :, hd * RET_DK:(hd + 1) * RET_DK]
            k = rk_ref[sq, :, hd * RET_DK:(hd + 1) * RET_DK]
            v = rv_ref[sq, :, hd * RET_DV:(hd + 1) * RET_DV]
            scores = _dot_nt(q, k) * decay_ref[hd]
            s_old = s_ref[sq, hd]
            o = _dot(scores.astype(BF16), v) + _dot(q, s_old.astype(BF16)) * q_decay
            kd = k.astype(F32) * k_decay
            s_ref[sq, hd] = math.exp(lg * ch) * s_old + _dot(kd.T.astype(BF16), v)
            on_ref[sq, :, hd * RET_DV:(hd + 1) * RET_DV] = _group_norm(o).astype(BF16)

    @pl.when(c == pl.num_programs(1) - 1)
    def _():
        sout_ref[...] = s_ref[...]


def _prompt_retention(rq, rk, rv, kmeta, vmeta):
    b, t, hr = rq.shape
    hv = rv.shape[2]
    ch = min(t, RET_CHUNK)
    n_seq = math.gcd(b, RET_SEQS)
    assert t % ch == 0
    tok = lambda width: pl.BlockSpec((n_seq, ch, width), lambda bi, ci: (bi, ci, 0))
    return pl.pallas_call(
        _ret_body,
        grid=(b // n_seq, t // ch),
        in_specs=[tok(hr), tok(hr), tok(hv), _const_spec(kmeta.shape), _const_spec(vmeta.shape)],
        out_specs=[tok(hv), pl.BlockSpec((n_seq, RET_HEADS, RET_DK, RET_DV), lambda bi, ci: (bi, 0, 0, 0))],
        out_shape=[jax.ShapeDtypeStruct((b, t, hv), BF16),
                   jax.ShapeDtypeStruct((b, RET_HEADS, RET_DK, RET_DV), F32)],
        scratch_shapes=[pltpu.VMEM((n_seq, RET_HEADS, RET_DK, RET_DV), F32), pltpu.VMEM((RET_HEADS, ch, ch), F32)],
        compiler_params=_params("parallel", "arbitrary"),
        name="prompt_retention",
    )(rq, rk, rv, kmeta, vmeta)


def _dec_attn_body(pt_ref, qlat_ref, qrope_ref, kself_ref, wuv_ref, ckv_hbm, kr_hbm, o_ref,
                   kbuf, krbuf, sem_k, sem_r, m_ref, l_ref, acc_ref, *, pages, n_chunks):
    b = pl.program_id(0)
    n_req = pl.num_programs(0)
    ahead = DEC_SLOTS - 1
    ql = qlat_ref[...]
    qr = qrope_ref[...]

    def page_copies(page, slot, p):
        lat = pltpu.make_async_copy(ckv_hbm.at[page], kbuf.at[slot, p], sem_k.at[slot])
        rope = pltpu.make_async_copy(kr_hbm.at[page], krbuf.at[slot, p], sem_r.at[slot])
        return lat, rope

    def start_chunk(req, chunk, slot):
        def one(p, carry):
            for cp in page_copies(pt_ref[req, chunk * pages + p], slot, p):
                cp.start()
            return carry
        lax.fori_loop(0, pages, one, 0)

    def wait_chunk(slot):
        def one(p, carry):
            for cp in page_copies(0, slot, p):
                cp.wait()
            return carry
        lax.fori_loop(0, pages, one, 0)

    @pl.when(b == 0)
    def _():
        for g in range(ahead):
            start_chunk(0, g, g)

    ks = kself_ref[...].astype(F32)
    qf = jnp.concatenate([ql, qr], axis=1).astype(F32)
    m_ref[...] = jnp.sum(qf * ks, axis=-1, keepdims=True)
    l_ref[...] = jnp.ones(l_ref.shape, F32)
    acc_ref[...] = jnp.broadcast_to(ks[:, :KV_RANK], acc_ref.shape)

    sub = min(pages, DEC_SUB_PAGES)
    n_sub = pages // sub
    sub_keys = sub * PAGE_SIZE

    def ring_round(rnd, carry):
        for slot in range(DEC_SLOTS):
            chunk = rnd * DEC_SLOTS + slot
            nxt = chunk + ahead
            wrap = nxt >= n_chunks
            nxt_req = jnp.where(wrap, b + 1, b)

            @pl.when(nxt_req < n_req)
            def _():
                start_chunk(nxt_req, jnp.where(wrap, nxt - n_chunks, nxt), (slot + ahead) % DEC_SLOTS)

            wait_chunk(slot)
            k_lat, scores = [], []
            for q in range(n_sub):
                k_q = kbuf[slot, q * sub:(q + 1) * sub].reshape(sub_keys, KV_RANK).astype(BF16)
                kr_q = jnp.concatenate([krbuf[slot, q * sub + j] for j in range(sub)], axis=1).astype(BF16)
                k_lat.append(k_q)
                scores.append(_dot_nt(ql, k_q) + _dot(qr, kr_q))
            s = jnp.concatenate(scores, axis=1)
            m_prev = m_ref[...]
            m_new = jnp.maximum(m_prev, jnp.max(s, axis=-1, keepdims=True))
            alpha = jnp.exp2(m_prev - m_new)
            p = jnp.exp2(s - m_new)
            l_ref[...] = alpha * l_ref[...] + jnp.sum(p, axis=-1, keepdims=True)
            p = p.astype(BF16)
            pv = _dot(p[:, :sub_keys], k_lat[0])
            for q in range(1, n_sub):
                pv = pv + _dot(p[:, q * sub_keys:(q + 1) * sub_keys], k_lat[q])
            acc_ref[...] = alpha * acc_ref[...] + pv
            m_ref[...] = m_new
        return carry

    lax.fori_loop(0, n_chunks // DEC_SLOTS, ring_round, 0)

    o = (acc_ref[...] / l_ref[...]).astype(BF16)
    z = _dot(o, wuv_ref[...])
    head = lax.broadcasted_iota(jnp.int32, z.shape, 0)
    col_head = lax.broadcasted_iota(jnp.int32, z.shape, 1) // MLA_V
    o_ref[...] = jnp.sum(jnp.where(head == col_head, z, 0.0), axis=0, keepdims=True).astype(BF16)


def _decode_attention(page_table, qlat, qrope, kself, wuv_all, cache_ckv, cache_kr_t):
    bd, nh, r = qlat.shape
    n_pages = page_table.shape[1]
    pages = min(n_pages // DEC_SLOTS, DEC_PAGES_PER_STEP)
    n_chunks = n_pages // pages
    assert n_chunks * pages == n_pages and n_chunks % DEC_SLOTS == 0 and pages % min(pages, DEC_SUB_PAGES) == 0

    per_req = lambda shape: pl.BlockSpec((None,) + shape, lambda b, pt: (b, 0, 0))
    grid_spec = pltpu.PrefetchScalarGridSpec(
        num_scalar_prefetch=1,
        grid=(bd,),
        in_specs=[per_req((nh, r)), per_req((nh, MLA_ROPE)), per_req((1, r + MLA_ROPE)),
                  pl.BlockSpec(wuv_all.shape, lambda b, pt: (0, 0)),
                  pl.BlockSpec(memory_space=pl.ANY), pl.BlockSpec(memory_space=pl.ANY)],
        out_specs=per_req((1, nh * MLA_V)),
        scratch_shapes=[
            pltpu.VMEM((DEC_SLOTS, pages, PAGE_SIZE, r), F32),
            pltpu.VMEM((DEC_SLOTS, pages, MLA_ROPE, PAGE_SIZE), F32),
            pltpu.SemaphoreType.DMA((DEC_SLOTS,)),
            pltpu.SemaphoreType.DMA((DEC_SLOTS,)),
            pltpu.VMEM((nh, 1), F32), pltpu.VMEM((nh, 1), F32), pltpu.VMEM((nh, r), F32),
        ],
    )
    return pl.pallas_call(
        functools.partial(_dec_attn_body, pages=pages, n_chunks=n_chunks),
        grid_spec=grid_spec,
        out_shape=jax.ShapeDtypeStruct((bd, 1, nh * MLA_V), BF16),
        compiler_params=_params("arbitrary"),
        name="decode_attention",
    )(page_table, qlat, qrope, kself, wuv_all, cache_ckv, cache_kr_t)


def _dec_ret_body(rq_ref, rk_ref, rv_ref, s_ref, on_ref, snew_ref):
    eye = (lax.broadcasted_iota(jnp.int32, (RET_DK, RET_DK), 0)
           == lax.broadcasted_iota(jnp.int32, (RET_DK, RET_DK), 1))

    def column(x):
        return jnp.sum(jnp.where(eye, jnp.broadcast_to(x, (RET_DK, RET_DK)), 0.0), axis=1, keepdims=True)

    for req in range(rq_ref.shape[0]):
        for hd in range(RET_HEADS):
            gamma = math.exp(RET_LOG_GAMMA[hd])
            q = rq_ref[req, :, hd * RET_DK:(hd + 1) * RET_DK].astype(F32)
            k = rk_ref[req, :, hd * RET_DK:(hd + 1) * RET_DK].astype(F32)
            v = rv_ref[req, :, hd * RET_DV:(hd + 1) * RET_DV].astype(F32)
            s_old = s_ref[req, hd]
            qs = jnp.sum(column(q) * s_old, axis=0, keepdims=True)
            o = jnp.sum(q * k, axis=-1, keepdims=True) * v + qs * gamma
            snew_ref[req, hd] = gamma * s_old + column(k) * v
            on_ref[req, :, hd * RET_DV:(hd + 1) * RET_DV] = _group_norm(o).astype(BF16)


def _decode_retention(rq, rk, rv, state):
    bd = rq.shape[0]
    per = math.gcd(bd, DEC_RET_REQUESTS)
    row = lambda width: pl.BlockSpec((per, 1, width), lambda b: (b, 0, 0))
    st = pl.BlockSpec((per, RET_HEADS, RET_DK, RET_DV), lambda b: (b, 0, 0, 0))
    return pl.pallas_call(
        _dec_ret_body,
        grid=(bd // per,),
        in_specs=[row(rq.shape[2]), row(rk.shape[2]), row(rv.shape[2]), st],
        out_specs=[row(rv.shape[2]), st],
        out_shape=[jax.ShapeDtypeStruct((bd, 1, rv.shape[2]), BF16), jax.ShapeDtypeStruct(state.shape, F32)],
        compiler_params=_params("parallel"),
        name="decode_retention",
    )(rq, rk, rv, state)


def _mix_out_body(h_ref, a_ref, on_ref, srg_ref, sga_ref, sgb_ref, gn_ref, wa_ref, wr_ref, wo_ref, o_ref):
    a = _dot(a_ref[...], wa_ref[...])
    gated = srg_ref[...].astype(F32) * (on_ref[...].astype(F32) * gn_ref[...])
    r = _dot(gated.astype(BF16), wr_ref[...])
    m = sga_ref[...].astype(F32) * a + sgb_ref[...].astype(F32) * r
    o_ref[...] = h_ref[...] + _dot(m.astype(BF16), wo_ref[...])


def _mix_out(h, apre, on, srg, sga, sgb, W):
    m, d = h.shape
    tm = min(m, MIX_OUT_TILE)
    assert m % tm == 0
    tok = lambda width: pl.BlockSpec((tm, width), lambda i: (i, 0))
    weights = [W["ret_gn"], W["w_mla_o"], W["w_ret_o"], W["w_out"]]
    return pl.pallas_call(
        _mix_out_body,
        grid=(m // tm,),
        in_specs=[tok(d), tok(apre.shape[1]), tok(on.shape[1]), tok(d), tok(d), tok(d)]
                 + [_const_spec(w.shape) for w in weights],
        out_specs=tok(d),
        out_shape=jax.ShapeDtypeStruct((m, d), F32),
        compiler_params=_params("parallel"),
        name="mixer_output",
    )(h, apre, on, srg, sga, sgb, *weights)


def _rope_tables(pos):
    pos = np.asarray(pos, np.float64)[:, None]

    def table(width, reps):
        half = width // 2
        inv_freq = ROPE_BASE ** (-np.arange(half, dtype=np.float64) / half)
        ang = pos * inv_freq[None, :]
        cos, sin = np.cos(ang), np.sin(ang)
        return np.stack([np.tile(np.concatenate([cos, cos], axis=1), (1, reps)),
                         np.tile(np.concatenate([-sin, sin], axis=1), (1, reps))]).astype(np.float32)

    return table(RET_DK, 1), table(MLA_ROPE, MLA_HEADS)


def _prep_weights(ffn1_norm, ffn1_gate, ffn1_up, ffn1_down, mix_norm, w_in, q_norm, kv_norm, w_uq, w_uk, w_uv,
                  w_mla_o, ret_gn, w_ret_o, w_out, ffn2_norm, ffn2_gate, ffn2_up, ffn2_down, final_norm):
    d = w_in.shape[0]
    hr, hv = RET_HEADS * RET_DK, RET_HEADS * RET_DV
    nh, hd = MLA_HEADS, MLA_NOPE + MLA_ROPE

    bounds = [0]
    for width in (Q_RANK, KV_RANK, MLA_ROPE, hr, hr, hv, hv, d, d):
        bounds.append(bounds[-1] + width)
    assert bounds[-1] == w_in.shape[1] and all(bd % 16 == 0 for bd in bounds)
    w_in_cols = [(bounds[i], bounds[i + 1]) for i in range(9)]
    w_in_packed = w_in.T.astype(BF16)

    w_uq3 = w_uq.reshape(Q_RANK, nh, hd)
    zeros = jnp.zeros((MLA_NOPE, KV_RANK), F32)
    w_uk_t = w_uk.transpose(1, 2, 0)
    w_uk_pair = jnp.stack([jnp.block([[w_uk_t[2 * p], zeros], [zeros, w_uk_t[2 * p + 1]]])
                           for p in range(nh // 2)]).astype(BF16)
    w_uv_t = w_uv.transpose(1, 0, 2)
    zv = jnp.zeros((KV_RANK, MLA_V), F32)
    w_uv_pair = jnp.stack([jnp.block([[w_uv_t[2 * p], zv], [zv, w_uv_t[2 * p + 1]]])
                           for p in range(nh // 2)]).astype(BF16)
    row = lambda v: v.reshape(1, -1)
    return dict(
        ffn1_norm=row(ffn1_norm), ffn1=(ffn1_gate, ffn1_up, ffn1_down),
        ffn2_norm=row(ffn2_norm), ffn2=(ffn2_gate, ffn2_up, ffn2_down), final_norm=row(final_norm),
        mix_norm=row(mix_norm), w_in=w_in_packed, w_in_cols=tuple(w_in_cols), q_norm=row(q_norm), kv_norm=row(kv_norm),
        w_uq_nope=w_uq3[:, :, :MLA_NOPE].reshape(Q_RANK, nh * MLA_NOPE).astype(BF16),
        w_uq_rope=w_uq3[:, :, MLA_NOPE:].reshape(Q_RANK, nh * MLA_ROPE).astype(BF16),
        w_uk_pair=w_uk_pair, w_uv_pair=w_uv_pair, w_uv_all=w_uv.reshape(KV_RANK, nh * MLA_V).astype(BF16),
        ret_gn=row(ret_gn), w_mla_o=w_mla_o.astype(BF16), w_ret_o=w_ret_o.astype(BF16), w_out=w_out.astype(BF16),
    )


def kernel(x_prompt, x_sample, cache_ckv, cache_krope, state_ret, page_table, meta_tokens, ffn1_norm, ffn1_gate, ffn1_up, ffn1_down, mix_norm, w_in, q_norm, kv_norm, w_uq, w_uk, w_uv, w_mla_o, ret_gn, w_ret_o, w_out, ffn2_norm, ffn2_gate, ffn2_up, ffn2_down, final_norm):
    assert ffn1_gate.shape[0] == 1, "single-layer trunk"
    b, t, d = x_prompt.shape
    bd, dec_seq, _ = x_sample.shape
    assert dec_seq == 1
    n_meta = meta_tokens.shape[0]
    assert n_meta == N_META
    past_len = page_table.shape[1] * PAGE_SIZE
    W = _prep_weights(ffn1_norm[0], ffn1_gate[0], ffn1_up[0], ffn1_down[0], mix_norm[0], w_in[0], q_norm[0],
                      kv_norm[0], w_uq[0], w_uk[0], w_uv[0], w_mla_o[0], ret_gn[0], w_ret_o[0], w_out[0],
                      ffn2_norm[0], ffn2_gate[0], ffn2_up[0], ffn2_down[0], final_norm)

    rope_r, rope_q = _rope_tables(n_meta + np.arange(t))
    rope_r_s, rope_q_s = _rope_tables(np.concatenate([np.arange(n_meta), np.full((bd,), past_len)]))
    x_small = jnp.concatenate([meta_tokens.astype(F32), x_sample.reshape(bd, d)], axis=0)

    h1, h1s = _ffn(x_prompt.reshape(b * t, d), x_small, W["ffn1_norm"], *W["ffn1"])
    S = _mix_in(h1s.reshape(1, n_meta + bd, d), W, rope_r_s, rope_q_s)
    meta_keys = (S["kt"][0, :, :n_meta].astype(F32), S["kcat"][0, :n_meta, :KV_RANK].astype(F32))
    P = _mix_in(h1.reshape(b, t, d), W, rope_r, rope_q, prev_keys=meta_keys)

    apre = _prompt_attention(P["qcat"], P["kts"], P["vs"], P["kt"], P["kcat"], W["w_uv_pair"])
    on, s_prompt = _prompt_retention(P["rq"], P["rk"], P["rv"], S["rk"][0, :n_meta], S["rv"][0, :n_meta])

    nh = MLA_HEADS
    qlat_s = S["qcat"][0, :, n_meta:, :KV_RANK].transpose(1, 0, 2)
    qrope_s = S["qrope"][0, n_meta:].reshape(bd, nh, MLA_ROPE)
    kself = S["kcat"][0, n_meta:, :KV_RANK + MLA_ROPE].reshape(bd, 1, KV_RANK + MLA_ROPE)
    apre_s = _decode_attention(page_table, qlat_s, qrope_s, kself, W["w_uv_all"], cache_ckv[0],
                               cache_krope[0].transpose(0, 2, 1))
    dec = lambda name: S[name][0, n_meta:].reshape(bd, 1, -1)
    on_s, s_sample = _decode_retention(dec("rq"), dec("rk"), dec("rv"), state_ret[0])

    flat = lambda a: a.reshape(b * t, -1)
    h2 = _mix_out(h1, flat(apre), flat(on), flat(P["srg"]), flat(P["sga"]), flat(P["sgb"]), W)
    tail = lambda name: S[name][0, n_meta:]
    h2s = _mix_out(h1s[n_meta:], apre_s.reshape(bd, -1), on_s.reshape(bd, -1), tail("srg"), tail("sga"), tail("sgb"), W)
    y_prompt, y_sample = _ffn(h2, h2s, W["ffn2_norm"], *W["ffn2"], final_norm=W["final_norm"])

    with_meta = lambda small, main: jnp.concatenate(
        [jnp.broadcast_to(small[:, :n_meta], (b, n_meta, small.shape[2])), main], axis=1)[None]
    return (
        y_prompt.reshape(b, t, d),
        y_sample.reshape(bd, 1, d),
        with_meta(S["ckv"], P["ckv"]),
        with_meta(S["kr"], P["kr"]),
        s_prompt[None],
        S["ckv"][0, n_meta:].reshape(1, bd, 1, KV_RANK),
        S["kr"][0, n_meta:].reshape(1, bd, 1, MLA_ROPE),
        s_sample[None],
    )
```

```python
import functools
import math

import jax
import jax.numpy as jnp
import numpy as np
from jax import lax
from jax.experimental import pallas as pl
from jax.experimental.pallas import tpu as pltpu

N_META = 16
PAGE_SIZE = 128
MLA_HEADS = 8
MLA_NOPE = 64
MLA_ROPE = 32
MLA_V = 64
Q_RANK = 384
KV_RANK = 256
MLA_SCALE = (MLA_NOPE + MLA_ROPE) ** -0.5
Q_SCALE = MLA_SCALE * math.log2(math.e)
RET_HEADS = 4
RET_DK = 128
RET_DV = 256
ROPE_BASE = 10000.0
NORM_EPS = 1e-6
RET_LOG_GAMMA = tuple(math.log1p(-(2.0 ** (-5.0 - h))) for h in range(RET_HEADS))

VMEM_LIMIT_BYTES = 56 * 1024 * 1024
FFN_CHUNK = 256
TOKEN_TILE = 512
MIX_OUT_TILE = 1024
ATTN_BLOCK = 256
ATTN_GROUP_HEADS = 1
ATTN_ROW_CHUNK = 32
RET_CHUNK = 256
RET_SEQS = 8
DEC_PAGES_PER_STEP = 64
DEC_SUB_PAGES = 16
DEC_SLOTS = 2
DEC_RET_REQUESTS = 4
MASK_VALUE = -1e30

F32 = jnp.float32
BF16 = jnp.bfloat16


def _params(*sem):
    return pltpu.CompilerParams(dimension_semantics=sem, vmem_limit_bytes=VMEM_LIMIT_BYTES)


def _const_spec(shape):
    n = len(shape)
    return pl.BlockSpec(shape, lambda *_: (0,) * n, pipeline_mode=pl.Buffered(1))


def _rms(x, g):
    return x * lax.rsqrt(jnp.mean(x * x, axis=-1, keepdims=True) + NORM_EPS) * g


def _sigmoid(x):
    return 1.0 / (1.0 + jnp.exp(-x))


def _dot(a, b):
    return jnp.dot(a, b, preferred_element_type=F32)


def _dot_nt(a, b):
    return lax.dot_general(a, b, (((1,), (1,)), ((), ())), preferred_element_type=F32)


def _ffn_body(x_ref, xs_ref, g_ref, wg_ref, wu_ref, wd_ref, *rest, final):
    o_ref, os_ref = rest[-2:]

    def apply(src_ref, dst_ref):
        x = src_ref[...]
        u = _rms(x, g_ref[...]).astype(BF16)
        acc = None
        for c in range(0, wg_ref.shape[1], FFN_CHUNK):
            cols = slice(c, c + FFN_CHUNK)
            gate = _dot(u, wg_ref[:, cols].astype(BF16))
            up = _dot(u, wu_ref[:, cols].astype(BF16))
            act = (gate * _sigmoid(gate) * up).astype(BF16)
            part = _dot(act, wd_ref[cols, :].astype(BF16))
            acc = part if acc is None else acc + part
        y = x + 0.5 * acc
        if final:
            y = _rms(y, rest[0][...])
        dst_ref[...] = y

    last = pl.num_programs(0) - 1
    pl.when(pl.program_id(0) < last)(functools.partial(apply, x_ref, o_ref))
    pl.when(pl.program_id(0) == last)(functools.partial(apply, xs_ref, os_ref))


def _ffn(x, x_small, norm, wg, wu, wd, final_norm=None):
    m, d = x.shape
    tm = min(m, TOKEN_TILE)
    assert m % tm == 0 and wg.shape[1] % FFN_CHUNK == 0
    n = m // tm
    final = final_norm is not None
    main = pl.BlockSpec((tm, d), lambda i: (jnp.minimum(i, n - 1), 0))
    in_specs = [main, _const_spec(x_small.shape), _const_spec((1, d)),
                _const_spec(wg.shape), _const_spec(wu.shape), _const_spec(wd.shape)]
    args = [x, x_small, norm, wg, wu, wd]
    if final:
        in_specs.append(_const_spec((1, d)))
        args.append(final_norm)
    return pl.pallas_call(
        functools.partial(_ffn_body, final=final),
        grid=(n + 1,),
        in_specs=in_specs,
        out_specs=[main, pl.BlockSpec(x_small.shape, lambda i: (0, 0))],
        out_shape=[jax.ShapeDtypeStruct((m, d), F32), jax.ShapeDtypeStruct(x_small.shape, F32)],
        compiler_params=_params("arbitrary"),
        name="ffn_final" if final else "ffn",
    )(*args)


def _rope32(x, cos, sin):
    lane = lax.broadcasted_iota(jnp.int32, x.shape, 1)
    width = x.shape[1]
    partner = jnp.where((lane & 31) < 16, pltpu.roll(x, width - 16, axis=1), pltpu.roll(x, 16, axis=1))
    return x * cos + partner * sin


def _mix_in_body(h_ref, g_ref, win_ref, qn_ref, kvn_ref, wuqn_ref, wuqr_ref, wuk_ref, rope_r_ref, rope_q_ref,
                 *rest, shifted, cols):
    def project(x, piece, reps=1):
        lo, hi = cols[piece]
        w = win_ref[lo:hi, :].astype(BF16)
        return _dot_nt(x, w if reps == 1 else jnp.concatenate([w] * reps, axis=0))

    if shifted:
        (kprev_ref, vprev_ref, qlat_ref, qrope_ref, ckv_ref, kr_ref, kcat_ref, kt_ref, rq_ref, rk_ref, rv_ref,
         srg_ref, sga_ref, sgb_ref, kts_ref, vs_ref, kcarry_ref, vcarry_ref) = rest
    else:
        (qlat_ref, qrope_ref, ckv_ref, kr_ref, kcat_ref, kt_ref, rq_ref, rk_ref, rv_ref,
         srg_ref, sga_ref, sgb_ref) = rest
    if shifted:
        @pl.when(pl.program_id(1) == 0)
        def _():
            kcarry_ref[...] = kprev_ref[...]
            vcarry_ref[...] = vprev_ref[...]

    u = _rms(h_ref[...], g_ref[...]).astype(BF16)
    cos_q, sin_q = rope_q_ref[0], rope_q_ref[1]
    cos_r, sin_r = rope_r_ref[0], rope_r_ref[1]

    cq = _rms(project(u, 0), qn_ref[...]).astype(BF16)
    q_nope = (_dot(cq, wuqn_ref[...]) * Q_SCALE).astype(BF16)
    for p in range(MLA_HEADS // 2):
        pair = _dot(q_nope[:, p * 128:(p + 1) * 128], wuk_ref[p])
        qlat_ref[2 * p] = pair[:, :KV_RANK].astype(BF16)
        qlat_ref[2 * p + 1] = pair[:, KV_RANK:].astype(BF16)
    q_rope = _rope32(_dot(cq, wuqr_ref[...]), cos_q, sin_q) * Q_SCALE
    qrope_ref[...] = q_rope.astype(BF16)

    ckv = _rms(project(u, 1), kvn_ref[...])
    ckv_ref[...] = ckv
    kcat_ref[:, :KV_RANK] = ckv.astype(BF16)
    kr = _rope32(project(u, 2, reps=MLA_HEADS), cos_q, sin_q)
    kr_ref[...] = kr[:, :MLA_ROPE]
    kcat_ref[:, KV_RANK:] = kr.astype(BF16)
    k_t = jnp.concatenate([ckv, kr], axis=1).T
    kt_ref[...] = k_t.astype(BF16)
    if shifted:
        keep = k_t.shape[1] - N_META
        k_sh = jnp.concatenate([kcarry_ref[...], k_t[:, :keep]], axis=1).astype(BF16)
        blk = kts_ref.shape[2]
        for j in range(kts_ref.shape[0]):
            kts_ref[j] = k_sh[:, j * blk:(j + 1) * blk]
        vs_ref[...] = jnp.concatenate([vcarry_ref[...], ckv[:keep]], axis=0).astype(BF16)
        kcarry_ref[...] = k_t[:, keep:]
        vcarry_ref[...] = ckv[keep:]

    rq = project(u, 3)
    rk = project(u, 4)
    for hd in range(RET_HEADS):
        sl = slice(hd * RET_DK, (hd + 1) * RET_DK)
        xq, xk = rq[:, sl], rk[:, sl]
        rq_ref[:, sl] = (xq * cos_r + pltpu.roll(xq, RET_DK // 2, axis=1) * sin_r).astype(BF16)
        rk_ref[:, sl] = ((xk * cos_r + pltpu.roll(xk, RET_DK // 2, axis=1) * sin_r) * (RET_DK ** -0.5)).astype(BF16)
    rv_ref[...] = project(u, 5).astype(BF16)

    rg = project(u, 6)
    srg_ref[...] = (rg * _sigmoid(rg)).astype(BF16)
    sga_ref[...] = _sigmoid(project(u, 7)).astype(BF16)
    sgb_ref[...] = _sigmoid(project(u, 8)).astype(BF16)


def _mix_in(h, W, rope_r, rope_q, prev_keys=None):
    b, t, d = h.shape
    tm = min(t, TOKEN_TILE)
    assert t % tm == 0
    hr = RET_HEADS * RET_DK
    hv = RET_HEADS * RET_DV
    qw = MLA_HEADS * MLA_ROPE
    shifted = prev_keys is not None
    weights = [W["mix_norm"], W["w_in"], W["q_norm"], W["kv_norm"], W["w_uq_nope"], W["w_uq_rope"], W["w_uk_pair"]]
    tok = lambda width: pl.BlockSpec((None, tm, width), lambda bi, ti: (bi, ti, 0))
    in_specs = ([tok(d)] + [_const_spec(w.shape) for w in weights]
                + [pl.BlockSpec((2, tm, RET_DK), lambda bi, ti: (0, ti, 0)),
                   pl.BlockSpec((2, tm, qw), lambda bi, ti: (0, ti, 0))])
    out_specs = [
        pl.BlockSpec((None, MLA_HEADS, tm, KV_RANK), lambda bi, ti: (bi, 0, ti, 0)),
        tok(qw), tok(KV_RANK), tok(MLA_ROPE), tok(2 * KV_RANK),
        pl.BlockSpec((None, 2 * KV_RANK, tm), lambda bi, ti: (bi, 0, ti)),
        tok(hr), tok(hr), tok(hv), tok(hv), tok(d), tok(d),
    ]
    sds = jax.ShapeDtypeStruct
    out_shape = [
        sds((b, MLA_HEADS, t, KV_RANK), BF16), sds((b, t, qw), BF16), sds((b, t, KV_RANK), F32),
        sds((b, t, MLA_ROPE), F32), sds((b, t, 2 * KV_RANK), BF16), sds((b, 2 * KV_RANK, t), BF16),
        sds((b, t, hr), BF16), sds((b, t, hr), BF16),
        sds((b, t, hv), BF16), sds((b, t, hv), BF16), sds((b, t, d), BF16), sds((b, t, d), BF16),
    ]
    names = ["qlat", "qrope", "ckv", "kr", "kcat", "kt", "rq", "rk", "rv", "srg", "sga", "sgb"]
    extra_in, scratch = [], []
    if shifted:
        blk = ATTN_BLOCK
        assert tm % blk == 0 and N_META % 8 == 0
        extra_in = list(prev_keys)
        in_specs += [_const_spec(a.shape) for a in extra_in]
        out_specs += [pl.BlockSpec((None, tm // blk, 2 * KV_RANK, blk), lambda bi, ti: (bi, ti, 0, 0)), tok(KV_RANK)]
        out_shape += [sds((b, t // blk, 2 * KV_RANK, blk), BF16), sds((b, t, KV_RANK), BF16)]
        names += ["kts", "vs"]
        scratch = [pltpu.VMEM((2 * KV_RANK, N_META), F32), pltpu.VMEM((N_META, KV_RANK), F32)]
    outs = pl.pallas_call(
        functools.partial(_mix_in_body, shifted=shifted, cols=W["w_in_cols"]),
        grid=(b, t // tm),
        in_specs=in_specs,
        out_specs=out_specs,
        out_shape=out_shape,
        scratch_shapes=scratch,
        compiler_params=_params("parallel", "arbitrary"),
        name="mixer_inputs",
    )(h, *weights, rope_r, rope_q, *extra_in)
    return dict(zip(names, outs))


def _lanes(x, width):
    return x if width == x.shape[1] else jnp.concatenate([x] * (width // x.shape[1]), axis=1)


def _attn_body(qlat_ref, qrope_ref, kt_ref, v_ref, ktail_ref, vtail_ref, wuv_ref, o_ref,
               qcat_ref, s_ref, p_ref, alpha_ref, m_ref, l_ref, acc_ref):
    blk = qrope_ref.shape[0]
    grp = ATTN_GROUP_HEADS * blk
    n_grp = MLA_HEADS // ATTN_GROUP_HEADS

    def blocks(j, n_blk, first):
        width = n_blk * blk
        v = v_ref[pl.ds(pl.multiple_of(j * blk, blk), width), :]
        if first:
            tok = lax.broadcasted_iota(jnp.int32, (ATTN_ROW_CHUNK, width), 0)
            key = lax.broadcasted_iota(jnp.int32, (ATTN_ROW_CHUNK, width), 1)
        for g in range(n_grp):
            rows = slice(g * grp, (g + 1) * grp)
            for n in range(n_blk):
                s_ref[rows, n * blk:(n + 1) * blk] = _dot(qcat_ref[rows, :], kt_ref[j + n])
        for g in range(n_grp):
            for c in range(g * grp, (g + 1) * grp, ATTN_ROW_CHUNK):
                r = slice(c, c + ATTN_ROW_CHUNK)
                s = s_ref[r, :width]
                if first:
                    s = jnp.where(key <= tok + (c % blk + N_META), s, MASK_VALUE)
                    m_new = jnp.broadcast_to(jnp.max(s, axis=-1, keepdims=True), (ATTN_ROW_CHUNK, 128))
                    p = jnp.exp2(s - _lanes(m_new, width))
                    l_ref[r, :] = jnp.broadcast_to(jnp.sum(p, axis=-1, keepdims=True), (ATTN_ROW_CHUNK, 128))
                else:
                    m_prev = m_ref[r, :]
                    m_new = jnp.maximum(m_prev, jnp.max(s, axis=-1, keepdims=True))
                    alpha = jnp.exp2(m_prev - m_new)
                    p = jnp.exp2(s - _lanes(m_new, width))
                    l_ref[r, :] = alpha * l_ref[r, :] + jnp.sum(p, axis=-1, keepdims=True)
                    alpha_ref[r, :] = alpha
                m_ref[r, :] = m_new
                p_ref[r, :width] = p.astype(BF16)
            rows = slice(g * grp, (g + 1) * grp)
            pv = _dot(p_ref[rows, :width], v)
            acc_ref[rows, :] = pv if first else _lanes(alpha_ref[rows, :], KV_RANK) * acc_ref[rows, :] + pv

    i = pl.program_id(1)
    lane_head = lax.broadcasted_iota(jnp.int32, (blk, MLA_HEADS * MLA_ROPE), 1) // MLA_ROPE
    qr = qrope_ref[...]
    for hd in range(MLA_HEADS):
        qcat_ref[hd * blk:(hd + 1) * blk, :KV_RANK] = qlat_ref[hd]
        qcat_ref[hd * blk:(hd + 1) * blk, KV_RANK:] = jnp.where(lane_head == hd, qr, jnp.zeros_like(qr))

    blocks(i, 1, True)

    def two_full_blocks(jj, carry):
        blocks(2 * jj, 2, False)
        return carry

    lax.fori_loop(0, i // 2, two_full_blocks, 0)

    @pl.when(i % 2 == 1)
    def _():
        blocks(i - 1, 1, False)

    tails = [slice((hd + 1) * blk - N_META, (hd + 1) * blk) for hd in range(MLA_HEADS)]
    gather = lambda ref: jnp.concatenate([ref[r, :] for r in tails], axis=0)
    s = _dot(gather(qcat_ref), ktail_ref[...])
    q_off = lax.broadcasted_iota(jnp.int32, s.shape, 0) & (N_META - 1)
    col = lax.broadcasted_iota(jnp.int32, s.shape, 1) - (128 - N_META)
    s = jnp.where((col >= 0) & (col <= q_off), s, MASK_VALUE)
    m_prev = gather(m_ref)
    m_new = jnp.maximum(m_prev, jnp.max(s, axis=-1, keepdims=True))
    alpha = jnp.exp2(m_prev - m_new)
    p = jnp.exp2(s - m_new)
    l_new = alpha * gather(l_ref) + jnp.sum(p, axis=-1, keepdims=True)
    acc_new = _lanes(alpha, KV_RANK) * gather(acc_ref) + _dot(p.astype(BF16), vtail_ref[...])
    for hd, r in enumerate(tails):
        part = slice(hd * N_META, (hd + 1) * N_META)
        l_ref[r, :] = l_new[part]
        acc_ref[r, :] = acc_new[part]

    for pr in range(MLA_HEADS // 2):
        halves = []
        for hd in (2 * pr, 2 * pr + 1):
            rows = slice(hd * blk, (hd + 1) * blk)
            halves.append((acc_ref[rows, :] / _lanes(l_ref[rows, :], KV_RANK)).astype(BF16))
        o_ref[:, pr * 2 * MLA_V:(pr + 1) * 2 * MLA_V] = _dot(jnp.concatenate(halves, axis=1), wuv_ref[pr]).astype(BF16)


def _prompt_attention(qlat, qrope, kt_seq, v_seq, kt_tok, kcat, wuv_pair):
    b, nh, t, r = qlat.shape
    nq, blk = kt_seq.shape[1], kt_seq.shape[3]
    assert nq * blk == t and v_seq.shape[1] == t
    assert blk % 128 == 0 and blk & (blk - 1) == 0 and N_META <= 128
    rows = nh * blk
    last = blk // 128
    return pl.pallas_call(
        _attn_body,
        grid=(b, nq),
        in_specs=[
            pl.BlockSpec((None, nh, blk, r), lambda bi, i: (bi, 0, i, 0)),
            pl.BlockSpec((None, blk, qrope.shape[2]), lambda bi, i: (bi, i, 0)),
            pl.BlockSpec((None,) + kt_seq.shape[1:], lambda bi, i: (bi, 0, 0, 0)),
            pl.BlockSpec((None,) + v_seq.shape[1:], lambda bi, i: (bi, 0, 0)),
            pl.BlockSpec((None, kt_tok.shape[1], 128), lambda bi, i: (bi, 0, (i + 1) * last - 1)),
            pl.BlockSpec((None, 128, r), lambda bi, i: (bi, (i + 1) * last - 1, 0)),
            _const_spec(wuv_pair.shape),
        ],
        out_specs=pl.BlockSpec((None, blk, nh * MLA_V), lambda bi, i: (bi, i, 0)),
        out_shape=jax.ShapeDtypeStruct((b, t, nh * MLA_V), BF16),
        scratch_shapes=[
            pltpu.VMEM((rows, 2 * r), BF16),
            pltpu.VMEM((rows, 2 * blk), F32),
            pltpu.VMEM((rows, 2 * blk), BF16),
            pltpu.VMEM((rows, 128), F32),
            pltpu.VMEM((rows, 128), F32),
            pltpu.VMEM((rows, 128), F32),
            pltpu.VMEM((rows, r), F32),
        ],
        compiler_params=_params("parallel", "arbitrary"),
        name="prompt_attention",
    )(qlat, qrope, kt_seq, v_seq, kt_tok, kcat, wuv_pair)


def _group_norm(o):
    mu = jnp.mean(o, axis=-1, keepdims=True)
    c = o - mu
    return c * lax.rsqrt(jnp.mean(c * c, axis=-1, keepdims=True) + NORM_EPS)


def _ret_body(rq_ref, rk_ref, rv_ref, kmeta_ref, vmeta_ref, on_ref, sout_ref, s_ref, decay_ref):
    c = pl.program_id(1)
    n_seq, ch = rq_ref.shape[:2]
    n_meta = kmeta_ref.shape[0]

    @pl.when(c == 0)
    def _():
        j = lax.broadcasted_iota(jnp.int32, (n_meta, 1), 0).astype(F32)
        diff = (lax.broadcasted_iota(jnp.int32, (ch, ch), 0) - lax.broadcasted_iota(jnp.int32, (ch, ch), 1)).astype(F32)
        for hd in range(RET_HEADS):
            kd = kmeta_ref[:, hd * RET_DK:(hd + 1) * RET_DK].astype(F32) * jnp.exp(RET_LOG_GAMMA[hd] * (n_meta - 1.0 - j))
            s_meta = _dot(kd.T.astype(BF16), vmeta_ref[:, hd * RET_DV:(hd + 1) * RET_DV])
            for sq in range(n_seq):
                s_ref[sq, hd] = s_meta
            decay_ref[hd] = jnp.where(diff >= 0, jnp.exp(RET_LOG_GAMMA[hd] * jnp.maximum(diff, 0.0)), 0.0)

    row = lax.broadcasted_iota(jnp.int32, (ch, 1), 0).astype(F32)
    for hd in range(RET_HEADS):
        lg = RET_LOG_GAMMA[hd]
        q_decay = jnp.exp(lg * (row + 1.0))
        k_decay = jnp.exp(lg * (ch - 1.0 - row))
        for sq in range(n_seq):
            q = rq_ref[sq, :, hd * RET_DK:(hd + 1) * RET_DK]
            k = rk_ref[sq, :, hd * RET_DK:(hd + 1) * RET_DK]
            v = rv_ref[sq, :, hd * RET_DV:(hd + 1) * RET_DV]
            scores = _dot_nt(q, k) * decay_ref[hd]
            s_old = s_ref[sq, hd]
            o = _dot(scores.astype(BF16), v) + _dot(q, s_old.astype(BF16)) * q_decay
            kd = k.astype(F32) * k_decay
            s_ref[sq, hd] = math.exp(lg * ch) * s_old + _dot(kd.T.astype(BF16), v)
            on_ref[sq, :, hd * RET_DV:(hd + 1) * RET_DV] = _group_norm(o).astype(BF16)

    @pl.when(c == pl.num_programs(1) - 1)
    def _():
        sout_ref[...] = s_ref[...]


def _prompt_retention(rq, rk, rv, kmeta, vmeta):
    b, t, hr = rq.shape
    hv = rv.shape[2]
    ch = min(t, RET_CHUNK)
    n_seq = math.gcd(b, RET_SEQS)
    assert t % ch == 0
    tok = lambda width: pl.BlockSpec((n_seq, ch, width), lambda bi, ci: (bi, ci, 0))
    return pl.pallas_call(
        _ret_body,
        grid=(b // n_seq, t // ch),
        in_specs=[tok(hr), tok(hr), tok(hv), _const_spec(kmeta.shape), _const_spec(vmeta.shape)],
        out_specs=[tok(hv), pl.BlockSpec((n_seq, RET_HEADS, RET_DK, RET_DV), lambda bi, ci: (bi, 0, 0, 0))],
        out_shape=[jax.ShapeDtypeStruct((b, t, hv), BF16),
                   jax.ShapeDtypeStruct((b, RET_HEADS, RET_DK, RET_DV), F32)],
        scratch_shapes=[pltpu.VMEM((n_seq, RET_HEADS, RET_DK, RET_DV), F32), pltpu.VMEM((RET_HEADS, ch, ch), F32)],
        compiler_params=_params("parallel", "arbitrary"),
        name="prompt_retention",
    )(rq, rk, rv, kmeta, vmeta)


def _dec_attn_body(pt_ref, qlat_ref, qrope_ref, kself_ref, wuv_ref, ckv_hbm, kr_hbm, o_ref,
                   kbuf, krbuf, sem_k, sem_r, m_ref, l_ref, acc_ref, *, pages, n_chunks):
    b = pl.program_id(0)
    n_req = pl.num_programs(0)
    ahead = DEC_SLOTS - 1
    ql = qlat_ref[...]
    qr = qrope_ref[...]

    def page_copies(page, slot, p):
        lat = pltpu.make_async_copy(ckv_hbm.at[page], kbuf.at[slot, p], sem_k.at[slot])
        rope = pltpu.make_async_copy(kr_hbm.at[page], krbuf.at[slot, p], sem_r.at[slot])
        return lat, rope

    def start_chunk(req, chunk, slot):
        def one(p, carry):
            for cp in page_copies(pt_ref[req, chunk * pages + p], slot, p):
                cp.start()
            return carry
        lax.fori_loop(0, pages, one, 0)

    def wait_chunk(slot):
        def one(p, carry):
            for cp in page_copies(0, slot, p):
                cp.wait()
            return carry
        lax.fori_loop(0, pages, one, 0)

    @pl.when(b == 0)
    def _():
        for g in range(ahead):
            start_chunk(0, g, g)

    ks = kself_ref[...].astype(F32)
    qf = jnp.concatenate([ql, qr], axis=1).astype(F32)
    m_ref[...] = jnp.sum(qf * ks, axis=-1, keepdims=True)
    l_ref[...] = jnp.ones(l_ref.shape, F32)
    acc_ref[...] = jnp.broadcast_to(ks[:, :KV_RANK], acc_ref.shape)

    sub = min(pages, DEC_SUB_PAGES)
    n_sub = pages // sub
    sub_keys = sub * PAGE_SIZE

    def ring_round(rnd, carry):
        for slot in range(DEC_SLOTS):
            chunk = rnd * DEC_SLOTS + slot
            nxt = chunk + ahead
            wrap = nxt >= n_chunks
            nxt_req = jnp.where(wrap, b + 1, b)

            @pl.when(nxt_req < n_req)
            def _():
                start_chunk(nxt_req, jnp.where(wrap, nxt - n_chunks, nxt), (slot + ahead) % DEC_SLOTS)

            wait_chunk(slot)
            k_lat, scores = [], []
            for q in range(n_sub):
                k_q = kbuf[slot, q * sub:(q + 1) * sub].reshape(sub_keys, KV_RANK).astype(BF16)
                kr_q = jnp.concatenate([krbuf[slot, q * sub + j] for j in range(sub)], axis=1).astype(BF16)
                k_lat.append(k_q)
                scores.append(_dot_nt(ql, k_q) + _dot(qr, kr_q))
            s = jnp.concatenate(scores, axis=1)
            m_prev = m_ref[...]
            m_new = jnp.maximum(m_prev, jnp.max(s, axis=-1, keepdims=True))
            alpha = jnp.exp2(m_prev - m_new)
            p = jnp.exp2(s - m_new)
            l_ref[...] = alpha * l_ref[...] + jnp.sum(p, axis=-1, keepdims=True)
            p = p.astype(BF16)
            pv = _dot(p[:, :sub_keys], k_lat[0])
            for q in range(1, n_sub):
                pv = pv + _dot(p[:, q * sub_keys:(q + 1) * sub_keys], k_lat[q])
            acc_ref[...] = alpha * acc_ref[...] + pv
            m_ref[...] = m_new
        return carry

    lax.fori_loop(0, n_chunks // DEC_SLOTS, ring_round, 0)

    o = (acc_ref[...] / l_ref[...]).astype(BF16)
    z = _dot(o, wuv_ref[...])
    head = lax.broadcasted_iota(jnp.int32, z.shape, 0)
    col_head = lax.broadcasted_iota(jnp.int32, z.shape, 1) // MLA_V
    o_ref[...] = jnp.sum(jnp.where(head == col_head, z, 0.0), axis=0, keepdims=True).astype(BF16)


def _decode_attention(page_table, qlat, qrope, kself, wuv_all, cache_ckv, cache_kr_t):
    bd, nh, r = qlat.shape
    n_pages = page_table.shape[1]
    pages = min(n_pages // DEC_SLOTS, DEC_PAGES_PER_STEP)
    n_chunks = n_pages // pages
    assert n_chunks * pages == n_pages and n_chunks % DEC_SLOTS == 0 and pages % min(pages, DEC_SUB_PAGES) == 0

    per_req = lambda shape: pl.BlockSpec((None,) + shape, lambda b, pt: (b, 0, 0))
    grid_spec = pltpu.PrefetchScalarGridSpec(
        num_scalar_prefetch=1,
        grid=(bd,),
        in_specs=[per_req((nh, r)), per_req((nh, MLA_ROPE)), per_req((1, r + MLA_ROPE)),
                  pl.BlockSpec(wuv_all.shape, lambda b, pt: (0, 0)),
                  pl.BlockSpec(memory_space=pl.ANY), pl.BlockSpec(memory_space=pl.ANY)],
        out_specs=per_req((1, nh * MLA_V)),
        scratch_shapes=[
            pltpu.VMEM((DEC_SLOTS, pages, PAGE_SIZE, r), F32),
            pltpu.VMEM((DEC_SLOTS, pages, MLA_ROPE, PAGE_SIZE), F32),
            pltpu.SemaphoreType.DMA((DEC_SLOTS,)),
            pltpu.SemaphoreType.DMA((DEC_SLOTS,)),
            pltpu.VMEM((nh, 1), F32), pltpu.VMEM((nh, 1), F32), pltpu.VMEM((nh, r), F32),
        ],
    )
    return pl.pallas_call(
        functools.partial(_dec_attn_body, pages=pages, n_chunks=n_chunks),
        grid_spec=grid_spec,
        out_shape=jax.ShapeDtypeStruct((bd, 1, nh * MLA_V), BF16),
        compiler_params=_params("arbitrary"),
        name="decode_attention",
    )(page_table, qlat, qrope, kself, wuv_all, cache_ckv, cache_kr_t)


def _dec_ret_body(rq_ref, rk_ref, rv_ref, s_ref, on_ref, snew_ref):
    eye = (lax.broadcasted_iota(jnp.int32, (RET_DK, RET_DK), 0)
           == lax.broadcasted_iota(jnp.int32, (RET_DK, RET_DK), 1))

    def column(x):
        return jnp.sum(jnp.where(eye, jnp.broadcast_to(x, (RET_DK, RET_DK)), 0.0), axis=1, keepdims=True)

    for req in range(rq_ref.shape[0]):
        for hd in range(RET_HEADS):
            gamma = math.exp(RET_LOG_GAMMA[hd])
            q = rq_ref[req, :, hd * RET_DK:(hd + 1) * RET_DK].astype(F32)
            k = rk_ref[req, :, hd * RET_DK:(hd + 1) * RET_DK].astype(F32)
            v = rv_ref[req, :, hd * RET_DV:(hd + 1) * RET_DV].astype(F32)
            s_old = s_ref[req, hd]
            qs = jnp.sum(column(q) * s_old, axis=0, keepdims=True)
            o = jnp.sum(q * k, axis=-1, keepdims=True) * v + qs * gamma
            snew_ref[req, hd] = gamma * s_old + column(k) * v
            on_ref[req, :, hd * RET_DV:(hd + 1) * RET_DV] = _group_norm(o).astype(BF16)


def _decode_retention(rq, rk, rv, state):
    bd = rq.shape[0]
    per = math.gcd(bd, DEC_RET_REQUESTS)
    row = lambda width: pl.BlockSpec((per, 1, width), lambda b: (b, 0, 0))
    st = pl.BlockSpec((per, RET_HEADS, RET_DK, RET_DV), lambda b: (b, 0, 0, 0))
    return pl.pallas_call(
        _dec_ret_body,
        grid=(bd // per,),
        in_specs=[row(rq.shape[2]), row(rk.shape[2]), row(rv.shape[2]), st],
        out_specs=[row(rv.shape[2]), st],
        out_shape=[jax.ShapeDtypeStruct((bd, 1, rv.shape[2]), BF16), jax.ShapeDtypeStruct(state.shape, F32)],
        compiler_params=_params("parallel"),
        name="decode_retention",
    )(rq, rk, rv, state)


def _mix_out_body(h_ref, a_ref, on_ref, srg_ref, sga_ref, sgb_ref, gn_ref, wa_ref, wr_ref, wo_ref, o_ref):
    a = _dot(a_ref[...], wa_ref[...].astype(BF16))
    gated = srg_ref[...].astype(F32) * (on_ref[...].astype(F32) * gn_ref[...])
    r = _dot(gated.astype(BF16), wr_ref[...].astype(BF16))
    m = sga_ref[...].astype(F32) * a + sgb_ref[...].astype(F32) * r
    o_ref[...] = h_ref[...] + _dot(m.astype(BF16), wo_ref[...].astype(BF16))


def _mix_out(h, apre, on, srg, sga, sgb, W):
    m, d = h.shape
    tm = min(m, MIX_OUT_TILE)
    assert m % tm == 0
    tok = lambda width: pl.BlockSpec((tm, width), lambda i: (i, 0))
    weights = [W["ret_gn"], W["w_mla_o"], W["w_ret_o"], W["w_out"]]
    return pl.pallas_call(
        _mix_out_body,
        grid=(m // tm,),
        in_specs=[tok(d), tok(apre.shape[1]), tok(on.shape[1]), tok(d), tok(d), tok(d)]
                 + [_const_spec(w.shape) for w in weights],
        out_specs=tok(d),
        out_shape=jax.ShapeDtypeStruct((m, d), F32),
        compiler_params=_params("parallel"),
        name="mixer_output",
    )(h, apre, on, srg, sga, sgb, *weights)


def _rope_tables(pos):
    pos = np.asarray(pos, np.float64)[:, None]

    def table(width, reps):
        half = width // 2
        inv_freq = ROPE_BASE ** (-np.arange(half, dtype=np.float64) / half)
        ang = pos * inv_freq[None, :]
        cos, sin = np.cos(ang), np.sin(ang)
        return np.stack([np.tile(np.concatenate([cos, cos], axis=1), (1, reps)),
                         np.tile(np.concatenate([-sin, sin], axis=1), (1, reps))]).astype(np.float32)

    return table(RET_DK, 1), table(MLA_ROPE, MLA_HEADS)


def _prep_weights(ffn1_norm, ffn1_gate, ffn1_up, ffn1_down, mix_norm, w_in, q_norm, kv_norm, w_uq, w_uk, w_uv,
                  w_mla_o, ret_gn, w_ret_o, w_out, ffn2_norm, ffn2_gate, ffn2_up, ffn2_down, final_norm):
    d = w_in.shape[0]
    hr, hv = RET_HEADS * RET_DK, RET_HEADS * RET_DV
    nh, hd = MLA_HEADS, MLA_NOPE + MLA_ROPE

    bounds = [0]
    for width in (Q_RANK, KV_RANK, MLA_ROPE, hr, hr, hv, hv, d, d):
        bounds.append(bounds[-1] + width)
    assert bounds[-1] == w_in.shape[1] and all(bd % 16 == 0 for bd in bounds)
    w_in_cols = [(bounds[i], bounds[i + 1]) for i in range(9)]
    w_in_packed = w_in.T

    w_uq3 = w_uq.reshape(Q_RANK, nh, hd)
    zeros = jnp.zeros((MLA_NOPE, KV_RANK), F32)
    w_uk_t = w_uk.transpose(1, 2, 0)
    w_uk_pair = jnp.stack([jnp.block([[w_uk_t[2 * p], zeros], [zeros, w_uk_t[2 * p + 1]]])
                           for p in range(nh // 2)]).astype(BF16)
    w_uv_t = w_uv.transpose(1, 0, 2)
    zv = jnp.zeros((KV_RANK, MLA_V), F32)
    w_uv_pair = jnp.stack([jnp.block([[w_uv_t[2 * p], zv], [zv, w_uv_t[2 * p + 1]]])
                           for p in range(nh // 2)]).astype(BF16)
    row = lambda v: v.reshape(1, -1)
    return dict(
        ffn1_norm=row(ffn1_norm), ffn1=(ffn1_gate, ffn1_up, ffn1_down),
        ffn2_norm=row(ffn2_norm), ffn2=(ffn2_gate, ffn2_up, ffn2_down), final_norm=row(final_norm),
        mix_norm=row(mix_norm), w_in=w_in_packed, w_in_cols=tuple(w_in_cols), q_norm=row(q_norm), kv_norm=row(kv_norm),
        w_uq_nope=w_uq3[:, :, :MLA_NOPE].reshape(Q_RANK, nh * MLA_NOPE).astype(BF16),
        w_uq_rope=w_uq3[:, :, MLA_NOPE:].reshape(Q_RANK, nh * MLA_ROPE).astype(BF16),
        w_uk_pair=w_uk_pair, w_uv_pair=w_uv_pair, w_uv_all=w_uv.reshape(KV_RANK, nh * MLA_V).astype(BF16),
        ret_gn=row(ret_gn), w_mla_o=w_mla_o, w_ret_o=w_ret_o, w_out=w_out,
    )


def kernel(x_prompt, x_sample, cache_ckv, cache_krope, state_ret, page_table, meta_tokens, ffn1_norm, ffn1_gate, ffn1_up, ffn1_down, mix_norm, w_in, q_norm, kv_norm, w_uq, w_uk, w_uv, w_mla_o, ret_gn, w_ret_o, w_out, ffn2_norm, ffn2_gate, ffn2_up, ffn2_down, final_norm):
    assert ffn1_gate.shape[0] == 1, "single-layer trunk"
    b, t, d = x_prompt.shape
    bd, dec_seq, _ = x_sample.shape
    assert dec_seq == 1
    n_meta = meta_tokens.shape[0]
    assert n_meta == N_META
    past_len = page_table.shape[1] * PAGE_SIZE
    W = _prep_weights(ffn1_norm[0], ffn1_gate[0], ffn1_up[0], ffn1_down[0], mix_norm[0], w_in[0], q_norm[0],
                      kv_norm[0], w_uq[0], w_uk[0], w_uv[0], w_mla_o[0], ret_gn[0], w_ret_o[0], w_out[0],
                      ffn2_norm[0], ffn2_gate[0], ffn2_up[0], ffn2_down[0], final_norm)

    rope_r, rope_q = _rope_tables(n_meta + np.arange(t))
    rope_r_s, rope_q_s = _rope_tables(np.concatenate([np.arange(n_meta), np.full((bd,), past_len)]))
    x_small = jnp.concatenate([meta_tokens.astype(F32), x_sample.reshape(bd, d)], axis=0)

    h1, h1s = _ffn(x_prompt.reshape(b * t, d), x_small, W["ffn1_norm"], *W["ffn1"])
    S = _mix_in(h1s.reshape(1, n_meta + bd, d), W, rope_r_s, rope_q_s)
    meta_keys = (S["kt"][0, :, :n_meta].astype(F32), S["kcat"][0, :n_meta, :KV_RANK].astype(F32))
    P = _mix_in(h1.reshape(b, t, d), W, rope_r, rope_q, prev_keys=meta_keys)

    apre = _prompt_attention(P["qlat"], P["qrope"], P["kts"], P["vs"], P["kt"], P["kcat"], W["w_uv_pair"])
    on, s_prompt = _prompt_retention(P["rq"], P["rk"], P["rv"], S["rk"][0, :n_meta], S["rv"][0, :n_meta])

    nh = MLA_HEADS
    qlat_s = S["qlat"][0, :, n_meta:].transpose(1, 0, 2)
    qrope_s = S["qrope"][0, n_meta:].reshape(bd, nh, MLA_ROPE)
    kself = S["kcat"][0, n_meta:, :KV_RANK + MLA_ROPE].reshape(bd, 1, KV_RANK + MLA_ROPE)
    apre_s = _decode_attention(page_table, qlat_s, qrope_s, kself, W["w_uv_all"], cache_ckv[0],
                               cache_krope[0].transpose(0, 2, 1))
    dec = lambda name: S[name][0, n_meta:].reshape(bd, 1, -1)
    on_s, s_sample = _decode_retention(dec("rq"), dec("rk"), dec("rv"), state_ret[0])

    flat = lambda a: a.reshape(b * t, -1)
    h2 = _mix_out(h1, flat(apre), flat(on), flat(P["srg"]), flat(P["sga"]), flat(P["sgb"]), W)
    tail = lambda name: S[name][0, n_meta:]
    h2s = _mix_out(h1s[n_meta:], apre_s.reshape(bd, -1), on_s.reshape(bd, -1), tail("srg"), tail("sga"), tail("sgb"), W)
    y_prompt, y_sample = _ffn(h2, h2s, W["ffn2_norm"], *W["ffn2"], final_norm=W["final_norm"])

    with_meta = lambda small, main: jnp.concatenate(
        [jnp.broadcast_to(small[:, :n_meta], (b, n_meta, small.shape[2])), main], axis=1)[None]
    return (
        y_prompt.reshape(b, t, d),
        y_sample.reshape(bd, 1, d),
        with_meta(S["ckv"], P["ckv"]),
        with_meta(S["kr"], P["kr"]),
        s_prompt[None],
        S["ckv"][0, n_meta:].reshape(1, bd, 1, KV_RANK),
        S["kr"][0, n_meta:].reshape(1, bd, 1, MLA_ROPE),
        s_sample[None],
    )
```

```python
import functools
import math

import jax
import jax.numpy as jnp
import numpy as np
from jax import lax
from jax.experimental import pallas as pl
from jax.experimental.pallas import tpu as pltpu

N_META = 16
PAGE_SIZE = 128
MLA_HEADS = 8
MLA_NOPE = 64
MLA_ROPE = 32
MLA_V = 64
Q_RANK = 384
KV_RANK = 256
MLA_SCALE = (MLA_NOPE + MLA_ROPE) ** -0.5
Q_SCALE = MLA_SCALE * math.log2(math.e)
RET_HEADS = 4
RET_DK = 128
RET_DV = 256
ROPE_BASE = 10000.0
NORM_EPS = 1e-6
RET_LOG_GAMMA = tuple(math.log1p(-(2.0 ** (-5.0 - h))) for h in range(RET_HEADS))

VMEM_LIMIT_BYTES = 56 * 1024 * 1024
FFN_CHUNK = 256
TOKEN_TILE = 512
MIX_OUT_TILE = 1024
ATTN_BLOCK = 256
ATTN_GROUP_HEADS = 1
ATTN_ROW_CHUNK = 32
RET_CHUNK = 256
RET_SEQS = 8
DEC_PAGES_PER_STEP = 64
DEC_SUB_PAGES = 16
DEC_SLOTS = 2
DEC_RET_REQUESTS = 4
MASK_VALUE = -1e30

F32 = jnp.float32
BF16 = jnp.bfloat16


def _params(*sem):
    return pltpu.CompilerParams(dimension_semantics=sem, vmem_limit_bytes=VMEM_LIMIT_BYTES)


def _const_spec(shape):
    n = len(shape)
    return pl.BlockSpec(shape, lambda *_: (0,) * n, pipeline_mode=pl.Buffered(1))


def _rms(x, g):
    return x * lax.rsqrt(jnp.mean(x * x, axis=-1, keepdims=True) + NORM_EPS) * g


def _sigmoid(x):
    return 1.0 / (1.0 + jnp.exp(-x))


def _dot(a, b):
    return jnp.dot(a, b, preferred_element_type=F32)


def _dot_nt(a, b):
    return lax.dot_general(a, b, (((1,), (1,)), ((), ())), preferred_element_type=F32)


def _ffn_body(x_ref, xs_ref, g_ref, wg_ref, wu_ref, wd_ref, *rest, final):
    o_ref, os_ref = rest[-2:]

    def apply(with_small):
        rows = x_ref.shape[0]
        x = jnp.concatenate([x_ref[...], xs_ref[...]], axis=0) if with_small else x_ref[...]
        u = _rms(x, g_ref[...]).astype(BF16)
        acc = None
        for c in range(0, wg_ref.shape[1], FFN_CHUNK):
            cols = slice(c, c + FFN_CHUNK)
            gate = _dot(u, wg_ref[:, cols].astype(BF16))
            up = _dot(u, wu_ref[:, cols].astype(BF16))
            act = (gate * _sigmoid(gate) * up).astype(BF16)
            part = _dot(act, wd_ref[cols, :].astype(BF16))
            acc = part if acc is None else acc + part
        y = x + 0.5 * acc
        if final:
            y = _rms(y, rest[0][...])
        o_ref[...] = y[:rows]
        if with_small:
            os_ref[...] = y[rows:]

    last = pl.num_programs(0) - 1
    pl.when(pl.program_id(0) < last)(functools.partial(apply, False))
    pl.when(pl.program_id(0) == last)(functools.partial(apply, True))


def _ffn(x, x_small, norm, wg, wu, wd, final_norm=None):
    m, d = x.shape
    tm = min(m, TOKEN_TILE)
    assert m % tm == 0 and wg.shape[1] % FFN_CHUNK == 0
    n = m // tm
    final = final_norm is not None
    main = pl.BlockSpec((tm, d), lambda i: (i, 0))
    in_specs = [main, _const_spec(x_small.shape), _const_spec((1, d)),
                _const_spec(wg.shape), _const_spec(wu.shape), _const_spec(wd.shape)]
    args = [x, x_small, norm, wg, wu, wd]
    if final:
        in_specs.append(_const_spec((1, d)))
        args.append(final_norm)
    return pl.pallas_call(
        functools.partial(_ffn_body, final=final),
        grid=(n,),
        in_specs=in_specs,
        out_specs=[main, pl.BlockSpec(x_small.shape, lambda i: (0, 0))],
        out_shape=[jax.ShapeDtypeStruct((m, d), F32), jax.ShapeDtypeStruct(x_small.shape, F32)],
        compiler_params=_params("arbitrary"),
        name="ffn_final" if final else "ffn",
    )(*args)


def _rope32(x, cos, sin):
    lane = lax.broadcasted_iota(jnp.int32, x.shape, 1)
    width = x.shape[1]
    partner = jnp.where((lane & 31) < 16, pltpu.roll(x, width - 16, axis=1), pltpu.roll(x, 16, axis=1))
    return x * cos + partner * sin


def _mix_in_body(h_ref, g_ref, win_ref, qn_ref, kvn_ref, wuqn_ref, wuqr_ref, wuk_ref, rope_r_ref, rope_q_ref,
                 *rest, shifted, cols):
    def project(x, piece, reps=1):
        lo, hi = cols[piece]
        w = win_ref[lo:hi, :].astype(BF16)
        return _dot_nt(x, w if reps == 1 else jnp.concatenate([w] * reps, axis=0))

    if shifted:
        (kprev_ref, vprev_ref, qlat_ref, qrope_ref, ckv_ref, kr_ref, kcat_ref, kt_ref, rq_ref, rk_ref, rv_ref,
         srg_ref, sga_ref, sgb_ref, kts_ref, vs_ref, kcarry_ref, vcarry_ref) = rest
    else:
        (qlat_ref, qrope_ref, ckv_ref, kr_ref, kcat_ref, kt_ref, rq_ref, rk_ref, rv_ref,
         srg_ref, sga_ref, sgb_ref) = rest
    if shifted:
        @pl.when(pl.program_id(1) == 0)
        def _():
            kcarry_ref[...] = kprev_ref[...]
            vcarry_ref[...] = vprev_ref[...]

    u = _rms(h_ref[...], g_ref[...]).astype(BF16)
    cos_q, sin_q = rope_q_ref[0], rope_q_ref[1]
    cos_r, sin_r = rope_r_ref[0], rope_r_ref[1]

    cq = _rms(project(u, 0), qn_ref[...]).astype(BF16)
    q_nope = (_dot(cq, wuqn_ref[...]) * Q_SCALE).astype(BF16)
    for p in range(MLA_HEADS // 2):
        pair = _dot(q_nope[:, p * 128:(p + 1) * 128], wuk_ref[p])
        qlat_ref[2 * p] = pair[:, :KV_RANK].astype(BF16)
        qlat_ref[2 * p + 1] = pair[:, KV_RANK:].astype(BF16)
    q_rope = _rope32(_dot(cq, wuqr_ref[...]), cos_q, sin_q) * Q_SCALE
    qrope_ref[...] = q_rope.astype(BF16)

    ckv = _rms(project(u, 1), kvn_ref[...])
    ckv_ref[...] = ckv
    kcat_ref[:, :KV_RANK] = ckv.astype(BF16)
    kr = _rope32(project(u, 2, reps=MLA_HEADS), cos_q, sin_q)
    kr_ref[...] = kr[:, :MLA_ROPE]
    kcat_ref[:, KV_RANK:] = kr.astype(BF16)
    k_t = jnp.concatenate([ckv, kr], axis=1).T
    kt_ref[...] = k_t.astype(BF16)
    if shifted:
        keep = k_t.shape[1] - N_META
        k_sh = jnp.concatenate([kcarry_ref[...], k_t[:, :keep]], axis=1).astype(BF16)
        blk = kts_ref.shape[2]
        for j in range(kts_ref.shape[0]):
            kts_ref[j] = k_sh[:, j * blk:(j + 1) * blk]
        vs_ref[...] = jnp.concatenate([vcarry_ref[...], ckv[:keep]], axis=0).astype(BF16)
        kcarry_ref[...] = k_t[:, keep:]
        vcarry_ref[...] = ckv[keep:]

    rq = project(u, 3)
    rk = project(u, 4)
    for hd in range(RET_HEADS):
        sl = slice(hd * RET_DK, (hd + 1) * RET_DK)
        xq, xk = rq[:, sl], rk[:, sl]
        rq_ref[:, sl] = (xq * cos_r + pltpu.roll(xq, RET_DK // 2, axis=1) * sin_r).astype(BF16)
        rk_ref[:, sl] = ((xk * cos_r + pltpu.roll(xk, RET_DK // 2, axis=1) * sin_r) * (RET_DK ** -0.5)).astype(BF16)
    rv_ref[...] = project(u, 5).astype(BF16)

    rg = project(u, 6)
    srg_ref[...] = (rg * _sigmoid(rg)).astype(BF16)
    sga_ref[...] = _sigmoid(project(u, 7)).astype(BF16)
    sgb_ref[...] = _sigmoid(project(u, 8)).astype(BF16)


def _mix_in(h, W, rope_r, rope_q, prev_keys=None):
    b, t, d = h.shape
    tm = min(t, TOKEN_TILE)
    assert t % tm == 0
    hr = RET_HEADS * RET_DK
    hv = RET_HEADS * RET_DV
    qw = MLA_HEADS * MLA_ROPE
    shifted = prev_keys is not None
    weights = [W["mix_norm"], W["w_in"], W["q_norm"], W["kv_norm"], W["w_uq_nope"], W["w_uq_rope"], W["w_uk_pair"]]
    tok = lambda width: pl.BlockSpec((None, tm, width), lambda bi, ti: (bi, ti, 0))
    in_specs = ([tok(d)] + [_const_spec(w.shape) for w in weights]
                + [pl.BlockSpec((2, tm, RET_DK), lambda bi, ti: (0, ti, 0)),
                   pl.BlockSpec((2, tm, qw), lambda bi, ti: (0, ti, 0))])
    out_specs = [
        pl.BlockSpec((None, MLA_HEADS, tm, KV_RANK), lambda bi, ti: (bi, 0, ti, 0)),
        tok(qw), tok(KV_RANK), tok(MLA_ROPE), tok(2 * KV_RANK),
        pl.BlockSpec((None, 2 * KV_RANK, tm), lambda bi, ti: (bi, 0, ti)),
        tok(hr), tok(hr), tok(hv), tok(hv), tok(d), tok(d),
    ]
    sds = jax.ShapeDtypeStruct
    out_shape = [
        sds((b, MLA_HEADS, t, KV_RANK), BF16), sds((b, t, qw), BF16), sds((b, t, KV_RANK), F32),
        sds((b, t, MLA_ROPE), F32), sds((b, t, 2 * KV_RANK), BF16), sds((b, 2 * KV_RANK, t), BF16),
        sds((b, t, hr), BF16), sds((b, t, hr), BF16),
        sds((b, t, hv), BF16), sds((b, t, hv), BF16), sds((b, t, d), BF16), sds((b, t, d), BF16),
    ]
    names = ["qlat", "qrope", "ckv", "kr", "kcat", "kt", "rq", "rk", "rv", "srg", "sga", "sgb"]
    extra_in, scratch = [], []
    if shifted:
        blk = ATTN_BLOCK
        assert tm % blk == 0 and N_META % 8 == 0
        extra_in = list(prev_keys)
        in_specs += [_const_spec(a.shape) for a in extra_in]
        out_specs += [pl.BlockSpec((None, tm // blk, 2 * KV_RANK, blk), lambda bi, ti: (bi, ti, 0, 0)), tok(KV_RANK)]
        out_shape += [sds((b, t // blk, 2 * KV_RANK, blk), BF16), sds((b, t, KV_RANK), BF16)]
        names += ["kts", "vs"]
        scratch = [pltpu.VMEM((2 * KV_RANK, N_META), F32), pltpu.VMEM((N_META, KV_RANK), F32)]
    outs = pl.pallas_call(
        functools.partial(_mix_in_body, shifted=shifted, cols=W["w_in_cols"]),
        grid=(b, t // tm),
        in_specs=in_specs,
        out_specs=out_specs,
        out_shape=out_shape,
        scratch_shapes=scratch,
        compiler_params=_params("parallel", "arbitrary"),
        name="mixer_inputs",
    )(h, *weights, rope_r, rope_q, *extra_in)
    return dict(zip(names, outs))


def _lanes(x, width):
    return x if width == x.shape[1] else jnp.concatenate([x] * (width // x.shape[1]), axis=1)


def _attn_body(qlat_ref, qrope_ref, kt_ref, v_ref, ktail_ref, vtail_ref, wuv_ref, o_ref,
               qcat_ref, s_ref, p_ref, alpha_ref, m_ref, l_ref, acc_ref):
    blk = qrope_ref.shape[0]
    grp = ATTN_GROUP_HEADS * blk
    n_grp = MLA_HEADS // ATTN_GROUP_HEADS

    def blocks(j, n_blk, first):
        width = n_blk * blk
        v = v_ref[pl.ds(pl.multiple_of(j * blk, blk), width), :]
        if first:
            tok = lax.broadcasted_iota(jnp.int32, (ATTN_ROW_CHUNK, width), 0)
            key = lax.broadcasted_iota(jnp.int32, (ATTN_ROW_CHUNK, width), 1)
        for g in range(n_grp):
            rows = slice(g * grp, (g + 1) * grp)
            for n in range(n_blk):
                s_ref[rows, n * blk:(n + 1) * blk] = _dot(qcat_ref[rows, :], kt_ref[j + n])
        for g in range(n_grp):
            for c in range(g * grp, (g + 1) * grp, ATTN_ROW_CHUNK):
                r = slice(c, c + ATTN_ROW_CHUNK)
                s = s_ref[r, :width]
                if first:
                    s = jnp.where(key <= tok + (c % blk + N_META), s, MASK_VALUE)
                    m_new = jnp.broadcast_to(jnp.max(s, axis=-1, keepdims=True), (ATTN_ROW_CHUNK, 128))
                    p = jnp.exp2(s - _lanes(m_new, width))
                    l_ref[r, :] = jnp.broadcast_to(jnp.sum(p, axis=-1, keepdims=True), (ATTN_ROW_CHUNK, 128))
                else:
                    m_prev = m_ref[r, :]
                    m_new = jnp.maximum(m_prev, jnp.max(s, axis=-1, keepdims=True))
                    alpha = jnp.exp2(m_prev - m_new)
                    p = jnp.exp2(s - _lanes(m_new, width))
                    l_ref[r, :] = alpha * l_ref[r, :] + jnp.sum(p, axis=-1, keepdims=True)
                    alpha_ref[r, :] = alpha
                m_ref[r, :] = m_new
                p_ref[r, :width] = p.astype(BF16)
            rows = slice(g * grp, (g + 1) * grp)
            pv = _dot(p_ref[rows, :width], v)
            acc_ref[rows, :] = pv if first else _lanes(alpha_ref[rows, :], KV_RANK) * acc_ref[rows, :] + pv

    i = pl.program_id(1)
    lane_head = lax.broadcasted_iota(jnp.int32, (blk, MLA_HEADS * MLA_ROPE), 1) // MLA_ROPE
    qr = qrope_ref[...]
    for hd in range(MLA_HEADS):
        qcat_ref[hd * blk:(hd + 1) * blk, :KV_RANK] = qlat_ref[hd]
        qcat_ref[hd * blk:(hd + 1) * blk, KV_RANK:] = jnp.where(lane_head == hd, qr, jnp.zeros_like(qr))

    blocks(i, 1, True)

    def two_full_blocks(jj, carry):
        blocks(2 * jj, 2, False)
        return carry

    lax.fori_loop(0, i // 2, two_full_blocks, 0)

    @pl.when(i % 2 == 1)
    def _():
        blocks(i - 1, 1, False)

    tails = [slice((hd + 1) * blk - N_META, (hd + 1) * blk) for hd in range(MLA_HEADS)]
    gather = lambda ref: jnp.concatenate([ref[r, :] for r in tails], axis=0)
    s = _dot(gather(qcat_ref), ktail_ref[...])
    q_off = lax.broadcasted_iota(jnp.int32, s.shape, 0) & (N_META - 1)
    col = lax.broadcasted_iota(jnp.int32, s.shape, 1) - (128 - N_META)
    s = jnp.where((col >= 0) & (col <= q_off), s, MASK_VALUE)
    m_prev = gather(m_ref)
    m_new = jnp.maximum(m_prev, jnp.max(s, axis=-1, keepdims=True))
    alpha = jnp.exp2(m_prev - m_new)
    p = jnp.exp2(s - m_new)
    l_new = alpha * gather(l_ref) + jnp.sum(p, axis=-1, keepdims=True)
    acc_new = _lanes(alpha, KV_RANK) * gather(acc_ref) + _dot(p.astype(BF16), vtail_ref[...])
    for hd, r in enumerate(tails):
        part = slice(hd * N_META, (hd + 1) * N_META)
        l_ref[r, :] = l_new[part]
        acc_ref[r, :] = acc_new[part]

    for pr in range(MLA_HEADS // 2):
        halves = []
        for hd in (2 * pr, 2 * pr + 1):
            rows = slice(hd * blk, (hd + 1) * blk)
            halves.append((acc_ref[rows, :] / _lanes(l_ref[rows, :], KV_RANK)).astype(BF16))
        o_ref[:, pr * 2 * MLA_V:(pr + 1) * 2 * MLA_V] = _dot(jnp.concatenate(halves, axis=1), wuv_ref[pr]).astype(BF16)


def _prompt_attention(qlat, qrope, kt_seq, v_seq, kt_tok, kcat, wuv_pair):
    b, nh, t, r = qlat.shape
    nq, blk = kt_seq.shape[1], kt_seq.shape[3]
    assert nq * blk == t and v_seq.shape[1] == t
    assert blk % 128 == 0 and blk & (blk - 1) == 0 and N_META <= 128
    rows = nh * blk
    last = blk // 128
    return pl.pallas_call(
        _attn_body,
        grid=(b, nq),
        in_specs=[
            pl.BlockSpec((None, nh, blk, r), lambda bi, i: (bi, 0, i, 0)),
            pl.BlockSpec((None, blk, qrope.shape[2]), lambda bi, i: (bi, i, 0)),
            pl.BlockSpec((None,) + kt_seq.shape[1:], lambda bi, i: (bi, 0, 0, 0)),
            pl.BlockSpec((None,) + v_seq.shape[1:], lambda bi, i: (bi, 0, 0)),
            pl.BlockSpec((None, kt_tok.shape[1], 128), lambda bi, i: (bi, 0, (i + 1) * last - 1)),
            pl.BlockSpec((None, 128, r), lambda bi, i: (bi, (i + 1) * last - 1, 0)),
            _const_spec(wuv_pair.shape),
        ],
        out_specs=pl.BlockSpec((None, blk, nh * MLA_V), lambda bi, i: (bi, i, 0)),
        out_shape=jax.ShapeDtypeStruct((b, t, nh * MLA_V), BF16),
        scratch_shapes=[
            pltpu.VMEM((rows, 2 * r), BF16),
            pltpu.VMEM((rows, 2 * blk), F32),
            pltpu.VMEM((rows, 2 * blk), BF16),
            pltpu.VMEM((rows, 128), F32),
            pltpu.VMEM((rows, 128), F32),
            pltpu.VMEM((rows, 128), F32),
            pltpu.VMEM((rows, r), F32),
        ],
        compiler_params=_params("parallel", "arbitrary"),
        name="prompt_attention",
    )(qlat, qrope, kt_seq, v_seq, kt_tok, kcat, wuv_pair)


def _group_norm(o):
    mu = jnp.mean(o, axis=-1, keepdims=True)
    c = o - mu
    return c * lax.rsqrt(jnp.mean(c * c, axis=-1, keepdims=True) + NORM_EPS)


def _ret_body(rq_ref, rk_ref, rv_ref, kmeta_ref, vmeta_ref, on_ref, sout_ref, s_ref, decay_ref):
    c = pl.program_id(1)
    n_seq, ch = rq_ref.shape[:2]
    n_meta = kmeta_ref.shape[0]

    @pl.when(c == 0)
    def _():
        j = lax.broadcasted_iota(jnp.int32, (n_meta, 1), 0).astype(F32)
        diff = (lax.broadcasted_iota(jnp.int32, (ch, ch), 0) - lax.broadcasted_iota(jnp.int32, (ch, ch), 1)).astype(F32)
        for hd in range(RET_HEADS):
            kd = kmeta_ref[:, hd * RET_DK:(hd + 1) * RET_DK].astype(F32) * jnp.exp(RET_LOG_GAMMA[hd] * (n_meta - 1.0 - j))
            s_meta = _dot(kd.T.astype(BF16), vmeta_ref[:, hd * RET_DV:(hd + 1) * RET_DV])
            for sq in range(n_seq):
                s_ref[sq, hd] = s_meta
            decay_ref[hd] = jnp.where(diff >= 0, jnp.exp(RET_LOG_GAMMA[hd] * jnp.maximum(diff, 0.0)), 0.0)

    row = lax.broadcasted_iota(jnp.int32, (ch, 1), 0).astype(F32)
    for hd in range(RET_HEADS):
        lg = RET_LOG_GAMMA[hd]
        q_decay = jnp.exp(lg * (row + 1.0))
        k_decay = jnp.exp(lg * (ch - 1.0 - row))
        for sq in range(n_seq):
            q = rq_ref[sq, :, hd * RET_DK:(hd + 1) * RET_DK]
            k = rk_ref[sq, :, hd * RET_DK:(hd + 1) * RET_DK]
            v = rv_ref[sq, :, hd * RET_DV:(hd + 1) * RET_DV]
            scores = _dot_nt(q, k) * decay_ref[hd]
            s_old = s_ref[sq, hd]
            o = _dot(scores.astype(BF16), v) + _dot(q, s_old.astype(BF16)) * q_decay
            kd = k.astype(F32) * k_decay
            s_ref[sq, hd] = math.exp(lg * ch) * s_old + _dot(kd.T.astype(BF16), v)
            on_ref[sq, :, hd * RET_DV:(hd + 1) * RET_DV] = _group_norm(o).astype(BF16)

    @pl.when(c == pl.num_programs(1) - 1)
    def _():
        sout_ref[...] = s_ref[...]


def _prompt_retention(rq, rk, rv, kmeta, vmeta):
    b, t, hr = rq.shape
    hv = rv.shape[2]
    ch = min(t, RET_CHUNK)
    n_seq = math.gcd(b, RET_SEQS)
    assert t % ch == 0
    tok = lambda width: pl.BlockSpec((n_seq, ch, width), lambda bi, ci: (bi, ci, 0))
    return pl.pallas_call(
        _ret_body,
        grid=(b // n_seq, t // ch),
        in_specs=[tok(hr), tok(hr), tok(hv), _const_spec(kmeta.shape), _const_spec(vmeta.shape)],
        out_specs=[tok(hv), pl.BlockSpec((n_seq, RET_HEADS, RET_DK, RET_DV), lambda bi, ci: (bi, 0, 0, 0))],
        out_shape=[jax.ShapeDtypeStruct((b, t, hv), BF16),
                   jax.ShapeDtypeStruct((b, RET_HEADS, RET_DK, RET_DV), F32)],
        scratch_shapes=[pltpu.VMEM((n_seq, RET_HEADS, RET_DK, RET_DV), F32), pltpu.VMEM((RET_HEADS, ch, ch), F32)],
        compiler_params=_params("parallel", "arbitrary"),
        name="prompt_retention",
    )(rq, rk, rv, kmeta, vmeta)


def _dec_attn_body(pt_ref, qlat_ref, qrope_ref, kself_ref, wuv_ref, ckv_hbm, kr_hbm, o_ref,
                   kbuf, krbuf, sem_k, sem_r, m_ref, l_ref, acc_ref, *, pages, n_chunks):
    b = pl.program_id(0)
    n_req = pl.num_programs(0)
    ahead = DEC_SLOTS - 1
    ql = qlat_ref[...]
    qr = qrope_ref[...]

    def page_copies(page, slot, p):
        lat = pltpu.make_async_copy(ckv_hbm.at[page], kbuf.at[slot, p], sem_k.at[slot])
        rope = pltpu.make_async_copy(kr_hbm.at[page], krbuf.at[slot, p], sem_r.at[slot])
        return lat, rope

    def start_chunk(req, chunk, slot):
        def one(p, carry):
            for cp in page_copies(pt_ref[req, chunk * pages + p], slot, p):
                cp.start()
            return carry
        lax.fori_loop(0, pages, one, 0)

    def wait_chunk(slot):
        def one(p, carry):
            for cp in page_copies(0, slot, p):
                cp.wait()
            return carry
        lax.fori_loop(0, pages, one, 0)

    @pl.when(b == 0)
    def _():
        for g in range(ahead):
            start_chunk(0, g, g)

    ks = kself_ref[...].astype(F32)
    qf = jnp.concatenate([ql, qr], axis=1).astype(F32)
    m_ref[...] = jnp.sum(qf * ks, axis=-1, keepdims=True)
    l_ref[...] = jnp.ones(l_ref.shape, F32)
    acc_ref[...] = jnp.broadcast_to(ks[:, :KV_RANK], acc_ref.shape)

    sub = min(pages, DEC_SUB_PAGES)
    n_sub = pages // sub
    sub_keys = sub * PAGE_SIZE

    def ring_round(rnd, carry):
        for slot in range(DEC_SLOTS):
            chunk = rnd * DEC_SLOTS + slot
            nxt = chunk + ahead
            wrap = nxt >= n_chunks
            nxt_req = jnp.where(wrap, b + 1, b)

            @pl.when(nxt_req < n_req)
            def _():
                start_chunk(nxt_req, jnp.where(wrap, nxt - n_chunks, nxt), (slot + ahead) % DEC_SLOTS)

            wait_chunk(slot)
            k_lat, scores = [], []
            for q in range(n_sub):
                k_q = kbuf[slot, q * sub:(q + 1) * sub].reshape(sub_keys, KV_RANK).astype(BF16)
                kr_q = jnp.concatenate([krbuf[slot, q * sub + j] for j in range(sub)], axis=1).astype(BF16)
                k_lat.append(k_q)
                scores.append(_dot_nt(ql, k_q) + _dot(qr, kr_q))
            s = jnp.concatenate(scores, axis=1)
            m_prev = m_ref[...]
            m_new = jnp.maximum(m_prev, jnp.max(s, axis=-1, keepdims=True))
            alpha = jnp.exp2(m_prev - m_new)
            p = jnp.exp2(s - m_new)
            l_ref[...] = alpha * l_ref[...] + jnp.sum(p, axis=-1, keepdims=True)
            p = p.astype(BF16)
            pv = _dot(p[:, :sub_keys], k_lat[0])
            for q in range(1, n_sub):
                pv = pv + _dot(p[:, q * sub_keys:(q + 1) * sub_keys], k_lat[q])
            acc_ref[...] = alpha * acc_ref[...] + pv
            m_ref[...] = m_new
        return carry

    lax.fori_loop(0, n_chunks // DEC_SLOTS, ring_round, 0)

    o = (acc_ref[...] / l_ref[...]).astype(BF16)
    z = _dot(o, wuv_ref[...])
    head = lax.broadcasted_iota(jnp.int32, z.shape, 0)
    col_head = lax.broadcasted_iota(jnp.int32, z.shape, 1) // MLA_V
    o_ref[...] = jnp.sum(jnp.where(head == col_head, z, 0.0), axis=0, keepdims=True).astype(BF16)


def _decode_attention(page_table, qlat, qrope, kself, wuv_all, cache_ckv, cache_kr_t):
    bd, nh, r = qlat.shape
    n_pages = page_table.shape[1]
    pages = min(n_pages // DEC_SLOTS, DEC_PAGES_PER_STEP)
    n_chunks = n_pages // pages
    assert n_chunks * pages == n_pages and n_chunks % DEC_SLOTS == 0 and pages % min(pages, DEC_SUB_PAGES) == 0

    per_req = lambda shape: pl.BlockSpec((None,) + shape, lambda b, pt: (b, 0, 0))
    grid_spec = pltpu.PrefetchScalarGridSpec(
        num_scalar_prefetch=1,
        grid=(bd,),
        in_specs=[per_req((nh, r)), per_req((nh, MLA_ROPE)), per_req((1, r + MLA_ROPE)),
                  pl.BlockSpec(wuv_all.shape, lambda b, pt: (0, 0)),
                  pl.BlockSpec(memory_space=pl.ANY), pl.BlockSpec(memory_space=pl.ANY)],
        out_specs=per_req((1, nh * MLA_V)),
        scratch_shapes=[
            pltpu.VMEM((DEC_SLOTS, pages, PAGE_SIZE, r), F32),
            pltpu.VMEM((DEC_SLOTS, pages, MLA_ROPE, PAGE_SIZE), F32),
            pltpu.SemaphoreType.DMA((DEC_SLOTS,)),
            pltpu.SemaphoreType.DMA((DEC_SLOTS,)),
            pltpu.VMEM((nh, 1), F32), pltpu.VMEM((nh, 1), F32), pltpu.VMEM((nh, r), F32),
        ],
    )
    return pl.pallas_call(
        functools.partial(_dec_attn_body, pages=pages, n_chunks=n_chunks),
        grid_spec=grid_spec,
        out_shape=jax.ShapeDtypeStruct((bd, 1, nh * MLA_V), BF16),
        compiler_params=_params("arbitrary"),
        name="decode_attention",
    )(page_table, qlat, qrope, kself, wuv_all, cache_ckv, cache_kr_t)


def _dec_ret_body(rq_ref, rk_ref, rv_ref, s_ref, on_ref, snew_ref):
    eye = (lax.broadcasted_iota(jnp.int32, (RET_DK, RET_DK), 0)
           == lax.broadcasted_iota(jnp.int32, (RET_DK, RET_DK), 1))

    def column(x):
        return jnp.sum(jnp.where(eye, jnp.broadcast_to(x, (RET_DK, RET_DK)), 0.0), axis=1, keepdims=True)

    for req in range(rq_ref.shape[0]):
        for hd in range(RET_HEADS):
            gamma = math.exp(RET_LOG_GAMMA[hd])
            q = rq_ref[req, :, hd * RET_DK:(hd + 1) * RET_DK].astype(F32)
            k = rk_ref[req, :, hd * RET_DK:(hd + 1) * RET_DK].astype(F32)
            v = rv_ref[req, :, hd * RET_DV:(hd + 1) * RET_DV].astype(F32)
            s_old = s_ref[req, hd]
            qs = jnp.sum(column(q) * s_old, axis=0, keepdims=True)
            o = jnp.sum(q * k, axis=-1, keepdims=True) * v + qs * gamma
            snew_ref[req, hd] = gamma * s_old + column(k) * v
            on_ref[req, :, hd * RET_DV:(hd + 1) * RET_DV] = _group_norm(o).astype(BF16)


def _decode_retention(rq, rk, rv, state):
    bd = rq.shape[0]
    per = math.gcd(bd, DEC_RET_REQUESTS)
    row = lambda width: pl.BlockSpec((per, 1, width), lambda b: (b, 0, 0))
    st = pl.BlockSpec((per, RET_HEADS, RET_DK, RET_DV), lambda b: (b, 0, 0, 0))
    return pl.pallas_call(
        _dec_ret_body,
        grid=(bd // per,),
        in_specs=[row(rq.shape[2]), row(rk.shape[2]), row(rv.shape[2]), st],
        out_specs=[row(rv.shape[2]), st],
        out_shape=[jax.ShapeDtypeStruct((bd, 1, rv.shape[2]), BF16), jax.ShapeDtypeStruct(state.shape, F32)],
        compiler_params=_params("parallel"),
        name="decode_retention",
    )(rq, rk, rv, state)


def _mix_out_body(h_ref, a_ref, on_ref, srg_ref, sga_ref, sgb_ref, gn_ref, wa_ref, wr_ref, wo_ref, o_ref):
    a = _dot(a_ref[...], wa_ref[...].astype(BF16))
    gated = srg_ref[...].astype(F32) * (on_ref[...].astype(F32) * gn_ref[...])
    r = _dot(gated.astype(BF16), wr_ref[...].astype(BF16))
    m = sga_ref[...].astype(F32) * a + sgb_ref[...].astype(F32) * r
    o_ref[...] = h_ref[...] + _dot(m.astype(BF16), wo_ref[...].astype(BF16))


def _mix_out(h, apre, on, srg, sga, sgb, W):
    m, d = h.shape
    tm = min(m, MIX_OUT_TILE)
    assert m % tm == 0
    tok = lambda width: pl.BlockSpec((tm, width), lambda i: (i, 0))
    weights = [W["ret_gn"], W["w_mla_o"], W["w_ret_o"], W["w_out"]]
    return pl.pallas_call(
        _mix_out_body,
        grid=(m // tm,),
        in_specs=[tok(d), tok(apre.shape[1]), tok(on.shape[1]), tok(d), tok(d), tok(d)]
                 + [_const_spec(w.shape) for w in weights],
        out_specs=tok(d),
        out_shape=jax.ShapeDtypeStruct((m, d), F32),
        compiler_params=_params("parallel"),
        name="mixer_output",
    )(h, apre, on, srg, sga, sgb, *weights)


def _rope_tables(pos):
    pos = np.asarray(pos, np.float64)[:, None]

    def table(width, reps):
        half = width // 2
        inv_freq = ROPE_BASE ** (-np.arange(half, dtype=np.float64) / half)
        ang = pos * inv_freq[None, :]
        cos, sin = np.cos(ang), np.sin(ang)
        return np.stack([np.tile(np.concatenate([cos, cos], axis=1), (1, reps)),
                         np.tile(np.concatenate([-sin, sin], axis=1), (1, reps))]).astype(np.float32)

    return table(RET_DK, 1), table(MLA_ROPE, MLA_HEADS)


def _prep_weights(ffn1_norm, ffn1_gate, ffn1_up, ffn1_down, mix_norm, w_in, q_norm, kv_norm, w_uq, w_uk, w_uv,
                  w_mla_o, ret_gn, w_ret_o, w_out, ffn2_norm, ffn2_gate, ffn2_up, ffn2_down, final_norm):
    d = w_in.shape[0]
    hr, hv = RET_HEADS * RET_DK, RET_HEADS * RET_DV
    nh, hd = MLA_HEADS, MLA_NOPE + MLA_ROPE

    bounds = [0]
    for width in (Q_RANK, KV_RANK, MLA_ROPE, hr, hr, hv, hv, d, d):
        bounds.append(bounds[-1] + width)
    assert bounds[-1] == w_in.shape[1] and all(bd % 16 == 0 for bd in bounds)
    w_in_cols = [(bounds[i], bounds[i + 1]) for i in range(9)]
    w_in_packed = w_in.T

    w_uq3 = w_uq.reshape(Q_RANK, nh, hd)
    zeros = jnp.zeros((MLA_NOPE, KV_RANK), F32)
    w_uk_t = w_uk.transpose(1, 2, 0)
    w_uk_pair = jnp.stack([jnp.block([[w_uk_t[2 * p], zeros], [zeros, w_uk_t[2 * p + 1]]])
                           for p in range(nh // 2)]).astype(BF16)
    w_uv_t = w_uv.transpose(1, 0, 2)
    zv = jnp.zeros((KV_RANK, MLA_V), F32)
    w_uv_pair = jnp.stack([jnp.block([[w_uv_t[2 * p], zv], [zv, w_uv_t[2 * p + 1]]])
                           for p in range(nh // 2)]).astype(BF16)
    row = lambda v: v.reshape(1, -1)
    return dict(
        ffn1_norm=row(ffn1_norm), ffn1=(ffn1_gate, ffn1_up, ffn1_down),
        ffn2_norm=row(ffn2_norm), ffn2=(ffn2_gate, ffn2_up, ffn2_down), final_norm=row(final_norm),
        mix_norm=row(mix_norm), w_in=w_in_packed, w_in_cols=tuple(w_in_cols), q_norm=row(q_norm), kv_norm=row(kv_norm),
        w_uq_nope=w_uq3[:, :, :MLA_NOPE].reshape(Q_RANK, nh * MLA_NOPE).astype(BF16),
        w_uq_rope=w_uq3[:, :, MLA_NOPE:].reshape(Q_RANK, nh * MLA_ROPE).astype(BF16),
        w_uk_pair=w_uk_pair, w_uv_pair=w_uv_pair, w_uv_all=w_uv.reshape(KV_RANK, nh * MLA_V).astype(BF16),
        ret_gn=row(ret_gn), w_mla_o=w_mla_o, w_ret_o=w_ret_o, w_out=w_out,
    )


def kernel(x_prompt, x_sample, cache_ckv, cache_krope, state_ret, page_table, meta_tokens, ffn1_norm, ffn1_gate, ffn1_up, ffn1_down, mix_norm, w_in, q_norm, kv_norm, w_uq, w_uk, w_uv, w_mla_o, ret_gn, w_ret_o, w_out, ffn2_norm, ffn2_gate, ffn2_up, ffn2_down, final_norm):
    assert ffn1_gate.shape[0] == 1, "single-layer trunk"
    b, t, d = x_prompt.shape
    bd, dec_seq, _ = x_sample.shape
    assert dec_seq == 1
    n_meta = meta_tokens.shape[0]
    assert n_meta == N_META
    past_len = page_table.shape[1] * PAGE_SIZE
    W = _prep_weights(ffn1_norm[0], ffn1_gate[0], ffn1_up[0], ffn1_down[0], mix_norm[0], w_in[0], q_norm[0],
                      kv_norm[0], w_uq[0], w_uk[0], w_uv[0], w_mla_o[0], ret_gn[0], w_ret_o[0], w_out[0],
                      ffn2_norm[0], ffn2_gate[0], ffn2_up[0], ffn2_down[0], final_norm)

    rope_r, rope_q = _rope_tables(n_meta + np.arange(t))
    rope_r_s, rope_q_s = _rope_tables(np.concatenate([np.arange(n_meta), np.full((bd,), past_len)]))
    x_small = jnp.concatenate([meta_tokens.astype(F32), x_sample.reshape(bd, d)], axis=0)

    h1, h1s = _ffn(x_prompt.reshape(b * t, d), x_small, W["ffn1_norm"], *W["ffn1"])
    S = _mix_in(h1s.reshape(1, n_meta + bd, d), W, rope_r_s, rope_q_s)
    meta_keys = (S["kt"][0, :, :n_meta].astype(F32), S["kcat"][0, :n_meta, :KV_RANK].astype(F32))
    P = _mix_in(h1.reshape(b, t, d), W, rope_r, rope_q, prev_keys=meta_keys)

    apre = _prompt_attention(P["qlat"], P["qrope"], P["kts"], P["vs"], P["kt"], P["kcat"], W["w_uv_pair"])
    on, s_prompt = _prompt_retention(P["rq"], P["rk"], P["rv"], S["rk"][0, :n_meta], S["rv"][0, :n_meta])

    nh = MLA_HEADS
    qlat_s = S["qlat"][0, :, n_meta:].transpose(1, 0, 2)
    qrope_s = S["qrope"][0, n_meta:].reshape(bd, nh, MLA_ROPE)
    kself = S["kcat"][0, n_meta:, :KV_RANK + MLA_ROPE].reshape(bd, 1, KV_RANK + MLA_ROPE)
    apre_s = _decode_attention(page_table, qlat_s, qrope_s, kself, W["w_uv_all"], cache_ckv[0],
                               cache_krope[0].transpose(0, 2, 1))
    dec = lambda name: S[name][0, n_meta:].reshape(bd, 1, -1)
    on_s, s_sample = _decode_retention(dec("rq"), dec("rk"), dec("rv"), state_ret[0])

    flat = lambda a: a.reshape(b * t, -1)
    h2 = _mix_out(h1, flat(apre), flat(on), flat(P["srg"]), flat(P["sga"]), flat(P["sgb"]), W)
    tail = lambda name: S[name][0, n_meta:]
    h2s = _mix_out(h1s[n_meta:], apre_s.reshape(bd, -1), on_s.reshape(bd, -1), tail("srg"), tail("sga"), tail("sgb"), W)
    y_prompt, y_sample = _ffn(h2, h2s, W["ffn2_norm"], *W["ffn2"], final_norm=W["final_norm"])

    with_meta = lambda small, main: jnp.concatenate(
        [jnp.broadcast_to(small[:, :n_meta], (b, n_meta, small.shape[2])), main], axis=1)[None]
    return (
        y_prompt.reshape(b, t, d),
        y_sample.reshape(bd, 1, d),
        with_meta(S["ckv"], P["ckv"]),
        with_meta(S["kr"], P["kr"]),
        s_prompt[None],
        S["ckv"][0, n_meta:].reshape(1, bd, 1, KV_RANK),
        S["kr"][0, n_meta:].reshape(1, bd, 1, MLA_ROPE),
        s_sample[None],
    )
```

```python
import functools
import math

import jax
import jax.numpy as jnp
import numpy as np
from jax import lax
from jax.experimental import pallas as pl
from jax.experimental.pallas import tpu as pltpu

N_META = 16
PAGE_SIZE = 128
MLA_HEADS = 8
MLA_NOPE = 64
MLA_ROPE = 32
MLA_V = 64
Q_RANK = 384
KV_RANK = 256
MLA_SCALE = (MLA_NOPE + MLA_ROPE) ** -0.5
Q_SCALE = MLA_SCALE * math.log2(math.e)
RET_HEADS = 4
RET_DK = 128
RET_DV = 256
ROPE_BASE = 10000.0
NORM_EPS = 1e-6
RET_LOG_GAMMA = tuple(math.log1p(-(2.0 ** (-5.0 - h))) for h in range(RET_HEADS))

VMEM_LIMIT_BYTES = 56 * 1024 * 1024
FFN_CHUNK = 256
TOKEN_TILE = 512
MIX_OUT_TILE = 1024
ATTN_BLOCK = 256
ATTN_GROUP_HEADS = 1
ATTN_ROW_CHUNK = 32
RET_CHUNK = 256
RET_SEQS = 8
DEC_PAGES_PER_STEP = 64
DEC_SUB_PAGES = 16
DEC_SLOTS = 2
DEC_RET_REQUESTS = 4
MASK_VALUE = -1e30

F32 = jnp.float32
BF16 = jnp.bfloat16


def _params(*sem):
    return pltpu.CompilerParams(dimension_semantics=sem, vmem_limit_bytes=VMEM_LIMIT_BYTES)


def _const_spec(shape):
    n = len(shape)
    return pl.BlockSpec(shape, lambda *_: (0,) * n, pipeline_mode=pl.Buffered(1))


def _rms(x, g):
    return x * lax.rsqrt(jnp.mean(x * x, axis=-1, keepdims=True) + NORM_EPS) * g


def _sigmoid(x):
    return 1.0 / (1.0 + jnp.exp(-x))


def _dot(a, b):
    return jnp.dot(a, b, preferred_element_type=F32)


def _dot_nt(a, b):
    return lax.dot_general(a, b, (((1,), (1,)), ((), ())), preferred_element_type=F32)


def _ffn_body(x_ref, xs_ref, g_ref, wg_ref, wu_ref, wd_ref, *rest, final):
    o_ref, os_ref = rest[-2:]

    def apply(with_small):
        rows = x_ref.shape[0]
        x = jnp.concatenate([x_ref[...], xs_ref[...]], axis=0) if with_small else x_ref[...]
        u = _rms(x, g_ref[...]).astype(BF16)
        acc = None
        for c in range(0, wg_ref.shape[1], FFN_CHUNK):
            cols = slice(c, c + FFN_CHUNK)
            gate = _dot(u, wg_ref[:, cols].astype(BF16))
            up = _dot(u, wu_ref[:, cols].astype(BF16))
            act = (gate * _sigmoid(gate) * up).astype(BF16)
            part = _dot(act, wd_ref[cols, :].astype(BF16))
            acc = part if acc is None else acc + part
        y = x + 0.5 * acc
        if final:
            y = _rms(y, rest[0][...])
        o_ref[...] = y[:rows]
        if with_small:
            os_ref[...] = y[rows:]

    last = pl.num_programs(0) - 1
    pl.when(pl.program_id(0) < last)(functools.partial(apply, False))
    pl.when(pl.program_id(0) == last)(functools.partial(apply, True))


def _ffn(x, x_small, norm, wg, wu, wd, final_norm=None):
    m, d = x.shape
    tm = min(m, TOKEN_TILE)
    assert m % tm == 0 and wg.shape[1] % FFN_CHUNK == 0
    n = m // tm
    final = final_norm is not None
    main = pl.BlockSpec((tm, d), lambda i: (i, 0))
    in_specs = [main, _const_spec(x_small.shape), _const_spec((1, d)),
                _const_spec(wg.shape), _const_spec(wu.shape), _const_spec(wd.shape)]
    args = [x, x_small, norm, wg, wu, wd]
    if final:
        in_specs.append(_const_spec((1, d)))
        args.append(final_norm)
    return pl.pallas_call(
        functools.partial(_ffn_body, final=final),
        grid=(n,),
        in_specs=in_specs,
        out_specs=[main, pl.BlockSpec(x_small.shape, lambda i: (0, 0))],
        out_shape=[jax.ShapeDtypeStruct((m, d), F32), jax.ShapeDtypeStruct(x_small.shape, F32)],
        compiler_params=_params("arbitrary"),
        name="ffn_final" if final else "ffn",
    )(*args)


def _rope32(x, cos, sin):
    lane = lax.broadcasted_iota(jnp.int32, x.shape, 1)
    width = x.shape[1]
    partner = jnp.where((lane & 31) < 16, pltpu.roll(x, width - 16, axis=1), pltpu.roll(x, 16, axis=1))
    return x * cos + partner * sin


def _mix_in_body(h_ref, g_ref, win_ref, qn_ref, kvn_ref, wuqn_ref, wuqr_ref, wuk_ref, rope_r_ref, rope_q_ref,
                 *rest, shifted, cols):
    def project(x, piece, reps=1):
        lo, hi = cols[piece]
        w = win_ref[lo:hi, :].astype(BF16)
        return _dot_nt(x, w if reps == 1 else jnp.concatenate([w] * reps, axis=0))

    if shifted:
        (kprev_ref, vprev_ref, qlat_ref, qrope_ref, ckv_ref, kr_ref, kcat_ref, kt_ref, rq_ref, rk_ref, rv_ref,
         srg_ref, sga_ref, sgb_ref, kts_ref, vs_ref, kcarry_ref, vcarry_ref) = rest
    else:
        (qlat_ref, qrope_ref, ckv_ref, kr_ref, kcat_ref, kt_ref, rq_ref, rk_ref, rv_ref,
         srg_ref, sga_ref, sgb_ref) = rest
    if shifted:
        @pl.when(pl.program_id(1) == 0)
        def _():
            kcarry_ref[...] = kprev_ref[...]
            vcarry_ref[...] = vprev_ref[...]

    u = _rms(h_ref[...], g_ref[...]).astype(BF16)
    cos_q, sin_q = rope_q_ref[0], rope_q_ref[1]
    cos_r, sin_r = rope_r_ref[0], rope_r_ref[1]

    cq = _rms(project(u, 0), qn_ref[...]).astype(BF16)
    q_nope = (_dot(cq, wuqn_ref[...]) * Q_SCALE).astype(BF16)
    for p in range(MLA_HEADS // 2):
        pair = _dot(q_nope[:, p * 128:(p + 1) * 128], wuk_ref[p])
        qlat_ref[2 * p] = pair[:, :KV_RANK].astype(BF16)
        qlat_ref[2 * p + 1] = pair[:, KV_RANK:].astype(BF16)
    q_rope = _rope32(_dot(cq, wuqr_ref[...]), cos_q, sin_q) * Q_SCALE
    qrope_ref[...] = q_rope.astype(BF16)

    ckv = _rms(project(u, 1), kvn_ref[...])
    ckv_ref[...] = ckv
    kcat_ref[:, :KV_RANK] = ckv.astype(BF16)
    kr = _rope32(project(u, 2, reps=MLA_HEADS), cos_q, sin_q)
    kr_ref[...] = kr[:, :MLA_ROPE]
    kcat_ref[:, KV_RANK:] = kr.astype(BF16)
    k_t = jnp.concatenate([ckv, kr], axis=1).T
    kt_ref[...] = k_t.astype(BF16)
    if shifted:
        keep = k_t.shape[1] - N_META
        k_sh = jnp.concatenate([kcarry_ref[...], k_t[:, :keep]], axis=1).astype(BF16)
        blk = kts_ref.shape[2]
        for j in range(kts_ref.shape[0]):
            kts_ref[j] = k_sh[:, j * blk:(j + 1) * blk]
        vs_ref[...] = jnp.concatenate([vcarry_ref[...], ckv[:keep]], axis=0).astype(BF16)
        kcarry_ref[...] = k_t[:, keep:]
        vcarry_ref[...] = ckv[keep:]

    rq = project(u, 3)
    rk = project(u, 4)
    for hd in range(RET_HEADS):
        sl = slice(hd * RET_DK, (hd + 1) * RET_DK)
        xq, xk = rq[:, sl], rk[:, sl]
        rq_ref[:, sl] = (xq * cos_r + pltpu.roll(xq, RET_DK // 2, axis=1) * sin_r).astype(BF16)
        rk_ref[:, sl] = ((xk * cos_r + pltpu.roll(xk, RET_DK // 2, axis=1) * sin_r) * (RET_DK ** -0.5)).astype(BF16)
    rv_ref[...] = project(u, 5).astype(BF16)

    rg = project(u, 6)
    srg_ref[...] = (rg * _sigmoid(rg)).astype(BF16)
    sga_ref[...] = _sigmoid(project(u, 7)).astype(BF16)
    sgb_ref[...] = _sigmoid(project(u, 8)).astype(BF16)


def _mix_in(h, W, rope_r, rope_q, prev_keys=None):
    b, t, d = h.shape
    tm = min(t, TOKEN_TILE)
    assert t % tm == 0
    hr = RET_HEADS * RET_DK
    hv = RET_HEADS * RET_DV
    qw = MLA_HEADS * MLA_ROPE
    shifted = prev_keys is not None
    weights = [W["mix_norm"], W["w_in"], W["q_norm"], W["kv_norm"], W["w_uq_nope"], W["w_uq_rope"], W["w_uk_pair"]]
    tok = lambda width: pl.BlockSpec((None, tm, width), lambda bi, ti: (bi, ti, 0))
    in_specs = ([tok(d)] + [_const_spec(w.shape) for w in weights]
                + [pl.BlockSpec((2, tm, RET_DK), lambda bi, ti: (0, ti, 0)),
                   pl.BlockSpec((2, tm, qw), lambda bi, ti: (0, ti, 0))])
    out_specs = [
        pl.BlockSpec((None, MLA_HEADS, tm, KV_RANK), lambda bi, ti: (bi, 0, ti, 0)),
        tok(qw), tok(KV_RANK), tok(MLA_ROPE), tok(2 * KV_RANK),
        pl.BlockSpec((None, 2 * KV_RANK, tm), lambda bi, ti: (bi, 0, ti)),
        tok(hr), tok(hr), tok(hv), tok(hv), tok(d), tok(d),
    ]
    sds = jax.ShapeDtypeStruct
    out_shape = [
        sds((b, MLA_HEADS, t, KV_RANK), BF16), sds((b, t, qw), BF16), sds((b, t, KV_RANK), F32),
        sds((b, t, MLA_ROPE), F32), sds((b, t, 2 * KV_RANK), BF16), sds((b, 2 * KV_RANK, t), BF16),
        sds((b, t, hr), BF16), sds((b, t, hr), BF16),
        sds((b, t, hv), BF16), sds((b, t, hv), BF16), sds((b, t, d), BF16), sds((b, t, d), BF16),
    ]
    names = ["qlat", "qrope", "ckv", "kr", "kcat", "kt", "rq", "rk", "rv", "srg", "sga", "sgb"]
    extra_in, scratch = [], []
    if shifted:
        blk = ATTN_BLOCK
        assert tm % blk == 0 and N_META % 8 == 0
        extra_in = list(prev_keys)
        in_specs += [_const_spec(a.shape) for a in extra_in]
        out_specs += [pl.BlockSpec((None, tm // blk, 2 * KV_RANK, blk), lambda bi, ti: (bi, ti, 0, 0)), tok(KV_RANK)]
        out_shape += [sds((b, t // blk, 2 * KV_RANK, blk), BF16), sds((b, t, KV_RANK), BF16)]
        names += ["kts", "vs"]
        scratch = [pltpu.VMEM((2 * KV_RANK, N_META), F32), pltpu.VMEM((N_META, KV_RANK), F32)]
    outs = pl.pallas_call(
        functools.partial(_mix_in_body, shifted=shifted, cols=W["w_in_cols"]),
        grid=(b, t // tm),
        in_specs=in_specs,
        out_specs=out_specs,
        out_shape=out_shape,
        scratch_shapes=scratch,
        compiler_params=_params("parallel", "arbitrary"),
        name="mixer_inputs",
    )(h, *weights, rope_r, rope_q, *extra_in)
    return dict(zip(names, outs))


def _lanes(x, width):
    return x if width == x.shape[1] else jnp.concatenate([x] * (width // x.shape[1]), axis=1)


def _attn_body(qlat_ref, qrope_ref, kt_ref, v_ref, ktail_ref, vtail_ref, wuv_ref, o_ref,
               qcat_ref, s_ref, p_ref, alpha_ref, m_ref, l_ref, acc_ref):
    blk = qrope_ref.shape[0]
    grp = ATTN_GROUP_HEADS * blk
    n_grp = MLA_HEADS // ATTN_GROUP_HEADS

    def blocks(j, n_blk, first):
        width = n_blk * blk
        v = v_ref[pl.ds(pl.multiple_of(j * blk, blk), width), :]
        if first:
            tok = lax.broadcasted_iota(jnp.int32, (ATTN_ROW_CHUNK, width), 0)
            key = lax.broadcasted_iota(jnp.int32, (ATTN_ROW_CHUNK, width), 1)
        for g in range(n_grp):
            rows = slice(g * grp, (g + 1) * grp)
            for n in range(n_blk):
                s_ref[rows, n * blk:(n + 1) * blk] = _dot(qcat_ref[rows, :], kt_ref[j + n])
        for g in range(n_grp):
            for c in range(g * grp, (g + 1) * grp, ATTN_ROW_CHUNK):
                r = slice(c, c + ATTN_ROW_CHUNK)
                s = s_ref[r, :width]
                if first:
                    s = jnp.where(key <= tok + (c % blk + N_META), s, MASK_VALUE)
                    m_new = jnp.broadcast_to(jnp.max(s, axis=-1, keepdims=True), (ATTN_ROW_CHUNK, 128))
                    p = jnp.exp2(s - _lanes(m_new, width))
                    l_ref[r, :] = jnp.broadcast_to(jnp.sum(p, axis=-1, keepdims=True), (ATTN_ROW_CHUNK, 128))
                else:
                    m_prev = m_ref[r, :]
                    m_new = jnp.maximum(m_prev, jnp.max(s, axis=-1, keepdims=True))
                    alpha = jnp.exp2(m_prev - m_new)
                    p = jnp.exp2(s - _lanes(m_new, width))
                    l_ref[r, :] = alpha * l_ref[r, :] + jnp.sum(p, axis=-1, keepdims=True)
                    alpha_ref[r, :] = alpha
                m_ref[r, :] = m_new
                p_ref[r, :width] = p.astype(BF16)
            rows = slice(g * grp, (g + 1) * grp)
            pv = _dot(p_ref[rows, :width], v)
            acc_ref[rows, :] = pv if first else _lanes(alpha_ref[rows, :], KV_RANK) * acc_ref[rows, :] + pv

    i = pl.program_id(1)
    lane_head = lax.broadcasted_iota(jnp.int32, (blk, MLA_HEADS * MLA_ROPE), 1) // MLA_ROPE
    qr = qrope_ref[...]
    for hd in range(MLA_HEADS):
        qcat_ref[hd * blk:(hd + 1) * blk, :KV_RANK] = qlat_ref[hd]
        qcat_ref[hd * blk:(hd + 1) * blk, KV_RANK:] = jnp.where(lane_head == hd, qr, jnp.zeros_like(qr))

    blocks(i, 1, True)

    def two_full_blocks(jj, carry):
        blocks(2 * jj, 2, False)
        return carry

    lax.fori_loop(0, i // 2, two_full_blocks, 0)

    @pl.when(i % 2 == 1)
    def _():
        blocks(i - 1, 1, False)

    tails = [slice((hd + 1) * blk - N_META, (hd + 1) * blk) for hd in range(MLA_HEADS)]
    gather = lambda ref: jnp.concatenate([ref[r, :] for r in tails], axis=0)
    s = _dot(gather(qcat_ref), ktail_ref[...])
    q_off = lax.broadcasted_iota(jnp.int32, s.shape, 0) & (N_META - 1)
    col = lax.broadcasted_iota(jnp.int32, s.shape, 1) - (128 - N_META)
    s = jnp.where((col >= 0) & (col <= q_off), s, MASK_VALUE)
    m_prev = gather(m_ref)
    m_new = jnp.maximum(m_prev, jnp.max(s, axis=-1, keepdims=True))
    alpha = jnp.exp2(m_prev - m_new)
    p = jnp.exp2(s - m_new)
    l_new = alpha * gather(l_ref) + jnp.sum(p, axis=-1, keepdims=True)
    acc_new = _lanes(alpha, KV_RANK) * gather(acc_ref) + _dot(p.astype(BF16), vtail_ref[...])
    for hd, r in enumerate(tails):
        part = slice(hd * N_META, (hd + 1) * N_META)
        l_ref[r, :] = l_new[part]
        acc_ref[r, :] = acc_new[part]

    for pr in range(MLA_HEADS // 2):
        halves = []
        for hd in (2 * pr, 2 * pr + 1):
            rows = slice(hd * blk, (hd + 1) * blk)
            halves.append((acc_ref[rows, :] / _lanes(l_ref[rows, :], KV_RANK)).astype(BF16))
        o_ref[:, pr * 2 * MLA_V:(pr + 1) * 2 * MLA_V] = _dot(jnp.concatenate(halves, axis=1), wuv_ref[pr]).astype(BF16)


def _prompt_attention(qlat, qrope, kt_seq, v_seq, kt_tok, kcat, wuv_pair):
    b, nh, t, r = qlat.shape
    nq, blk = kt_seq.shape[1], kt_seq.shape[3]
    assert nq * blk == t and v_seq.shape[1] == t
    assert blk % 128 == 0 and blk & (blk - 1) == 0 and N_META <= 128
    rows = nh * blk
    last = blk // 128
    return pl.pallas_call(
        _attn_body,
        grid=(b, nq),
        in_specs=[
            pl.BlockSpec((None, nh, blk, r), lambda bi, i: (bi, 0, i, 0)),
            pl.BlockSpec((None, blk, qrope.shape[2]), lambda bi, i: (bi, i, 0)),
            pl.BlockSpec((None,) + kt_seq.shape[1:], lambda bi, i: (bi, 0, 0, 0)),
            pl.BlockSpec((None,) + v_seq.shape[1:], lambda bi, i: (bi, 0, 0)),
            pl.BlockSpec((None, kt_tok.shape[1], 128), lambda bi, i: (bi, 0, (i + 1) * last - 1)),
            pl.BlockSpec((None, 128, r), lambda bi, i: (bi, (i + 1) * last - 1, 0)),
            _const_spec(wuv_pair.shape),
        ],
        out_specs=pl.BlockSpec((None, blk, nh * MLA_V), lambda bi, i: (bi, i, 0)),
        out_shape=jax.ShapeDtypeStruct((b, t, nh * MLA_V), BF16),
        scratch_shapes=[
            pltpu.VMEM((rows, 2 * r), BF16),
            pltpu.VMEM((rows, 2 * blk), F32),
            pltpu.VMEM((rows, 2 * blk), BF16),
            pltpu.VMEM((rows, 128), F32),
            pltpu.VMEM((rows, 128), F32),
            pltpu.VMEM((rows, 128), F32),
            pltpu.VMEM((rows, r), F32),
        ],
        compiler_params=_params("parallel", "arbitrary"),
        name="prompt_attention",
    )(qlat, qrope, kt_seq, v_seq, kt_tok, kcat, wuv_pair)


def _group_norm(o):
    mu = jnp.mean(o, axis=-1, keepdims=True)
    c = o - mu
    return c * lax.rsqrt(jnp.mean(c * c, axis=-1, keepdims=True) + NORM_EPS)


def _ret_body(rq_ref, rk_ref, rv_ref, kmeta_ref, vmeta_ref, on_ref, sout_ref, s_ref, decay_ref):
    c = pl.program_id(1)
    n_seq, ch = rq_ref.shape[:2]
    n_meta = kmeta_ref.shape[0]

    @pl.when(c == 0)
    def _():
        j = lax.broadcasted_iota(jnp.int32, (n_meta, 1), 0).astype(F32)
        diff = (lax.broadcasted_iota(jnp.int32, (ch, ch), 0) - lax.broadcasted_iota(jnp.int32, (ch, ch), 1)).astype(F32)
        for hd in range(RET_HEADS):
            kd = kmeta_ref[:, hd * RET_DK:(hd + 1) * RET_DK].astype(F32) * jnp.exp(RET_LOG_GAMMA[hd] * (n_meta - 1.0 - j))
            s_meta = _dot(kd.T.astype(BF16), vmeta_ref[:, hd * RET_DV:(hd + 1) * RET_DV])
            for sq in range(n_seq):
                s_ref[sq, hd] = s_meta
            decay_ref[hd] = jnp.where(diff >= 0, jnp.exp(RET_LOG_GAMMA[hd] * jnp.maximum(diff, 0.0)), 0.0)

    row = lax.broadcasted_iota(jnp.int32, (ch, 1), 0).astype(F32)
    for hd in range(RET_HEADS):
        lg = RET_LOG_GAMMA[hd]
        q_decay = jnp.exp(lg * (row + 1.0))
        k_decay = jnp.exp(lg * (ch - 1.0 - row))
        for sq in range(n_seq):
            q = rq_ref[sq, :, hd * RET_DK:(hd + 1) * RET_DK]
            k = rk_ref[sq, :, hd * RET_DK:(hd + 1) * RET_DK]
            v = rv_ref[sq, :, hd * RET_DV:(hd + 1) * RET_DV]
            scores = _dot_nt(q, k) * decay_ref[hd]
            s_old = s_ref[sq, hd]
            o = _dot(scores.astype(BF16), v) + _dot(q, s_old.astype(BF16)) * q_decay
            kd = k.astype(F32) * k_decay
            s_ref[sq, hd] = math.exp(lg * ch) * s_old + _dot(kd.T.astype(BF16), v)
            on_ref[sq, :, hd * RET_DV:(hd + 1) * RET_DV] = _group_norm(o).astype(BF16)

    @pl.when(c == pl.num_programs(1) - 1)
    def _():
        sout_ref[...] = s_ref[...]


def _prompt_retention(rq, rk, rv, kmeta, vmeta):
    b, t, hr = rq.shape
    hv = rv.shape[2]
    ch = min(t, RET_CHUNK)
    n_seq = math.gcd(b, RET_SEQS)
    assert t % ch == 0
    tok = lambda width: pl.BlockSpec((n_seq, ch, width), lambda bi, ci: (bi, ci, 0))
    return pl.pallas_call(
        _ret_body,
        grid=(b // n_seq, t // ch),
        in_specs=[tok(hr), tok(hr), tok(hv), _const_spec(kmeta.shape), _const_spec(vmeta.shape)],
        out_specs=[tok(hv), pl.BlockSpec((n_seq, RET_HEADS, RET_DK, RET_DV), lambda bi, ci: (bi, 0, 0, 0))],
        out_shape=[jax.ShapeDtypeStruct((b, t, hv), BF16),
                   jax.ShapeDtypeStruct((b, RET_HEADS, RET_DK, RET_DV), F32)],
        scratch_shapes=[pltpu.VMEM((n_seq, RET_HEADS, RET_DK, RET_DV), F32), pltpu.VMEM((RET_HEADS, ch, ch), F32)],
        compiler_params=_params("parallel", "arbitrary"),
        name="prompt_retention",
    )(rq, rk, rv, kmeta, vmeta)


def _dec_attn_body(pt_ref, qlat_ref, qrope_ref, kself_ref, wuv_ref, ckv_hbm, kr_hbm, o_ref,
                   kbuf, krbuf, sem_k, sem_r, m_ref, l_ref, acc_ref, *, pages, n_chunks):
    b = pl.program_id(0)
    n_req = pl.num_programs(0)
    ahead = DEC_SLOTS - 1
    ql = qlat_ref[...]
    qr = qrope_ref[...]

    def page_copies(page, slot, p):
        lat = pltpu.make_async_copy(ckv_hbm.at[page], kbuf.at[slot, p], sem_k.at[slot])
        rope = pltpu.make_async_copy(kr_hbm.at[page], krbuf.at[slot, p], sem_r.at[slot])
        return lat, rope

    def start_chunk(req, chunk, slot):
        def one(p, carry):
            for cp in page_copies(pt_ref[req, chunk * pages + p], slot, p):
                cp.start()
            return carry
        lax.fori_loop(0, pages, one, 0)

    def wait_chunk(slot):
        def one(p, carry):
            for cp in page_copies(0, slot, p):
                cp.wait()
            return carry
        lax.fori_loop(0, pages, one, 0)

    @pl.when(b == 0)
    def _():
        for g in range(ahead):
            start_chunk(0, g, g)

    ks = kself_ref[...].astype(F32)
    qf = jnp.concatenate([ql, qr], axis=1).astype(F32)
    m_ref[...] = jnp.sum(qf * ks, axis=-1, keepdims=True)
    l_ref[...] = jnp.ones(l_ref.shape, F32)
    acc_ref[...] = jnp.broadcast_to(ks[:, :KV_RANK], acc_ref.shape)

    sub = min(pages, DEC_SUB_PAGES)
    n_sub = pages // sub
    sub_keys = sub * PAGE_SIZE

    def ring_round(rnd, carry):
        for slot in range(DEC_SLOTS):
            chunk = rnd * DEC_SLOTS + slot
            nxt = chunk + ahead
            wrap = nxt >= n_chunks
            nxt_req = jnp.where(wrap, b + 1, b)

            @pl.when(nxt_req < n_req)
            def _():
                start_chunk(nxt_req, jnp.where(wrap, nxt - n_chunks, nxt), (slot + ahead) % DEC_SLOTS)

            wait_chunk(slot)
            k_lat, scores = [], []
            for q in range(n_sub):
                k_q = kbuf[slot, q * sub:(q + 1) * sub].reshape(sub_keys, KV_RANK).astype(BF16)
                kr_q = jnp.concatenate([krbuf[slot, q * sub + j] for j in range(sub)], axis=1).astype(BF16)
                k_lat.append(k_q)
                scores.append(_dot_nt(ql, k_q) + _dot(qr, kr_q))
            s = jnp.concatenate(scores, axis=1)
            m_prev = m_ref[...]
            m_new = jnp.maximum(m_prev, jnp.max(s, axis=-1, keepdims=True))
            alpha = jnp.exp2(m_prev - m_new)
            p = jnp.exp2(s - m_new)
            l_ref[...] = alpha * l_ref[...] + jnp.sum(p, axis=-1, keepdims=True)
            p = p.astype(BF16)
            pv = _dot(p[:, :sub_keys], k_lat[0])
            for q in range(1, n_sub):
                pv = pv + _dot(p[:, q * sub_keys:(q + 1) * sub_keys], k_lat[q])
            acc_ref[...] = alpha * acc_ref[...] + pv
            m_ref[...] = m_new
        return carry

    lax.fori_loop(0, n_chunks // DEC_SLOTS, ring_round, 0)

    o = (acc_ref[...] / l_ref[...]).astype(BF16)
    z = _dot(o, wuv_ref[...])
    head = lax.broadcasted_iota(jnp.int32, z.shape, 0)
    col_head = lax.broadcasted_iota(jnp.int32, z.shape, 1) // MLA_V
    o_ref[...] = jnp.sum(jnp.where(head == col_head, z, 0.0), axis=0, keepdims=True).astype(BF16)


def _decode_attention(page_table, qlat, qrope, kself, wuv_all, cache_ckv, cache_kr_t):
    bd, nh, r = qlat.shape
    n_pages = page_table.shape[1]
    pages = min(n_pages // DEC_SLOTS, DEC_PAGES_PER_STEP)
    n_chunks = n_pages // pages
    assert n_chunks * pages == n_pages and n_chunks % DEC_SLOTS == 0 and pages % min(pages, DEC_SUB_PAGES) == 0

    per_req = lambda shape: pl.BlockSpec((None,) + shape, lambda b, pt: (b, 0, 0))
    grid_spec = pltpu.PrefetchScalarGridSpec(
        num_scalar_prefetch=1,
        grid=(bd,),
        in_specs=[per_req((nh, r)), per_req((nh, MLA_ROPE)), per_req((1, r + MLA_ROPE)),
                  pl.BlockSpec(wuv_all.shape, lambda b, pt: (0, 0)),
                  pl.BlockSpec(memory_space=pl.ANY), pl.BlockSpec(memory_space=pl.ANY)],
        out_specs=per_req((1, nh * MLA_V)),
        scratch_shapes=[
            pltpu.VMEM((DEC_SLOTS, pages, PAGE_SIZE, r), F32),
            pltpu.VMEM((DEC_SLOTS, pages, MLA_ROPE, PAGE_SIZE), F32),
            pltpu.SemaphoreType.DMA((DEC_SLOTS,)),
            pltpu.SemaphoreType.DMA((DEC_SLOTS,)),
            pltpu.VMEM((nh, 1), F32), pltpu.VMEM((nh, 1), F32), pltpu.VMEM((nh, r), F32),
        ],
    )
    return pl.pallas_call(
        functools.partial(_dec_attn_body, pages=pages, n_chunks=n_chunks),
        grid_spec=grid_spec,
        out_shape=jax.ShapeDtypeStruct((bd, 1, nh * MLA_V), BF16),
        compiler_params=_params("arbitrary"),
        name="decode_attention",
    )(page_table, qlat, qrope, kself, wuv_all, cache_ckv, cache_kr_t)


def _dec_ret_body(rq_ref, rk_ref, rv_ref, s_ref, on_ref, snew_ref):
    eye = (lax.broadcasted_iota(jnp.int32, (RET_DK, RET_DK), 0)
           == lax.broadcasted_iota(jnp.int32, (RET_DK, RET_DK), 1))

    def column(x):
        return jnp.sum(jnp.where(eye, jnp.broadcast_to(x, (RET_DK, RET_DK)), 0.0), axis=1, keepdims=True)

    for req in range(rq_ref.shape[0]):
        for hd in range(RET_HEADS):
            gamma = math.exp(RET_LOG_GAMMA[hd])
            q = rq_ref[req, :, hd * RET_DK:(hd + 1) * RET_DK].astype(F32)
            k = rk_ref[req, :, hd * RET_DK:(hd + 1) * RET_DK].astype(F32)
            v = rv_ref[req, :, hd * RET_DV:(hd + 1) * RET_DV].astype(F32)
            s_old = s_ref[req, hd]
            qs = jnp.sum(column(q) * s_old, axis=0, keepdims=True)
            o = jnp.sum(q * k, axis=-1, keepdims=True) * v + qs * gamma
            snew_ref[req, hd] = gamma * s_old + column(k) * v
            on_ref[req, :, hd * RET_DV:(hd + 1) * RET_DV] = _group_norm(o).astype(BF16)


def _decode_retention(rq, rk, rv, state):
    bd = rq.shape[0]
    per = math.gcd(bd, DEC_RET_REQUESTS)
    row = lambda width: pl.BlockSpec((per, 1, width), lambda b: (b, 0, 0))
    st = pl.BlockSpec((per, RET_HEADS, RET_DK, RET_DV), lambda b: (b, 0, 0, 0))
    return pl.pallas_call(
        _dec_ret_body,
        grid=(bd // per,),
        in_specs=[row(rq.shape[2]), row(rk.shape[2]), row(rv.shape[2]), st],
        out_specs=[row(rv.shape[2]), st],
        out_shape=[jax.ShapeDtypeStruct((bd, 1, rv.shape[2]), BF16), jax.ShapeDtypeStruct(state.shape, F32)],
        compiler_params=_params("parallel"),
        name="decode_retention",
    )(rq, rk, rv, state)


def _mix_out_body(*refs):
    main, small = refs[:6], refs[6:12]
    gn_ref, wa_ref, wr_ref, wo_ref, o_ref, os_ref = refs[12:]

    def apply(with_small):
        rows = main[0].shape[0]
        load = lambda k: (jnp.concatenate([main[k][...], small[k][...]], axis=0) if with_small else main[k][...])
        h, apre, on, srg, sga, sgb = (load(k) for k in range(6))
        a = _dot(apre, wa_ref[...].astype(BF16))
        gated = srg.astype(F32) * (on.astype(F32) * gn_ref[...])
        r = _dot(gated.astype(BF16), wr_ref[...].astype(BF16))
        m = sga.astype(F32) * a + sgb.astype(F32) * r
        y = h + _dot(m.astype(BF16), wo_ref[...].astype(BF16))
        o_ref[...] = y[:rows]
        if with_small:
            os_ref[...] = y[rows:]

    last = pl.num_programs(0) - 1
    pl.when(pl.program_id(0) < last)(functools.partial(apply, False))
    pl.when(pl.program_id(0) == last)(functools.partial(apply, True))


def _mix_out(main, small, W):
    m, d = main[0].shape
    tm = min(m, MIX_OUT_TILE)
    assert m % tm == 0
    weights = [W["ret_gn"], W["w_mla_o"], W["w_ret_o"], W["w_out"]]
    return pl.pallas_call(
        _mix_out_body,
        grid=(m // tm,),
        in_specs=[pl.BlockSpec((tm, a.shape[1]), lambda i: (i, 0)) for a in main]
                 + [_const_spec(a.shape) for a in small] + [_const_spec(w.shape) for w in weights],
        out_specs=[pl.BlockSpec((tm, d), lambda i: (i, 0)), pl.BlockSpec(small[0].shape, lambda i: (0, 0))],
        out_shape=[jax.ShapeDtypeStruct((m, d), F32), jax.ShapeDtypeStruct(small[0].shape, F32)],
        compiler_params=_params("arbitrary"),
        name="mixer_output",
    )(*main, *small, *weights)


def _rope_tables(pos):
    pos = np.asarray(pos, np.float64)[:, None]

    def table(width, reps):
        half = width // 2
        inv_freq = ROPE_BASE ** (-np.arange(half, dtype=np.float64) / half)
        ang = pos * inv_freq[None, :]
        cos, sin = np.cos(ang), np.sin(ang)
        return np.stack([np.tile(np.concatenate([cos, cos], axis=1), (1, reps)),
                         np.tile(np.concatenate([-sin, sin], axis=1), (1, reps))]).astype(np.float32)

    return table(RET_DK, 1), table(MLA_ROPE, MLA_HEADS)


def _prep_weights(ffn1_norm, ffn1_gate, ffn1_up, ffn1_down, mix_norm, w_in, q_norm, kv_norm, w_uq, w_uk, w_uv,
                  w_mla_o, ret_gn, w_ret_o, w_out, ffn2_norm, ffn2_gate, ffn2_up, ffn2_down, final_norm):
    d = w_in.shape[0]
    hr, hv = RET_HEADS * RET_DK, RET_HEADS * RET_DV
    nh, hd = MLA_HEADS, MLA_NOPE + MLA_ROPE

    bounds = [0]
    for width in (Q_RANK, KV_RANK, MLA_ROPE, hr, hr, hv, hv, d, d):
        bounds.append(bounds[-1] + width)
    assert bounds[-1] == w_in.shape[1] and all(bd % 16 == 0 for bd in bounds)
    w_in_cols = [(bounds[i], bounds[i + 1]) for i in range(9)]
    w_in_packed = w_in.T

    w_uq3 = w_uq.reshape(Q_RANK, nh, hd)
    zeros = jnp.zeros((MLA_NOPE, KV_RANK), F32)
    w_uk_t = w_uk.transpose(1, 2, 0)
    w_uk_pair = jnp.stack([jnp.block([[w_uk_t[2 * p], zeros], [zeros, w_uk_t[2 * p + 1]]])
                           for p in range(nh // 2)]).astype(BF16)
    w_uv_t = w_uv.transpose(1, 0, 2)
    zv = jnp.zeros((KV_RANK, MLA_V), F32)
    w_uv_pair = jnp.stack([jnp.block([[w_uv_t[2 * p], zv], [zv, w_uv_t[2 * p + 1]]])
                           for p in range(nh // 2)]).astype(BF16)
    row = lambda v: v.reshape(1, -1)
    return dict(
        ffn1_norm=row(ffn1_norm), ffn1=(ffn1_gate, ffn1_up, ffn1_down),
        ffn2_norm=row(ffn2_norm), ffn2=(ffn2_gate, ffn2_up, ffn2_down), final_norm=row(final_norm),
        mix_norm=row(mix_norm), w_in=w_in_packed, w_in_cols=tuple(w_in_cols), q_norm=row(q_norm), kv_norm=row(kv_norm),
        w_uq_nope=w_uq3[:, :, :MLA_NOPE].reshape(Q_RANK, nh * MLA_NOPE).astype(BF16),
        w_uq_rope=w_uq3[:, :, MLA_NOPE:].reshape(Q_RANK, nh * MLA_ROPE).astype(BF16),
        w_uk_pair=w_uk_pair, w_uv_pair=w_uv_pair, w_uv_all=w_uv.reshape(KV_RANK, nh * MLA_V).astype(BF16),
        ret_gn=row(ret_gn), w_mla_o=w_mla_o, w_ret_o=w_ret_o, w_out=w_out,
    )


def kernel(x_prompt, x_sample, cache_ckv, cache_krope, state_ret, page_table, meta_tokens, ffn1_norm, ffn1_gate, ffn1_up, ffn1_down, mix_norm, w_in, q_norm, kv_norm, w_uq, w_uk, w_uv, w_mla_o, ret_gn, w_ret_o, w_out, ffn2_norm, ffn2_gate, ffn2_up, ffn2_down, final_norm):
    assert ffn1_gate.shape[0] == 1, "single-layer trunk"
    b, t, d = x_prompt.shape
    bd, dec_seq, _ = x_sample.shape
    assert dec_seq == 1
    n_meta = meta_tokens.shape[0]
    assert n_meta == N_META
    past_len = page_table.shape[1] * PAGE_SIZE
    W = _prep_weights(ffn1_norm[0], ffn1_gate[0], ffn1_up[0], ffn1_down[0], mix_norm[0], w_in[0], q_norm[0],
                      kv_norm[0], w_uq[0], w_uk[0], w_uv[0], w_mla_o[0], ret_gn[0], w_ret_o[0], w_out[0],
                      ffn2_norm[0], ffn2_gate[0], ffn2_up[0], ffn2_down[0], final_norm)

    rope_r, rope_q = _rope_tables(n_meta + np.arange(t))
    rope_r_s, rope_q_s = _rope_tables(np.concatenate([np.arange(n_meta), np.full((bd,), past_len)]))
    x_small = jnp.concatenate([meta_tokens.astype(F32), x_sample.reshape(bd, d)], axis=0)

    h1, h1s = _ffn(x_prompt.reshape(b * t, d), x_small, W["ffn1_norm"], *W["ffn1"])
    S = _mix_in(h1s.reshape(1, n_meta + bd, d), W, rope_r_s, rope_q_s)
    meta_keys = (S["kt"][0, :, :n_meta].astype(F32), S["kcat"][0, :n_meta, :KV_RANK].astype(F32))
    P = _mix_in(h1.reshape(b, t, d), W, rope_r, rope_q, prev_keys=meta_keys)

    apre = _prompt_attention(P["qlat"], P["qrope"], P["kts"], P["vs"], P["kt"], P["kcat"], W["w_uv_pair"])
    on, s_prompt = _prompt_retention(P["rq"], P["rk"], P["rv"], S["rk"][0, :n_meta], S["rv"][0, :n_meta])

    nh = MLA_HEADS
    qlat_s = S["qlat"][0, :, n_meta:].transpose(1, 0, 2)
    qrope_s = S["qrope"][0, n_meta:].reshape(bd, nh, MLA_ROPE)
    kself = S["kcat"][0, n_meta:, :KV_RANK + MLA_ROPE].reshape(bd, 1, KV_RANK + MLA_ROPE)
    apre_s = _decode_attention(page_table, qlat_s, qrope_s, kself, W["w_uv_all"], cache_ckv[0],
                               cache_krope[0].transpose(0, 2, 1))
    dec = lambda name: S[name][0, n_meta:].reshape(bd, 1, -1)
    on_s, s_sample = _decode_retention(dec("rq"), dec("rk"), dec("rv"), state_ret[0])

    flat = lambda a: a.reshape(b * t, -1)
    tail = lambda name: S[name][0, n_meta:]
    h2, h2s = _mix_out(
        (h1, flat(apre), flat(on), flat(P["srg"]), flat(P["sga"]), flat(P["sgb"])),
        (h1s[n_meta:], apre_s.reshape(bd, -1), on_s.reshape(bd, -1), tail("srg"), tail("sga"), tail("sgb")), W)
    y_prompt, y_sample = _ffn(h2, h2s, W["ffn2_norm"], *W["ffn2"], final_norm=W["final_norm"])

    with_meta = lambda small, main: jnp.concatenate(
        [jnp.broadcast_to(small[:, :n_meta], (b, n_meta, small.shape[2])), main], axis=1)[None]
    return (
        y_prompt.reshape(b, t, d),
        y_sample.reshape(bd, 1, d),
        with_meta(S["ckv"], P["ckv"]),
        with_meta(S["kr"], P["kr"]),
        s_prompt[None],
        S["ckv"][0, n_meta:].reshape(1, bd, 1, KV_RANK),
        S["kr"][0, n_meta:].reshape(1, bd, 1, MLA_ROPE),
        s_sample[None],
    )
```
